```python
import jax, jax.numpy as jnp
from jax import lax
import numpy as np

D_MODEL = 1024
BATCH = 4
SEQ = 8192
DEPTH = 1
DEC_BATCH = 32
DEC_SEQ = 32
PAST_LEN = 2048

CHUNK = 64
D_MIX = 2 * D_MODEL
SSD_WIDTH = D_MODEL
SSD_HEAD_DIM = 64
SSD_HEADS = SSD_WIDTH // SSD_HEAD_DIM
SSD_GROUPS = 2
SSD_STATE = 128
SSD_CONV_W = 4
SSD_CHUNK = CHUNK
CONV_DIM = SSD_WIDTH + 2 * SSD_GROUPS * SSD_STATE
GMLP_WIDTH = D_MIX - SSD_WIDTH
GMLP_GROUPS = 8
GMLP_GROUP_DIM = GMLP_WIDTH // GMLP_GROUPS
GMLP_CHUNK = 128
D_IN_PROJ = SSD_WIDTH + CONV_DIM + SSD_HEADS + 3 * GMLP_WIDTH
EPS = 1e-6

kernel_name = "hybrid_ssd_gmlp_stream_step"


def rms_norm(x, g):
    xf = x.astype(jnp.float32)
    y = xf * lax.rsqrt(jnp.mean(xf * xf, axis=-1, keepdims=True) + EPS)
    return (y * g.astype(jnp.float32)).astype(x.dtype)


def layer_norm(x, g, b):
    xf = x.astype(jnp.float32)
    mu = jnp.mean(xf, axis=-1, keepdims=True)
    xc = xf - mu
    y = xc * lax.rsqrt(jnp.mean(xc * xc, axis=-1, keepdims=True) + EPS)
    return (y * g.astype(jnp.float32) + b.astype(jnp.float32)).astype(x.dtype)


def causal_conv(xbc, prev, w, b):
    xp = jnp.concatenate([prev.astype(xbc.dtype), xbc], axis=1)
    L = xbc.shape[1]
    out = b
    for k in range(SSD_CONV_W):
        out = out + w[k] * xp[:, k:k + L]
    return jax.nn.silu(out), xp[:, -(SSD_CONV_W - 1):]


def ssd_scan(x, dt, a, bm, cm, h0):
    f32 = jnp.float32
    bsz, L, H, P = x.shape
    G, N = SSD_GROUPS, SSD_STATE
    R = H // G
    lc = min(L, SSD_CHUNK)
    nc = L // lc
    xs = x.astype(f32).reshape(bsz, nc, lc, G, R, P)
    dts = dt.astype(f32).reshape(bsz, nc, lc, G, R)
    bs = bm.astype(f32).reshape(bsz, nc, lc, G, N)
    cs = cm.astype(f32).reshape(bsz, nc, lc, G, N)
    a_cum = jnp.cumsum(dts * a.astype(f32).reshape(G, R), axis=2)
    mask = jnp.tril(jnp.ones((lc, lc), dtype=bool))
    seg = a_cum[:, :, :, None] - a_cum[:, :, None, :]
    decay = jnp.exp(jnp.where(mask[:, :, None, None], seg, -jnp.inf))
    cb = jnp.einsum('bclgn,bcsgn->bclsg', cs, bs)
    w_ls = cb[..., None] * decay * dts[:, :, None]
    y_diag = jnp.einsum('bclsgr,bcsgrp->bclgrp', w_ls, xs)
    decay_end = jnp.exp(a_cum[:, :, -1:] - a_cum)
    chunk_states = jnp.einsum('bclgn,bclgr,bclgrp->bcgrpn', bs, decay_end * dts, xs)
    chunk_decay = jnp.exp(a_cum[:, :, -1])

    def step(h, inp):
        st, dc = inp
        return dc[..., None, None] * h + st, h

    h_init = h0.astype(f32).reshape(bsz, G, R, P, N)
    h_last, h_prev = lax.scan(step, h_init,
                              (jnp.moveaxis(chunk_states, 1, 0), jnp.moveaxis(chunk_decay, 1, 0)))
    h_prev = jnp.moveaxis(h_prev, 0, 1)
    y_off = jnp.einsum('bclgn,bclgr,bcgrpn->bclgrp', cs, jnp.exp(a_cum), h_prev)
    y = (y_diag + y_off).reshape(bsz, L, H, P)
    return y.astype(x.dtype), h_last.reshape(bsz, H, P, N).astype(x.dtype)


def gmlp_spatial(v, w_s, b_s):
    bsz, L, _ = v.shape
    lc = min(L, GMLP_CHUNK)
    nc = L // lc
    vs = v.reshape(bsz, nc, lc, GMLP_GROUPS, GMLP_GROUP_DIM)
    mask = jnp.tril(jnp.ones((lc, lc), dtype=bool))
    w = jnp.where(mask[None], w_s[:, :lc, :lc], 0).astype(v.dtype)
    mixed = jnp.einsum('gts,bnsgd->bntgd', w, vs) + b_s[:, :lc].T[None, None, :, :, None]
    return mixed.reshape(bsz, L, GMLP_WIDTH)


def hybrid_layer(x, c, conv_prev, ssm_prev, w_ada, b_ada, g_pre, g_post, w_in, conv_w, conv_b,
                 dt_bias, a_log, d_skip, g_v, beta_v, w_s, b_s, w_out):
    bsz, L, _ = x.shape
    mod = jax.nn.silu(c) @ w_ada + b_ada
    shift, scale, gate = jnp.split(mod, 3, axis=-1)
    h = rms_norm(x, g_pre) * (1 + scale[:, None]) + shift[:, None]
    proj = h @ w_in
    i1 = SSD_WIDTH
    i2 = i1 + CONV_DIM
    i3 = i2 + SSD_HEADS
    i4 = i3 + GMLP_WIDTH
    i5 = i4 + GMLP_WIDTH
    z, xbc, dt_raw, u, v, gt = jnp.split(proj, [i1, i2, i3, i4, i5], axis=-1)
    xbc, conv_new = causal_conv(xbc, conv_prev, conv_w, conv_b)
    xs, bm, cm = jnp.split(xbc, [SSD_WIDTH, SSD_WIDTH + SSD_GROUPS * SSD_STATE], axis=-1)
    xs = xs.reshape(bsz, L, SSD_HEADS, SSD_HEAD_DIM)
    dt = jax.nn.softplus(dt_raw + dt_bias)
    a = -jnp.exp(a_log.astype(jnp.float32))
    y, ssm_new = ssd_scan(xs, dt, a,
                          bm.reshape(bsz, L, SSD_GROUPS, SSD_STATE),
                          cm.reshape(bsz, L, SSD_GROUPS, SSD_STATE), ssm_prev)
    y = y + d_skip[:, None] * xs
    ssd_out = y.reshape(bsz, L, SSD_WIDTH) * jax.nn.silu(z)
    v_n = layer_norm(v, g_v, beta_v)
    gmlp_out = jax.nn.silu(gt) * u * gmlp_spatial(v_n, w_s, b_s)
    mix = jnp.concatenate([ssd_out, gmlp_out], axis=-1) @ w_out
    out = x + gate[:, None] * rms_norm(mix, g_post)
    return out, conv_new, ssm_new, v_n


def setup_inputs(seed: int = 0) -> dict:
    key = jax.random.key(seed)
    ks = jax.random.split(key, 24)
    f32 = jnp.float32
    nrm = lambda k, s: jax.random.normal(k, s, dtype=f32)
    dt0 = jnp.exp(jax.random.uniform(ks[10], (DEPTH, SSD_HEADS), dtype=f32,
                                     minval=np.log(1e-3), maxval=np.log(1e-1)))
    return {
        "x_prompt": nrm(ks[0], (BATCH, SEQ, D_MODEL)),
        "x_sample": nrm(ks[1], (DEC_BATCH, DEC_SEQ, D_MODEL)),
        "state_conv": nrm(ks[2], (DEPTH, DEC_BATCH, SSD_CONV_W - 1, CONV_DIM)),
        "state_ssm": 0.1 * nrm(ks[3], (DEPTH, DEC_BATCH, SSD_HEADS, SSD_HEAD_DIM, SSD_STATE)),
        "c_prompt": nrm(ks[4], (BATCH, D_MODEL)),
        "c_sample": nrm(ks[5], (DEC_BATCH, D_MODEL)),
        "w_ada": 0.3 * D_MODEL ** -0.5 * nrm(ks[6], (DEPTH, D_MODEL, 3 * D_MODEL)),
        "b_ada": 0.01 * nrm(ks[7], (DEPTH, 3 * D_MODEL)),
        "g_pre": 1.0 + 0.1 * nrm(ks[8], (DEPTH, D_MODEL)),
        "g_post": 1.0 + 0.1 * nrm(ks[9], (DEPTH, D_MODEL)),
        "w_in": D_MODEL ** -0.5 * nrm(ks[11], (DEPTH, D_MODEL, D_IN_PROJ)),
        "conv_w": SSD_CONV_W ** -0.5 * nrm(ks[12], (DEPTH, SSD_CONV_W, CONV_DIM)),
        "conv_b": 0.01 * nrm(ks[13], (DEPTH, CONV_DIM)),
        "dt_bias": dt0 + jnp.log(-jnp.expm1(-dt0)),
        "a_log": jnp.log(jax.random.uniform(ks[14], (DEPTH, SSD_HEADS), dtype=f32, minval=1.0, maxval=16.0)),
        "d_skip": 1.0 + 0.1 * nrm(ks[15], (DEPTH, SSD_HEADS)),
        "g_v": 1.0 + 0.1 * nrm(ks[16], (DEPTH, GMLP_WIDTH)),
        "beta_v": 0.01 * nrm(ks[17], (DEPTH, GMLP_WIDTH)),
        "w_s": GMLP_CHUNK ** -0.5 * nrm(ks[18], (DEPTH, GMLP_GROUPS, GMLP_CHUNK, GMLP_CHUNK)),
        "b_s": 1.0 + 0.1 * nrm(ks[19], (DEPTH, GMLP_GROUPS, GMLP_CHUNK)),
        "w_out": D_MIX ** -0.5 * nrm(ks[20], (DEPTH, D_MIX, D_MODEL)),
    }


def reference(x_prompt, x_sample, state_conv, state_ssm, c_prompt, c_sample, w_ada, b_ada, g_pre,
              g_post, w_in, conv_w, conv_b, dt_bias, a_log, d_skip, g_v, beta_v, w_s, b_s, w_out):
    yp, ys = x_prompt, x_sample
    conv_p_l, ssm_p_l, conv_s_l, ssm_s_l, v_s_l = [], [], [], [], []
    for l in range(DEPTH):
        wl = (w_ada[l], b_ada[l], g_pre[l], g_post[l], w_in[l], conv_w[l], conv_b[l], dt_bias[l],
              a_log[l], d_skip[l], g_v[l], beta_v[l], w_s[l], b_s[l], w_out[l])
        conv0 = jnp.zeros((yp.shape[0], SSD_CONV_W - 1, CONV_DIM), dtype=yp.dtype)
        ssm0 = jnp.zeros((yp.shape[0], SSD_HEADS, SSD_HEAD_DIM, SSD_STATE), dtype=yp.dtype)
        yp, conv_p, ssm_p, _ = hybrid_layer(yp, c_prompt, conv0, ssm0, *wl)
        ys, conv_s, ssm_s, v_s = hybrid_layer(ys, c_sample, state_conv[l], state_ssm[l], *wl)
        conv_p_l.append(conv_p)
        ssm_p_l.append(ssm_p)
        conv_s_l.append(conv_s)
        ssm_s_l.append(ssm_s)
        v_s_l.append(v_s)
    conv_prompt = jnp.stack(conv_p_l)
    ssm_prompt = jnp.stack(ssm_p_l)
    conv_sample = jnp.stack(conv_s_l)
    ssm_sample = jnp.stack(ssm_s_l)
    gmlp_v_sample = jnp.stack(v_s_l)
    return (yp, ys, conv_prompt, ssm_prompt, conv_sample, ssm_sample, gmlp_v_sample)
```

```python
import functools

import jax
import jax.numpy as jnp
import numpy as np
from jax import lax
from jax.experimental import pallas as pl
from jax.experimental.pallas import tpu as pltpu

F32 = jnp.float32
BF16 = jnp.bfloat16

D_MODEL = 1024
SSD_WIDTH = 1024
HEAD_DIM = 64
HEADS = SSD_WIDTH // HEAD_DIM
GROUPS = 2
STATE = 128
CONV_W = 4
CONV_DIM = SSD_WIDTH + 2 * GROUPS * STATE
GMLP_WIDTH = 1024
GMLP_GROUPS = 8
GMLP_GROUP_DIM = GMLP_WIDTH // GMLP_GROUPS
SSD_CHUNK = 64
GMLP_CHUNK = 128
EPS = 1e-6

LANES = 128
HALF = LANES // 2
PAIRS = HEADS // 2
PAIRS_PER_GROUP = PAIRS // GROUPS
TILE_ROWS = 256
TAIL_ROWS = 8
VMEM_LIMIT_BYTES = 60 * 1024 * 1024

OFF_Z = 0
OFF_XBC = OFF_Z + SSD_WIDTH
OFF_U = OFF_XBC + CONV_DIM
OFF_V = OFF_U + GMLP_WIDTH
OFF_G = OFF_V + GMLP_WIDTH
OFF_DT = OFF_G + GMLP_WIDTH
W_CAT_COLS = OFF_DT + LANES


def _dot(a, b):
    return jnp.dot(a, b, preferred_element_type=F32)


def _dot_nt(a, b):
    return lax.dot_general(a, b, (((1,), (1,)), ((), ())), preferred_element_type=F32)


def _dot_tn(a, b):
    return lax.dot_general(a, b, (((0,), (0,)), ((), ())), preferred_element_type=F32)


def _split3(x):
    hi = x.astype(BF16)
    r1 = x - hi.astype(F32)
    mid = r1.astype(BF16)
    lo = (r1 - mid.astype(F32)).astype(BF16)
    return hi, mid, lo


def _dot_f32_lhs(x, w01):
    hi, mid, lo = _split3(x)
    return _dot(hi, w01) + _dot(mid, w01) + _dot(lo, w01)


def _dot_f32_rhs(w01, x):
    hi, mid, lo = _split3(x)
    return _dot(w01, hi) + _dot(w01, mid) + _dot(w01, lo)


def _silu(x):
    return x / (1.0 + jnp.exp(-x))


def _softplus(x):
    return jnp.maximum(x, 0.0) + jnp.log1p(jnp.exp(-jnp.abs(x)))


def _rms_norm(x, g):
    return x * lax.rsqrt(jnp.mean(x * x, axis=-1, keepdims=True) + EPS) * g


def _mod_kernel(c_ref, w_ref, b_ref, o_ref):
    c = _silu(c_ref[...]).astype(BF16)
    o_ref[...] = _dot(c, w_ref[...].astype(BF16)) + b_ref[...]


def _modulation(c, w_ada, b_ada):
    rows = c.shape[0]
    n = w_ada.shape[1]
    bn = 512
    return pl.pallas_call(
        _mod_kernel,
        grid=(n // bn,),
        in_specs=[
            pl.BlockSpec((rows, D_MODEL), lambda j: (0, 0)),
            pl.BlockSpec((D_MODEL, bn), lambda j: (0, j)),
            pl.BlockSpec((1, bn), lambda j: (0, j)),
        ],
        out_specs=pl.BlockSpec((rows, bn), lambda j: (0, j)),
        out_shape=jax.ShapeDtypeStruct((rows, n), F32),
        name="adaln_mod",
    )(c, w_ada, b_ada.reshape(1, n))


def _layer_kernel(*refs, nseq, seglen, lc, has_state, emit_v):
    it = iter(refs)
    x_ref = next(it)
    shift_ref, scale_ref, gate_ref = next(it), next(it), next(it)
    conv0_ref = next(it) if has_state else None
    ssm0_ref = next(it) if has_state else None
    (gpre_ref, gpost_ref, wcat_ref, wdtT_ref, convw_ref, convb_ref, dtb_ref, alog_ref,
     dtbT_ref, alogT_ref, dskip_ref, gv_ref, betav_ref, mixw_ref, mixb_ref, wout_ref,
     expand_ref, tril_ref, ua_ref, ub_ref) = (next(it) for _ in range(20))
    y_ref, conv_out_ref, ssm_out_ref = next(it), next(it), next(it)
    vn_out_ref = next(it) if emit_v else None
    h_ref, xp_ref, st_ref, ys_ref, mix_ref = (next(it) for _ in range(5))

    t = pl.program_id(1)
    nt = pl.num_programs(1)
    nchunk = seglen // lc
    pad_rows = HALF - lc

    for s in range(nseq):
        rows = slice(s * seglen, (s + 1) * seglen)
        hs = _rms_norm(x_ref[rows, :], gpre_ref[...])
        hs = hs * (1.0 + scale_ref[s:s + 1, :]) + shift_ref[s:s + 1, :]
        h_ref[rows, :] = hs.astype(BF16)
    h = h_ref[...]

    xbc = _dot(h, wcat_ref[:, OFF_XBC:OFF_XBC + CONV_DIM])
    if not has_state:
        @pl.when(t == 0)
        def _():
            xp_ref[0:TAIL_ROWS, :] = jnp.zeros((TAIL_ROWS, CONV_DIM), F32)
            st_ref[...] = jnp.zeros_like(st_ref)

    conv_parts = []
    for s in range(nseq):
        if has_state:
            xp_ref[TAIL_ROWS - (CONV_W - 1):TAIL_ROWS, :] = conv0_ref[s]
        xp_ref[TAIL_ROWS:TAIL_ROWS + seglen, :] = xbc[s * seglen:(s + 1) * seglen, :]
        acc = convb_ref[...]
        for k in range(CONV_W):
            start = TAIL_ROWS - (CONV_W - 1) + k
            acc = acc + convw_ref[k:k + 1, :] * xp_ref[start:start + seglen, :]
        conv_parts.append(_silu(acc))
        conv_out_ref[s] = xp_ref[TAIL_ROWS + seglen - (CONV_W - 1):TAIL_ROWS + seglen, :]
        if not has_state:
            xp_ref[0:TAIL_ROWS, :] = xp_ref[seglen:seglen + TAIL_ROWS, :]
    conv = conv_parts[0] if nseq == 1 else jnp.concatenate(conv_parts, axis=0)
    xs = conv[:, 0:SSD_WIDTH]
    bmat = conv[:, SSD_WIDTH:SSD_WIDTH + GROUPS * STATE].astype(BF16)
    cmat = conv[:, SSD_WIDTH + GROUPS * STATE:CONV_DIM].astype(BF16)

    a_row = -jnp.exp(alog_ref[...])
    dt = _softplus(_dot(h, wcat_ref[:, OFF_DT:OFF_DT + LANES]) + dtb_ref[...])
    a_cum = _dot_f32_rhs(tril_ref[...], dt * a_row)
    a_exp = _dot_f32_lhs(a_cum, expand_ref[...])
    dt_exp = _dot_f32_lhs(dt, expand_ref[...])
    dt_t = _softplus(_dot_nt(wdtT_ref[...], h) + dtbT_ref[...])
    da_t = dt_t * (-jnp.exp(alogT_ref[...]))
    acum_t = (_dot_f32_lhs(da_t[0:PAIRS, :], ua_ref[...])
              + _dot_f32_lhs(da_t[PAIRS:HEADS, :], ub_ref[...]))

    xdt = xs * dt_exp
    xdt_b = xdt.astype(BF16)
    e1 = jnp.exp(a_exp)

    li = lax.broadcasted_iota(jnp.int32, (lc, LANES), 0)
    si = lax.broadcasted_iota(jnp.int32, (lc, LANES), 1) % HALF
    causal = (si <= li) & (si < lc)
    lane = lax.broadcasted_iota(jnp.int32, (lc, LANES), 1)
    low_half = lane < HALF

    def pad_to_half(v):
        if pad_rows == 0:
            return [v]
        return [v, jnp.zeros((pad_rows, v.shape[1]), v.dtype)]

    for s in range(nseq):
        if has_state:
            st_ref[...] = ssm0_ref[s].T
        for c in range(nchunk):
            ci = s * nchunk + c
            rows = slice(ci * lc, (ci + 1) * lc)
            a_c = a_exp[rows, :]
            e1_c = e1[rows, :]
            last = a_c[lc - 1:lc, :]
            xw_b = (xdt[rows, :] * jnp.exp(last - a_c)).astype(BF16)
            st_b = st_ref[...].astype(BF16)
            st_new = []
            for g in range(GROUPS):
                gl = slice(g * STATE, (g + 1) * STATE)
                hl = slice(g * (SSD_WIDTH // GROUPS), (g + 1) * (SSD_WIDTH // GROUPS))
                c_g = cmat[rows, gl]
                b_g = bmat[rows, gl]
                b2 = jnp.concatenate(pad_to_half(b_g) + pad_to_half(b_g), axis=0)
                cb2 = _dot_nt(c_g, b2)
                y_off = _dot(c_g, st_b[:, hl])
                st_new.append(_dot_tn(b_g, xw_b[:, hl]))
                for i in range(PAIRS_PER_GROUP):
                    p = g * PAIRS_PER_GROUP + i
                    pl_ = slice(p * LANES, (p + 1) * LANES)
                    seg = a_c[:, pl_] - acum_t[p:p + 1, ci * LANES:(ci + 1) * LANES]
                    w_p = (cb2 * jnp.exp(jnp.where(causal, seg, -jnp.inf))).astype(BF16)
                    x_p = xdt_b[rows, pl_]
                    zero = jnp.zeros_like(x_p)
                    bd = jnp.concatenate(pad_to_half(jnp.where(low_half, x_p, zero))
                                         + pad_to_half(jnp.where(low_half, zero, x_p)), axis=0)
                    y_p = (_dot(w_p, bd) + e1_c[:, pl_] * y_off[:, i * LANES:(i + 1) * LANES]
                           + dskip_ref[:, pl_] * xs[rows, pl_])
                    ys_ref[rows, pl_] = y_p
            st_ref[...] = st_ref[...] * e1_c[lc - 1:lc, :] + jnp.concatenate(st_new, axis=1)
        if has_state:
            ssm_out_ref[s] = st_ref[...].T

    if not has_state:
        @pl.when(t == nt - 1)
        def _():
            ssm_out_ref[0] = st_ref[...].T

    z = _dot(h, wcat_ref[:, OFF_Z:OFF_Z + SSD_WIDTH])
    mix_ref[:, 0:SSD_WIDTH] = (ys_ref[...] * _silu(z)).astype(BF16)

    v = _dot(h, wcat_ref[:, OFF_V:OFF_V + GMLP_WIDTH])
    mu = jnp.mean(v, axis=-1, keepdims=True)
    vc = v - mu
    vn = vc * lax.rsqrt(jnp.mean(vc * vc, axis=-1, keepdims=True) + EPS) * gv_ref[...] + betav_ref[...]
    if emit_v:
        vn_out_ref[...] = vn
    vn_b = vn.astype(BF16)
    mixed = jnp.concatenate(
        [_dot(mixw_ref[g], vn_b[:, g * GMLP_GROUP_DIM:(g + 1) * GMLP_GROUP_DIM])
         for g in range(GMLP_GROUPS)], axis=1) + mixb_ref[...]
    u = _dot(h, wcat_ref[:, OFF_U:OFF_U + GMLP_WIDTH])
    gt = _dot(h, wcat_ref[:, OFF_G:OFF_G + GMLP_WIDTH])
    mix_ref[:, SSD_WIDTH:SSD_WIDTH + GMLP_WIDTH] = (_silu(gt) * u * mixed).astype(BF16)

    mo = _rms_norm(_dot(mix_ref[...], wout_ref[...]), gpost_ref[...])
    for s in range(nseq):
        rows = slice(s * seglen, (s + 1) * seglen)
        y_ref[rows, :] = x_ref[rows, :] + gate_ref[s:s + 1, :] * mo[rows, :]


def _const_spec(shape):
    nd = len(shape)
    return pl.BlockSpec(shape, lambda b, t: (0,) * nd, pipeline_mode=pl.Buffered(1))


def _chunk_constants(nseq, seglen, lc):
    rows = nseq * seglen
    nck = rows // lc
    expand = np.zeros((LANES, SSD_WIDTH), np.float32)
    for hd in range(HEADS):
        expand[hd, hd * HEAD_DIM:(hd + 1) * HEAD_DIM] = 1.0
    r = np.arange(rows)
    tril = ((r[:, None] >= r[None, :]) & (r[:, None] // lc == r[None, :] // lc)).astype(np.float32)
    ua = np.zeros((rows, nck * LANES), np.float32)
    ub = np.zeros((rows, nck * LANES), np.float32)
    for ck in range(nck):
        for sp in range(lc):
            ua[ck * lc + sp, ck * LANES + sp:ck * LANES + lc] = 1.0
            ub[ck * lc + sp, ck * LANES + HALF + sp:ck * LANES + HALF + lc] = 1.0
    return (jnp.asarray(expand, BF16), jnp.asarray(tril, BF16), jnp.asarray(ua, BF16),
            jnp.asarray(ub, BF16))


def _mix_weights(w_s, b_s, nseq, seglen, lc):
    rows = nseq * seglen
    nck = rows // lc
    mask = jnp.tril(jnp.ones((lc, lc), dtype=bool))
    w = jnp.where(mask[None], w_s[:, :lc, :lc], 0)
    eye = jnp.eye(nck, dtype=w.dtype)
    wbd = jnp.einsum("ab,gts->gatbs", eye, w).reshape(GMLP_GROUPS, rows, rows).astype(BF16)
    bias = jnp.repeat(b_s[:, :lc].T, GMLP_GROUP_DIM, axis=1)
    bias = jnp.tile(bias, (nck, 1))
    return wbd, bias


def _layer(x, shift, scale, gate, conv0, ssm0, wl, *, nseq, seglen, lc_ssd, lc_gmlp, emit_v):
    (g_pre, g_post, w_cat, w_dtT, conv_w, conv_b, dtb, alog, dtbT, alogT, dskip, g_v, beta_v,
     w_s, b_s, w_out) = wl
    assert lc_ssd == lc_gmlp or nseq == 1
    has_state = conv0 is not None
    nb = shift.shape[0]
    rows_total = x.shape[0]
    nt = rows_total // (nb * TILE_ROWS)
    assert nseq * seglen == TILE_ROWS and nb * nt * TILE_ROWS == rows_total
    expand, tril, ua, ub = _chunk_constants(nseq, seglen, lc_ssd)
    mixw, mixb = _mix_weights(w_s, b_s, nseq, seglen, lc_gmlp)
    nck = TILE_ROWS // lc_ssd
    dtbT_b = jnp.broadcast_to(dtbT[:, None], (HEADS, TILE_ROWS))
    alogT_b = jnp.broadcast_to(alogT[:, None], (HEADS, TILE_ROWS))

    row_spec = pl.BlockSpec((TILE_ROWS, D_MODEL), lambda b, t: (b * nt + t, 0))
    mod_spec = pl.BlockSpec((None, nseq, D_MODEL), lambda b, t: (b, 0, 0))
    in_specs = [row_spec, mod_spec, mod_spec, mod_spec]
    args = [x, shift, scale, gate]
    if has_state:
        in_specs += [pl.BlockSpec((None, nseq, CONV_W - 1, CONV_DIM), lambda b, t: (b, 0, 0, 0)),
                     pl.BlockSpec((None, nseq, SSD_WIDTH, STATE), lambda b, t: (b, 0, 0, 0))]
        args += [conv0, ssm0]
    consts = [g_pre, g_post, w_cat, w_dtT, conv_w, conv_b, dtb, alog, dtbT_b, alogT_b, dskip,
              g_v, beta_v, mixw, mixb, w_out, expand, tril, ua, ub]
    in_specs += [_const_spec(a.shape) for a in consts]
    args += consts

    out_shape = [jax.ShapeDtypeStruct((rows_total, D_MODEL), F32),
                 jax.ShapeDtypeStruct((nb, nseq, CONV_W - 1, CONV_DIM), F32),
                 jax.ShapeDtypeStruct((nb, nseq, SSD_WIDTH, STATE), F32)]
    out_specs = [row_spec,
                 pl.BlockSpec((None, nseq, CONV_W - 1, CONV_DIM), lambda b, t: (b, 0, 0, 0)),
                 pl.BlockSpec((None, nseq, SSD_WIDTH, STATE), lambda b, t: (b, 0, 0, 0))]
    if emit_v:
        out_shape.append(jax.ShapeDtypeStruct((rows_total, GMLP_WIDTH), F32))
        out_specs.append(pl.BlockSpec((TILE_ROWS, GMLP_WIDTH), lambda b, t: (b * nt + t, 0)))

    scratch = [pltpu.VMEM((TILE_ROWS, D_MODEL), BF16),
               pltpu.VMEM((TAIL_ROWS + seglen, CONV_DIM), F32),
               pltpu.VMEM((STATE, SSD_WIDTH), F32),
               pltpu.VMEM((TILE_ROWS, SSD_WIDTH), F32),
               pltpu.VMEM((TILE_ROWS, SSD_WIDTH + GMLP_WIDTH), BF16)]
    kern = functools.partial(_layer_kernel, nseq=nseq, seglen=seglen, lc=lc_ssd,
                             has_state=has_state, emit_v=emit_v)
    return pl.pallas_call(
        kern,
        grid=(nb, nt),
        in_specs=in_specs,
        out_specs=out_specs,
        out_shape=out_shape,
        scratch_shapes=scratch,
        compiler_params=pltpu.CompilerParams(
            dimension_semantics=("arbitrary", "arbitrary"),
            vmem_limit_bytes=VMEM_LIMIT_BYTES),
        name="hybrid_layer_state" if has_state else "hybrid_layer_stream",
    )(*args)


def _prep_weights(g_pre, g_post, w_in, conv_w, conv_b, dt_bias, a_log, d_skip, g_v, beta_v,
                  w_s, b_s, w_out):
    i1 = SSD_WIDTH
    i2 = i1 + CONV_DIM
    i3 = i2 + HEADS
    i4 = i3 + GMLP_WIDTH
    i5 = i4 + GMLP_WIDTH
    wz, wxbc, wdt, wu, wv, wg = (w_in[:, :i1], w_in[:, i1:i2], w_in[:, i2:i3], w_in[:, i3:i4],
                                 w_in[:, i4:i5], w_in[:, i5:])
    wdt_pad = jnp.pad(wdt, ((0, 0), (0, LANES - HEADS)))
    w_cat = jnp.concatenate([wz, wxbc, wu, wv, wg, wdt_pad], axis=1).astype(BF16)
    perm = np.concatenate([np.arange(0, HEADS, 2), np.arange(1, HEADS, 2)])
    w_dtT = wdt.T[perm].astype(BF16)
    pad = (0, LANES - HEADS)
    return (g_pre.reshape(1, -1), g_post.reshape(1, -1), w_cat, w_dtT, conv_w,
            conv_b.reshape(1, -1), jnp.pad(dt_bias, pad).reshape(1, -1),
            jnp.pad(a_log, pad).reshape(1, -1), dt_bias[perm], a_log[perm],
            jnp.repeat(d_skip, HEAD_DIM).reshape(1, -1), g_v.reshape(1, -1),
            beta_v.reshape(1, -1), w_s, b_s, w_out.astype(BF16))


def kernel(x_prompt, x_sample, state_conv, state_ssm, c_prompt, c_sample, w_ada, b_ada, g_pre,
           g_post, w_in, conv_w, conv_b, dt_bias, a_log, d_skip, g_v, beta_v, w_s, b_s, w_out):
    depth = w_ada.shape[0]
    bp, lp, _ = x_prompt.shape
    bs, ls, _ = x_sample.shape
    seq_per_tile = TILE_ROWS // ls
    yp = x_prompt.reshape(bp * lp, D_MODEL)
    ys = x_sample.reshape(bs * ls, D_MODEL)
    c_all = jnp.concatenate([c_prompt, c_sample], axis=0)
    c_pad = jnp.pad(c_all, ((0, (-c_all.shape[0]) % 8), (0, 0)))
    outs = [[] for _ in range(5)]
    for l in range(depth):
        wl = _prep_weights(g_pre[l], g_post[l], w_in[l], conv_w[l], conv_b[l], dt_bias[l],
                           a_log[l], d_skip[l], g_v[l], beta_v[l], w_s[l], b_s[l], w_out[l])
        mod = _modulation(c_pad, w_ada[l], b_ada[l])
        shift, scale, gate = (mod[:, i * D_MODEL:(i + 1) * D_MODEL] for i in range(3))
        mp = [m[:bp].reshape(bp, 1, D_MODEL) for m in (shift, scale, gate)]
        ms = [m[bp:bp + bs].reshape(bs // seq_per_tile, seq_per_tile, D_MODEL)
              for m in (shift, scale, gate)]
        yp, conv_p, ssm_p = _layer(
            yp, *mp, None, None, wl, nseq=1, seglen=TILE_ROWS, lc_ssd=min(lp, SSD_CHUNK),
            lc_gmlp=min(lp, GMLP_CHUNK), emit_v=False)
        ys, conv_s, ssm_s, v_s = _layer(
            ys, *ms,
            state_conv[l].reshape(bs // seq_per_tile, seq_per_tile, CONV_W - 1, CONV_DIM),
            state_ssm[l].reshape(bs // seq_per_tile, seq_per_tile, SSD_WIDTH, STATE), wl,
            nseq=seq_per_tile, seglen=ls, lc_ssd=min(ls, SSD_CHUNK), lc_gmlp=min(ls, GMLP_CHUNK),
            emit_v=True)
        outs[0].append(conv_p.reshape(bp, CONV_W - 1, CONV_DIM))
        outs[1].append(ssm_p.reshape(bp, HEADS, HEAD_DIM, STATE))
        outs[2].append(conv_s.reshape(bs, CONV_W - 1, CONV_DIM))
        outs[3].append(ssm_s.reshape(bs, HEADS, HEAD_DIM, STATE))
        outs[4].append(v_s.reshape(bs, ls, GMLP_WIDTH))
    return (yp.reshape(bp, lp, D_MODEL), ys.reshape(bs, ls, D_MODEL), jnp.stack(outs[0]),
            jnp.stack(outs[1]), jnp.stack(outs[2]), jnp.stack(outs[3]), jnp.stack(outs[4]))
```

```python
import functools

import jax
import jax.numpy as jnp
import numpy as np
from jax import lax
from jax.experimental import pallas as pl
from jax.experimental.pallas import tpu as pltpu

F32 = jnp.float32
BF16 = jnp.bfloat16

D_MODEL = 1024
SSD_WIDTH = 1024
HEAD_DIM = 64
HEADS = SSD_WIDTH // HEAD_DIM
GROUPS = 2
GROUP_WIDTH = SSD_WIDTH // GROUPS
STATE = 128
CONV_W = 4
CONV_DIM = SSD_WIDTH + 2 * GROUPS * STATE
GMLP_WIDTH = 1024
GMLP_GROUPS = 8
GMLP_GROUP_DIM = GMLP_WIDTH // GMLP_GROUPS
SSD_CHUNK = 64
GMLP_CHUNK = 128
EPS = 1e-6
LOG2E = 1.4426950408889634

LANES = 128
MXU_COLS = 256
HALF = LANES // 2
PAIRS = HEADS // 2
PAIRS_PER_GROUP = PAIRS // GROUPS
PIECES = 3
TILE_ROWS = 256
TAIL_ROWS = 8
TAP0 = TAIL_ROWS - (CONV_W - 1)
MOD_BLOCK_COLS = 512
VMEM_LIMIT_BYTES = 60 * 1024 * 1024

OFF_Z = 0
OFF_XBC = OFF_Z + SSD_WIDTH
OFF_U = OFF_XBC + CONV_DIM
OFF_V = OFF_U + GMLP_WIDTH
OFF_G = OFF_V + GMLP_WIDTH
OFF_DT = OFF_G + GMLP_WIDTH
W_CAT_COLS = OFF_DT + LANES


def _dot(a, b):
    return jnp.dot(a, b, preferred_element_type=F32)


def _dot_nt(a, b):
    return lax.dot_general(a, b, (((1,), (1,)), ((), ())), preferred_element_type=F32)


def _dot_tn(a, b):
    return lax.dot_general(a, b, (((0,), (0,)), ((), ())), preferred_element_type=F32)


def _cat3(x):
    hi = x.astype(BF16).astype(F32)
    r1 = x - hi
    mid = r1.astype(BF16).astype(F32)
    lo = r1 - mid
    return hi + pltpu.roll(mid, HEADS, axis=1) + pltpu.roll(lo, 2 * HEADS, axis=1)


def _silu(x):
    return x / (1.0 + jnp.exp2(x * (-LOG2E)))


def _softplus(x):
    return jnp.maximum(x, 0.0) + jnp.log1p(jnp.exp(-jnp.abs(x)))


def _rms_norm(x, g):
    return x * lax.rsqrt(jnp.mean(x * x, axis=-1, keepdims=True) + EPS) * g


def _mod_kernel(c_ref, w_ref, b_ref, o_ref):
    c = _silu(c_ref[...]).astype(BF16)
    o_ref[...] = _dot(c, w_ref[...].astype(BF16)) + b_ref[...]


def _modulation(c, w_ada, b_ada):
    rows = c.shape[0]
    n = w_ada.shape[1]
    return pl.pallas_call(
        _mod_kernel,
        grid=(n // MOD_BLOCK_COLS,),
        in_specs=[
            pl.BlockSpec((rows, D_MODEL), lambda j: (0, 0)),
            pl.BlockSpec((D_MODEL, MOD_BLOCK_COLS), lambda j: (0, j)),
            pl.BlockSpec((1, MOD_BLOCK_COLS), lambda j: (0, j)),
        ],
        out_specs=pl.BlockSpec((rows, MOD_BLOCK_COLS), lambda j: (0, j)),
        out_shape=jax.ShapeDtypeStruct((rows, n), F32),
        name="adaln_mod",
    )(c, w_ada, b_ada.reshape(1, n))


def _interleave(primary, secondary):
    order = []
    done = 0
    for i, task in enumerate(primary):
        order.append(task)
        want = ((i + 1) * len(secondary)) // len(primary)
        order.extend(secondary[done:want])
        done = want
    return order


def _layer_kernel(*refs, nseq, seglen, lc, nt, pipelined, emit_v):
    statics = dict(nseq=nseq, seglen=seglen, lc=lc, pipelined=pipelined, emit_v=emit_v)
    n_scratch = 19
    shared, (proj_a, proj_b, xsave_a, xsave_b) = refs[:-4], refs[-4:]
    ssm_out_ref = refs[len(refs) - n_scratch - (2 if emit_v else 1)]
    tail_ref, st_ref = refs[-n_scratch + 1], refs[-n_scratch + 3]
    if not pipelined:
        _tile_body(shared, proj_a, proj_a, xsave_a, xsave_a, **statics)
        return

    j = pl.program_id(0)
    first_of_batch = (j == 0) | (lax.rem(j - 1, nt) == 0)
    last_of_batch = (j > 0) & (lax.rem(j - 1, nt) == nt - 1)

    @pl.when(j == 0)
    def _():
        proj_b[...] = jnp.zeros_like(proj_b)
        xsave_b[...] = jnp.zeros_like(xsave_b)

    @pl.when(first_of_batch)
    def _():
        tail_ref[...] = jnp.zeros_like(tail_ref)
        st_ref[...] = jnp.zeros_like(st_ref)

    @pl.when(lax.rem(j, 2) == 0)
    def _():
        _tile_body(shared, proj_a, proj_b, xsave_a, xsave_b, **statics)

    @pl.when(lax.rem(j, 2) == 1)
    def _():
        _tile_body(shared, proj_b, proj_a, xsave_b, xsave_a, **statics)

    @pl.when(last_of_batch)
    def _():
        ssm_out_ref[0] = st_ref[...].T


def _tile_body(refs, pm, pv, xsave_m, xsave_v, *, nseq, seglen, lc, pipelined, emit_v):
    has_state = not pipelined
    it = iter(refs)
    x_ref = next(it)
    shift_ref, scale_ref, gate_ref = next(it), next(it), next(it)
    conv0_ref = next(it) if has_state else None
    ssm0_ref = next(it) if has_state else None
    (gpre_ref, gpost_ref, wcat_ref, convw_ref, convb_ref, dtb_ref, alog_ref, dskip_ref, gv_ref,
     betav_ref, mixw_ref, mixb_ref, wout_ref, expand_ref, tril_ref, psel_ref) = (
         next(it) for _ in range(16))
    y_ref, conv_out_ref, ssm_out_ref = next(it), next(it), next(it)
    vn_out_ref = next(it) if emit_v else None
    (h_ref, tail_ref, xp_ref, st_ref, stnew_ref, xs_ref, bc_ref, aexp_ref, xdt_ref, xlo_ref,
     xhi_ref, acumt_ref, vnb_ref, mix_ref, mo_ref) = (next(it) for _ in range(15))

    rows_all = slice(TAIL_ROWS, TAIL_ROWS + TILE_ROWS)
    nchunk = seglen // lc
    pad_rows = HALF - lc

    def pad_to_half(v):
        if pad_rows == 0:
            return [v]
        return [v, jnp.zeros((pad_rows, v.shape[1]), v.dtype)]

    def m_norm():
        for s in range(nseq):
            rows = slice(s * seglen, (s + 1) * seglen)
            hs = _rms_norm(x_ref[rows, :], gpre_ref[...])
            hs = hs * (1.0 + scale_ref[s:s + 1, :]) + shift_ref[s:s + 1, :]
            h_ref[rows, :] = hs.astype(BF16)
        xsave_m[...] = x_ref[...]

    def m_dot(c0, c1):
        def task():
            pm[rows_all, c0:c1] = _dot(h_ref[...], wcat_ref[:, c0:c1])
        return task

    m_tasks = [m_dot(c0, min(c0 + MXU_COLS, W_CAT_COLS)) for c0 in range(0, W_CAT_COLS, MXU_COLS)]

    lane1 = lax.broadcasted_iota(jnp.int32, (TILE_ROWS, LANES), 1)

    def v_dt():
        a_row = -jnp.exp(alog_ref[...])
        dt = _softplus(pv[rows_all, OFF_DT:OFF_DT + LANES] + dtb_ref[...])
        dt = jnp.where(lane1 < HEADS, dt, 0.0)
        csum = _dot(tril_ref[...], _cat3(dt * a_row).astype(BF16))
        a_cum = csum + pltpu.roll(csum, LANES - HEADS, axis=1) + pltpu.roll(csum, LANES - 2 * HEADS, axis=1)
        a_cat = _cat3(jnp.where(lane1 < HEADS, a_cum, 0.0))
        aexp_ref[...] = _dot(a_cat.astype(BF16), expand_ref[...])
        dt_exp = _dot(_cat3(dt).astype(BF16), expand_ref[...])
        even = lax.rem(lane1, 2) == 0
        a_even = jnp.where(even, a_cat, 0.0)
        a_odd = (a_cat - a_even).astype(BF16)
        a_even = a_even.astype(BF16)
        parts = []
        for ck in range(TILE_ROWS // lc):
            rows = slice(ck * lc, (ck + 1) * lc)
            parts += pad_to_half(a_even[rows, :]) + pad_to_half(a_odd[rows, :])
        acumt_ref[...] = _dot_nt(psel_ref[...], jnp.concatenate(parts, axis=0))
        xdt = xs_ref[...] * dt_exp
        lane_w = lax.broadcasted_iota(jnp.int32, (TILE_ROWS, SSD_WIDTH), 1)
        xlo = jnp.where(lax.rem(lane_w, LANES) < HALF, xdt, 0.0)
        xdt_ref[...] = xdt
        xlo_ref[...] = xlo.astype(BF16)
        xhi_ref[...] = (xdt - xlo).astype(BF16)

    def conv_store(c, val, rows):
        if c < SSD_WIDTH // LANES:
            xs_ref[rows, c * LANES:(c + 1) * LANES] = val
        else:
            c2 = c - SSD_WIDTH // LANES
            bc_ref[rows, c2 * LANES:(c2 + 1) * LANES] = val.astype(BF16)

    def v_conv(c):
        cols = slice(OFF_XBC + c * LANES, OFF_XBC + (c + 1) * LANES)
        wcols = slice(c * LANES, (c + 1) * LANES)

        def task_stream():
            pv[0:TAIL_ROWS, cols] = tail_ref[:, wcols]
            acc = convb_ref[:, wcols]
            for k in range(CONV_W):
                acc = acc + convw_ref[k:k + 1, wcols] * pv[TAP0 + k:TAP0 + k + TILE_ROWS, cols]
            conv_store(c, _silu(acc), slice(0, TILE_ROWS))
            tail_ref[:, wcols] = pv[TILE_ROWS:TILE_ROWS + TAIL_ROWS, cols]
            conv_out_ref[0, :, wcols] = pv[TAIL_ROWS + TILE_ROWS - (CONV_W - 1):TAIL_ROWS + TILE_ROWS, cols]

        def task_state():
            for s in range(nseq):
                rows = slice(s * seglen, (s + 1) * seglen)
                xp_ref[TAP0:TAIL_ROWS, :] = conv0_ref[s, :, wcols]
                xp_ref[TAIL_ROWS:TAIL_ROWS + seglen, :] = pv[TAIL_ROWS + s * seglen:TAIL_ROWS + (s + 1) * seglen, cols]
                acc = convb_ref[:, wcols]
                for k in range(CONV_W):
                    acc = acc + convw_ref[k:k + 1, wcols] * xp_ref[TAP0 + k:TAP0 + k + seglen, :]
                conv_store(c, _silu(acc), rows)
                conv_out_ref[s, :, wcols] = xp_ref[TAIL_ROWS + seglen - (CONV_W - 1):TAIL_ROWS + seglen, :]

        return task_stream if pipelined else task_state

    li = lax.broadcasted_iota(jnp.int32, (lc, LANES), 0)
    si = lax.rem(lax.broadcasted_iota(jnp.int32, (lc, LANES), 1), HALF)
    causal = (si <= li) & (si < lc)

    def v_ssd(ci, g):
        rows = slice(ci * lc, (ci + 1) * lc)
        prow = slice(TAIL_ROWS + ci * lc, TAIL_ROWS + (ci + 1) * lc)
        hl = slice(g * GROUP_WIDTH, (g + 1) * GROUP_WIDTH)

        def task():
            last = aexp_ref[(ci + 1) * lc - 1:(ci + 1) * lc, hl]
            a_c = aexp_ref[rows, hl]
            e1_c = jnp.exp(a_c)
            xw_b = (xdt_ref[rows, hl] * jnp.exp(last - a_c)).astype(BF16)
            b_g = bc_ref[rows, g * STATE:(g + 1) * STATE]
            c_g = bc_ref[rows, (GROUPS + g) * STATE:(GROUPS + g + 1) * STATE]
            b2 = jnp.concatenate(pad_to_half(b_g) + pad_to_half(b_g), axis=0)
            cb2 = _dot_nt(c_g, b2)
            y_off = _dot(c_g, st_ref[:, hl].astype(BF16))
            stnew_ref[:, hl] = _dot_tn(b_g, xw_b)
            for i in range(PAIRS_PER_GROUP):
                p = g * PAIRS_PER_GROUP + i
                pcols = slice(p * LANES, (p + 1) * LANES)
                lcols = slice(i * LANES, (i + 1) * LANES)
                seg = a_c[:, lcols] - acumt_ref[p:p + 1, ci * LANES:(ci + 1) * LANES]
                w_p = (cb2 * jnp.exp(jnp.where(causal, seg, -jnp.inf))).astype(BF16)
                bd = jnp.concatenate(pad_to_half(xlo_ref[rows, pcols]) + pad_to_half(xhi_ref[rows, pcols]),
                                     axis=0)
                y_p = (_dot(w_p, bd) + e1_c[:, lcols] * y_off[:, lcols]
                       + dskip_ref[:, pcols] * xs_ref[rows, pcols])
                zg = _silu(pv[prow, OFF_Z + p * LANES:OFF_Z + (p + 1) * LANES])
                mix_ref[rows, pcols] = (y_p * zg).astype(BF16)
        return task

    def v_state(ci, s):
        def task():
            e_last = jnp.exp(aexp_ref[(ci + 1) * lc - 1:(ci + 1) * lc, :])
            st_ref[...] = st_ref[...] * e_last + stnew_ref[...]
            if has_state and ci % nchunk == nchunk - 1:
                ssm_out_ref[s] = st_ref[...].T
        return task

    def v_state_load(s):
        def task():
            st_ref[...] = ssm0_ref[s].T
        return task

    def v_ln():
        v = pv[rows_all, OFF_V:OFF_V + GMLP_WIDTH]
        mu = jnp.mean(v, axis=-1, keepdims=True)
        vc = v - mu
        vn = vc * lax.rsqrt(jnp.mean(vc * vc, axis=-1, keepdims=True) + EPS) * gv_ref[...] + betav_ref[...]
        if emit_v:
            vn_out_ref[...] = vn
        vnb_ref[...] = vn.astype(BF16)

    def v_gmlp(g):
        gcols = slice(g * GMLP_GROUP_DIM, (g + 1) * GMLP_GROUP_DIM)

        def task():
            mixed = _dot(mixw_ref[g], vnb_ref[:, gcols]) + mixb_ref[:, gcols]
            u = pv[rows_all, OFF_U + g * GMLP_GROUP_DIM:OFF_U + (g + 1) * GMLP_GROUP_DIM]
            gt = pv[rows_all, OFF_G + g * GMLP_GROUP_DIM:OFF_G + (g + 1) * GMLP_GROUP_DIM]
            mix_ref[:, SSD_WIDTH + g * GMLP_GROUP_DIM:SSD_WIDTH + (g + 1) * GMLP_GROUP_DIM] = (
                _silu(gt) * u * mixed).astype(BF16)
        return task

    def v_out(n):
        def task():
            mo_ref[:, n * MXU_COLS:(n + 1) * MXU_COLS] = _dot(
                mix_ref[...], wout_ref[:, n * MXU_COLS:(n + 1) * MXU_COLS])
        return task

    def v_final():
        mo = _rms_norm(mo_ref[...], gpost_ref[...])
        for s in range(nseq):
            rows = slice(s * seglen, (s + 1) * seglen)
            y_ref[rows, :] = xsave_v[rows, :] + gate_ref[s:s + 1, :] * mo[rows, :]

    v_tasks = [v_conv(c) for c in range(CONV_DIM // LANES)] + [v_dt, v_ln]
    for s in range(nseq):
        if has_state:
            v_tasks.append(v_state_load(s))
        for c in range(nchunk):
            ci = s * nchunk + c
            v_tasks += [v_ssd(ci, 0), v_ssd(ci, 1), v_state(ci, s)]
    v_tail = [v_gmlp(g) for g in range(GMLP_GROUPS)] + [v_out(n) for n in range(D_MODEL // MXU_COLS)]
    v_tail.append(v_final)

    m_norm()
    order = _interleave(v_tasks, m_tasks) if pipelined else m_tasks + v_tasks
    for task in order + v_tail:
        task()


def _const_spec(shape):
    nd = len(shape)
    return pl.BlockSpec(shape, lambda j: (0,) * nd, pipeline_mode=pl.Buffered(1))


def _chunk_constants(lc):
    expand = np.zeros((LANES, SSD_WIDTH), np.float32)
    psel = np.zeros((PAIRS, LANES), np.float32)
    for piece in range(PIECES):
        for hd in range(HEADS):
            expand[piece * HEADS + hd, hd * HEAD_DIM:(hd + 1) * HEAD_DIM] = 1.0
            psel[hd // 2, piece * HEADS + hd] = 1.0
    r = np.arange(TILE_ROWS)
    tril = ((r[:, None] >= r[None, :]) & (r[:, None] // lc == r[None, :] // lc)).astype(np.float32)
    return jnp.asarray(expand, BF16), jnp.asarray(tril, BF16), jnp.asarray(psel, BF16)


def _mix_weights(w_s, b_s, lc):
    nck = TILE_ROWS // lc
    mask = jnp.tril(jnp.ones((lc, lc), dtype=bool))
    w = jnp.where(mask[None], w_s[:, :lc, :lc], 0)
    eye = jnp.eye(nck, dtype=w.dtype)
    wbd = jnp.einsum("ab,gts->gatbs", eye, w).reshape(GMLP_GROUPS, TILE_ROWS, TILE_ROWS).astype(BF16)
    bias = jnp.repeat(b_s[:, :lc].T, GMLP_GROUP_DIM, axis=1)
    return wbd, jnp.tile(bias, (nck, 1))


def _layer(x, shift, scale, gate, conv0, ssm0, wl, *, nseq, seglen, lc_ssd, lc_gmlp, nt, emit_v):
    (g_pre, g_post, w_cat, conv_w, conv_b, dtb, alog, dskip, g_v, beta_v, w_s, b_s, w_out) = wl
    assert lc_ssd == lc_gmlp or nseq == 1
    pipelined = conv0 is None
    nb = shift.shape[0]
    rows_total = x.shape[0]
    ntiles = rows_total // TILE_ROWS
    assert nseq * seglen == TILE_ROWS and nb * nt == ntiles
    expand, tril, psel = _chunk_constants(lc_ssd)
    mixw, mixb = _mix_weights(w_s, b_s, lc_gmlp)
    nck = TILE_ROWS // lc_ssd

    if pipelined:
        def tile_m(j):
            return jnp.minimum(j, ntiles - 1)

        def tile_v(j):
            return jnp.maximum(j - 1, 0)
        steps = ntiles + 1
    else:
        def tile_m(j):
            return j
        tile_v = tile_m
        steps = ntiles

    def mod_spec(tile_of):
        return pl.BlockSpec((None, nseq, D_MODEL), lambda j: (tile_of(j) // nt, 0, 0))

    in_specs = [pl.BlockSpec((TILE_ROWS, D_MODEL), lambda j: (tile_m(j), 0)),
                mod_spec(tile_m), mod_spec(tile_m), mod_spec(tile_v)]
    args = [x, shift, scale, gate]
    if not pipelined:
        in_specs += [pl.BlockSpec((None, nseq, CONV_W - 1, CONV_DIM), lambda j: (j // nt, 0, 0, 0)),
                     pl.BlockSpec((None, nseq, SSD_WIDTH, STATE), lambda j: (j // nt, 0, 0, 0))]
        args += [conv0, ssm0]
    consts = [g_pre, g_post, w_cat, conv_w, conv_b, dtb, alog, dskip, g_v, beta_v, mixw, mixb,
              w_out, expand, tril, psel]
    in_specs += [_const_spec(a.shape) for a in consts]
    args += consts

    out_shape = [jax.ShapeDtypeStruct((rows_total, D_MODEL), F32),
                 jax.ShapeDtypeStruct((nb, nseq, CONV_W - 1, CONV_DIM), F32),
                 jax.ShapeDtypeStruct((nb, nseq, SSD_WIDTH, STATE), F32)]
    out_specs = [pl.BlockSpec((TILE_ROWS, D_MODEL), lambda j: (tile_v(j), 0)),
                 pl.BlockSpec((None, nseq, CONV_W - 1, CONV_DIM), lambda j: (tile_v(j) // nt, 0, 0, 0)),
                 pl.BlockSpec((None, nseq, SSD_WIDTH, STATE), lambda j: (tile_v(j) // nt, 0, 0, 0))]
    if emit_v:
        out_shape.append(jax.ShapeDtypeStruct((rows_total, GMLP_WIDTH), F32))
        out_specs.append(pl.BlockSpec((TILE_ROWS, GMLP_WIDTH), lambda j: (tile_v(j), 0)))

    proj_shape = (TAIL_ROWS + TILE_ROWS, W_CAT_COLS)
    spare = (TAIL_ROWS, LANES)
    scratch = [pltpu.VMEM((TILE_ROWS, D_MODEL), BF16),
               pltpu.VMEM((TAIL_ROWS, CONV_DIM), F32),
               pltpu.VMEM((TAIL_ROWS + seglen, LANES), F32),
               pltpu.VMEM((STATE, SSD_WIDTH), F32),
               pltpu.VMEM((STATE, SSD_WIDTH), F32),
               pltpu.VMEM((TILE_ROWS, SSD_WIDTH), F32),
               pltpu.VMEM((TILE_ROWS, 2 * GROUPS * STATE), BF16),
               pltpu.VMEM((TILE_ROWS, SSD_WIDTH), F32),
               pltpu.VMEM((TILE_ROWS, SSD_WIDTH), F32),
               pltpu.VMEM((TILE_ROWS, SSD_WIDTH), BF16),
               pltpu.VMEM((TILE_ROWS, SSD_WIDTH), BF16),
               pltpu.VMEM((PAIRS, nck * LANES), F32),
               pltpu.VMEM((TILE_ROWS, GMLP_WIDTH), BF16),
               pltpu.VMEM((TILE_ROWS, SSD_WIDTH + GMLP_WIDTH), BF16),
               pltpu.VMEM((TILE_ROWS, D_MODEL), F32),
               pltpu.VMEM(proj_shape, F32),
               pltpu.VMEM(proj_shape if pipelined else spare, F32),
               pltpu.VMEM((TILE_ROWS, D_MODEL), F32),
               pltpu.VMEM((TILE_ROWS, D_MODEL) if pipelined else spare, F32)]
    kern = functools.partial(_layer_kernel, nseq=nseq, seglen=seglen, lc=lc_ssd, nt=nt,
                             pipelined=pipelined, emit_v=emit_v)
    return pl.pallas_call(
        kern,
        grid=(steps,),
        in_specs=in_specs,
        out_specs=out_specs,
        out_shape=out_shape,
        scratch_shapes=scratch,
        compiler_params=pltpu.CompilerParams(
            dimension_semantics=("arbitrary",),
            vmem_limit_bytes=VMEM_LIMIT_BYTES),
        name="hybrid_layer_stream" if pipelined else "hybrid_layer_state",
    )(*args)


def _prep_weights(g_pre, g_post, w_in, conv_w, conv_b, dt_bias, a_log, d_skip, g_v, beta_v,
                  w_s, b_s, w_out):
    i1 = SSD_WIDTH
    i2 = i1 + CONV_DIM
    i3 = i2 + HEADS
    i4 = i3 + GMLP_WIDTH
    i5 = i4 + GMLP_WIDTH
    wz, wxbc, wdt, wu, wv, wg = (w_in[:, :i1], w_in[:, i1:i2], w_in[:, i2:i3], w_in[:, i3:i4],
                                 w_in[:, i4:i5], w_in[:, i5:])
    wdt_pad = jnp.pad(wdt, ((0, 0), (0, LANES - HEADS)))
    w_cat = jnp.concatenate([wz, wxbc, wu, wv, wg, wdt_pad], axis=1).astype(BF16)
    pad = (0, LANES - HEADS)
    return (g_pre.reshape(1, -1), g_post.reshape(1, -1), w_cat, conv_w, conv_b.reshape(1, -1),
            jnp.pad(dt_bias, pad).reshape(1, -1), jnp.pad(a_log, pad).reshape(1, -1),
            jnp.repeat(d_skip, HEAD_DIM).reshape(1, -1), g_v.reshape(1, -1),
            beta_v.reshape(1, -1), w_s, b_s, w_out.astype(BF16))


def kernel(x_prompt, x_sample, state_conv, state_ssm, c_prompt, c_sample, w_ada, b_ada, g_pre,
           g_post, w_in, conv_w, conv_b, dt_bias, a_log, d_skip, g_v, beta_v, w_s, b_s, w_out):
    depth = w_ada.shape[0]
    bp, lp, _ = x_prompt.shape
    bs, ls, _ = x_sample.shape
    seq_per_tile = TILE_ROWS // ls
    yp = x_prompt.reshape(bp * lp, D_MODEL)
    ys = x_sample.reshape(bs * ls, D_MODEL)
    c_all = jnp.concatenate([c_prompt, c_sample], axis=0)
    c_pad = jnp.pad(c_all, ((0, (-c_all.shape[0]) % 8), (0, 0)))
    outs = [[] for _ in range(5)]
    for l in range(depth):
        wl = _prep_weights(g_pre[l], g_post[l], w_in[l], conv_w[l], conv_b[l], dt_bias[l],
                           a_log[l], d_skip[l], g_v[l], beta_v[l], w_s[l], b_s[l], w_out[l])
        mod = _modulation(c_pad, w_ada[l], b_ada[l])
        shift, scale, gate = (mod[:, i * D_MODEL:(i + 1) * D_MODEL] for i in range(3))
        mp = [m[:bp].reshape(bp, 1, D_MODEL) for m in (shift, scale, gate)]
        ms = [m[bp:bp + bs].reshape(bs // seq_per_tile, seq_per_tile, D_MODEL)
              for m in (shift, scale, gate)]
        yp, conv_p, ssm_p = _layer(
            yp, *mp, None, None, wl, nseq=1, seglen=TILE_ROWS, lc_ssd=min(lp, SSD_CHUNK),
            lc_gmlp=min(lp, GMLP_CHUNK), nt=lp // TILE_ROWS, emit_v=False)
        ys, conv_s, ssm_s, v_s = _layer(
            ys, *ms,
            state_conv[l].reshape(bs // seq_per_tile, seq_per_tile, CONV_W - 1, CONV_DIM),
            state_ssm[l].reshape(bs // seq_per_tile, seq_per_tile, SSD_WIDTH, STATE), wl,
            nseq=seq_per_tile, seglen=ls, lc_ssd=min(ls, SSD_CHUNK), lc_gmlp=min(ls, GMLP_CHUNK),
            nt=1, emit_v=True)
        outs[0].append(conv_p.reshape(bp, CONV_W - 1, CONV_DIM))
        outs[1].append(ssm_p.reshape(bp, HEADS, HEAD_DIM, STATE))
        outs[2].append(conv_s.reshape(bs, CONV_W - 1, CONV_DIM))
        outs[3].append(ssm_s.reshape(bs, HEADS, HEAD_DIM, STATE))
        outs[4].append(v_s.reshape(bs, ls, GMLP_WIDTH))
    return (yp.reshape(bp, lp, D_MODEL), ys.reshape(bs, ls, D_MODEL), jnp.stack(outs[0]),
            jnp.stack(outs[1]), jnp.stack(outs[2]), jnp.stack(outs[3]), jnp.stack(outs[4]))
```

```python
import collections
import functools

import jax
import jax.numpy as jnp
import numpy as np
from jax import lax
from jax.experimental import pallas as pl
from jax.experimental.pallas import tpu as pltpu

F32 = jnp.float32
BF16 = jnp.bfloat16

D_MODEL = 1024
SSD_WIDTH = 1024
HEAD_DIM = 64
HEADS = SSD_WIDTH // HEAD_DIM
GROUPS = 2
GROUP_WIDTH = SSD_WIDTH // GROUPS
STATE = 128
CONV_W = 4
CONV_DIM = SSD_WIDTH + 2 * GROUPS * STATE
GMLP_WIDTH = 1024
GMLP_GROUPS = 8
GMLP_GROUP_DIM = GMLP_WIDTH // GMLP_GROUPS
SSD_CHUNK = 64
GMLP_CHUNK = 128
EPS = 1e-6
LOG2E = 1.4426950408889634

LANES = 128
MXU_COLS = 256
HALF = LANES // 2
PAIRS = HEADS // 2
PAIRS_PER_GROUP = PAIRS // GROUPS
PIECES = 3
TILE_ROWS = 256
TAIL_ROWS = 8
TAP0 = TAIL_ROWS - (CONV_W - 1)
MOD_BLOCK_COLS = 512
VMEM_LIMIT_BYTES = 60 * 1024 * 1024

OFF_XBC = 0
OFF_Z = OFF_XBC + CONV_DIM
OFF_U = OFF_Z + SSD_WIDTH
OFF_V = OFF_U + GMLP_WIDTH
OFF_G = OFF_V + GMLP_WIDTH
OFF_DT = OFF_G + GMLP_WIDTH
W_CAT_COLS = OFF_DT + LANES

Bufs = collections.namedtuple("Bufs", "proj xs bc mix")


def _dot(a, b):
    return jnp.dot(a, b, preferred_element_type=F32)


def _dot_nt(a, b):
    return lax.dot_general(a, b, (((1,), (1,)), ((), ())), preferred_element_type=F32)


def _dot_tn(a, b):
    return lax.dot_general(a, b, (((0,), (0,)), ((), ())), preferred_element_type=F32)


def _cat3(x):
    hi = x.astype(BF16).astype(F32)
    r1 = x - hi
    mid = r1.astype(BF16).astype(F32)
    lo = r1 - mid
    return hi + pltpu.roll(mid, HEADS, axis=1) + pltpu.roll(lo, 2 * HEADS, axis=1)


def _silu(x):
    return x / (1.0 + jnp.exp2(x * (-LOG2E)))


def _softplus(x):
    return jnp.maximum(x, 0.0) + jnp.log1p(jnp.exp(-jnp.abs(x)))


def _row_mean(x):
    width = x.shape[-1]
    part = x[:, 0:LANES]
    for c in range(LANES, width, LANES):
        part = part + x[:, c:c + LANES]
    return jnp.sum(part, axis=-1, keepdims=True) * (1.0 / width)


def _rms_scale(x):
    return lax.rsqrt(_row_mean(x * x) + EPS)


def _mod_kernel(c_ref, w_ref, b_ref, o_ref):
    c = _silu(c_ref[...]).astype(BF16)
    o_ref[...] = _dot(c, w_ref[...].astype(BF16)) + b_ref[...]


def _modulation(c, w_ada, b_ada):
    rows = c.shape[0]
    n = w_ada.shape[1]
    return pl.pallas_call(
        _mod_kernel,
        grid=(n // MOD_BLOCK_COLS,),
        in_specs=[
            pl.BlockSpec((rows, D_MODEL), lambda j: (0, 0)),
            pl.BlockSpec((D_MODEL, MOD_BLOCK_COLS), lambda j: (0, j)),
            pl.BlockSpec((1, MOD_BLOCK_COLS), lambda j: (0, j)),
        ],
        out_specs=pl.BlockSpec((rows, MOD_BLOCK_COLS), lambda j: (0, j)),
        out_shape=jax.ShapeDtypeStruct((rows, n), F32),
        name="adaln_mod",
    )(c, w_ada, b_ada.reshape(1, n))


N_SHARED_SCRATCH = 13
N_STAGE_SCRATCH = 8


def _interleave(primary, secondary):
    order = []
    done = 0
    for i, task in enumerate(primary):
        order.append(task)
        want = ((i + 1) * len(secondary)) // len(primary)
        order.extend(secondary[done:want])
        done = want
    return order


def _layer_kernel(*refs, nseq, seglen, lc, nt, pipelined, emit_v):
    statics = dict(nseq=nseq, seglen=seglen, lc=lc, pipelined=pipelined, emit_v=emit_v)
    shared, stage = refs[:-N_STAGE_SCRATCH], refs[-N_STAGE_SCRATCH:]
    buf_a, buf_b = Bufs(*stage[0::2]), Bufs(*stage[1::2])
    n_scratch = N_SHARED_SCRATCH + N_STAGE_SCRATCH
    ssm_out_ref = refs[len(refs) - n_scratch - (2 if emit_v else 1)]
    tail_ref, st_ref = refs[-n_scratch + 1], refs[-n_scratch + 3]
    if not pipelined:
        _tile_body(shared, buf_a, buf_a, buf_a, buf_a, **statics)
        return

    j = pl.program_id(0)
    m_first = lax.rem(j, nt) == 0
    v_first = (j == 0) | (lax.rem(j - 1, nt) == 0)
    v_last = (j > 0) & (lax.rem(j - 1, nt) == nt - 1)

    @pl.when(j == 0)
    def _():
        for ref in buf_b:
            ref[...] = jnp.zeros_like(ref)

    @pl.when(m_first)
    def _():
        tail_ref[...] = jnp.zeros_like(tail_ref)

    @pl.when(v_first)
    def _():
        st_ref[...] = jnp.zeros_like(st_ref)

    @pl.when(lax.rem(j, 2) == 0)
    def _():
        _tile_body(shared, buf_a, buf_b, buf_a, buf_b, **statics)

    @pl.when(lax.rem(j, 2) == 1)
    def _():
        _tile_body(shared, buf_b, buf_a, buf_b, buf_a, **statics)

    @pl.when(v_last)
    def _():
        ssm_out_ref[0] = st_ref[...].T


def _tile_body(refs, bm, bv, bvo, bo, *, nseq, seglen, lc, pipelined, emit_v):
    has_state = not pipelined
    it = iter(refs)
    x_ref, xo_ref = next(it), next(it)
    shift_ref, scale_ref, gate_ref = next(it), next(it), next(it)
    conv0_ref = next(it) if has_state else None
    ssm0_ref = next(it) if has_state else None
    (gpre_ref, gpost_ref, wcat_ref, convw_ref, convb_ref, dtb_ref, alog_ref, dskip_ref, gv_ref,
     betav_ref, mixw_ref, mixb_ref, wout_ref, expand_ref, tril_ref, psel_ref) = (
         next(it) for _ in range(16))
    y_ref, conv_out_ref, ssm_out_ref = next(it), next(it), next(it)
    vn_out_ref = next(it) if emit_v else None
    (h_ref, tail_ref, xp_ref, st_ref, stnew_ref, aexp_ref, xdt_ref, xlo_ref, xhi_ref, acumt_ref,
     vnb_ref, mo_ref, _) = (next(it) for _ in range(N_SHARED_SCRATCH))
    pm, pv = bm.proj, bv.proj

    rows_all = slice(TAIL_ROWS, TAIL_ROWS + TILE_ROWS)
    nchunk = seglen // lc
    pad_rows = HALF - lc

    def pad_to_half(v):
        if pad_rows == 0:
            return [v]
        return [v, jnp.zeros((pad_rows, v.shape[1]), v.dtype)]

    def m_norm():
        for s in range(nseq):
            rows = slice(s * seglen, (s + 1) * seglen)
            xr = x_ref[rows, :]
            gain = gpre_ref[...] * (1.0 + scale_ref[s:s + 1, :])
            hs = (xr * _rms_scale(xr)) * gain + shift_ref[s:s + 1, :]
            h_ref[rows, :] = hs.astype(BF16)

    def m_dot(c0, c1):
        def task():
            pm[rows_all, c0:c1] = _dot(h_ref[...], wcat_ref[:, c0:c1])
        return task

    m_dots = [m_dot(c0, min(c0 + MXU_COLS, W_CAT_COLS)) for c0 in range(0, W_CAT_COLS, MXU_COLS)]
    n_xbc_dots = CONV_DIM // MXU_COLS

    def conv_store(c, val, rows):
        if c < SSD_WIDTH // LANES:
            bm.xs[rows, c * LANES:(c + 1) * LANES] = val
        else:
            c2 = c - SSD_WIDTH // LANES
            bm.bc[rows, c2 * LANES:(c2 + 1) * LANES] = val.astype(BF16)

    def m_conv(c):
        cols = slice(OFF_XBC + c * LANES, OFF_XBC + (c + 1) * LANES)
        wcols = slice(c * LANES, (c + 1) * LANES)

        def task_stream():
            pm[0:TAIL_ROWS, cols] = tail_ref[:, wcols]
            xfull = pm[:, cols]
            acc = convw_ref[0:1, wcols] * xfull
            for k in range(1, CONV_W):
                acc = pltpu.roll(acc, 1, axis=0) + convw_ref[k:k + 1, wcols] * xfull
            acc = acc[TAIL_ROWS:, :] + convb_ref[:, wcols]
            conv_store(c, _silu(acc), slice(0, TILE_ROWS))
            tail_ref[:, wcols] = pm[TILE_ROWS:TILE_ROWS + TAIL_ROWS, cols]
            conv_out_ref[0, :, wcols] = pm[TAIL_ROWS + TILE_ROWS - (CONV_W - 1):TAIL_ROWS + TILE_ROWS, cols]

        def task_state():
            for s in range(nseq):
                rows = slice(s * seglen, (s + 1) * seglen)
                xp_ref[TAP0:TAIL_ROWS, :] = conv0_ref[s, :, wcols]
                xp_ref[TAIL_ROWS:TAIL_ROWS + seglen, :] = pm[TAIL_ROWS + s * seglen:TAIL_ROWS + (s + 1) * seglen, cols]
                acc = convb_ref[:, wcols]
                for k in range(CONV_W):
                    acc = acc + convw_ref[k:k + 1, wcols] * xp_ref[TAP0 + k:TAP0 + k + seglen, :]
                conv_store(c, _silu(acc), rows)
                conv_out_ref[s, :, wcols] = xp_ref[TAIL_ROWS + seglen - (CONV_W - 1):TAIL_ROWS + seglen, :]

        return task_stream if pipelined else task_state

    m_convs = [m_conv(c) for c in range(CONV_DIM // LANES)]

    lane1 = lax.broadcasted_iota(jnp.int32, (TILE_ROWS, LANES), 1)

    def v_dt():
        a_row = -jnp.exp(alog_ref[...])
        dt = _softplus(pv[rows_all, OFF_DT:OFF_DT + LANES] + dtb_ref[...])
        dt = jnp.where(lane1 < HEADS, dt, 0.0)
        csum = _dot(tril_ref[...], _cat3(dt * a_row).astype(BF16))
        a_cum = csum + pltpu.roll(csum, LANES - HEADS, axis=1) + pltpu.roll(csum, LANES - 2 * HEADS, axis=1)
        a_cat = _cat3(jnp.where(lane1 < HEADS, a_cum, 0.0))
        aexp_ref[...] = _dot(a_cat.astype(BF16), expand_ref[...])
        dt_exp = _dot(_cat3(dt).astype(BF16), expand_ref[...])
        even = lax.rem(lane1, 2) == 0
        a_even = jnp.where(even, a_cat, 0.0)
        a_odd = (a_cat - a_even).astype(BF16)
        a_even = a_even.astype(BF16)
        parts = []
        for ck in range(TILE_ROWS // lc):
            rows = slice(ck * lc, (ck + 1) * lc)
            parts += pad_to_half(a_even[rows, :]) + pad_to_half(a_odd[rows, :])
        acumt_ref[...] = _dot_nt(psel_ref[...], jnp.concatenate(parts, axis=0))
        xdt = bv.xs[...] * dt_exp
        lane_w = lax.broadcasted_iota(jnp.int32, (TILE_ROWS, SSD_WIDTH), 1)
        xlo = jnp.where(lax.rem(lane_w, LANES) < HALF, xdt, 0.0)
        xdt_ref[...] = xdt
        xlo_ref[...] = xlo.astype(BF16)
        xhi_ref[...] = (xdt - xlo).astype(BF16)

    li = lax.broadcasted_iota(jnp.int32, (lc, LANES), 0)
    si = lax.rem(lax.broadcasted_iota(jnp.int32, (lc, LANES), 1), HALF)
    causal = (si <= li) & (si < lc)

    def v_ssd(ci, g):
        rows = slice(ci * lc, (ci + 1) * lc)
        prow = slice(TAIL_ROWS + ci * lc, TAIL_ROWS + (ci + 1) * lc)
        hl = slice(g * GROUP_WIDTH, (g + 1) * GROUP_WIDTH)

        def task():
            last = aexp_ref[(ci + 1) * lc - 1:(ci + 1) * lc, hl]
            a_c = aexp_ref[rows, hl]
            e1_c = jnp.exp(a_c)
            xw_b = (xdt_ref[rows, hl] * jnp.exp(last - a_c)).astype(BF16)
            b_g = bv.bc[rows, g * STATE:(g + 1) * STATE]
            c_g = bv.bc[rows, (GROUPS + g) * STATE:(GROUPS + g + 1) * STATE]
            b2 = jnp.concatenate(pad_to_half(b_g) + pad_to_half(b_g), axis=0)
            cb2 = _dot_nt(c_g, b2)
            y_off = _dot(c_g, st_ref[:, hl].astype(BF16))
            stnew_ref[:, hl] = _dot_tn(b_g, xw_b)
            for i in range(PAIRS_PER_GROUP):
                p = g * PAIRS_PER_GROUP + i
                pcols = slice(p * LANES, (p + 1) * LANES)
                lcols = slice(i * LANES, (i + 1) * LANES)
                seg = a_c[:, lcols] - acumt_ref[p:p + 1, ci * LANES:(ci + 1) * LANES]
                w_p = (cb2 * jnp.exp(jnp.where(causal, seg, -jnp.inf))).astype(BF16)
                bd = jnp.concatenate(pad_to_half(xlo_ref[rows, pcols]) + pad_to_half(xhi_ref[rows, pcols]),
                                     axis=0)
                y_p = (_dot(w_p, bd) + e1_c[:, lcols] * y_off[:, lcols]
                       + dskip_ref[:, pcols] * bv.xs[rows, pcols])
                zg = _silu(pv[prow, OFF_Z + p * LANES:OFF_Z + (p + 1) * LANES])
                bvo.mix[rows, pcols] = (y_p * zg).astype(BF16)
        return task

    def v_state(ci, s):
        def task():
            e_last = jnp.exp(aexp_ref[(ci + 1) * lc - 1:(ci + 1) * lc, :])
            st_ref[...] = st_ref[...] * e_last + stnew_ref[...]
            if has_state and ci % nchunk == nchunk - 1:
                ssm_out_ref[s] = st_ref[...].T
        return task

    def v_state_load(s):
        def task():
            st_ref[...] = ssm0_ref[s].T
        return task

    def v_ln():
        v = pv[rows_all, OFF_V:OFF_V + GMLP_WIDTH]
        vc = v - _row_mean(v)
        vn = (vc * _rms_scale(vc)) * gv_ref[...] + betav_ref[...]
        if emit_v:
            vn_out_ref[...] = vn
        vnb_ref[...] = vn.astype(BF16)

    def v_gmlp(g):
        gcols = slice(g * GMLP_GROUP_DIM, (g + 1) * GMLP_GROUP_DIM)

        def task():
            mixed = _dot(mixw_ref[g], vnb_ref[:, gcols]) + mixb_ref[:, gcols]
            u = pv[rows_all, OFF_U + g * GMLP_GROUP_DIM:OFF_U + (g + 1) * GMLP_GROUP_DIM]
            gt = pv[rows_all, OFF_G + g * GMLP_GROUP_DIM:OFF_G + (g + 1) * GMLP_GROUP_DIM]
            bvo.mix[:, SSD_WIDTH + g * GMLP_GROUP_DIM:SSD_WIDTH + (g + 1) * GMLP_GROUP_DIM] = (
                _silu(gt) * u * mixed).astype(BF16)
        return task

    v_scan = []
    for s in range(nseq):
        if has_state:
            v_scan.append(v_state_load(s))
        for c in range(nchunk):
            ci = s * nchunk + c
            v_scan += [v_ssd(ci, 0), v_ssd(ci, 1), v_state(ci, s)]
    v_gmlps = [v_gmlp(g) for g in range(GMLP_GROUPS)]

    def o_out(n):
        def task():
            mo_ref[:, n * MXU_COLS:(n + 1) * MXU_COLS] = _dot(
                bo.mix[...], wout_ref[:, n * MXU_COLS:(n + 1) * MXU_COLS])
        return task

    def o_final():
        mo = mo_ref[...]
        mo = mo * _rms_scale(mo)
        for s in range(nseq):
            rows = slice(s * seglen, (s + 1) * seglen)
            gain = gpost_ref[...] * gate_ref[s:s + 1, :]
            y_ref[rows, :] = xo_ref[rows, :] + gain * mo[rows, :]

    o_outs = [o_out(n) for n in range(D_MODEL // MXU_COLS)]

    if pipelined:
        head = [m_norm] + o_outs + [v_dt] + m_dots[:n_xbc_dots]
        body = m_convs + [o_final, v_ln] + v_scan + v_gmlps
        order = head + _interleave(body, m_dots[n_xbc_dots:])
    else:
        order = ([m_norm] + m_dots + m_convs + [v_dt, v_ln] + v_scan + v_gmlps + o_outs + [o_final])
    for task in order:
        task()


def _const_spec(shape):
    nd = len(shape)
    return pl.BlockSpec(shape, lambda j: (0,) * nd, pipeline_mode=pl.Buffered(1))


def _chunk_constants(lc):
    expand = np.zeros((LANES, SSD_WIDTH), np.float32)
    psel = np.zeros((PAIRS, LANES), np.float32)
    for piece in range(PIECES):
        for hd in range(HEADS):
            expand[piece * HEADS + hd, hd * HEAD_DIM:(hd + 1) * HEAD_DIM] = 1.0
            psel[hd // 2, piece * HEADS + hd] = 1.0
    r = np.arange(TILE_ROWS)
    tril = ((r[:, None] >= r[None, :]) & (r[:, None] // lc == r[None, :] // lc)).astype(np.float32)
    return jnp.asarray(expand, BF16), jnp.asarray(tril, BF16), jnp.asarray(psel, BF16)


def _mix_weights(w_s, b_s, lc):
    nck = TILE_ROWS // lc
    mask = jnp.tril(jnp.ones((lc, lc), dtype=bool))
    w = jnp.where(mask[None], w_s[:, :lc, :lc], 0)
    eye = jnp.eye(nck, dtype=w.dtype)
    wbd = jnp.einsum("ab,gts->gatbs", eye, w).reshape(GMLP_GROUPS, TILE_ROWS, TILE_ROWS).astype(BF16)
    bias = jnp.repeat(b_s[:, :lc].T, GMLP_GROUP_DIM, axis=1)
    return wbd, jnp.tile(bias, (nck, 1))


def _layer(x, shift, scale, gate, conv0, ssm0, wl, *, nseq, seglen, lc_ssd, lc_gmlp, nt, emit_v):
    (g_pre, g_post, w_cat, conv_w, conv_b, dtb, alog, dskip, g_v, beta_v, w_s, b_s, w_out) = wl
    assert lc_ssd == lc_gmlp or nseq == 1
    pipelined = conv0 is None
    nb = shift.shape[0]
    rows_total = x.shape[0]
    ntiles = rows_total // TILE_ROWS
    assert nseq * seglen == TILE_ROWS and nb * nt == ntiles
    expand, tril, psel = _chunk_constants(lc_ssd)
    mixw, mixb = _mix_weights(w_s, b_s, lc_gmlp)
    nck = TILE_ROWS // lc_ssd

    if pipelined:
        def tile_m(j):
            return jnp.minimum(j, ntiles - 1)

        def tile_v(j):
            return jnp.clip(j - 1, 0, ntiles - 1)

        def tile_o(j):
            return jnp.maximum(j - 2, 0)
        steps = ntiles + 2
    else:
        def tile_m(j):
            return j
        tile_v = tile_o = tile_m
        steps = ntiles

    def mod_spec(tile_of):
        return pl.BlockSpec((None, nseq, D_MODEL), lambda j: (tile_of(j) // nt, 0, 0))

    in_specs = [pl.BlockSpec((TILE_ROWS, D_MODEL), lambda j: (tile_m(j), 0)),
                pl.BlockSpec((TILE_ROWS, D_MODEL), lambda j: (tile_o(j), 0)),
                mod_spec(tile_m), mod_spec(tile_m), mod_spec(tile_o)]
    args = [x, x, shift, scale, gate]
    if not pipelined:
        in_specs += [pl.BlockSpec((None, nseq, CONV_W - 1, CONV_DIM), lambda j: (j // nt, 0, 0, 0)),
                     pl.BlockSpec((None, nseq, SSD_WIDTH, STATE), lambda j: (j // nt, 0, 0, 0))]
        args += [conv0, ssm0]
    consts = [g_pre, g_post, w_cat, conv_w, conv_b, dtb, alog, dskip, g_v, beta_v, mixw, mixb,
              w_out, expand, tril, psel]
    in_specs += [_const_spec(a.shape) for a in consts]
    args += consts

    out_shape = [jax.ShapeDtypeStruct((rows_total, D_MODEL), F32),
                 jax.ShapeDtypeStruct((nb, nseq, CONV_W - 1, CONV_DIM), F32),
                 jax.ShapeDtypeStruct((nb, nseq, SSD_WIDTH, STATE), F32)]
    out_specs = [pl.BlockSpec((TILE_ROWS, D_MODEL), lambda j: (tile_o(j), 0)),
                 pl.BlockSpec((None, nseq, CONV_W - 1, CONV_DIM), lambda j: (tile_m(j) // nt, 0, 0, 0)),
                 pl.BlockSpec((None, nseq, SSD_WIDTH, STATE), lambda j: (tile_v(j) // nt, 0, 0, 0))]
    if emit_v:
        out_shape.append(jax.ShapeDtypeStruct((rows_total, GMLP_WIDTH), F32))
        out_specs.append(pl.BlockSpec((TILE_ROWS, GMLP_WIDTH), lambda j: (tile_v(j), 0)))

    spare = (TAIL_ROWS, LANES)
    stage_shapes = [((TAIL_ROWS + TILE_ROWS, W_CAT_COLS), F32),
                    ((TILE_ROWS, SSD_WIDTH), F32),
                    ((TILE_ROWS, 2 * GROUPS * STATE), BF16),
                    ((TILE_ROWS, SSD_WIDTH + GMLP_WIDTH), BF16)]
    scratch = [pltpu.VMEM((TILE_ROWS, D_MODEL), BF16),
               pltpu.VMEM((TAIL_ROWS, CONV_DIM), F32),
               pltpu.VMEM((TAIL_ROWS + seglen, LANES), F32),
               pltpu.VMEM((STATE, SSD_WIDTH), F32),
               pltpu.VMEM((STATE, SSD_WIDTH), F32),
               pltpu.VMEM((TILE_ROWS, SSD_WIDTH), F32),
               pltpu.VMEM((TILE_ROWS, SSD_WIDTH), F32),
               pltpu.VMEM((TILE_ROWS, SSD_WIDTH), BF16),
               pltpu.VMEM((TILE_ROWS, SSD_WIDTH), BF16),
               pltpu.VMEM((PAIRS, nck * LANES), F32),
               pltpu.VMEM((TILE_ROWS, GMLP_WIDTH), BF16),
               pltpu.VMEM((TILE_ROWS, D_MODEL), F32),
               pltpu.VMEM(spare, F32)]
    assert len(scratch) == N_SHARED_SCRATCH
    for shape, dtype in stage_shapes:
        scratch += [pltpu.VMEM(shape, dtype), pltpu.VMEM(shape if pipelined else spare, dtype)]
    kern = functools.partial(_layer_kernel, nseq=nseq, seglen=seglen, lc=lc_ssd, nt=nt,
                             pipelined=pipelined, emit_v=emit_v)
    return pl.pallas_call(
        kern,
        grid=(steps,),
        in_specs=in_specs,
        out_specs=out_specs,
        out_shape=out_shape,
        scratch_shapes=scratch,
        compiler_params=pltpu.CompilerParams(
            dimension_semantics=("arbitrary",),
            vmem_limit_bytes=VMEM_LIMIT_BYTES),
        name="hybrid_layer_stream" if pipelined else "hybrid_layer_state",
    )(*args)


def _prep_weights(g_pre, g_post, w_in, conv_w, conv_b, dt_bias, a_log, d_skip, g_v, beta_v,
                  w_s, b_s, w_out):
    i1 = SSD_WIDTH
    i2 = i1 + CONV_DIM
    i3 = i2 + HEADS
    i4 = i3 + GMLP_WIDTH
    i5 = i4 + GMLP_WIDTH
    wz, wxbc, wdt, wu, wv, wg = (w_in[:, :i1], w_in[:, i1:i2], w_in[:, i2:i3], w_in[:, i3:i4],
                                 w_in[:, i4:i5], w_in[:, i5:])
    wdt_pad = jnp.pad(wdt, ((0, 0), (0, LANES - HEADS)))
    w_cat = jnp.concatenate([wxbc, wz, wu, wv, wg, wdt_pad], axis=1).astype(BF16)
    pad = (0, LANES - HEADS)
    return (g_pre.reshape(1, -1), g_post.reshape(1, -1), w_cat, conv_w, conv_b.reshape(1, -1),
            jnp.pad(dt_bias, pad).reshape(1, -1), jnp.pad(a_log, pad).reshape(1, -1),
            jnp.repeat(d_skip, HEAD_DIM).reshape(1, -1), g_v.reshape(1, -1),
            beta_v.reshape(1, -1), w_s, b_s, w_out.astype(BF16))


def kernel(x_prompt, x_sample, state_conv, state_ssm, c_prompt, c_sample, w_ada, b_ada, g_pre,
           g_post, w_in, conv_w, conv_b, dt_bias, a_log, d_skip, g_v, beta_v, w_s, b_s, w_out):
    depth = w_ada.shape[0]
    bp, lp, _ = x_prompt.shape
    bs, ls, _ = x_sample.shape
    seq_per_tile = TILE_ROWS // ls
    yp = x_prompt.reshape(bp * lp, D_MODEL)
    ys = x_sample.reshape(bs * ls, D_MODEL)
    c_all = jnp.concatenate([c_prompt, c_sample], axis=0)
    c_pad = jnp.pad(c_all, ((0, (-c_all.shape[0]) % 8), (0, 0)))
    outs = [[] for _ in range(5)]
    for l in range(depth):
        wl = _prep_weights(g_pre[l], g_post[l], w_in[l], conv_w[l], conv_b[l], dt_bias[l],
                           a_log[l], d_skip[l], g_v[l], beta_v[l], w_s[l], b_s[l], w_out[l])
        mod = _modulation(c_pad, w_ada[l], b_ada[l])
        shift, scale, gate = (mod[:, i * D_MODEL:(i + 1) * D_MODEL] for i in range(3))
        mp = [m[:bp].reshape(bp, 1, D_MODEL) for m in (shift, scale, gate)]
        ms = [m[bp:bp + bs].reshape(bs // seq_per_tile, seq_per_tile, D_MODEL)
              for m in (shift, scale, gate)]
        yp, conv_p, ssm_p = _layer(
            yp, *mp, None, None, wl, nseq=1, seglen=TILE_ROWS, lc_ssd=min(lp, SSD_CHUNK),
            lc_gmlp=min(lp, GMLP_CHUNK), nt=lp // TILE_ROWS, emit_v=False)
        ys, conv_s, ssm_s, v_s = _layer(
            ys, *ms,
            state_conv[l].reshape(bs // seq_per_tile, seq_per_tile, CONV_W - 1, CONV_DIM),
            state_ssm[l].reshape(bs // seq_per_tile, seq_per_tile, SSD_WIDTH, STATE), wl,
            nseq=seq_per_tile, seglen=ls, lc_ssd=min(ls, SSD_CHUNK), lc_gmlp=min(ls, GMLP_CHUNK),
            nt=1, emit_v=True)
        outs[0].append(conv_p.reshape(bp, CONV_W - 1, CONV_DIM))
        outs[1].append(ssm_p.reshape(bp, HEADS, HEAD_DIM, STATE))
        outs[2].append(conv_s.reshape(bs, CONV_W - 1, CONV_DIM))
        outs[3].append(ssm_s.reshape(bs, HEADS, HEAD_DIM, STATE))
        outs[4].append(v_s.reshape(bs, ls, GMLP_WIDTH))
    return (yp.reshape(bp, lp, D_MODEL), ys.reshape(bs, ls, D_MODEL), jnp.stack(outs[0]),
            jnp.stack(outs[1]), jnp.stack(outs[2]), jnp.stack(outs[3]), jnp.stack(outs[4]))
```

```python
import collections
import functools

import jax
import jax.numpy as jnp
import numpy as np
from jax import lax
from jax.experimental import pallas as pl
from jax.experimental.pallas import tpu as pltpu

F32 = jnp.float32
BF16 = jnp.bfloat16

D_MODEL = 1024
SSD_WIDTH = 1024
HEAD_DIM = 64
HEADS = SSD_WIDTH // HEAD_DIM
GROUPS = 2
GROUP_WIDTH = SSD_WIDTH // GROUPS
STATE = 128
CONV_W = 4
CONV_DIM = SSD_WIDTH + 2 * GROUPS * STATE
GMLP_WIDTH = 1024
GMLP_GROUPS = 8
GMLP_GROUP_DIM = GMLP_WIDTH // GMLP_GROUPS
SSD_CHUNK = 64
GMLP_CHUNK = 128
EPS = 1e-6
LOG2E = 1.4426950408889634

LANES = 128
MXU_COLS = 256
HALF = LANES // 2
PAIRS = HEADS // 2
PAIRS_PER_GROUP = PAIRS // GROUPS
PIECES = 3
TILE_ROWS = 256
TAIL_ROWS = 8
TAP0 = TAIL_ROWS - (CONV_W - 1)
MOD_BLOCK_COLS = 512
VMEM_LIMIT_BYTES = 60 * 1024 * 1024

OFF_XBC = 0
OFF_Z = OFF_XBC + CONV_DIM
OFF_U = OFF_Z + SSD_WIDTH
OFF_V = OFF_U + GMLP_WIDTH
OFF_G = OFF_V + GMLP_WIDTH
OFF_DT = OFF_G + GMLP_WIDTH
W_CAT_COLS = OFF_DT + LANES

Bufs = collections.namedtuple("Bufs", "proj xs bc mix")


def _dot(a, b):
    return jnp.dot(a, b, preferred_element_type=F32)


def _dot_nt(a, b):
    return lax.dot_general(a, b, (((1,), (1,)), ((), ())), preferred_element_type=F32)


def _dot_tn(a, b):
    return lax.dot_general(a, b, (((0,), (0,)), ((), ())), preferred_element_type=F32)


def _cat3(x):
    hi = x.astype(BF16).astype(F32)
    r1 = x - hi
    mid = r1.astype(BF16).astype(F32)
    lo = r1 - mid
    return hi + pltpu.roll(mid, HEADS, axis=1) + pltpu.roll(lo, 2 * HEADS, axis=1)


def _silu(x):
    return x / (1.0 + jnp.exp2(x * (-LOG2E)))


def _softplus(x):
    return jnp.maximum(x, 0.0) + jnp.log1p(jnp.exp(-jnp.abs(x)))


def _row_mean(x):
    width = x.shape[-1]
    part = x[:, 0:LANES]
    for c in range(LANES, width, LANES):
        part = part + x[:, c:c + LANES]
    return jnp.sum(part, axis=-1, keepdims=True) * (1.0 / width)


def _rms_scale(x):
    return lax.rsqrt(_row_mean(x * x) + EPS)


def _mod_kernel(c_ref, w_ref, b_ref, o_ref):
    c = _silu(c_ref[...]).astype(BF16)
    o_ref[...] = _dot(c, w_ref[...].astype(BF16)) + b_ref[...]


def _modulation(c, w_ada, b_ada):
    rows = c.shape[0]
    n = w_ada.shape[1]
    return pl.pallas_call(
        _mod_kernel,
        grid=(n // MOD_BLOCK_COLS,),
        in_specs=[
            pl.BlockSpec((rows, D_MODEL), lambda j: (0, 0)),
            pl.BlockSpec((D_MODEL, MOD_BLOCK_COLS), lambda j: (0, j)),
            pl.BlockSpec((1, MOD_BLOCK_COLS), lambda j: (0, j)),
        ],
        out_specs=pl.BlockSpec((rows, MOD_BLOCK_COLS), lambda j: (0, j)),
        out_shape=jax.ShapeDtypeStruct((rows, n), F32),
        name="adaln_mod",
    )(c, w_ada, b_ada.reshape(1, n))


N_SHARED_SCRATCH = 16
N_STAGE_SCRATCH = 8


def _interleave(primary, secondary):
    total = sum(weight for _, weight in primary)
    groups = []
    done = 0
    seen = 0
    for task, weight in primary:
        seen += weight
        want = (seen * len(secondary)) // total
        groups.append([task] + secondary[done:want])
        done = want
    return groups


def _tag(task, name, needs=()):
    task.name, task.needs = name, tuple(needs)
    return task


def _run_groups(groups):
    stored = set()
    for group in groups:
        for task in group:
            missing = [n for n in getattr(task, "needs", ()) if n not in stored]
            assert not missing, (getattr(task, "name", task), missing)
        commits = [task() for task in group]
        for commit in commits:
            if commit is not None:
                commit()
        stored.update(getattr(task, "name", None) for task in group)


def _layer_kernel(*refs, nseq, seglen, lc, nt, pipelined, emit_v):
    statics = dict(nseq=nseq, seglen=seglen, lc=lc, pipelined=pipelined, emit_v=emit_v)
    shared, stage = refs[:-N_STAGE_SCRATCH], refs[-N_STAGE_SCRATCH:]
    buf_a, buf_b = Bufs(*stage[0::2]), Bufs(*stage[1::2])
    n_scratch = N_SHARED_SCRATCH + N_STAGE_SCRATCH
    ssm_out_ref = refs[len(refs) - n_scratch - (2 if emit_v else 1)]
    tail_ref, st_ref = refs[-n_scratch + 1], refs[-n_scratch + 3]
    if not pipelined:
        _tile_body(shared, buf_a, buf_a, buf_a, buf_a, **statics)
        return

    j = pl.program_id(0)
    m_first = lax.rem(j, nt) == 0
    v_first = (j == 0) | (lax.rem(j - 1, nt) == 0)
    v_last = (j > 0) & (lax.rem(j - 1, nt) == nt - 1)

    @pl.when(j == 0)
    def _():
        for ref in buf_b:
            ref[...] = jnp.zeros_like(ref)

    @pl.when(m_first)
    def _():
        tail_ref[...] = jnp.zeros_like(tail_ref)

    @pl.when(v_first)
    def _():
        st_ref[...] = jnp.zeros_like(st_ref)

    @pl.when(lax.rem(j, 2) == 0)
    def _():
        _tile_body(shared, buf_a, buf_b, buf_a, buf_b, **statics)

    @pl.when(lax.rem(j, 2) == 1)
    def _():
        _tile_body(shared, buf_b, buf_a, buf_b, buf_a, **statics)

    @pl.when(v_last)
    def _():
        ssm_out_ref[0] = st_ref[...].T


def _tile_body(refs, bm, bv, bvo, bo, *, nseq, seglen, lc, pipelined, emit_v):
    has_state = not pipelined
    it = iter(refs)
    x_ref, xo_ref = next(it), next(it)
    shift_ref, scale_ref, gate_ref = next(it), next(it), next(it)
    conv0_ref = next(it) if has_state else None
    ssm0_ref = next(it) if has_state else None
    (gpre_ref, gpost_ref, wcat_ref, convw_ref, convb_ref, dtb_ref, alog_ref, dskip_ref, gv_ref,
     betav_ref, mixw_ref, mixb_ref, wout_ref, expand_ref, tril_ref, psel_ref) = (
         next(it) for _ in range(16))
    y_ref, conv_out_ref, ssm_out_ref = next(it), next(it), next(it)
    vn_out_ref = next(it) if emit_v else None
    (h_ref, tail_ref, xp_ref, st_ref, stnew_ref, aexp_ref, xdt_ref, xlo_ref, xhi_ref, acumt_ref,
     vnb_ref, mo_ref, stb_ref, ys_ref, cb_ref, yoff_ref) = (next(it) for _ in range(N_SHARED_SCRATCH))
    pm, pv = bm.proj, bv.proj

    rows_all = slice(TAIL_ROWS, TAIL_ROWS + TILE_ROWS)
    nchunk = seglen // lc
    pad_rows = HALF - lc

    def pad_to_half(v):
        if pad_rows == 0:
            return [v]
        return [v, jnp.zeros((pad_rows, v.shape[1]), v.dtype)]

    def m_norm():
        for s in range(nseq):
            rows = slice(s * seglen, (s + 1) * seglen)
            xr = x_ref[rows, :]
            gain = gpre_ref[...] * (1.0 + scale_ref[s:s + 1, :])
            hs = (xr * _rms_scale(xr)) * gain + shift_ref[s:s + 1, :]
            h_ref[rows, :] = hs.astype(BF16)

    def m_dot(c0, c1):
        def task():
            val = _dot(h_ref[...], wcat_ref[:, c0:c1])

            def commit():
                pm[rows_all, c0:c1] = val
            return commit
        return task

    m_dots = [_tag(m_dot(c0, min(c0 + MXU_COLS, W_CAT_COLS)), f"m_dot{c0 // MXU_COLS}")
              for c0 in range(0, W_CAT_COLS, MXU_COLS)]

    def conv_store(c, val, rows):
        if c < SSD_WIDTH // LANES:
            bm.xs[rows, c * LANES:(c + 1) * LANES] = val
        else:
            c2 = c - SSD_WIDTH // LANES
            bm.bc[rows, c2 * LANES:(c2 + 1) * LANES] = val.astype(BF16)

    def m_conv(c):
        cols = slice(OFF_XBC + c * LANES, OFF_XBC + (c + 1) * LANES)
        wcols = slice(c * LANES, (c + 1) * LANES)

        def task_stream():
            pm[0:TAIL_ROWS, cols] = tail_ref[:, wcols]
            xfull = pm[:, cols]
            acc = convw_ref[0:1, wcols] * xfull
            for k in range(1, CONV_W):
                acc = pltpu.roll(acc, 1, axis=0) + convw_ref[k:k + 1, wcols] * xfull
            val = _silu(acc[TAIL_ROWS:, :] + convb_ref[:, wcols])
            tail_ref[:, wcols] = pm[TILE_ROWS:TILE_ROWS + TAIL_ROWS, cols]
            conv_out_ref[0, :, wcols] = pm[TAIL_ROWS + TILE_ROWS - (CONV_W - 1):TAIL_ROWS + TILE_ROWS, cols]

            def commit():
                conv_store(c, val, slice(0, TILE_ROWS))
            return commit

        def task_state():
            for s in range(nseq):
                rows = slice(s * seglen, (s + 1) * seglen)
                xp_ref[TAP0:TAIL_ROWS, :] = conv0_ref[s, :, wcols]
                xp_ref[TAIL_ROWS:TAIL_ROWS + seglen, :] = pm[TAIL_ROWS + s * seglen:TAIL_ROWS + (s + 1) * seglen, cols]
                acc = convb_ref[:, wcols]
                for k in range(CONV_W):
                    acc = acc + convw_ref[k:k + 1, wcols] * xp_ref[TAP0 + k:TAP0 + k + seglen, :]
                conv_store(c, _silu(acc), rows)
                conv_out_ref[s, :, wcols] = xp_ref[TAIL_ROWS + seglen - (CONV_W - 1):TAIL_ROWS + seglen, :]

        return task_stream if pipelined else task_state

    m_convs = [_tag(m_conv(c), f"m_conv{c}", [f"m_dot{(c * LANES) // MXU_COLS}"])
               for c in range(CONV_DIM // LANES)]

    lane1 = lax.broadcasted_iota(jnp.int32, (TILE_ROWS, LANES), 1)

    def v_dt():
        a_row = -jnp.exp(alog_ref[...])
        dt = _softplus(pv[rows_all, OFF_DT:OFF_DT + LANES] + dtb_ref[...])
        dt = jnp.where(lane1 < HEADS, dt, 0.0)
        csum = _dot(tril_ref[...], _cat3(dt * a_row).astype(BF16))
        a_cum = csum + pltpu.roll(csum, LANES - HEADS, axis=1) + pltpu.roll(csum, LANES - 2 * HEADS, axis=1)
        a_cat = _cat3(jnp.where(lane1 < HEADS, a_cum, 0.0))
        aexp_ref[...] = _dot(a_cat.astype(BF16), expand_ref[...])
        dt_exp = _dot(_cat3(dt).astype(BF16), expand_ref[...])
        even = lax.rem(lane1, 2) == 0
        a_even = jnp.where(even, a_cat, 0.0)
        a_odd = (a_cat - a_even).astype(BF16)
        a_even = a_even.astype(BF16)
        parts = []
        for ck in range(TILE_ROWS // lc):
            rows = slice(ck * lc, (ck + 1) * lc)
            parts += pad_to_half(a_even[rows, :]) + pad_to_half(a_odd[rows, :])
        acumt_ref[...] = _dot_nt(psel_ref[...], jnp.concatenate(parts, axis=0))
        xdt = bv.xs[...] * dt_exp
        lane_w = lax.broadcasted_iota(jnp.int32, (TILE_ROWS, SSD_WIDTH), 1)
        xlo = jnp.where(lax.rem(lane_w, LANES) < HALF, xdt, 0.0)
        xdt_ref[...] = xdt
        xlo_ref[...] = xlo.astype(BF16)
        xhi_ref[...] = (xdt - xlo).astype(BF16)

    li = lax.broadcasted_iota(jnp.int32, (lc, LANES), 0)
    si = lax.rem(lax.broadcasted_iota(jnp.int32, (lc, LANES), 1), HALF)
    causal = (si <= li) & (si < lc)

    def v_local(ci, g):
        rows = slice(ci * lc, (ci + 1) * lc)
        hl = slice(g * GROUP_WIDTH, (g + 1) * GROUP_WIDTH)

        def products():
            last = aexp_ref[(ci + 1) * lc - 1:(ci + 1) * lc, hl]
            xw_b = (xdt_ref[rows, hl] * jnp.exp(last - aexp_ref[rows, hl])).astype(BF16)
            b_g = bv.bc[rows, g * STATE:(g + 1) * STATE]
            c_g = bv.bc[rows, (GROUPS + g) * STATE:(GROUPS + g + 1) * STATE]
            b2 = jnp.concatenate(pad_to_half(b_g) + pad_to_half(b_g), axis=0)
            cb_ref[ci * GROUPS + g] = _dot_nt(c_g, b2)
            stnew_ref[ci, :, hl] = _dot_tn(b_g, xw_b)

        def diag():
            cb2 = cb_ref[ci * GROUPS + g]
            for i in range(PAIRS_PER_GROUP):
                p = g * PAIRS_PER_GROUP + i
                pcols = slice(p * LANES, (p + 1) * LANES)
                seg = aexp_ref[rows, pcols] - acumt_ref[p:p + 1, ci * LANES:(ci + 1) * LANES]
                w_p = (cb2 * jnp.exp(jnp.where(causal, seg, -jnp.inf))).astype(BF16)
                bd = jnp.concatenate(pad_to_half(xlo_ref[rows, pcols]) + pad_to_half(xhi_ref[rows, pcols]),
                                     axis=0)
                ys_ref[rows, pcols] = _dot(w_p, bd) + dskip_ref[:, pcols] * bv.xs[rows, pcols]
        return products, diag

    def v_state(ci, s):
        def task():
            e_last = jnp.exp(aexp_ref[(ci + 1) * lc - 1:(ci + 1) * lc, :])
            st = st_ref[...]
            stb_ref[ci] = st.astype(BF16)
            st_ref[...] = st * e_last + stnew_ref[ci]
            if has_state and ci % nchunk == nchunk - 1:
                ssm_out_ref[s] = st_ref[...].T
        return task

    def v_readout(ci, g):
        rows = slice(ci * lc, (ci + 1) * lc)
        prow = slice(TAIL_ROWS + ci * lc, TAIL_ROWS + (ci + 1) * lc)
        hl = slice(g * GROUP_WIDTH, (g + 1) * GROUP_WIDTH)

        def product():
            c_g = bv.bc[rows, (GROUPS + g) * STATE:(GROUPS + g + 1) * STATE]
            yoff_ref[rows, hl] = _dot(c_g, stb_ref[ci, :, hl])

        def gate():
            y = ys_ref[rows, hl] + jnp.exp(aexp_ref[rows, hl]) * yoff_ref[rows, hl]
            zg = _silu(pv[prow, OFF_Z + g * GROUP_WIDTH:OFF_Z + (g + 1) * GROUP_WIDTH])
            bvo.mix[rows, hl] = (y * zg).astype(BF16)
        return product, gate

    def v_state_load(s):
        def task():
            st_ref[...] = ssm0_ref[s].T
        return task

    def v_ln():
        v = pv[rows_all, OFF_V:OFF_V + GMLP_WIDTH]
        vc = v - _row_mean(v)
        vn = (vc * _rms_scale(vc)) * gv_ref[...] + betav_ref[...]
        if emit_v:
            vn_out_ref[...] = vn
        vnb_ref[...] = vn.astype(BF16)

    def v_gmlp(g):
        gcols = slice(g * GMLP_GROUP_DIM, (g + 1) * GMLP_GROUP_DIM)

        def task():
            mixed = _dot(mixw_ref[g], vnb_ref[:, gcols]) + mixb_ref[:, gcols]
            u = pv[rows_all, OFF_U + g * GMLP_GROUP_DIM:OFF_U + (g + 1) * GMLP_GROUP_DIM]
            gt = pv[rows_all, OFF_G + g * GMLP_GROUP_DIM:OFF_G + (g + 1) * GMLP_GROUP_DIM]
            bvo.mix[:, SSD_WIDTH + g * GMLP_GROUP_DIM:SSD_WIDTH + (g + 1) * GMLP_GROUP_DIM] = (
                _silu(gt) * u * mixed).astype(BF16)
        return task

    chunk_groups = [(ci, g) for ci in range(nseq * nchunk) for g in range(GROUPS)]
    v_locals = [v_local(ci, g) for ci, g in chunk_groups]
    v_states = []
    for s in range(nseq):
        if has_state:
            v_states.append(v_state_load(s))
        v_states += [v_state(s * nchunk + c, s) for c in range(nchunk)]
    v_readouts = [v_readout(ci, g) for ci, g in chunk_groups]
    v_scan = ([(t[0], 0) for t in v_locals] + [(t[1], 1) for t in v_locals]
              + [(t, 1) for t in v_states]
              + [(t[0], 0) for t in v_readouts] + [(t[1], 1) for t in v_readouts])
    v_gmlps = [v_gmlp(g) for g in range(GMLP_GROUPS)]

    def o_out(n):
        def task():
            mo_ref[:, n * MXU_COLS:(n + 1) * MXU_COLS] = _dot(
                bo.mix[...], wout_ref[:, n * MXU_COLS:(n + 1) * MXU_COLS])
        return task

    def o_final():
        mo = mo_ref[...]
        mo = mo * _rms_scale(mo)
        for s in range(nseq):
            rows = slice(s * seglen, (s + 1) * seglen)
            gain = gpost_ref[...] * gate_ref[s:s + 1, :]
            y_ref[rows, :] = xo_ref[rows, :] + gain * mo[rows, :]

    o_outs = [_tag(o_out(n), f"o_out{n}") for n in range(D_MODEL // MXU_COLS)]
    _tag(o_final, "o_final", [t.name for t in o_outs])

    if pipelined:
        n_xbc_dots = CONV_DIM // MXU_COLS
        head = _interleave([(m_norm, 1), (v_dt, 2), (v_ln, 2)], o_outs + m_dots[:n_xbc_dots])
        body = [(t, 3) for t in m_convs] + [(o_final, 3)] + v_scan + [(t, 0) for t in v_gmlps]
        groups = head + _interleave(body, m_dots[n_xbc_dots:])
    else:
        order = ([m_norm] + m_dots + m_convs + [v_dt, v_ln] + [t for t, _ in v_scan] + v_gmlps
                 + o_outs + [o_final])
        groups = [[task] for task in order]
    _run_groups(groups)


def _const_spec(shape):
    nd = len(shape)
    return pl.BlockSpec(shape, lambda j: (0,) * nd, pipeline_mode=pl.Buffered(1))


def _chunk_constants(lc):
    expand = np.zeros((LANES, SSD_WIDTH), np.float32)
    psel = np.zeros((PAIRS, LANES), np.float32)
    for piece in range(PIECES):
        for hd in range(HEADS):
            expand[piece * HEADS + hd, hd * HEAD_DIM:(hd + 1) * HEAD_DIM] = 1.0
            psel[hd // 2, piece * HEADS + hd] = 1.0
    r = np.arange(TILE_ROWS)
    tril = ((r[:, None] >= r[None, :]) & (r[:, None] // lc == r[None, :] // lc)).astype(np.float32)
    return jnp.asarray(expand, BF16), jnp.asarray(tril, BF16), jnp.asarray(psel, BF16)


def _mix_weights(w_s, b_s, lc):
    nck = TILE_ROWS // lc
    mask = jnp.tril(jnp.ones((lc, lc), dtype=bool))
    w = jnp.where(mask[None], w_s[:, :lc, :lc], 0)
    eye = jnp.eye(nck, dtype=w.dtype)
    wbd = jnp.einsum("ab,gts->gatbs", eye, w).reshape(GMLP_GROUPS, TILE_ROWS, TILE_ROWS).astype(BF16)
    bias = jnp.repeat(b_s[:, :lc].T, GMLP_GROUP_DIM, axis=1)
    return wbd, jnp.tile(bias, (nck, 1))


def _layer(x, shift, scale, gate, conv0, ssm0, wl, *, nseq, seglen, lc_ssd, lc_gmlp, nt, emit_v):
    (g_pre, g_post, w_cat, conv_w, conv_b, dtb, alog, dskip, g_v, beta_v, w_s, b_s, w_out) = wl
    assert lc_ssd == lc_gmlp or nseq == 1
    pipelined = conv0 is None
    nb = shift.shape[0]
    rows_total = x.shape[0]
    ntiles = rows_total // TILE_ROWS
    assert nseq * seglen == TILE_ROWS and nb * nt == ntiles
    expand, tril, psel = _chunk_constants(lc_ssd)
    mixw, mixb = _mix_weights(w_s, b_s, lc_gmlp)
    nck = TILE_ROWS // lc_ssd

    if pipelined:
        def tile_m(j):
            return jnp.minimum(j, ntiles - 1)

        def tile_v(j):
            return jnp.clip(j - 1, 0, ntiles - 1)

        def tile_o(j):
            return jnp.maximum(j - 2, 0)
        steps = ntiles + 2
    else:
        def tile_m(j):
            return j
        tile_v = tile_o = tile_m
        steps = ntiles

    def mod_spec(tile_of):
        return pl.BlockSpec((None, nseq, D_MODEL), lambda j: (tile_of(j) // nt, 0, 0))

    in_specs = [pl.BlockSpec((TILE_ROWS, D_MODEL), lambda j: (tile_m(j), 0)),
                pl.BlockSpec((TILE_ROWS, D_MODEL), lambda j: (tile_o(j), 0)),
                mod_spec(tile_m), mod_spec(tile_m), mod_spec(tile_o)]
    args = [x, x, shift, scale, gate]
    if not pipelined:
        in_specs += [pl.BlockSpec((None, nseq, CONV_W - 1, CONV_DIM), lambda j: (j // nt, 0, 0, 0)),
                     pl.BlockSpec((None, nseq, SSD_WIDTH, STATE), lambda j: (j // nt, 0, 0, 0))]
        args += [conv0, ssm0]
    consts = [g_pre, g_post, w_cat, conv_w, conv_b, dtb, alog, dskip, g_v, beta_v, mixw, mixb,
              w_out, expand, tril, psel]
    in_specs += [_const_spec(a.shape) for a in consts]
    args += consts

    out_shape = [jax.ShapeDtypeStruct((rows_total, D_MODEL), F32),
                 jax.ShapeDtypeStruct((nb, nseq, CONV_W - 1, CONV_DIM), F32),
                 jax.ShapeDtypeStruct((nb, nseq, SSD_WIDTH, STATE), F32)]
    out_specs = [pl.BlockSpec((TILE_ROWS, D_MODEL), lambda j: (tile_o(j), 0)),
                 pl.BlockSpec((None, nseq, CONV_W - 1, CONV_DIM), lambda j: (tile_m(j) // nt, 0, 0, 0)),
                 pl.BlockSpec((None, nseq, SSD_WIDTH, STATE), lambda j: (tile_v(j) // nt, 0, 0, 0))]
    if emit_v:
        out_shape.append(jax.ShapeDtypeStruct((rows_total, GMLP_WIDTH), F32))
        out_specs.append(pl.BlockSpec((TILE_ROWS, GMLP_WIDTH), lambda j: (tile_v(j), 0)))

    spare = (TAIL_ROWS, LANES)
    stage_shapes = [((TAIL_ROWS + TILE_ROWS, W_CAT_COLS), F32),
                    ((TILE_ROWS, SSD_WIDTH), F32),
                    ((TILE_ROWS, 2 * GROUPS * STATE), BF16),
                    ((TILE_ROWS, SSD_WIDTH + GMLP_WIDTH), BF16)]
    scratch = [pltpu.VMEM((TILE_ROWS, D_MODEL), BF16),
               pltpu.VMEM((TAIL_ROWS, CONV_DIM), F32),
               pltpu.VMEM((TAIL_ROWS + seglen, LANES), F32),
               pltpu.VMEM((STATE, SSD_WIDTH), F32),
               pltpu.VMEM((nck, STATE, SSD_WIDTH), F32),
               pltpu.VMEM((TILE_ROWS, SSD_WIDTH), F32),
               pltpu.VMEM((TILE_ROWS, SSD_WIDTH), F32),
               pltpu.VMEM((TILE_ROWS, SSD_WIDTH), BF16),
               pltpu.VMEM((TILE_ROWS, SSD_WIDTH), BF16),
               pltpu.VMEM((PAIRS, nck * LANES), F32),
               pltpu.VMEM((TILE_ROWS, GMLP_WIDTH), BF16),
               pltpu.VMEM((TILE_ROWS, D_MODEL), F32),
               pltpu.VMEM((nck, STATE, SSD_WIDTH), BF16),
               pltpu.VMEM((TILE_ROWS, SSD_WIDTH), F32),
               pltpu.VMEM((nck * GROUPS, lc_ssd, LANES), F32),
               pltpu.VMEM((TILE_ROWS, SSD_WIDTH), F32)]
    assert len(scratch) == N_SHARED_SCRATCH
    for shape, dtype in stage_shapes:
        scratch += [pltpu.VMEM(shape, dtype), pltpu.VMEM(shape if pipelined else spare, dtype)]
    kern = functools.partial(_layer_kernel, nseq=nseq, seglen=seglen, lc=lc_ssd, nt=nt,
                             pipelined=pipelined, emit_v=emit_v)
    return pl.pallas_call(
        kern,
        grid=(steps,),
        in_specs=in_specs,
        out_specs=out_specs,
        out_shape=out_shape,
        scratch_shapes=scratch,
        compiler_params=pltpu.CompilerParams(
            dimension_semantics=("arbitrary",),
            vmem_limit_bytes=VMEM_LIMIT_BYTES),
        name="hybrid_layer_stream" if pipelined else "hybrid_layer_state",
    )(*args)


def _prep_weights(g_pre, g_post, w_in, conv_w, conv_b, dt_bias, a_log, d_skip, g_v, beta_v,
                  w_s, b_s, w_out):
    i1 = SSD_WIDTH
    i2 = i1 + CONV_DIM
    i3 = i2 + HEADS
    i4 = i3 + GMLP_WIDTH
    i5 = i4 + GMLP_WIDTH
    wz, wxbc, wdt, wu, wv, wg = (w_in[:, :i1], w_in[:, i1:i2], w_in[:, i2:i3], w_in[:, i3:i4],
                                 w_in[:, i4:i5], w_in[:, i5:])
    wdt_pad = jnp.pad(wdt, ((0, 0), (0, LANES - HEADS)))
    w_cat = jnp.concatenate([wxbc, wz, wu, wv, wg, wdt_pad], axis=1).astype(BF16)
    pad = (0, LANES - HEADS)
    return (g_pre.reshape(1, -1), g_post.reshape(1, -1), w_cat, conv_w, conv_b.reshape(1, -1),
            jnp.pad(dt_bias, pad).reshape(1, -1), jnp.pad(a_log, pad).reshape(1, -1),
            jnp.repeat(d_skip, HEAD_DIM).reshape(1, -1), g_v.reshape(1, -1),
            beta_v.reshape(1, -1), w_s, b_s, w_out.astype(BF16))


def kernel(x_prompt, x_sample, state_conv, state_ssm, c_prompt, c_sample, w_ada, b_ada, g_pre,
           g_post, w_in, conv_w, conv_b, dt_bias, a_log, d_skip, g_v, beta_v, w_s, b_s, w_out):
    depth = w_ada.shape[0]
    bp, lp, _ = x_prompt.shape
    bs, ls, _ = x_sample.shape
    seq_per_tile = TILE_ROWS // ls
    yp = x_prompt.reshape(bp * lp, D_MODEL)
    ys = x_sample.reshape(bs * ls, D_MODEL)
    c_all = jnp.concatenate([c_prompt, c_sample], axis=0)
    c_pad = jnp.pad(c_all, ((0, (-c_all.shape[0]) % 8), (0, 0)))
    outs = [[] for _ in range(5)]
    for l in range(depth):
        wl = _prep_weights(g_pre[l], g_post[l], w_in[l], conv_w[l], conv_b[l], dt_bias[l],
                           a_log[l], d_skip[l], g_v[l], beta_v[l], w_s[l], b_s[l], w_out[l])
        mod = _modulation(c_pad, w_ada[l], b_ada[l])
        shift, scale, gate = (mod[:, i * D_MODEL:(i + 1) * D_MODEL] for i in range(3))
        mp = [m[:bp].reshape(bp, 1, D_MODEL) for m in (shift, scale, gate)]
        ms = [m[bp:bp + bs].reshape(bs // seq_per_tile, seq_per_tile, D_MODEL)
              for m in (shift, scale, gate)]
        yp, conv_p, ssm_p = _layer(
            yp, *mp, None, None, wl, nseq=1, seglen=TILE_ROWS, lc_ssd=min(lp, SSD_CHUNK),
            lc_gmlp=min(lp, GMLP_CHUNK), nt=lp // TILE_ROWS, emit_v=False)
        ys, conv_s, ssm_s, v_s = _layer(
            ys, *ms,
            state_conv[l].reshape(bs // seq_per_tile, seq_per_tile, CONV_W - 1, CONV_DIM),
            state_ssm[l].reshape(bs // seq_per_tile, seq_per_tile, SSD_WIDTH, STATE), wl,
            nseq=seq_per_tile, seglen=ls, lc_ssd=min(ls, SSD_CHUNK), lc_gmlp=min(ls, GMLP_CHUNK),
            nt=1, emit_v=True)
        outs[0].append(conv_p.reshape(bp, CONV_W - 1, CONV_DIM))
        outs[1].append(ssm_p.reshape(bp, HEADS, HEAD_DIM, STATE))
        outs[2].append(conv_s.reshape(bs, CONV_W - 1, CONV_DIM))
        outs[3].append(ssm_s.reshape(bs, HEADS, HEAD_DIM, STATE))
        outs[4].append(v_s.reshape(bs, ls, GMLP_WIDTH))
    return (yp.reshape(bp, lp, D_MODEL), ys.reshape(bs, ls, D_MODEL), jnp.stack(outs[0]),
            jnp.stack(outs[1]), jnp.stack(outs[2]), jnp.stack(outs[3]), jnp.stack(outs[4]))
```

```python
import collections
import functools

import jax
import jax.numpy as jnp
import numpy as np
from jax import lax
from jax.experimental import pallas as pl
from jax.experimental.pallas import tpu as pltpu

F32 = jnp.float32
BF16 = jnp.bfloat16

D_MODEL = 1024
SSD_WIDTH = 1024
HEAD_DIM = 64
HEADS = SSD_WIDTH // HEAD_DIM
GROUPS = 2
GROUP_WIDTH = SSD_WIDTH // GROUPS
STATE = 128
CONV_W = 4
CONV_DIM = SSD_WIDTH + 2 * GROUPS * STATE
GMLP_WIDTH = 1024
GMLP_GROUPS = 8
GMLP_GROUP_DIM = GMLP_WIDTH // GMLP_GROUPS
SSD_CHUNK = 64
GMLP_CHUNK = 128
EPS = 1e-6
LOG2E = 1.4426950408889634

LANES = 128
MXU_COLS = 256
HALF = LANES // 2
PAIRS = HEADS // 2
PAIRS_PER_GROUP = PAIRS // GROUPS
PIECES = 3
TILE_ROWS = 256
TAIL_ROWS = 8
TAP0 = TAIL_ROWS - (CONV_W - 1)
MOD_BLOCK_COLS = 512
VMEM_LIMIT_BYTES = 60 * 1024 * 1024

OFF_Z = 0
OFF_XBC = OFF_Z + SSD_WIDTH
OFF_U = OFF_XBC + CONV_DIM
OFF_V = OFF_U + GMLP_WIDTH
OFF_G = OFF_V + GMLP_WIDTH
OFF_DT = OFF_G + GMLP_WIDTH
W_CAT_COLS = OFF_DT + LANES

Bufs = collections.namedtuple("Bufs", "proj xs bc mix")


def _dot(a, b):
    return jnp.dot(a, b, preferred_element_type=F32)


def _dot_nt(a, b):
    return lax.dot_general(a, b, (((1,), (1,)), ((), ())), preferred_element_type=F32)


def _dot_tn(a, b):
    return lax.dot_general(a, b, (((0,), (0,)), ((), ())), preferred_element_type=F32)


def _cat3(x):
    hi = x.astype(BF16).astype(F32)
    r1 = x - hi
    mid = r1.astype(BF16).astype(F32)
    lo = r1 - mid
    return hi + pltpu.roll(mid, HEADS, axis=1) + pltpu.roll(lo, 2 * HEADS, axis=1)


def _silu(x):
    return x / (1.0 + jnp.exp2(x * (-LOG2E)))


def _softplus(x):
    return jnp.maximum(x, 0.0) + jnp.log1p(jnp.exp(-jnp.abs(x)))


def _row_mean(x):
    width = x.shape[-1]
    part = x[:, 0:LANES]
    for c in range(LANES, width, LANES):
        part = part + x[:, c:c + LANES]
    return jnp.sum(part, axis=-1, keepdims=True) * (1.0 / width)


def _rms_scale(x):
    return lax.rsqrt(_row_mean(x * x) + EPS)


def _mod_kernel(c_ref, w_ref, b_ref, o_ref):
    c = _silu(c_ref[...]).astype(BF16)
    o_ref[...] = _dot(c, w_ref[...].astype(BF16)) + b_ref[...]


def _modulation(c, w_ada, b_ada):
    rows = c.shape[0]
    n = w_ada.shape[1]
    return pl.pallas_call(
        _mod_kernel,
        grid=(n // MOD_BLOCK_COLS,),
        in_specs=[
            pl.BlockSpec((rows, D_MODEL), lambda j: (0, 0)),
            pl.BlockSpec((D_MODEL, MOD_BLOCK_COLS), lambda j: (0, j)),
            pl.BlockSpec((1, MOD_BLOCK_COLS), lambda j: (0, j)),
        ],
        out_specs=pl.BlockSpec((rows, MOD_BLOCK_COLS), lambda j: (0, j)),
        out_shape=jax.ShapeDtypeStruct((rows, n), F32),
        name="adaln_mod",
    )(c, w_ada, b_ada.reshape(1, n))


N_SHARED_SCRATCH = 16
N_STAGE_SCRATCH = 8


def _interleave(primary, secondary):
    total = sum(weight for _, weight in primary)
    groups = []
    done = 0
    seen = 0
    for task, weight in primary:
        seen += weight
        want = (seen * len(secondary)) // total
        groups.append([task] + secondary[done:want])
        done = want
    return groups


def _tag(task, name, needs=()):
    task.name, task.needs = name, tuple(needs)
    return task


def _run_groups(groups):
    stored = set()
    for group in groups:
        for task in group:
            missing = [n for n in getattr(task, "needs", ()) if n not in stored]
            assert not missing, (getattr(task, "name", task), missing)
        commits = [task() for task in group]
        for commit in commits:
            if commit is not None:
                commit()
        stored.update(getattr(task, "name", None) for task in group)


def _layer_kernel(*refs, nseq, seglen, lc, nt, pipelined, emit_v):
    statics = dict(nseq=nseq, seglen=seglen, lc=lc, pipelined=pipelined, emit_v=emit_v)
    shared, stage = refs[:-N_STAGE_SCRATCH], refs[-N_STAGE_SCRATCH:]
    buf_a, buf_b = Bufs(*stage[0::2]), Bufs(*stage[1::2])
    n_scratch = N_SHARED_SCRATCH + N_STAGE_SCRATCH
    ssm_out_ref = refs[len(refs) - n_scratch - (2 if emit_v else 1)]
    tail_ref, st_ref = refs[-n_scratch + 1], refs[-n_scratch + 3]
    if not pipelined:
        _tile_body(shared, buf_a, buf_a, buf_a, buf_a, **statics)
        return

    j = pl.program_id(0)
    m_first = lax.rem(j, nt) == 0
    v_first = (j == 0) | (lax.rem(j - 1, nt) == 0)
    v_last = (j > 0) & (lax.rem(j - 1, nt) == nt - 1)

    @pl.when(j == 0)
    def _():
        for ref in buf_b:
            ref[...] = jnp.zeros_like(ref)

    @pl.when(m_first)
    def _():
        tail_ref[...] = jnp.zeros_like(tail_ref)

    @pl.when(v_first)
    def _():
        st_ref[...] = jnp.zeros_like(st_ref)

    @pl.when(lax.rem(j, 2) == 0)
    def _():
        _tile_body(shared, buf_a, buf_b, buf_a, buf_b, **statics)

    @pl.when(lax.rem(j, 2) == 1)
    def _():
        _tile_body(shared, buf_b, buf_a, buf_b, buf_a, **statics)

    @pl.when(v_last)
    def _():
        ssm_out_ref[0] = st_ref[...].T.reshape(HEADS, HEAD_DIM, STATE)


def _tile_body(refs, bm, bv, bvo, bo, *, nseq, seglen, lc, pipelined, emit_v):
    has_state = not pipelined
    it = iter(refs)
    x_ref, xo_ref = next(it), next(it)
    shift_ref, scale_ref, gate_ref = next(it), next(it), next(it)
    conv0_ref = next(it) if has_state else None
    ssm0_ref = next(it) if has_state else None
    (gpre_ref, gpost_ref, wa_ref, wb_ref, wdt_ref, convw_ref, convb_ref, dtb_ref, alog_ref,
     dskip_ref, gv_ref, betav_ref, mixw_ref, mixb_ref, wout_ref, expand_ref, tril_ref, psel_ref) = (
         next(it) for _ in range(18))
    y_ref, conv_out_ref, ssm_out_ref = next(it), next(it), next(it)
    vn_out_ref = next(it) if emit_v else None
    (h_ref, tail_ref, xp_ref, st_ref, stnew_ref, aexp_ref, xdt_ref, xlo_ref, xhi_ref, acumt_ref,
     vnb_ref, mo_ref, stb_ref, ys_ref, cb_ref, yoff_ref) = (next(it) for _ in range(N_SHARED_SCRATCH))
    pm, pv = bm.proj, bv.proj

    rows_all = slice(TAIL_ROWS, TAIL_ROWS + TILE_ROWS)
    nchunk = seglen // lc
    pad_rows = HALF - lc

    def pad_to_half(v):
        if pad_rows == 0:
            return [v]
        return [v, jnp.zeros((pad_rows, v.shape[1]), v.dtype)]

    def m_norm():
        for s in range(nseq):
            rows = slice(s * seglen, (s + 1) * seglen)
            xr = x_ref[rows, :]
            gain = gpre_ref[...] * (1.0 + scale_ref[s:s + 1, :])
            hs = (xr * _rms_scale(xr)) * gain + shift_ref[s:s + 1, :]
            h_ref[rows, :] = hs.astype(BF16)

    def m_dot(w_ref, w0, dst0, width):
        def task():
            val = _dot(h_ref[...], w_ref[:, w0:w0 + width])

            def commit():
                pm[rows_all, dst0:dst0 + width] = val
            return commit
        return _tag(task, f"m_dot{dst0 // MXU_COLS}")

    dst_starts = (list(range(OFF_XBC, OFF_U, MXU_COLS)) + list(range(OFF_Z, OFF_XBC, MXU_COLS))
                  + list(range(OFF_U, OFF_DT, MXU_COLS)))
    m_dots = [m_dot(wa_ref, d, d, MXU_COLS) if d < OFF_U else m_dot(wb_ref, d - OFF_U, d, MXU_COLS)
              for d in dst_starts] + [m_dot(wdt_ref, 0, OFF_DT, LANES)]

    def conv_store(c, val, rows):
        if c < SSD_WIDTH // LANES:
            bm.xs[rows, c * LANES:(c + 1) * LANES] = val
        else:
            c2 = c - SSD_WIDTH // LANES
            bm.bc[rows, c2 * LANES:(c2 + 1) * LANES] = val.astype(BF16)

    def m_conv(c):
        cols = slice(OFF_XBC + c * LANES, OFF_XBC + (c + 1) * LANES)
        wcols = slice(c * LANES, (c + 1) * LANES)

        def task_stream():
            pm[0:TAIL_ROWS, cols] = tail_ref[:, wcols]
            xfull = pm[:, cols]
            acc = convw_ref[0:1, wcols] * xfull
            for k in range(1, CONV_W):
                acc = pltpu.roll(acc, 1, axis=0) + convw_ref[k:k + 1, wcols] * xfull
            val = _silu(acc[TAIL_ROWS:, :] + convb_ref[:, wcols])
            tail_ref[:, wcols] = pm[TILE_ROWS:TILE_ROWS + TAIL_ROWS, cols]
            conv_out_ref[0, :, wcols] = pm[TAIL_ROWS + TILE_ROWS - (CONV_W - 1):TAIL_ROWS + TILE_ROWS, cols]

            def commit():
                conv_store(c, val, slice(0, TILE_ROWS))
            return commit

        def task_state():
            for s in range(nseq):
                rows = slice(s * seglen, (s + 1) * seglen)
                xp_ref[TAP0:TAIL_ROWS, :] = conv0_ref[s, :, wcols]
                xp_ref[TAIL_ROWS:TAIL_ROWS + seglen, :] = pm[TAIL_ROWS + s * seglen:TAIL_ROWS + (s + 1) * seglen, cols]
                acc = convb_ref[:, wcols]
                for k in range(CONV_W):
                    acc = acc + convw_ref[k:k + 1, wcols] * xp_ref[TAP0 + k:TAP0 + k + seglen, :]
                conv_store(c, _silu(acc), rows)
                conv_out_ref[s, :, wcols] = xp_ref[TAIL_ROWS + seglen - (CONV_W - 1):TAIL_ROWS + seglen, :]

        return task_stream if pipelined else task_state

    m_convs = [_tag(m_conv(c), f"m_conv{c}", [f"m_dot{(OFF_XBC + c * LANES) // MXU_COLS}"])
               for c in range(CONV_DIM // LANES)]

    lane1 = lax.broadcasted_iota(jnp.int32, (TILE_ROWS, LANES), 1)

    def v_dt():
        a_row = -jnp.exp(alog_ref[...])
        dt = _softplus(pv[rows_all, OFF_DT:OFF_DT + LANES] + dtb_ref[...])
        dt = jnp.where(lane1 < HEADS, dt, 0.0)
        csum = _dot(tril_ref[...], _cat3(dt * a_row).astype(BF16))
        a_cum = csum + pltpu.roll(csum, LANES - HEADS, axis=1) + pltpu.roll(csum, LANES - 2 * HEADS, axis=1)
        a_cat = _cat3(jnp.where(lane1 < HEADS, a_cum, 0.0))
        aexp_ref[...] = _dot(a_cat.astype(BF16), expand_ref[...])
        dt_exp = _dot(_cat3(dt).astype(BF16), expand_ref[...])
        even = lax.rem(lane1, 2) == 0
        a_even = jnp.where(even, a_cat, 0.0)
        a_odd = (a_cat - a_even).astype(BF16)
        a_even = a_even.astype(BF16)
        parts = []
        for ck in range(TILE_ROWS // lc):
            rows = slice(ck * lc, (ck + 1) * lc)
            parts += pad_to_half(a_even[rows, :]) + pad_to_half(a_odd[rows, :])
        acumt_ref[...] = _dot_nt(psel_ref[...], jnp.concatenate(parts, axis=0))
        xdt = bv.xs[...] * dt_exp
        lane_w = lax.broadcasted_iota(jnp.int32, (TILE_ROWS, SSD_WIDTH), 1)
        xlo = jnp.where(lax.rem(lane_w, LANES) < HALF, xdt, 0.0)
        xdt_ref[...] = xdt
        xlo_ref[...] = xlo.astype(BF16)
        xhi_ref[...] = (xdt - xlo).astype(BF16)

    li = lax.broadcasted_iota(jnp.int32, (lc, LANES), 0)
    si = lax.rem(lax.broadcasted_iota(jnp.int32, (lc, LANES), 1), HALF)
    causal = (si <= li) & (si < lc)

    def v_local(ci, g):
        rows = slice(ci * lc, (ci + 1) * lc)
        hl = slice(g * GROUP_WIDTH, (g + 1) * GROUP_WIDTH)

        def products():
            last = aexp_ref[(ci + 1) * lc - 1:(ci + 1) * lc, hl]
            xw_b = (xdt_ref[rows, hl] * jnp.exp(last - aexp_ref[rows, hl])).astype(BF16)
            b_g = bv.bc[rows, g * STATE:(g + 1) * STATE]
            c_g = bv.bc[rows, (GROUPS + g) * STATE:(GROUPS + g + 1) * STATE]
            b2 = jnp.concatenate(pad_to_half(b_g) + pad_to_half(b_g), axis=0)
            cb_ref[ci * GROUPS + g] = _dot_nt(c_g, b2)
            stnew_ref[ci, :, hl] = _dot_tn(b_g, xw_b)

        def diag():
            cb2 = cb_ref[ci * GROUPS + g]
            for i in range(PAIRS_PER_GROUP):
                p = g * PAIRS_PER_GROUP + i
                pcols = slice(p * LANES, (p + 1) * LANES)
                seg = aexp_ref[rows, pcols] - acumt_ref[p:p + 1, ci * LANES:(ci + 1) * LANES]
                w_p = (cb2 * jnp.exp(jnp.where(causal, seg, -jnp.inf))).astype(BF16)
                bd = jnp.concatenate(pad_to_half(xlo_ref[rows, pcols]) + pad_to_half(xhi_ref[rows, pcols]),
                                     axis=0)
                ys_ref[rows, pcols] = _dot(w_p, bd) + dskip_ref[:, pcols] * bv.xs[rows, pcols]
        return products, diag

    def v_state(ci, s):
        def task():
            e_last = jnp.exp(aexp_ref[(ci + 1) * lc - 1:(ci + 1) * lc, :])
            st = st_ref[...]
            stb_ref[ci] = st.astype(BF16)
            st_ref[...] = st * e_last + stnew_ref[ci]
            if has_state and ci % nchunk == nchunk - 1:
                ssm_out_ref[s] = st_ref[...].T.reshape(HEADS, HEAD_DIM, STATE)
        return task

    def v_readout(ci, g):
        rows = slice(ci * lc, (ci + 1) * lc)
        prow = slice(TAIL_ROWS + ci * lc, TAIL_ROWS + (ci + 1) * lc)
        hl = slice(g * GROUP_WIDTH, (g + 1) * GROUP_WIDTH)

        def product():
            c_g = bv.bc[rows, (GROUPS + g) * STATE:(GROUPS + g + 1) * STATE]
            yoff_ref[rows, hl] = _dot(c_g, stb_ref[ci, :, hl])

        def gate():
            y = ys_ref[rows, hl] + jnp.exp(aexp_ref[rows, hl]) * yoff_ref[rows, hl]
            zg = _silu(pv[prow, OFF_Z + g * GROUP_WIDTH:OFF_Z + (g + 1) * GROUP_WIDTH])
            bvo.mix[rows, hl] = (y * zg).astype(BF16)
        return product, gate

    def v_state_load(s):
        def task():
            st_ref[...] = ssm0_ref[s].reshape(SSD_WIDTH, STATE).T
        return task

    def v_ln():
        v = pv[rows_all, OFF_V:OFF_V + GMLP_WIDTH]
        vc = v - _row_mean(v)
        vn = (vc * _rms_scale(vc)) * gv_ref[...] + betav_ref[...]
        if emit_v:
            vn_out_ref[...] = vn
        vnb_ref[...] = vn.astype(BF16)

    def v_gmlp(g):
        gcols = slice(g * GMLP_GROUP_DIM, (g + 1) * GMLP_GROUP_DIM)

        def task():
            if mixw_ref.shape[1] == TILE_ROWS:
                mixed = _dot(mixw_ref[g], vnb_ref[:, gcols])
            else:
                lcg = mixw_ref.shape[1]
                blocks = [vnb_ref[r:r + lcg, gcols] for r in range(0, TILE_ROWS, lcg)]
                wide = _dot(mixw_ref[g], jnp.concatenate(blocks, axis=1))
                mixed = jnp.concatenate(
                    [wide[:, k * GMLP_GROUP_DIM:(k + 1) * GMLP_GROUP_DIM] for k in range(len(blocks))],
                    axis=0)
            mixed = mixed + mixb_ref[:, gcols]
            u = pv[rows_all, OFF_U + g * GMLP_GROUP_DIM:OFF_U + (g + 1) * GMLP_GROUP_DIM]
            gt = pv[rows_all, OFF_G + g * GMLP_GROUP_DIM:OFF_G + (g + 1) * GMLP_GROUP_DIM]
            bvo.mix[:, SSD_WIDTH + g * GMLP_GROUP_DIM:SSD_WIDTH + (g + 1) * GMLP_GROUP_DIM] = (
                _silu(gt) * u * mixed).astype(BF16)
        return task

    chunk_groups = [(ci, g) for ci in range(nseq * nchunk) for g in range(GROUPS)]
    v_locals = [v_local(ci, g) for ci, g in chunk_groups]
    v_states = []
    for s in range(nseq):
        if has_state:
            v_states.append(v_state_load(s))
        v_states += [v_state(s * nchunk + c, s) for c in range(nchunk)]
    v_readouts = [v_readout(ci, g) for ci, g in chunk_groups]
    v_scan = ([(t[0], 0) for t in v_locals] + [(t[1], 1) for t in v_locals]
              + [(t, 1) for t in v_states]
              + [(t[0], 0) for t in v_readouts] + [(t[1], 1) for t in v_readouts])
    v_gmlps = [v_gmlp(g) for g in range(GMLP_GROUPS)]

    def o_out(n):
        def task():
            mo_ref[:, n * MXU_COLS:(n + 1) * MXU_COLS] = _dot(
                bo.mix[...], wout_ref[:, n * MXU_COLS:(n + 1) * MXU_COLS])
        return task

    def o_final():
        mo = mo_ref[...]
        mo = mo * _rms_scale(mo)
        for s in range(nseq):
            rows = slice(s * seglen, (s + 1) * seglen)
            gain = gpost_ref[...] * gate_ref[s:s + 1, :]
            y_ref[rows, :] = xo_ref[rows, :] + gain * mo[rows, :]

    o_outs = [_tag(o_out(n), f"o_out{n}") for n in range(D_MODEL // MXU_COLS)]
    _tag(o_final, "o_final", [t.name for t in o_outs])

    if pipelined:
        n_xbc_dots = CONV_DIM // MXU_COLS
        head = _interleave([(m_norm, 1), (v_dt, 2), (v_ln, 2)], o_outs + m_dots[:n_xbc_dots])
        body = [(t, 3) for t in m_convs] + [(o_final, 3)] + v_scan + [(t, 0) for t in v_gmlps]
        groups = head + _interleave(body, m_dots[n_xbc_dots:])
    else:
        order = ([m_norm] + m_dots + m_convs + [v_dt, v_ln] + [t for t, _ in v_scan] + v_gmlps
                 + o_outs + [o_final])
        groups = [[task] for task in order]
    _run_groups(groups)


def _const_spec(shape):
    nd = len(shape)
    return pl.BlockSpec(shape, lambda j: (0,) * nd, pipeline_mode=pl.Buffered(1))


def _chunk_constants(lc):
    expand = np.zeros((LANES, SSD_WIDTH), np.float32)
    psel = np.zeros((PAIRS, LANES), np.float32)
    for piece in range(PIECES):
        for hd in range(HEADS):
            expand[piece * HEADS + hd, hd * HEAD_DIM:(hd + 1) * HEAD_DIM] = 1.0
            psel[hd // 2, piece * HEADS + hd] = 1.0
    r = np.arange(TILE_ROWS)
    tril = ((r[:, None] >= r[None, :]) & (r[:, None] // lc == r[None, :] // lc)).astype(np.float32)
    return jnp.asarray(expand, BF16), jnp.asarray(tril, BF16), jnp.asarray(psel, BF16)


def _mix_weights(w_s, b_s, lc):
    nck = TILE_ROWS // lc
    mask = jnp.tril(jnp.ones((lc, lc), dtype=bool))
    w = jnp.where(mask[None], w_s[:, :lc, :lc], 0)
    if nck * GMLP_GROUP_DIM > MXU_COLS:
        eye = jnp.eye(nck, dtype=w.dtype)
        w = jnp.einsum("ab,gts->gatbs", eye, w).reshape(GMLP_GROUPS, TILE_ROWS, TILE_ROWS)
    bias = jnp.repeat(b_s[:, :lc].T, GMLP_GROUP_DIM, axis=1)
    return w.astype(BF16), jnp.tile(bias, (nck, 1))


def _layer(x, shift, scale, gate, conv0, ssm0, wl, *, nseq, seglen, lc_ssd, lc_gmlp, nt, emit_v):
    (g_pre, g_post, w_a, w_b, w_dt, conv_w, conv_b, dtb, alog, dskip, g_v, beta_v, w_s, b_s,
     w_out) = wl
    assert lc_ssd == lc_gmlp or nseq == 1
    pipelined = conv0 is None
    nb = shift.shape[0]
    rows_total = x.shape[0]
    ntiles = rows_total // TILE_ROWS
    assert nseq * seglen == TILE_ROWS and nb * nt == ntiles
    expand, tril, psel = _chunk_constants(lc_ssd)
    mixw, mixb = _mix_weights(w_s, b_s, lc_gmlp)
    nck = TILE_ROWS // lc_ssd

    if pipelined:
        def tile_m(j):
            return jnp.minimum(j, ntiles - 1)

        def tile_v(j):
            return jnp.clip(j - 1, 0, ntiles - 1)

        def tile_o(j):
            return jnp.maximum(j - 2, 0)
        steps = ntiles + 2
    else:
        def tile_m(j):
            return j
        tile_v = tile_o = tile_m
        steps = ntiles

    def mod_spec(tile_of):
        return pl.BlockSpec((None, nseq, D_MODEL), lambda j: (tile_of(j) // nt, 0, 0))

    in_specs = [pl.BlockSpec((TILE_ROWS, D_MODEL), lambda j: (tile_m(j), 0)),
                pl.BlockSpec((TILE_ROWS, D_MODEL), lambda j: (tile_o(j), 0)),
                mod_spec(tile_m), mod_spec(tile_m), mod_spec(tile_o)]
    args = [x, x, shift, scale, gate]
    if not pipelined:
        in_specs += [pl.BlockSpec((None, nseq, CONV_W - 1, CONV_DIM), lambda j: (j // nt, 0, 0, 0)),
                     pl.BlockSpec((nseq, HEADS, HEAD_DIM, STATE), lambda j: (j, 0, 0, 0))]
        args += [conv0, ssm0]
    consts = [g_pre, g_post, w_a, w_b, w_dt, conv_w, conv_b, dtb, alog, dskip, g_v, beta_v, mixw,
              mixb, w_out, expand, tril, psel]
    in_specs += [_const_spec(a.shape) for a in consts]
    args += consts

    out_shape = [jax.ShapeDtypeStruct((rows_total, D_MODEL), F32),
                 jax.ShapeDtypeStruct((nb, nseq, CONV_W - 1, CONV_DIM), F32),
                 jax.ShapeDtypeStruct((nb * nseq, HEADS, HEAD_DIM, STATE), F32)]
    out_specs = [pl.BlockSpec((TILE_ROWS, D_MODEL), lambda j: (tile_o(j), 0)),
                 pl.BlockSpec((None, nseq, CONV_W - 1, CONV_DIM), lambda j: (tile_m(j) // nt, 0, 0, 0)),
                 pl.BlockSpec((nseq, HEADS, HEAD_DIM, STATE), lambda j: (tile_v(j) // nt, 0, 0, 0))]
    if emit_v:
        out_shape.append(jax.ShapeDtypeStruct((rows_total, GMLP_WIDTH), F32))
        out_specs.append(pl.BlockSpec((TILE_ROWS, GMLP_WIDTH), lambda j: (tile_v(j), 0)))

    spare = (TAIL_ROWS, LANES)
    stage_shapes = [((TAIL_ROWS + TILE_ROWS, W_CAT_COLS), F32),
                    ((TILE_ROWS, SSD_WIDTH), F32),
                    ((TILE_ROWS, 2 * GROUPS * STATE), BF16),
                    ((TILE_ROWS, SSD_WIDTH + GMLP_WIDTH), BF16)]
    scratch = [pltpu.VMEM((TILE_ROWS, D_MODEL), BF16),
               pltpu.VMEM((TAIL_ROWS, CONV_DIM), F32),
               pltpu.VMEM((TAIL_ROWS + seglen, LANES), F32),
               pltpu.VMEM((STATE, SSD_WIDTH), F32),
               pltpu.VMEM((nck, STATE, SSD_WIDTH), F32),
               pltpu.VMEM((TILE_ROWS, SSD_WIDTH), F32),
               pltpu.VMEM((TILE_ROWS, SSD_WIDTH), F32),
               pltpu.VMEM((TILE_ROWS, SSD_WIDTH), BF16),
               pltpu.VMEM((TILE_ROWS, SSD_WIDTH), BF16),
               pltpu.VMEM((PAIRS, nck * LANES), F32),
               pltpu.VMEM((TILE_ROWS, GMLP_WIDTH), BF16),
               pltpu.VMEM((TILE_ROWS, D_MODEL), F32),
               pltpu.VMEM((nck, STATE, SSD_WIDTH), BF16),
               pltpu.VMEM((TILE_ROWS, SSD_WIDTH), F32),
               pltpu.VMEM((nck * GROUPS, lc_ssd, LANES), F32),
               pltpu.VMEM((TILE_ROWS, SSD_WIDTH), F32)]
    assert len(scratch) == N_SHARED_SCRATCH
    for shape, dtype in stage_shapes:
        scratch += [pltpu.VMEM(shape, dtype), pltpu.VMEM(shape if pipelined else spare, dtype)]
    kern = functools.partial(_layer_kernel, nseq=nseq, seglen=seglen, lc=lc_ssd, nt=nt,
                             pipelined=pipelined, emit_v=emit_v)
    return pl.pallas_call(
        kern,
        grid=(steps,),
        in_specs=in_specs,
        out_specs=out_specs,
        out_shape=out_shape,
        scratch_shapes=scratch,
        compiler_params=pltpu.CompilerParams(
            dimension_semantics=("arbitrary",),
            vmem_limit_bytes=VMEM_LIMIT_BYTES),
        name="hybrid_layer_stream" if pipelined else "hybrid_layer_state",
    )(*args)


def _prep_weights(g_pre, g_post, w_in, conv_w, conv_b, dt_bias, a_log, d_skip, g_v, beta_v,
                  w_s, b_s, w_out):
    i_dt = SSD_WIDTH + CONV_DIM
    w_a = w_in[:, :i_dt].astype(BF16)
    w_b = w_in[:, i_dt + HEADS:].astype(BF16)
    w_dt = jnp.pad(w_in[:, i_dt:i_dt + HEADS], ((0, 0), (0, LANES - HEADS))).astype(BF16)
    pad = (0, LANES - HEADS)
    return (g_pre.reshape(1, -1), g_post.reshape(1, -1), w_a, w_b, w_dt, conv_w, conv_b.reshape(1, -1),
            jnp.pad(dt_bias, pad).reshape(1, -1), jnp.pad(a_log, pad).reshape(1, -1),
            jnp.repeat(d_skip, HEAD_DIM).reshape(1, -1), g_v.reshape(1, -1),
            beta_v.reshape(1, -1), w_s, b_s, w_out.astype(BF16))


def kernel(x_prompt, x_sample, state_conv, state_ssm, c_prompt, c_sample, w_ada, b_ada, g_pre,
           g_post, w_in, conv_w, conv_b, dt_bias, a_log, d_skip, g_v, beta_v, w_s, b_s, w_out):
    depth = w_ada.shape[0]
    bp, lp, _ = x_prompt.shape
    bs, ls, _ = x_sample.shape
    seq_per_tile = TILE_ROWS // ls
    yp = x_prompt.reshape(bp * lp, D_MODEL)
    ys = x_sample.reshape(bs * ls, D_MODEL)
    c_all = jnp.concatenate([c_prompt, c_sample], axis=0)
    c_pad = jnp.pad(c_all, ((0, (-c_all.shape[0]) % 8), (0, 0)))
    outs = [[] for _ in range(5)]
    for l in range(depth):
        wl = _prep_weights(g_pre[l], g_post[l], w_in[l], conv_w[l], conv_b[l], dt_bias[l],
                           a_log[l], d_skip[l], g_v[l], beta_v[l], w_s[l], b_s[l], w_out[l])
        mod = _modulation(c_pad, w_ada[l], b_ada[l])
        shift, scale, gate = (mod[:, i * D_MODEL:(i + 1) * D_MODEL] for i in range(3))
        mp = [m[:bp].reshape(bp, 1, D_MODEL) for m in (shift, scale, gate)]
        ms = [m[bp:bp + bs].reshape(bs // seq_per_tile, seq_per_tile, D_MODEL)
              for m in (shift, scale, gate)]
        yp, conv_p, ssm_p = _layer(
            yp, *mp, None, None, wl, nseq=1, seglen=TILE_ROWS, lc_ssd=min(lp, SSD_CHUNK),
            lc_gmlp=min(lp, GMLP_CHUNK), nt=lp // TILE_ROWS, emit_v=False)
        ys, conv_s, ssm_s, v_s = _layer(
            ys, *ms,
            state_conv[l].reshape(bs // seq_per_tile, seq_per_tile, CONV_W - 1, CONV_DIM),
            state_ssm[l], wl,
            nseq=seq_per_tile, seglen=ls, lc_ssd=min(ls, SSD_CHUNK), lc_gmlp=min(ls, GMLP_CHUNK),
            nt=1, emit_v=True)
        outs[0].append(conv_p.reshape(bp, CONV_W - 1, CONV_DIM))
        outs[1].append(ssm_p)
        outs[2].append(conv_s.reshape(bs, CONV_W - 1, CONV_DIM))
        outs[3].append(ssm_s)
        outs[4].append(v_s.reshape(bs, ls, GMLP_WIDTH))
    return (yp.reshape(bp, lp, D_MODEL), ys.reshape(bs, ls, D_MODEL), jnp.stack(outs[0]),
            jnp.stack(outs[1]), jnp.stack(outs[2]), jnp.stack(outs[3]), jnp.stack(outs[4]))
```

```python
import collections
import functools

import jax
import jax.numpy as jnp
import numpy as np
from jax import lax
from jax.experimental import pallas as pl
from jax.experimental.pallas import tpu as pltpu

F32 = jnp.float32
BF16 = jnp.bfloat16

D_MODEL = 1024
SSD_WIDTH = 1024
HEAD_DIM = 64
HEADS = SSD_WIDTH // HEAD_DIM
GROUPS = 2
GROUP_WIDTH = SSD_WIDTH // GROUPS
STATE = 128
CONV_W = 4
CONV_DIM = SSD_WIDTH + 2 * GROUPS * STATE
GMLP_WIDTH = 1024
GMLP_GROUPS = 8
GMLP_GROUP_DIM = GMLP_WIDTH // GMLP_GROUPS
SSD_CHUNK = 64
GMLP_CHUNK = 128
EPS = 1e-6
LOG2E = 1.4426950408889634

LANES = 128
MXU_COLS = 256
HALF = LANES // 2
PAIRS = HEADS // 2
PAIRS_PER_GROUP = PAIRS // GROUPS
PIECES = 3
TILE_ROWS = 256
TAIL_ROWS = 8
TAP0 = TAIL_ROWS - (CONV_W - 1)
MOD_BLOCK_COLS = 512
VMEM_LIMIT_BYTES = 60 * 1024 * 1024

OFF_Z = 0
OFF_XBC = OFF_Z + SSD_WIDTH
OFF_U = OFF_XBC + CONV_DIM
OFF_V = OFF_U + GMLP_WIDTH
OFF_G = OFF_V + GMLP_WIDTH
OFF_DT = OFF_G + GMLP_WIDTH
W_CAT_COLS = OFF_DT + LANES

Bufs = collections.namedtuple("Bufs", "proj xs bc mix")


def _dot(a, b):
    return jnp.dot(a, b, preferred_element_type=F32)


def _dot_nt(a, b):
    return lax.dot_general(a, b, (((1,), (1,)), ((), ())), preferred_element_type=F32)


def _dot_tn(a, b):
    return lax.dot_general(a, b, (((0,), (0,)), ((), ())), preferred_element_type=F32)


def _cat3(x):
    hi = x.astype(BF16).astype(F32)
    r1 = x - hi
    mid = r1.astype(BF16).astype(F32)
    lo = r1 - mid
    return hi + pltpu.roll(mid, HEADS, axis=1) + pltpu.roll(lo, 2 * HEADS, axis=1)


def _silu(x):
    return x / (1.0 + jnp.exp2(x * (-LOG2E)))


def _softplus(x):
    return jnp.maximum(x, 0.0) + jnp.log1p(jnp.exp(-jnp.abs(x)))


def _row_mean(x):
    width = x.shape[-1]
    part = x[:, 0:LANES]
    for c in range(LANES, width, LANES):
        part = part + x[:, c:c + LANES]
    return jnp.sum(part, axis=-1, keepdims=True) * (1.0 / width)


def _rms_scale(x):
    return lax.rsqrt(_row_mean(x * x) + EPS)


def _mod_kernel(c_ref, w_ref, b_ref, o_ref):
    c = _silu(c_ref[...]).astype(BF16)
    o_ref[...] = _dot(c, w_ref[...].astype(BF16)) + b_ref[...]


def _modulation(c, w_ada, b_ada):
    rows = c.shape[0]
    n = w_ada.shape[1]
    return pl.pallas_call(
        _mod_kernel,
        grid=(n // MOD_BLOCK_COLS,),
        in_specs=[
            pl.BlockSpec((rows, D_MODEL), lambda j: (0, 0)),
            pl.BlockSpec((D_MODEL, MOD_BLOCK_COLS), lambda j: (0, j)),
            pl.BlockSpec((1, MOD_BLOCK_COLS), lambda j: (0, j)),
        ],
        out_specs=pl.BlockSpec((rows, MOD_BLOCK_COLS), lambda j: (0, j)),
        out_shape=jax.ShapeDtypeStruct((rows, n), F32),
        name="adaln_mod",
    )(c, w_ada, b_ada.reshape(1, n))


N_SHARED_SCRATCH = 16
N_STAGE_SCRATCH = 8


def _interleave(primary, secondary):
    total = sum(weight for _, weight in primary)
    groups = []
    done = 0
    seen = 0
    for task, weight in primary:
        seen += weight
        want = (seen * len(secondary)) // total
        groups.append([task] + secondary[done:want])
        done = want
    return groups


def _tag(task, name, needs=()):
    task.name, task.needs = name, tuple(needs)
    return task


def _run_groups(groups):
    stored = set()
    for group in groups:
        for task in group:
            missing = [n for n in getattr(task, "needs", ()) if n not in stored]
            assert not missing, (getattr(task, "name", task), missing)
        commits = [task() for task in group]
        for commit in commits:
            if commit is not None:
                commit()
        stored.update(getattr(task, "name", None) for task in group)


def _layer_kernel(*refs, nseq, seglen, lc, nt, pipelined, emit_v):
    statics = dict(nseq=nseq, seglen=seglen, lc=lc, pipelined=pipelined, emit_v=emit_v)
    shared, stage = refs[:-N_STAGE_SCRATCH], refs[-N_STAGE_SCRATCH:]
    buf_a, buf_b = Bufs(*stage[0::2]), Bufs(*stage[1::2])
    n_scratch = N_SHARED_SCRATCH + N_STAGE_SCRATCH
    ssm_out_ref = refs[len(refs) - n_scratch - (2 if emit_v else 1)]
    tail_ref, st_ref = refs[-n_scratch + 1], refs[-n_scratch + 3]
    if not pipelined:
        _tile_body(shared, buf_a, buf_a, buf_a, buf_a, **statics)
        return

    j = pl.program_id(0)
    m_first = lax.rem(j, nt) == 0
    v_first = (j == 0) | (lax.rem(j - 1, nt) == 0)
    v_last = (j > 0) & (lax.rem(j - 1, nt) == nt - 1)

    @pl.when(j == 0)
    def _():
        for ref in buf_b:
            ref[...] = jnp.zeros_like(ref)

    @pl.when(m_first)
    def _():
        tail_ref[...] = jnp.zeros_like(tail_ref)

    @pl.when(v_first)
    def _():
        st_ref[...] = jnp.zeros_like(st_ref)

    @pl.when(lax.rem(j, 2) == 0)
    def _():
        _tile_body(shared, buf_a, buf_b, buf_a, buf_b, **statics)

    @pl.when(lax.rem(j, 2) == 1)
    def _():
        _tile_body(shared, buf_b, buf_a, buf_b, buf_a, **statics)

    @pl.when(v_last)
    def _():
        ssm_out_ref[0] = st_ref[...].T.reshape(HEADS, HEAD_DIM, STATE)


def _tile_body(refs, bm, bv, bvo, bo, *, nseq, seglen, lc, pipelined, emit_v):
    has_state = not pipelined
    it = iter(refs)
    x_ref, xo_ref = next(it), next(it)
    shift_ref, scale_ref, gate_ref = next(it), next(it), next(it)
    conv0_ref = next(it) if has_state else None
    ssm0_ref = next(it) if has_state else None
    (gpre_ref, gpost_ref, wa_ref, wb_ref, wdt_ref, convw_ref, convb_ref, dtb_ref, alog_ref,
     dskip_ref, gv_ref, betav_ref, mixw_ref, mixb_ref, wout_ref, expand_ref, tril_ref, psel_ref) = (
         next(it) for _ in range(18))
    y_ref, conv_out_ref, ssm_out_ref = next(it), next(it), next(it)
    vn_out_ref = next(it) if emit_v else None
    (h_ref, tail_ref, xp_ref, st_ref, stnew_ref, aexp_ref, xdt_ref, xlo_ref, xhi_ref, acumt_ref,
     vnb_ref, mo_ref, stb_ref, ys_ref, cb_ref, yoff_ref) = (next(it) for _ in range(N_SHARED_SCRATCH))
    pm, pv = bm.proj, bv.proj

    rows_all = slice(TAIL_ROWS, TAIL_ROWS + TILE_ROWS)
    nchunk = seglen // lc
    pad_rows = HALF - lc

    def pad_to_half(v):
        if pad_rows == 0:
            return [v]
        return [v, jnp.zeros((pad_rows, v.shape[1]), v.dtype)]

    def m_norm():
        for s in range(nseq):
            rows = slice(s * seglen, (s + 1) * seglen)
            xr = x_ref[rows, :]
            gain = gpre_ref[...] * (1.0 + scale_ref[s:s + 1, :])
            hs = (xr * _rms_scale(xr)) * gain + shift_ref[s:s + 1, :]
            h_ref[rows, :] = hs.astype(BF16)

    def m_dot(w_ref, w0, dst0, width):
        def task():
            val = _dot(h_ref[...], w_ref[:, w0:w0 + width])

            def commit():
                pm[rows_all, dst0:dst0 + width] = val
            return commit
        return _tag(task, f"m_dot{dst0 // MXU_COLS}")

    dst_starts = (list(range(OFF_XBC, OFF_U, MXU_COLS)) + list(range(OFF_Z, OFF_XBC, MXU_COLS))
                  + list(range(OFF_U, OFF_DT, MXU_COLS)))
    m_dots = [m_dot(wa_ref, d, d, MXU_COLS) if d < OFF_U else m_dot(wb_ref, d - OFF_U, d, MXU_COLS)
              for d in dst_starts] + [m_dot(wdt_ref, 0, OFF_DT, LANES)]

    def conv_store(c, val, rows):
        if c < SSD_WIDTH // LANES:
            bm.xs[rows, c * LANES:(c + 1) * LANES] = val
        else:
            c2 = c - SSD_WIDTH // LANES
            bm.bc[rows, c2 * LANES:(c2 + 1) * LANES] = val.astype(BF16)

    def m_conv(c):
        cols = slice(OFF_XBC + c * LANES, OFF_XBC + (c + 1) * LANES)
        wcols = slice(c * LANES, (c + 1) * LANES)

        def task_stream():
            pm[0:TAIL_ROWS, cols] = tail_ref[:, wcols]
            xfull = pm[:, cols]
            acc = convw_ref[0:1, wcols] * xfull
            for k in range(1, CONV_W):
                acc = pltpu.roll(acc, 1, axis=0) + convw_ref[k:k + 1, wcols] * xfull
            val = _silu(acc[TAIL_ROWS:, :] + convb_ref[:, wcols])
            tail_ref[:, wcols] = pm[TILE_ROWS:TILE_ROWS + TAIL_ROWS, cols]
            conv_out_ref[0, :, wcols] = pm[TAIL_ROWS + TILE_ROWS - (CONV_W - 1):TAIL_ROWS + TILE_ROWS, cols]

            def commit():
                conv_store(c, val, slice(0, TILE_ROWS))
            return commit

        def task_state():
            for s in range(nseq):
                rows = slice(s * seglen, (s + 1) * seglen)
                xp_ref[TAP0:TAIL_ROWS, :] = conv0_ref[s, :, wcols]
                xp_ref[TAIL_ROWS:TAIL_ROWS + seglen, :] = pm[TAIL_ROWS + s * seglen:TAIL_ROWS + (s + 1) * seglen, cols]
                acc = convb_ref[:, wcols]
                for k in range(CONV_W):
                    acc = acc + convw_ref[k:k + 1, wcols] * xp_ref[TAP0 + k:TAP0 + k + seglen, :]
                conv_store(c, _silu(acc), rows)
                conv_out_ref[s, :, wcols] = xp_ref[TAIL_ROWS + seglen - (CONV_W - 1):TAIL_ROWS + seglen, :]

        return task_stream if pipelined else task_state

    m_convs = [_tag(m_conv(c), f"m_conv{c}", [f"m_dot{(OFF_XBC + c * LANES) // MXU_COLS}"])
               for c in range(CONV_DIM // LANES)]

    lane1 = lax.broadcasted_iota(jnp.int32, (TILE_ROWS, LANES), 1)

    def v_dt():
        a_row = -jnp.exp(alog_ref[...])
        dt = _softplus(pv[rows_all, OFF_DT:OFF_DT + LANES] + dtb_ref[...])
        dt = jnp.where(lane1 < HEADS, dt, 0.0)
        csum = _dot(tril_ref[...], _cat3(dt * a_row).astype(BF16))
        a_cum = csum + pltpu.roll(csum, LANES - HEADS, axis=1) + pltpu.roll(csum, LANES - 2 * HEADS, axis=1)
        a_cat = _cat3(jnp.where(lane1 < HEADS, a_cum, 0.0))
        aexp_ref[...] = _dot(a_cat.astype(BF16), expand_ref[...])
        dt_exp = _dot(_cat3(dt).astype(BF16), expand_ref[...])
        even = lax.rem(lane1, 2) == 0
        a_even = jnp.where(even, a_cat, 0.0)
        a_odd = (a_cat - a_even).astype(BF16)
        a_even = a_even.astype(BF16)
        parts = []
        for ck in range(TILE_ROWS // lc):
            rows = slice(ck * lc, (ck + 1) * lc)
            parts += pad_to_half(a_even[rows, :]) + pad_to_half(a_odd[rows, :])
        acumt_ref[...] = _dot_nt(psel_ref[...], jnp.concatenate(parts, axis=0))
        xdt = bv.xs[...] * dt_exp
        lane_w = lax.broadcasted_iota(jnp.int32, (TILE_ROWS, SSD_WIDTH), 1)
        xlo = jnp.where(lax.rem(lane_w, LANES) < HALF, xdt, 0.0)
        xdt_ref[...] = xdt
        xlo_ref[...] = xlo.astype(BF16)
        xhi_ref[...] = (xdt - xlo).astype(BF16)

    li = lax.broadcasted_iota(jnp.int32, (lc, LANES), 0)
    si = lax.rem(lax.broadcasted_iota(jnp.int32, (lc, LANES), 1), HALF)
    causal = (si <= li) & (si < lc)

    def v_local(ci, g):
        rows = slice(ci * lc, (ci + 1) * lc)
        hl = slice(g * GROUP_WIDTH, (g + 1) * GROUP_WIDTH)

        def products():
            last = aexp_ref[(ci + 1) * lc - 1:(ci + 1) * lc, hl]
            xw_b = (xdt_ref[rows, hl] * jnp.exp(last - aexp_ref[rows, hl])).astype(BF16)
            b_g = bv.bc[rows, g * STATE:(g + 1) * STATE]
            c_g = bv.bc[rows, (GROUPS + g) * STATE:(GROUPS + g + 1) * STATE]
            b2 = jnp.concatenate(pad_to_half(b_g) + pad_to_half(b_g), axis=0)
            cb_ref[ci * GROUPS + g] = _dot_nt(c_g, b2)
            stnew_ref[ci, :, hl] = _dot_tn(b_g, xw_b)

        def diag():
            cb2 = cb_ref[ci * GROUPS + g]
            for i in range(PAIRS_PER_GROUP):
                p = g * PAIRS_PER_GROUP + i
                pcols = slice(p * LANES, (p + 1) * LANES)
                seg = aexp_ref[rows, pcols] - acumt_ref[p:p + 1, ci * LANES:(ci + 1) * LANES]
                w_p = (cb2 * jnp.exp(jnp.where(causal, seg, -jnp.inf))).astype(BF16)
                bd = jnp.concatenate(pad_to_half(xlo_ref[rows, pcols]) + pad_to_half(xhi_ref[rows, pcols]),
                                     axis=0)
                ys_ref[rows, pcols] = _dot(w_p, bd) + dskip_ref[:, pcols] * bv.xs[rows, pcols]
        return products, diag

    def v_state(ci, s):
        def task():
            e_last = jnp.exp(aexp_ref[(ci + 1) * lc - 1:(ci + 1) * lc, :])
            st = st_ref[...]
            stb_ref[ci] = st.astype(BF16)
            st_ref[...] = st * e_last + stnew_ref[ci]
            if has_state and ci % nchunk == nchunk - 1:
                ssm_out_ref[s] = st_ref[...].T.reshape(HEADS, HEAD_DIM, STATE)
        return task

    def v_readout(ci, g):
        rows = slice(ci * lc, (ci + 1) * lc)
        prow = slice(TAIL_ROWS + ci * lc, TAIL_ROWS + (ci + 1) * lc)
        hl = slice(g * GROUP_WIDTH, (g + 1) * GROUP_WIDTH)

        def product():
            c_g = bv.bc[rows, (GROUPS + g) * STATE:(GROUPS + g + 1) * STATE]
            yoff_ref[rows, hl] = _dot(c_g, stb_ref[ci, :, hl])

        def gate():
            y = ys_ref[rows, hl] + jnp.exp(aexp_ref[rows, hl]) * yoff_ref[rows, hl]
            zg = _silu(pv[prow, OFF_Z + g * GROUP_WIDTH:OFF_Z + (g + 1) * GROUP_WIDTH])
            bvo.mix[rows, hl] = (y * zg).astype(BF16)
        return product, gate

    def v_state_load(s):
        def task():
            st_ref[...] = ssm0_ref[s].reshape(SSD_WIDTH, STATE).T
        return task

    def v_ln():
        v = pv[rows_all, OFF_V:OFF_V + GMLP_WIDTH]
        vc = v - _row_mean(v)
        vn = (vc * _rms_scale(vc)) * gv_ref[...] + betav_ref[...]
        if emit_v:
            vn_out_ref[...] = vn
        vnb_ref[...] = vn.astype(BF16)

    def v_gmlp(g):
        gcols = slice(g * GMLP_GROUP_DIM, (g + 1) * GMLP_GROUP_DIM)

        def task():
            if mixw_ref.shape[1] == TILE_ROWS:
                mixed = _dot(mixw_ref[g], vnb_ref[:, gcols])
            else:
                lcg = mixw_ref.shape[1]
                blocks = [vnb_ref[r:r + lcg, gcols] for r in range(0, TILE_ROWS, lcg)]
                wide = _dot(mixw_ref[g], jnp.concatenate(blocks, axis=1))
                mixed = jnp.concatenate(
                    [wide[:, k * GMLP_GROUP_DIM:(k + 1) * GMLP_GROUP_DIM] for k in range(len(blocks))],
                    axis=0)
            mixed = mixed + mixb_ref[:, gcols]
            u = pv[rows_all, OFF_U + g * GMLP_GROUP_DIM:OFF_U + (g + 1) * GMLP_GROUP_DIM]
            gt = pv[rows_all, OFF_G + g * GMLP_GROUP_DIM:OFF_G + (g + 1) * GMLP_GROUP_DIM]
            bvo.mix[:, SSD_WIDTH + g * GMLP_GROUP_DIM:SSD_WIDTH + (g + 1) * GMLP_GROUP_DIM] = (
                _silu(gt) * u * mixed).astype(BF16)
        return task

    chunk_groups = [(ci, g) for ci in range(nseq * nchunk) for g in range(GROUPS)]
    v_locals = [v_local(ci, g) for ci, g in chunk_groups]
    v_states = []
    for s in range(nseq):
        if has_state:
            v_states.append(v_state_load(s))
        v_states += [v_state(s * nchunk + c, s) for c in range(nchunk)]
    v_readouts = [v_readout(ci, g) for ci, g in chunk_groups]
    v_scan = ([(t[0], 0) for t in v_locals] + [(t[1], 1) for t in v_locals]
              + [(t, 1) for t in v_states]
              + [(t[0], 0) for t in v_readouts] + [(t[1], 1) for t in v_readouts])
    v_gmlps = [v_gmlp(g) for g in range(GMLP_GROUPS)]

    def o_out(n):
        def task():
            mo_ref[:, n * MXU_COLS:(n + 1) * MXU_COLS] = _dot(
                bo.mix[...], wout_ref[:, n * MXU_COLS:(n + 1) * MXU_COLS])
        return task

    def o_final():
        mo = mo_ref[...]
        mo = mo * _rms_scale(mo)
        for s in range(nseq):
            rows = slice(s * seglen, (s + 1) * seglen)
            gain = gpost_ref[...] * gate_ref[s:s + 1, :]
            y_ref[rows, :] = xo_ref[rows, :] + gain * mo[rows, :]

    o_outs = [_tag(o_out(n), f"o_out{n}") for n in range(D_MODEL // MXU_COLS)]
    _tag(o_final, "o_final", [t.name for t in o_outs])

    if pipelined:
        n_xbc_dots = CONV_DIM // MXU_COLS
        head = _interleave([(m_norm, 1), (v_dt, 2), (v_ln, 2)], o_outs + m_dots[:n_xbc_dots])
        body = [(t, 3) for t in m_convs] + [(o_final, 3)] + v_scan + [(t, 0) for t in v_gmlps]
        groups = head + _interleave(body, m_dots[n_xbc_dots:])
    else:
        order = ([m_norm] + m_dots + m_convs + [v_dt, v_ln] + [t for t, _ in v_scan] + v_gmlps
                 + o_outs + [o_final])
        groups = [[task] for task in order]
    _run_groups(groups)


def _const_spec(shape):
    nd = len(shape)
    return pl.BlockSpec(shape, lambda j: (0,) * nd, pipeline_mode=pl.Buffered(1))


def _chunk_constants(lc):
    expand = np.zeros((LANES, SSD_WIDTH), np.float32)
    psel = np.zeros((PAIRS, LANES), np.float32)
    for piece in range(PIECES):
        for hd in range(HEADS):
            expand[piece * HEADS + hd, hd * HEAD_DIM:(hd + 1) * HEAD_DIM] = 1.0
            psel[hd // 2, piece * HEADS + hd] = 1.0
    r = np.arange(TILE_ROWS)
    tril = ((r[:, None] >= r[None, :]) & (r[:, None] // lc == r[None, :] // lc)).astype(np.float32)
    return jnp.asarray(expand, BF16), jnp.asarray(tril, BF16), jnp.asarray(psel, BF16)


def _mix_weights(w_s, b_s, lc):
    nck = TILE_ROWS // lc
    mask = jnp.tril(jnp.ones((lc, lc), dtype=bool))
    w = jnp.where(mask[None], w_s[:, :lc, :lc], 0)
    if nck * GMLP_GROUP_DIM > MXU_COLS:
        eye = jnp.eye(nck, dtype=w.dtype)
        w = jnp.einsum("ab,gts->gatbs", eye, w).reshape(GMLP_GROUPS, TILE_ROWS, TILE_ROWS)
    bias = jnp.repeat(b_s[:, :lc].T, GMLP_GROUP_DIM, axis=1)
    return w.astype(BF16), jnp.tile(bias, (nck, 1))


def _layer(x, shift, scale, gate, conv0, ssm0, wl, *, nseq, seglen, lc_ssd, lc_gmlp, nt, emit_v,
           layer=0):
    (g_pre, g_post, w_a, w_b, w_dt, conv_w, conv_b, dtb, alog, dskip, g_v, beta_v, w_s, b_s,
     w_out) = wl
    assert lc_ssd == lc_gmlp or nseq == 1
    pipelined = conv0 is None
    nb = shift.shape[0]
    rows_total = x.shape[0]
    ntiles = rows_total // TILE_ROWS
    assert nseq * seglen == TILE_ROWS and nb * nt == ntiles
    expand, tril, psel = _chunk_constants(lc_ssd)
    mixw, mixb = _mix_weights(w_s, b_s, lc_gmlp)
    nck = TILE_ROWS // lc_ssd

    if pipelined:
        def tile_m(j):
            return jnp.minimum(j, ntiles - 1)

        def tile_v(j):
            return jnp.clip(j - 1, 0, ntiles - 1)

        def tile_o(j):
            return jnp.maximum(j - 2, 0)
        steps = ntiles + 2
    else:
        def tile_m(j):
            return j
        tile_v = tile_o = tile_m
        steps = ntiles

    def mod_spec(tile_of):
        return pl.BlockSpec((None, nseq, D_MODEL), lambda j: (tile_of(j) // nt, 0, 0))

    in_specs = [pl.BlockSpec((TILE_ROWS, D_MODEL), lambda j: (tile_m(j), 0)),
                pl.BlockSpec((TILE_ROWS, D_MODEL), lambda j: (tile_o(j), 0)),
                mod_spec(tile_m), mod_spec(tile_m), mod_spec(tile_o)]
    args = [x, x, shift, scale, gate]
    if not pipelined:
        in_specs += [pl.BlockSpec((None, nseq, CONV_W - 1, CONV_DIM), lambda j: (j // nt, 0, 0, 0)),
                     pl.BlockSpec((None, nseq, HEADS, HEAD_DIM, STATE), lambda j: (layer, j, 0, 0, 0))]
        args += [conv0, ssm0]
    consts = [g_pre, g_post, w_a, w_b, w_dt, conv_w, conv_b, dtb, alog, dskip, g_v, beta_v, mixw,
              mixb, w_out, expand, tril, psel]
    in_specs += [_const_spec(a.shape) for a in consts]
    args += consts

    out_shape = [jax.ShapeDtypeStruct((rows_total, D_MODEL), F32),
                 jax.ShapeDtypeStruct((nb, nseq, CONV_W - 1, CONV_DIM), F32),
                 jax.ShapeDtypeStruct((nb * nseq, HEADS, HEAD_DIM, STATE), F32)]
    out_specs = [pl.BlockSpec((TILE_ROWS, D_MODEL), lambda j: (tile_o(j), 0)),
                 pl.BlockSpec((None, nseq, CONV_W - 1, CONV_DIM), lambda j: (tile_m(j) // nt, 0, 0, 0)),
                 pl.BlockSpec((nseq, HEADS, HEAD_DIM, STATE), lambda j: (tile_v(j) // nt, 0, 0, 0))]
    if emit_v:
        out_shape.append(jax.ShapeDtypeStruct((rows_total, GMLP_WIDTH), F32))
        out_specs.append(pl.BlockSpec((TILE_ROWS, GMLP_WIDTH), lambda j: (tile_v(j), 0)))

    spare = (TAIL_ROWS, LANES)
    stage_shapes = [((TAIL_ROWS + TILE_ROWS, W_CAT_COLS), F32),
                    ((TILE_ROWS, SSD_WIDTH), F32),
                    ((TILE_ROWS, 2 * GROUPS * STATE), BF16),
                    ((TILE_ROWS, SSD_WIDTH + GMLP_WIDTH), BF16)]
    scratch = [pltpu.VMEM((TILE_ROWS, D_MODEL), BF16),
               pltpu.VMEM((TAIL_ROWS, CONV_DIM), F32),
               pltpu.VMEM((TAIL_ROWS + seglen, LANES), F32),
               pltpu.VMEM((STATE, SSD_WIDTH), F32),
               pltpu.VMEM((nck, STATE, SSD_WIDTH), F32),
               pltpu.VMEM((TILE_ROWS, SSD_WIDTH), F32),
               pltpu.VMEM((TILE_ROWS, SSD_WIDTH), F32),
               pltpu.VMEM((TILE_ROWS, SSD_WIDTH), BF16),
               pltpu.VMEM((TILE_ROWS, SSD_WIDTH), BF16),
               pltpu.VMEM((PAIRS, nck * LANES), F32),
               pltpu.VMEM((TILE_ROWS, GMLP_WIDTH), BF16),
               pltpu.VMEM((TILE_ROWS, D_MODEL), F32),
               pltpu.VMEM((nck, STATE, SSD_WIDTH), BF16),
               pltpu.VMEM((TILE_ROWS, SSD_WIDTH), F32),
               pltpu.VMEM((nck * GROUPS, lc_ssd, LANES), F32),
               pltpu.VMEM((TILE_ROWS, SSD_WIDTH), F32)]
    assert len(scratch) == N_SHARED_SCRATCH
    for shape, dtype in stage_shapes:
        scratch += [pltpu.VMEM(shape, dtype), pltpu.VMEM(shape if pipelined else spare, dtype)]
    kern = functools.partial(_layer_kernel, nseq=nseq, seglen=seglen, lc=lc_ssd, nt=nt,
                             pipelined=pipelined, emit_v=emit_v)
    return pl.pallas_call(
        kern,
        grid=(steps,),
        in_specs=in_specs,
        out_specs=out_specs,
        out_shape=out_shape,
        scratch_shapes=scratch,
        compiler_params=pltpu.CompilerParams(
            dimension_semantics=("arbitrary",),
            vmem_limit_bytes=VMEM_LIMIT_BYTES),
        name="hybrid_layer_stream" if pipelined else "hybrid_layer_state",
    )(*args)


def _prep_weights(g_pre, g_post, w_in, conv_w, conv_b, dt_bias, a_log, d_skip, g_v, beta_v,
                  w_s, b_s, w_out):
    i_dt = SSD_WIDTH + CONV_DIM
    w16 = w_in.astype(BF16)
    w_a = w16[:, :i_dt]
    w_b = w16[:, i_dt + HEADS:]
    w_dt = jnp.pad(w16[:, i_dt:i_dt + HEADS], ((0, 0), (0, LANES - HEADS)))
    pad = (0, LANES - HEADS)
    return (g_pre.reshape(1, -1), g_post.reshape(1, -1), w_a, w_b, w_dt, conv_w, conv_b.reshape(1, -1),
            jnp.pad(dt_bias, pad).reshape(1, -1), jnp.pad(a_log, pad).reshape(1, -1),
            jnp.repeat(d_skip, HEAD_DIM).reshape(1, -1), g_v.reshape(1, -1),
            beta_v.reshape(1, -1), w_s, b_s, w_out.astype(BF16))


def kernel(x_prompt, x_sample, state_conv, state_ssm, c_prompt, c_sample, w_ada, b_ada, g_pre,
           g_post, w_in, conv_w, conv_b, dt_bias, a_log, d_skip, g_v, beta_v, w_s, b_s, w_out):
    depth = w_ada.shape[0]
    bp, lp, _ = x_prompt.shape
    bs, ls, _ = x_sample.shape
    seq_per_tile = TILE_ROWS // ls
    yp = x_prompt.reshape(bp * lp, D_MODEL)
    ys = x_sample.reshape(bs * ls, D_MODEL)
    c_all = jnp.concatenate([c_prompt, c_sample], axis=0)
    c_pad = jnp.pad(c_all, ((0, (-c_all.shape[0]) % 8), (0, 0)))
    outs = [[] for _ in range(5)]
    for l in range(depth):
        wl = _prep_weights(g_pre[l], g_post[l], w_in[l], conv_w[l], conv_b[l], dt_bias[l],
                           a_log[l], d_skip[l], g_v[l], beta_v[l], w_s[l], b_s[l], w_out[l])
        mod = _modulation(c_pad, w_ada[l], b_ada[l])
        shift, scale, gate = (mod[:, i * D_MODEL:(i + 1) * D_MODEL] for i in range(3))
        mp = [m[:bp].reshape(bp, 1, D_MODEL) for m in (shift, scale, gate)]
        ms = [m[bp:bp + bs].reshape(bs // seq_per_tile, seq_per_tile, D_MODEL)
              for m in (shift, scale, gate)]
        yp, conv_p, ssm_p = _layer(
            yp, *mp, None, None, wl, nseq=1, seglen=TILE_ROWS, lc_ssd=min(lp, SSD_CHUNK),
            lc_gmlp=min(lp, GMLP_CHUNK), nt=lp // TILE_ROWS, emit_v=False)
        ys, conv_s, ssm_s, v_s = _layer(
            ys, *ms,
            state_conv[l].reshape(bs // seq_per_tile, seq_per_tile, CONV_W - 1, CONV_DIM),
            state_ssm, wl,
            nseq=seq_per_tile, seglen=ls, lc_ssd=min(ls, SSD_CHUNK), lc_gmlp=min(ls, GMLP_CHUNK),
            nt=1, emit_v=True, layer=l)
        outs[0].append(conv_p.reshape(bp, CONV_W - 1, CONV_DIM))
        outs[1].append(ssm_p)
        outs[2].append(conv_s.reshape(bs, CONV_W - 1, CONV_DIM))
        outs[3].append(ssm_s)
        outs[4].append(v_s.reshape(bs, ls, GMLP_WIDTH))
    stacked = [o[0][None] if depth == 1 else jnp.stack(o) for o in outs]
    return (yp.reshape(bp, lp, D_MODEL), ys.reshape(bs, ls, D_MODEL), *stacked)
```

```python
import collections
import functools

import jax
import jax.numpy as jnp
import numpy as np
from jax import lax
from jax.experimental import pallas as pl
from jax.experimental.pallas import tpu as pltpu

F32 = jnp.float32
BF16 = jnp.bfloat16

D_MODEL = 1024
SSD_WIDTH = 1024
HEAD_DIM = 64
HEADS = SSD_WIDTH // HEAD_DIM
GROUPS = 2
GROUP_WIDTH = SSD_WIDTH // GROUPS
STATE = 128
CONV_W = 4
CONV_DIM = SSD_WIDTH + 2 * GROUPS * STATE
GMLP_WIDTH = 1024
GMLP_GROUPS = 8
GMLP_GROUP_DIM = GMLP_WIDTH // GMLP_GROUPS
SSD_CHUNK = 64
GMLP_CHUNK = 128
EPS = 1e-6
LOG2E = 1.4426950408889634

LANES = 128
MXU_COLS = 256
HALF = LANES // 2
PAIRS = HEADS // 2
PAIRS_PER_GROUP = PAIRS // GROUPS
PIECES = 3
TILE_ROWS = 256
TAIL_ROWS = 8
TAP0 = TAIL_ROWS - (CONV_W - 1)
MOD_BLOCK_COLS = 512
VMEM_LIMIT_BYTES = 60 * 1024 * 1024

OFF_Z = 0
OFF_XBC = OFF_Z + SSD_WIDTH
OFF_U = OFF_XBC + CONV_DIM
OFF_V = OFF_U + GMLP_WIDTH
OFF_G = OFF_V + GMLP_WIDTH
OFF_DT = OFF_G + GMLP_WIDTH
W_CAT_COLS = OFF_DT + LANES

Bufs = collections.namedtuple("Bufs", "proj xs bc mix")


def _dot(a, b):
    return jnp.dot(a, b, preferred_element_type=F32)


def _dot_nt(a, b):
    return lax.dot_general(a, b, (((1,), (1,)), ((), ())), preferred_element_type=F32)


def _dot_tn(a, b):
    return lax.dot_general(a, b, (((0,), (0,)), ((), ())), preferred_element_type=F32)


def _cat3(x):
    hi = x.astype(BF16).astype(F32)
    r1 = x - hi
    mid = r1.astype(BF16).astype(F32)
    lo = r1 - mid
    return hi + pltpu.roll(mid, HEADS, axis=1) + pltpu.roll(lo, 2 * HEADS, axis=1)


def _silu(x):
    return x / (1.0 + jnp.exp2(x * (-LOG2E)))


def _softplus(x):
    return jnp.maximum(x, 0.0) + jnp.log1p(jnp.exp(-jnp.abs(x)))


def _row_mean(x):
    width = x.shape[-1]
    part = x[:, 0:LANES]
    for c in range(LANES, width, LANES):
        part = part + x[:, c:c + LANES]
    return jnp.sum(part, axis=-1, keepdims=True) * (1.0 / width)


def _rms_scale(x):
    return lax.rsqrt(_row_mean(x * x) + EPS)


def _mod_kernel(c_ref, w_ref, b_ref, o_ref):
    c = _silu(c_ref[...]).astype(BF16)
    o_ref[...] = _dot(c, w_ref[...].astype(BF16)) + b_ref[...]


def _modulation(c, w_ada, b_ada):
    rows = c.shape[0]
    n = w_ada.shape[1]
    return pl.pallas_call(
        _mod_kernel,
        grid=(n // MOD_BLOCK_COLS,),
        in_specs=[
            pl.BlockSpec((rows, D_MODEL), lambda j: (0, 0)),
            pl.BlockSpec((D_MODEL, MOD_BLOCK_COLS), lambda j: (0, j)),
            pl.BlockSpec((1, MOD_BLOCK_COLS), lambda j: (0, j)),
        ],
        out_specs=pl.BlockSpec((rows, MOD_BLOCK_COLS), lambda j: (0, j)),
        out_shape=jax.ShapeDtypeStruct((rows, n), F32),
        name="adaln_mod",
    )(c, w_ada, b_ada.reshape(1, n))


CAST_BLOCK_COLS = 512
A_COLS = SSD_WIDTH + CONV_DIM
B_COLS = 3 * GMLP_WIDTH


def _cast_kernel(a_ref, b_ref, b_next_ref, wa_ref, wb_ref, wdt_ref):
    j = pl.program_id(0)
    wa_ref[...] = a_ref[...].astype(BF16)
    wide = jnp.concatenate([b_ref[...], b_next_ref[...]], axis=1)
    shifted = pltpu.roll(wide, wide.shape[1] - HEADS, axis=1)
    wb_ref[...] = shifted[:, :CAST_BLOCK_COLS].astype(BF16)

    @pl.when(j == 0)
    def _():
        head = b_ref[:, 0:LANES]
        lane = lax.broadcasted_iota(jnp.int32, head.shape, 1)
        wdt_ref[...] = jnp.where(lane < HEADS, head, 0.0).astype(BF16)


def _cast_weights(w_in):
    a_blocks = A_COLS // CAST_BLOCK_COLS
    b_blocks = B_COLS // CAST_BLOCK_COLS
    lanes_per_block = CAST_BLOCK_COLS // LANES
    return pl.pallas_call(
        _cast_kernel,
        grid=(b_blocks,),
        in_specs=[
            pl.BlockSpec((D_MODEL, CAST_BLOCK_COLS), lambda j: (0, jnp.minimum(j, a_blocks - 1))),
            pl.BlockSpec((D_MODEL, CAST_BLOCK_COLS), lambda j: (0, a_blocks + j)),
            pl.BlockSpec((D_MODEL, LANES), lambda j: (0, (a_blocks + j + 1) * lanes_per_block)),
        ],
        out_specs=[
            pl.BlockSpec((D_MODEL, CAST_BLOCK_COLS), lambda j: (0, jnp.minimum(j, a_blocks - 1))),
            pl.BlockSpec((D_MODEL, CAST_BLOCK_COLS), lambda j: (0, j)),
            pl.BlockSpec((D_MODEL, LANES), lambda j: (0, 0)),
        ],
        out_shape=[jax.ShapeDtypeStruct((D_MODEL, A_COLS), BF16),
                   jax.ShapeDtypeStruct((D_MODEL, B_COLS), BF16),
                   jax.ShapeDtypeStruct((D_MODEL, LANES), BF16)],
        compiler_params=pltpu.CompilerParams(dimension_semantics=("arbitrary",)),
        name="cast_proj_weight",
    )(w_in, w_in, w_in)


N_SHARED_SCRATCH = 16
N_STAGE_SCRATCH = 8


def _interleave(primary, secondary):
    total = sum(weight for _, weight in primary)
    groups = []
    done = 0
    seen = 0
    for task, weight in primary:
        seen += weight
        want = (seen * len(secondary)) // total
        groups.append([task] + secondary[done:want])
        done = want
    return groups


def _tag(task, name, needs=()):
    task.name, task.needs = name, tuple(needs)
    return task


def _run_groups(groups):
    stored = set()
    for group in groups:
        for task in group:
            missing = [n for n in getattr(task, "needs", ()) if n not in stored]
            assert not missing, (getattr(task, "name", task), missing)
        commits = [task() for task in group]
        for commit in commits:
            if commit is not None:
                commit()
        stored.update(getattr(task, "name", None) for task in group)


def _layer_kernel(*refs, nseq, seglen, lc, nt, pipelined, emit_v):
    statics = dict(nseq=nseq, seglen=seglen, lc=lc, pipelined=pipelined, emit_v=emit_v)
    shared, stage = refs[:-N_STAGE_SCRATCH], refs[-N_STAGE_SCRATCH:]
    buf_a, buf_b = Bufs(*stage[0::2]), Bufs(*stage[1::2])
    n_scratch = N_SHARED_SCRATCH + N_STAGE_SCRATCH
    ssm_out_ref = refs[len(refs) - n_scratch - (2 if emit_v else 1)]
    tail_ref, st_ref = refs[-n_scratch + 1], refs[-n_scratch + 3]
    if not pipelined:
        _tile_body(shared, buf_a, buf_a, buf_a, buf_a, **statics)
        return

    j = pl.program_id(0)
    m_first = lax.rem(j, nt) == 0
    v_first = (j == 0) | (lax.rem(j - 1, nt) == 0)
    v_last = (j > 0) & (lax.rem(j - 1, nt) == nt - 1)

    @pl.when(j == 0)
    def _():
        for ref in buf_b:
            ref[...] = jnp.zeros_like(ref)

    @pl.when(m_first)
    def _():
        tail_ref[...] = jnp.zeros_like(tail_ref)

    @pl.when(v_first)
    def _():
        st_ref[...] = jnp.zeros_like(st_ref)

    @pl.when(lax.rem(j, 2) == 0)
    def _():
        _tile_body(shared, buf_a, buf_b, buf_a, buf_b, **statics)

    @pl.when(lax.rem(j, 2) == 1)
    def _():
        _tile_body(shared, buf_b, buf_a, buf_b, buf_a, **statics)

    @pl.when(v_last)
    def _():
        ssm_out_ref[0] = st_ref[...].T.reshape(HEADS, HEAD_DIM, STATE)


def _tile_body(refs, bm, bv, bvo, bo, *, nseq, seglen, lc, pipelined, emit_v):
    has_state = not pipelined
    it = iter(refs)
    x_ref, xo_ref = next(it), next(it)
    shift_ref, scale_ref, gate_ref = next(it), next(it), next(it)
    conv0_ref = next(it) if has_state else None
    ssm0_ref = next(it) if has_state else None
    (gpre_ref, gpost_ref, wa_ref, wb_ref, wdt_ref, convw_ref, convb_ref, dtb_ref, alog_ref,
     dskip_ref, gv_ref, betav_ref, mixw_ref, mixb_ref, wout_ref, expand_ref, tril_ref, psel_ref) = (
         next(it) for _ in range(18))
    y_ref, conv_out_ref, ssm_out_ref = next(it), next(it), next(it)
    vn_out_ref = next(it) if emit_v else None
    (h_ref, tail_ref, xp_ref, st_ref, stnew_ref, aexp_ref, xdt_ref, xlo_ref, xhi_ref, acumt_ref,
     vnb_ref, mo_ref, stb_ref, ys_ref, cb_ref, yoff_ref) = (next(it) for _ in range(N_SHARED_SCRATCH))
    pm, pv = bm.proj, bv.proj

    rows_all = slice(TAIL_ROWS, TAIL_ROWS + TILE_ROWS)
    nchunk = seglen // lc
    pad_rows = HALF - lc

    def pad_to_half(v):
        if pad_rows == 0:
            return [v]
        return [v, jnp.zeros((pad_rows, v.shape[1]), v.dtype)]

    def m_norm():
        for s in range(nseq):
            rows = slice(s * seglen, (s + 1) * seglen)
            xr = x_ref[rows, :]
            gain = gpre_ref[...] * (1.0 + scale_ref[s:s + 1, :])
            hs = (xr * _rms_scale(xr)) * gain + shift_ref[s:s + 1, :]
            h_ref[rows, :] = hs.astype(BF16)

    def m_dot(w_ref, w0, dst0, width):
        def task():
            val = _dot(h_ref[...], w_ref[:, w0:w0 + width])

            def commit():
                pm[rows_all, dst0:dst0 + width] = val
            return commit
        return _tag(task, f"m_dot{dst0 // MXU_COLS}")

    dst_starts = (list(range(OFF_XBC, OFF_U, MXU_COLS)) + list(range(OFF_Z, OFF_XBC, MXU_COLS))
                  + list(range(OFF_U, OFF_DT, MXU_COLS)))
    m_dots = [m_dot(wa_ref, d, d, MXU_COLS) if d < OFF_U else m_dot(wb_ref, d - OFF_U, d, MXU_COLS)
              for d in dst_starts] + [m_dot(wdt_ref, 0, OFF_DT, LANES)]

    def conv_store(c, val, rows):
        if c < SSD_WIDTH // LANES:
            bm.xs[rows, c * LANES:(c + 1) * LANES] = val
        else:
            c2 = c - SSD_WIDTH // LANES
            bm.bc[rows, c2 * LANES:(c2 + 1) * LANES] = val.astype(BF16)

    def m_conv(c):
        cols = slice(OFF_XBC + c * LANES, OFF_XBC + (c + 1) * LANES)
        wcols = slice(c * LANES, (c + 1) * LANES)

        def task_stream():
            pm[0:TAIL_ROWS, cols] = tail_ref[:, wcols]
            xfull = pm[:, cols]
            acc = convw_ref[0:1, wcols] * xfull
            for k in range(1, CONV_W):
                acc = pltpu.roll(acc, 1, axis=0) + convw_ref[k:k + 1, wcols] * xfull
            val = _silu(acc[TAIL_ROWS:, :] + convb_ref[:, wcols])
            tail_ref[:, wcols] = pm[TILE_ROWS:TILE_ROWS + TAIL_ROWS, cols]
            conv_out_ref[0, :, wcols] = pm[TAIL_ROWS + TILE_ROWS - (CONV_W - 1):TAIL_ROWS + TILE_ROWS, cols]

            def commit():
                conv_store(c, val, slice(0, TILE_ROWS))
            return commit

        def task_state():
            for s in range(nseq):
                rows = slice(s * seglen, (s + 1) * seglen)
                xp_ref[TAP0:TAIL_ROWS, :] = conv0_ref[s, :, wcols]
                xp_ref[TAIL_ROWS:TAIL_ROWS + seglen, :] = pm[TAIL_ROWS + s * seglen:TAIL_ROWS + (s + 1) * seglen, cols]
                acc = convb_ref[:, wcols]
                for k in range(CONV_W):
                    acc = acc + convw_ref[k:k + 1, wcols] * xp_ref[TAP0 + k:TAP0 + k + seglen, :]
                conv_store(c, _silu(acc), rows)
                conv_out_ref[s, :, wcols] = xp_ref[TAIL_ROWS + seglen - (CONV_W - 1):TAIL_ROWS + seglen, :]

        return task_stream if pipelined else task_state

    m_convs = [_tag(m_conv(c), f"m_conv{c}", [f"m_dot{(OFF_XBC + c * LANES) // MXU_COLS}"])
               for c in range(CONV_DIM // LANES)]

    lane1 = lax.broadcasted_iota(jnp.int32, (TILE_ROWS, LANES), 1)

    def v_dt():
        a_row = -jnp.exp(alog_ref[...])
        dt = _softplus(pv[rows_all, OFF_DT:OFF_DT + LANES] + dtb_ref[...])
        dt = jnp.where(lane1 < HEADS, dt, 0.0)
        csum = _dot(tril_ref[...], _cat3(dt * a_row).astype(BF16))
        a_cum = csum + pltpu.roll(csum, LANES - HEADS, axis=1) + pltpu.roll(csum, LANES - 2 * HEADS, axis=1)
        a_cat = _cat3(jnp.where(lane1 < HEADS, a_cum, 0.0))
        aexp_ref[...] = _dot(a_cat.astype(BF16), expand_ref[...])
        dt_exp = _dot(_cat3(dt).astype(BF16), expand_ref[...])
        even = lax.rem(lane1, 2) == 0
        a_even = jnp.where(even, a_cat, 0.0)
        a_odd = (a_cat - a_even).astype(BF16)
        a_even = a_even.astype(BF16)
        parts = []
        for ck in range(TILE_ROWS // lc):
            rows = slice(ck * lc, (ck + 1) * lc)
            parts += pad_to_half(a_even[rows, :]) + pad_to_half(a_odd[rows, :])
        acumt_ref[...] = _dot_nt(psel_ref[...], jnp.concatenate(parts, axis=0))
        xdt = bv.xs[...] * dt_exp
        lane_w = lax.broadcasted_iota(jnp.int32, (TILE_ROWS, SSD_WIDTH), 1)
        xlo = jnp.where(lax.rem(lane_w, LANES) < HALF, xdt, 0.0)
        xdt_ref[...] = xdt
        xlo_ref[...] = xlo.astype(BF16)
        xhi_ref[...] = (xdt - xlo).astype(BF16)

    li = lax.broadcasted_iota(jnp.int32, (lc, LANES), 0)
    si = lax.rem(lax.broadcasted_iota(jnp.int32, (lc, LANES), 1), HALF)
    causal = (si <= li) & (si < lc)

    def v_local(ci, g):
        rows = slice(ci * lc, (ci + 1) * lc)
        hl = slice(g * GROUP_WIDTH, (g + 1) * GROUP_WIDTH)

        def products():
            last = aexp_ref[(ci + 1) * lc - 1:(ci + 1) * lc, hl]
            xw_b = (xdt_ref[rows, hl] * jnp.exp(last - aexp_ref[rows, hl])).astype(BF16)
            b_g = bv.bc[rows, g * STATE:(g + 1) * STATE]
            c_g = bv.bc[rows, (GROUPS + g) * STATE:(GROUPS + g + 1) * STATE]
            b2 = jnp.concatenate(pad_to_half(b_g) + pad_to_half(b_g), axis=0)
            cb_ref[ci * GROUPS + g] = _dot_nt(c_g, b2)
            stnew_ref[ci, :, hl] = _dot_tn(b_g, xw_b)

        def diag():
            cb2 = cb_ref[ci * GROUPS + g]
            for i in range(PAIRS_PER_GROUP):
                p = g * PAIRS_PER_GROUP + i
                pcols = slice(p * LANES, (p + 1) * LANES)
                seg = aexp_ref[rows, pcols] - acumt_ref[p:p + 1, ci * LANES:(ci + 1) * LANES]
                w_p = (cb2 * jnp.exp(jnp.where(causal, seg, -jnp.inf))).astype(BF16)
                bd = jnp.concatenate(pad_to_half(xlo_ref[rows, pcols]) + pad_to_half(xhi_ref[rows, pcols]),
                                     axis=0)
                ys_ref[rows, pcols] = _dot(w_p, bd) + dskip_ref[:, pcols] * bv.xs[rows, pcols]
        return products, diag

    def v_state(ci, s):
        def task():
            e_last = jnp.exp(aexp_ref[(ci + 1) * lc - 1:(ci + 1) * lc, :])
            st = st_ref[...]
            stb_ref[ci] = st.astype(BF16)
            st_ref[...] = st * e_last + stnew_ref[ci]
            if has_state and ci % nchunk == nchunk - 1:
                ssm_out_ref[s] = st_ref[...].T.reshape(HEADS, HEAD_DIM, STATE)
        return task

    def v_readout(ci, g):
        rows = slice(ci * lc, (ci + 1) * lc)
        prow = slice(TAIL_ROWS + ci * lc, TAIL_ROWS + (ci + 1) * lc)
        hl = slice(g * GROUP_WIDTH, (g + 1) * GROUP_WIDTH)

        def product():
            c_g = bv.bc[rows, (GROUPS + g) * STATE:(GROUPS + g + 1) * STATE]
            yoff_ref[rows, hl] = _dot(c_g, stb_ref[ci, :, hl])

        def gate():
            y = ys_ref[rows, hl] + jnp.exp(aexp_ref[rows, hl]) * yoff_ref[rows, hl]
            zg = _silu(pv[prow, OFF_Z + g * GROUP_WIDTH:OFF_Z + (g + 1) * GROUP_WIDTH])
            bvo.mix[rows, hl] = (y * zg).astype(BF16)
        return product, gate

    def v_state_load(s):
        def task():
            st_ref[...] = ssm0_ref[s].reshape(SSD_WIDTH, STATE).T
        return task

    def v_ln():
        v = pv[rows_all, OFF_V:OFF_V + GMLP_WIDTH]
        vc = v - _row_mean(v)
        vn = (vc * _rms_scale(vc)) * gv_ref[...] + betav_ref[...]
        if emit_v:
            vn_out_ref[...] = vn
        vnb_ref[...] = vn.astype(BF16)

    def v_gmlp(g):
        gcols = slice(g * GMLP_GROUP_DIM, (g + 1) * GMLP_GROUP_DIM)

        def task():
            if mixw_ref.shape[1] == TILE_ROWS:
                mixed = _dot(mixw_ref[g], vnb_ref[:, gcols])
            else:
                lcg = mixw_ref.shape[1]
                blocks = [vnb_ref[r:r + lcg, gcols] for r in range(0, TILE_ROWS, lcg)]
                wide = _dot(mixw_ref[g], jnp.concatenate(blocks, axis=1))
                mixed = jnp.concatenate(
                    [wide[:, k * GMLP_GROUP_DIM:(k + 1) * GMLP_GROUP_DIM] for k in range(len(blocks))],
                    axis=0)
            mixed = mixed + mixb_ref[:, gcols]
            u = pv[rows_all, OFF_U + g * GMLP_GROUP_DIM:OFF_U + (g + 1) * GMLP_GROUP_DIM]
            gt = pv[rows_all, OFF_G + g * GMLP_GROUP_DIM:OFF_G + (g + 1) * GMLP_GROUP_DIM]
            bvo.mix[:, SSD_WIDTH + g * GMLP_GROUP_DIM:SSD_WIDTH + (g + 1) * GMLP_GROUP_DIM] = (
                _silu(gt) * u * mixed).astype(BF16)
        return task

    chunk_groups = [(ci, g) for ci in range(nseq * nchunk) for g in range(GROUPS)]
    v_locals = [v_local(ci, g) for ci, g in chunk_groups]
    v_states = []
    for s in range(nseq):
        if has_state:
            v_states.append(v_state_load(s))
        v_states += [v_state(s * nchunk + c, s) for c in range(nchunk)]
    v_readouts = [v_readout(ci, g) for ci, g in chunk_groups]
    v_scan = ([(t[0], 0) for t in v_locals] + [(t[1], 1) for t in v_locals]
              + [(t, 1) for t in v_states]
              + [(t[0], 0) for t in v_readouts] + [(t[1], 1) for t in v_readouts])
    v_gmlps = [v_gmlp(g) for g in range(GMLP_GROUPS)]

    def o_out(n):
        def task():
            mo_ref[:, n * MXU_COLS:(n + 1) * MXU_COLS] = _dot(
                bo.mix[...], wout_ref[:, n * MXU_COLS:(n + 1) * MXU_COLS])
        return task

    def o_final():
        mo = mo_ref[...]
        mo = mo * _rms_scale(mo)
        for s in range(nseq):
            rows = slice(s * seglen, (s + 1) * seglen)
            gain = gpost_ref[...] * gate_ref[s:s + 1, :]
            y_ref[rows, :] = xo_ref[rows, :] + gain * mo[rows, :]

    o_outs = [_tag(o_out(n), f"o_out{n}") for n in range(D_MODEL // MXU_COLS)]
    _tag(o_final, "o_final", [t.name for t in o_outs])

    if pipelined:
        n_xbc_dots = CONV_DIM // MXU_COLS
        head = _interleave([(m_norm, 1), (v_dt, 2), (v_ln, 2)], o_outs + m_dots[:n_xbc_dots])
        body = [(t, 3) for t in m_convs] + [(o_final, 3)] + v_scan + [(t, 0) for t in v_gmlps]
        groups = head + _interleave(body, m_dots[n_xbc_dots:])
    else:
        order = ([m_norm] + m_dots + m_convs + [v_dt, v_ln] + [t for t, _ in v_scan] + v_gmlps
                 + o_outs + [o_final])
        groups = [[task] for task in order]
    _run_groups(groups)


def _const_spec(shape):
    nd = len(shape)
    return pl.BlockSpec(shape, lambda j: (0,) * nd, pipeline_mode=pl.Buffered(1))


def _chunk_constants(lc):
    expand = np.zeros((LANES, SSD_WIDTH), np.float32)
    psel = np.zeros((PAIRS, LANES), np.float32)
    for piece in range(PIECES):
        for hd in range(HEADS):
            expand[piece * HEADS + hd, hd * HEAD_DIM:(hd + 1) * HEAD_DIM] = 1.0
            psel[hd // 2, piece * HEADS + hd] = 1.0
    r = np.arange(TILE_ROWS)
    tril = ((r[:, None] >= r[None, :]) & (r[:, None] // lc == r[None, :] // lc)).astype(np.float32)
    return jnp.asarray(expand, BF16), jnp.asarray(tril, BF16), jnp.asarray(psel, BF16)


def _mix_weights(w_s, b_s, lc):
    nck = TILE_ROWS // lc
    mask = jnp.tril(jnp.ones((lc, lc), dtype=bool))
    w = jnp.where(mask[None], w_s[:, :lc, :lc], 0)
    if nck * GMLP_GROUP_DIM > MXU_COLS:
        eye = jnp.eye(nck, dtype=w.dtype)
        w = jnp.einsum("ab,gts->gatbs", eye, w).reshape(GMLP_GROUPS, TILE_ROWS, TILE_ROWS)
    bias = jnp.repeat(b_s[:, :lc].T, GMLP_GROUP_DIM, axis=1)
    return w.astype(BF16), jnp.tile(bias, (nck, 1))


def _layer(x, shift, scale, gate, conv0, ssm0, wl, *, nseq, seglen, lc_ssd, lc_gmlp, nt, emit_v,
           layer=0):
    (g_pre, g_post, w_a, w_b, w_dt, conv_w, conv_b, dtb, alog, dskip, g_v, beta_v, w_s, b_s,
     w_out) = wl
    assert lc_ssd == lc_gmlp or nseq == 1
    pipelined = conv0 is None
    nb = shift.shape[0]
    rows_total = x.shape[0]
    ntiles = rows_total // TILE_ROWS
    assert nseq * seglen == TILE_ROWS and nb * nt == ntiles
    expand, tril, psel = _chunk_constants(lc_ssd)
    mixw, mixb = _mix_weights(w_s, b_s, lc_gmlp)
    nck = TILE_ROWS // lc_ssd

    if pipelined:
        def tile_m(j):
            return jnp.minimum(j, ntiles - 1)

        def tile_v(j):
            return jnp.clip(j - 1, 0, ntiles - 1)

        def tile_o(j):
            return jnp.maximum(j - 2, 0)
        steps = ntiles + 2
    else:
        def tile_m(j):
            return j
        tile_v = tile_o = tile_m
        steps = ntiles

    def mod_spec(tile_of):
        return pl.BlockSpec((None, nseq, D_MODEL), lambda j: (tile_of(j) // nt, 0, 0))

    in_specs = [pl.BlockSpec((TILE_ROWS, D_MODEL), lambda j: (tile_m(j), 0)),
                pl.BlockSpec((TILE_ROWS, D_MODEL), lambda j: (tile_o(j), 0)),
                mod_spec(tile_m), mod_spec(tile_m), mod_spec(tile_o)]
    args = [x, x, shift, scale, gate]
    if not pipelined:
        in_specs += [pl.BlockSpec((None, nseq, CONV_W - 1, CONV_DIM), lambda j: (j // nt, 0, 0, 0)),
                     pl.BlockSpec((None, nseq, HEADS, HEAD_DIM, STATE), lambda j: (layer, j, 0, 0, 0))]
        args += [conv0, ssm0]
    consts = [g_pre, g_post, w_a, w_b, w_dt, conv_w, conv_b, dtb, alog, dskip, g_v, beta_v, mixw,
              mixb, w_out, expand, tril, psel]
    in_specs += [_const_spec(a.shape) for a in consts]
    args += consts

    out_shape = [jax.ShapeDtypeStruct((rows_total, D_MODEL), F32),
                 jax.ShapeDtypeStruct((nb, nseq, CONV_W - 1, CONV_DIM), F32),
                 jax.ShapeDtypeStruct((nb * nseq, HEADS, HEAD_DIM, STATE), F32)]
    out_specs = [pl.BlockSpec((TILE_ROWS, D_MODEL), lambda j: (tile_o(j), 0)),
                 pl.BlockSpec((None, nseq, CONV_W - 1, CONV_DIM), lambda j: (tile_m(j) // nt, 0, 0, 0)),
                 pl.BlockSpec((nseq, HEADS, HEAD_DIM, STATE), lambda j: (tile_v(j) // nt, 0, 0, 0))]
    if emit_v:
        out_shape.append(jax.ShapeDtypeStruct((rows_total, GMLP_WIDTH), F32))
        out_specs.append(pl.BlockSpec((TILE_ROWS, GMLP_WIDTH), lambda j: (tile_v(j), 0)))

    spare = (TAIL_ROWS, LANES)
    stage_shapes = [((TAIL_ROWS + TILE_ROWS, W_CAT_COLS), F32),
                    ((TILE_ROWS, SSD_WIDTH), F32),
                    ((TILE_ROWS, 2 * GROUPS * STATE), BF16),
                    ((TILE_ROWS, SSD_WIDTH + GMLP_WIDTH), BF16)]
    scratch = [pltpu.VMEM((TILE_ROWS, D_MODEL), BF16),
               pltpu.VMEM((TAIL_ROWS, CONV_DIM), F32),
               pltpu.VMEM((TAIL_ROWS + seglen, LANES), F32),
               pltpu.VMEM((STATE, SSD_WIDTH), F32),
               pltpu.VMEM((nck, STATE, SSD_WIDTH), F32),
               pltpu.VMEM((TILE_ROWS, SSD_WIDTH), F32),
               pltpu.VMEM((TILE_ROWS, SSD_WIDTH), F32),
               pltpu.VMEM((TILE_ROWS, SSD_WIDTH), BF16),
               pltpu.VMEM((TILE_ROWS, SSD_WIDTH), BF16),
               pltpu.VMEM((PAIRS, nck * LANES), F32),
               pltpu.VMEM((TILE_ROWS, GMLP_WIDTH), BF16),
               pltpu.VMEM((TILE_ROWS, D_MODEL), F32),
               pltpu.VMEM((nck, STATE, SSD_WIDTH), BF16),
               pltpu.VMEM((TILE_ROWS, SSD_WIDTH), F32),
               pltpu.VMEM((nck * GROUPS, lc_ssd, LANES), F32),
               pltpu.VMEM((TILE_ROWS, SSD_WIDTH), F32)]
    assert len(scratch) == N_SHARED_SCRATCH
    for shape, dtype in stage_shapes:
        scratch += [pltpu.VMEM(shape, dtype), pltpu.VMEM(shape if pipelined else spare, dtype)]
    kern = functools.partial(_layer_kernel, nseq=nseq, seglen=seglen, lc=lc_ssd, nt=nt,
                             pipelined=pipelined, emit_v=emit_v)
    return pl.pallas_call(
        kern,
        grid=(steps,),
        in_specs=in_specs,
        out_specs=out_specs,
        out_shape=out_shape,
        scratch_shapes=scratch,
        compiler_params=pltpu.CompilerParams(
            dimension_semantics=("arbitrary",),
            vmem_limit_bytes=VMEM_LIMIT_BYTES),
        name="hybrid_layer_stream" if pipelined else "hybrid_layer_state",
    )(*args)


def _prep_weights(g_pre, g_post, w_in, conv_w, conv_b, dt_bias, a_log, d_skip, g_v, beta_v,
                  w_s, b_s, w_out):
    w_a, w_b, w_dt = _cast_weights(w_in)
    pad = (0, LANES - HEADS)
    return (g_pre.reshape(1, -1), g_post.reshape(1, -1), w_a, w_b, w_dt, conv_w, conv_b.reshape(1, -1),
            jnp.pad(dt_bias, pad).reshape(1, -1), jnp.pad(a_log, pad).reshape(1, -1),
            jnp.repeat(d_skip, HEAD_DIM).reshape(1, -1), g_v.reshape(1, -1),
            beta_v.reshape(1, -1), w_s, b_s, w_out.astype(BF16))


def kernel(x_prompt, x_sample, state_conv, state_ssm, c_prompt, c_sample, w_ada, b_ada, g_pre,
           g_post, w_in, conv_w, conv_b, dt_bias, a_log, d_skip, g_v, beta_v, w_s, b_s, w_out):
    depth = w_ada.shape[0]
    bp, lp, _ = x_prompt.shape
    bs, ls, _ = x_sample.shape
    seq_per_tile = TILE_ROWS // ls
    yp = x_prompt.reshape(bp * lp, D_MODEL)
    ys = x_sample.reshape(bs * ls, D_MODEL)
    c_all = jnp.concatenate([c_prompt, c_sample], axis=0)
    c_pad = jnp.pad(c_all, ((0, (-c_all.shape[0]) % 8), (0, 0)))
    outs = [[] for _ in range(5)]
    for l in range(depth):
        wl = _prep_weights(g_pre[l], g_post[l], w_in[l], conv_w[l], conv_b[l], dt_bias[l],
                           a_log[l], d_skip[l], g_v[l], beta_v[l], w_s[l], b_s[l], w_out[l])
        mod = _modulation(c_pad, w_ada[l], b_ada[l])
        shift, scale, gate = (mod[:, i * D_MODEL:(i + 1) * D_MODEL] for i in range(3))
        mp = [m[:bp].reshape(bp, 1, D_MODEL) for m in (shift, scale, gate)]
        ms = [m[bp:bp + bs].reshape(bs // seq_per_tile, seq_per_tile, D_MODEL)
              for m in (shift, scale, gate)]
        yp, conv_p, ssm_p = _layer(
            yp, *mp, None, None, wl, nseq=1, seglen=TILE_ROWS, lc_ssd=min(lp, SSD_CHUNK),
            lc_gmlp=min(lp, GMLP_CHUNK), nt=lp // TILE_ROWS, emit_v=False)
        ys, conv_s, ssm_s, v_s = _layer(
            ys, *ms,
            state_conv[l].reshape(bs // seq_per_tile, seq_per_tile, CONV_W - 1, CONV_DIM),
            state_ssm, wl,
            nseq=seq_per_tile, seglen=ls, lc_ssd=min(ls, SSD_CHUNK), lc_gmlp=min(ls, GMLP_CHUNK),
            nt=1, emit_v=True, layer=l)
        outs[0].append(conv_p.reshape(bp, CONV_W - 1, CONV_DIM))
        outs[1].append(ssm_p)
        outs[2].append(conv_s.reshape(bs, CONV_W - 1, CONV_DIM))
        outs[3].append(ssm_s)
        outs[4].append(v_s.reshape(bs, ls, GMLP_WIDTH))
    stacked = [o[0][None] if depth == 1 else jnp.stack(o) for o in outs]
    return (yp.reshape(bp, lp, D_MODEL), ys.reshape(bs, ls, D_MODEL), *stacked)
```

```python
import collections
import functools

import jax
import jax.numpy as jnp
import numpy as np
from jax import lax
from jax.experimental import pallas as pl
from jax.experimental.pallas import tpu as pltpu

F32 = jnp.float32
BF16 = jnp.bfloat16

D_MODEL = 1024
SSD_WIDTH = 1024
HEAD_DIM = 64
HEADS = SSD_WIDTH // HEAD_DIM
GROUPS = 2
GROUP_WIDTH = SSD_WIDTH // GROUPS
STATE = 128
CONV_W = 4
CONV_DIM = SSD_WIDTH + 2 * GROUPS * STATE
GMLP_WIDTH = 1024
GMLP_GROUPS = 8
GMLP_GROUP_DIM = GMLP_WIDTH // GMLP_GROUPS
SSD_CHUNK = 64
GMLP_CHUNK = 128
EPS = 1e-6
LOG2E = 1.4426950408889634

LANES = 128
MXU_COLS = 256
HALF = LANES // 2
PAIRS = HEADS // 2
PAIRS_PER_GROUP = PAIRS // GROUPS
PIECES = 3
TILE_ROWS = 256
TAIL_ROWS = 8
TAP0 = TAIL_ROWS - (CONV_W - 1)
MOD_BLOCK_COLS = 512
VMEM_LIMIT_BYTES = 60 * 1024 * 1024

OFF_Z = 0
OFF_XBC = OFF_Z + SSD_WIDTH
OFF_U = OFF_XBC + CONV_DIM
OFF_V = OFF_U + GMLP_WIDTH
OFF_G = OFF_V + GMLP_WIDTH
OFF_DT = OFF_G + GMLP_WIDTH
W_CAT_COLS = OFF_DT + LANES

Bufs = collections.namedtuple("Bufs", "proj xs bc mix")


def _dot(a, b):
    return jnp.dot(a, b, preferred_element_type=F32)


def _dot_nt(a, b):
    return lax.dot_general(a, b, (((1,), (1,)), ((), ())), preferred_element_type=F32)


def _dot_tn(a, b):
    return lax.dot_general(a, b, (((0,), (0,)), ((), ())), preferred_element_type=F32)


def _cat3(x):
    hi = x.astype(BF16).astype(F32)
    r1 = x - hi
    mid = r1.astype(BF16).astype(F32)
    lo = r1 - mid
    return hi + pltpu.roll(mid, HEADS, axis=1) + pltpu.roll(lo, 2 * HEADS, axis=1)


def _silu(x):
    return x / (1.0 + jnp.exp2(x * (-LOG2E)))


def _softplus(x):
    return jnp.maximum(x, 0.0) + jnp.log1p(jnp.exp(-jnp.abs(x)))


def _row_mean(x):
    width = x.shape[-1]
    part = x[:, 0:LANES]
    for c in range(LANES, width, LANES):
        part = part + x[:, c:c + LANES]
    return jnp.sum(part, axis=-1, keepdims=True) * (1.0 / width)


def _rms_scale(x):
    return lax.rsqrt(_row_mean(x * x) + EPS)


def _mod_kernel(c_ref, w_ref, b_ref, o_ref):
    c = _silu(c_ref[...]).astype(BF16)
    o_ref[...] = _dot(c, w_ref[...].astype(BF16)) + b_ref[...]


def _modulation(c, w_ada, b_ada):
    rows = c.shape[0]
    n = w_ada.shape[1]
    return pl.pallas_call(
        _mod_kernel,
        grid=(n // MOD_BLOCK_COLS,),
        in_specs=[
            pl.BlockSpec((rows, D_MODEL), lambda j: (0, 0)),
            pl.BlockSpec((D_MODEL, MOD_BLOCK_COLS), lambda j: (0, j)),
            pl.BlockSpec((1, MOD_BLOCK_COLS), lambda j: (0, j)),
        ],
        out_specs=pl.BlockSpec((rows, MOD_BLOCK_COLS), lambda j: (0, j)),
        out_shape=jax.ShapeDtypeStruct((rows, n), F32),
        name="adaln_mod",
    )(c, w_ada, b_ada.reshape(1, n))


CAST_BLOCK_COLS = 512
A_COLS = SSD_WIDTH + CONV_DIM
B_COLS = 3 * GMLP_WIDTH


def _cast_kernel(a_ref, b_ref, b_next_ref, wa_ref, wb_ref, wdt_ref):
    j = pl.program_id(0)
    wa_ref[...] = a_ref[...].astype(BF16)
    wide = jnp.concatenate([b_ref[...], b_next_ref[...]], axis=1)
    shifted = pltpu.roll(wide, wide.shape[1] - HEADS, axis=1)
    wb_ref[...] = shifted[:, :CAST_BLOCK_COLS].astype(BF16)

    @pl.when(j == 0)
    def _():
        head = b_ref[:, 0:LANES]
        lane = lax.broadcasted_iota(jnp.int32, head.shape, 1)
        wdt_ref[...] = jnp.where(lane < HEADS, head, 0.0).astype(BF16)


def _cast_weights(w_in):
    a_blocks = A_COLS // CAST_BLOCK_COLS
    b_blocks = B_COLS // CAST_BLOCK_COLS
    lanes_per_block = CAST_BLOCK_COLS // LANES
    return pl.pallas_call(
        _cast_kernel,
        grid=(b_blocks,),
        in_specs=[
            pl.BlockSpec((D_MODEL, CAST_BLOCK_COLS), lambda j: (0, jnp.minimum(j, a_blocks - 1))),
            pl.BlockSpec((D_MODEL, CAST_BLOCK_COLS), lambda j: (0, a_blocks + j)),
            pl.BlockSpec((D_MODEL, LANES), lambda j: (0, (a_blocks + j + 1) * lanes_per_block)),
        ],
        out_specs=[
            pl.BlockSpec((D_MODEL, CAST_BLOCK_COLS), lambda j: (0, jnp.minimum(j, a_blocks - 1))),
            pl.BlockSpec((D_MODEL, CAST_BLOCK_COLS), lambda j: (0, j)),
            pl.BlockSpec((D_MODEL, LANES), lambda j: (0, 0)),
        ],
        out_shape=[jax.ShapeDtypeStruct((D_MODEL, A_COLS), BF16),
                   jax.ShapeDtypeStruct((D_MODEL, B_COLS), BF16),
                   jax.ShapeDtypeStruct((D_MODEL, LANES), BF16)],
        compiler_params=pltpu.CompilerParams(dimension_semantics=("arbitrary",)),
        name="cast_proj_weight",
    )(w_in, w_in, w_in)


TILES_PER_STEP = 2
N_SHARED_SCRATCH = 16
N_STAGE_SCRATCH = 8


def _interleave(primary, secondary):
    total = sum(weight for _, weight in primary)
    groups = []
    done = 0
    seen = 0
    for task, weight in primary:
        seen += weight
        want = (seen * len(secondary)) // total
        groups.append([task] + secondary[done:want])
        done = want
    return groups


def _tag(task, name, needs=()):
    task.name, task.needs = name, tuple(needs)
    return task


def _run_groups(groups):
    stored = set()
    for group in groups:
        for task in group:
            missing = [n for n in getattr(task, "needs", ()) if n not in stored]
            assert not missing, (getattr(task, "name", task), missing)
        commits = [task() for task in group]
        for commit in commits:
            if commit is not None:
                commit()
        stored.update(getattr(task, "name", None) for task in group)


def _layer_kernel(*refs, nseq, seglen, lc, nt, pipelined, emit_v):
    statics = dict(nseq=nseq, seglen=seglen, lc=lc, pipelined=pipelined, emit_v=emit_v)
    shared, stage = refs[:-N_STAGE_SCRATCH], refs[-N_STAGE_SCRATCH:]
    buf_a, buf_b = Bufs(*stage[0::2]), Bufs(*stage[1::2])
    n_scratch = N_SHARED_SCRATCH + N_STAGE_SCRATCH
    ssm_out_ref = refs[len(refs) - n_scratch - (2 if emit_v else 1)]
    tail_ref, st_ref = refs[-n_scratch + 1], refs[-n_scratch + 3]
    if not pipelined:
        _tile_body(shared, buf_a, buf_a, buf_a, buf_a, **statics)
        return

    assert nt % TILES_PER_STEP == 0 and TILES_PER_STEP == 2
    i = pl.program_id(0)
    row_start = lax.rem(TILES_PER_STEP * i, nt) == 0
    n_in = len(shared) - N_SHARED_SCRATCH - 3
    views = []
    for half in range(TILES_PER_STEP):
        rows = slice(half * TILE_ROWS, (half + 1) * TILE_ROWS)
        view = list(shared)
        for k in (0, 1, n_in):
            view[k] = shared[k].at[rows, :]
        views.append(view)

    @pl.when(i == 0)
    def _():
        for ref in buf_b:
            ref[...] = jnp.zeros_like(ref)
        st_ref[...] = jnp.zeros_like(st_ref)

    @pl.when(row_start)
    def _():
        tail_ref[...] = jnp.zeros_like(tail_ref)

    _tile_body(views[0], buf_a, buf_b, buf_a, buf_b, **statics)

    @pl.when(row_start & (i > 0))
    def _():
        ssm_out_ref[0] = st_ref[...].T.reshape(HEADS, HEAD_DIM, STATE)

    @pl.when(row_start)
    def _():
        st_ref[...] = jnp.zeros_like(st_ref)

    _tile_body(views[1], buf_b, buf_a, buf_b, buf_a, **statics)


def _tile_body(refs, bm, bv, bvo, bo, *, nseq, seglen, lc, pipelined, emit_v):
    has_state = not pipelined
    it = iter(refs)
    x_ref, xo_ref = next(it), next(it)
    shift_ref, scale_ref, gate_ref = next(it), next(it), next(it)
    conv0_ref = next(it) if has_state else None
    ssm0_ref = next(it) if has_state else None
    (gpre_ref, gpost_ref, wa_ref, wb_ref, wdt_ref, convw_ref, convb_ref, dtb_ref, alog_ref,
     dskip_ref, gv_ref, betav_ref, mixw_ref, mixb_ref, wout_ref, expand_ref, tril_ref, psel_ref) = (
         next(it) for _ in range(18))
    y_ref, conv_out_ref, ssm_out_ref = next(it), next(it), next(it)
    vn_out_ref = next(it) if emit_v else None
    (h_ref, tail_ref, xp_ref, st_ref, stnew_ref, aexp_ref, xdt_ref, xlo_ref, xhi_ref, acumt_ref,
     vnb_ref, mo_ref, stb_ref, ys_ref, cb_ref, yoff_ref) = (next(it) for _ in range(N_SHARED_SCRATCH))
    pm, pv = bm.proj, bv.proj

    rows_all = slice(TAIL_ROWS, TAIL_ROWS + TILE_ROWS)
    nchunk = seglen // lc
    pad_rows = HALF - lc

    def pad_to_half(v):
        if pad_rows == 0:
            return [v]
        return [v, jnp.zeros((pad_rows, v.shape[1]), v.dtype)]

    def m_norm():
        for s in range(nseq):
            rows = slice(s * seglen, (s + 1) * seglen)
            xr = x_ref[rows, :]
            gain = gpre_ref[...] * (1.0 + scale_ref[s:s + 1, :])
            hs = (xr * _rms_scale(xr)) * gain + shift_ref[s:s + 1, :]
            h_ref[rows, :] = hs.astype(BF16)

    def m_dot(w_ref, w0, dst0, width):
        def task():
            val = _dot(h_ref[...], w_ref[:, w0:w0 + width])

            def commit():
                pm[rows_all, dst0:dst0 + width] = val
            return commit
        return _tag(task, f"m_dot{dst0 // MXU_COLS}")

    dst_starts = (list(range(OFF_XBC, OFF_U, MXU_COLS)) + list(range(OFF_Z, OFF_XBC, MXU_COLS))
                  + list(range(OFF_U, OFF_DT, MXU_COLS)))
    m_dots = [m_dot(wa_ref, d, d, MXU_COLS) if d < OFF_U else m_dot(wb_ref, d - OFF_U, d, MXU_COLS)
              for d in dst_starts] + [m_dot(wdt_ref, 0, OFF_DT, LANES)]

    def conv_store(c, val, rows):
        if c < SSD_WIDTH // LANES:
            bm.xs[rows, c * LANES:(c + 1) * LANES] = val
        else:
            c2 = c - SSD_WIDTH // LANES
            bm.bc[rows, c2 * LANES:(c2 + 1) * LANES] = val.astype(BF16)

    def m_conv(c):
        cols = slice(OFF_XBC + c * LANES, OFF_XBC + (c + 1) * LANES)
        wcols = slice(c * LANES, (c + 1) * LANES)

        def task_stream():
            pm[0:TAIL_ROWS, cols] = tail_ref[:, wcols]
            xfull = pm[:, cols]
            acc = convw_ref[0:1, wcols] * xfull
            for k in range(1, CONV_W):
                acc = pltpu.roll(acc, 1, axis=0) + convw_ref[k:k + 1, wcols] * xfull
            val = _silu(acc[TAIL_ROWS:, :] + convb_ref[:, wcols])
            tail_ref[:, wcols] = pm[TILE_ROWS:TILE_ROWS + TAIL_ROWS, cols]
            conv_out_ref[0, :, wcols] = pm[TAIL_ROWS + TILE_ROWS - (CONV_W - 1):TAIL_ROWS + TILE_ROWS, cols]

            def commit():
                conv_store(c, val, slice(0, TILE_ROWS))
            return commit

        def task_state():
            for s in range(nseq):
                rows = slice(s * seglen, (s + 1) * seglen)
                xp_ref[TAP0:TAIL_ROWS, :] = conv0_ref[s, :, wcols]
                xp_ref[TAIL_ROWS:TAIL_ROWS + seglen, :] = pm[TAIL_ROWS + s * seglen:TAIL_ROWS + (s + 1) * seglen, cols]
                acc = convb_ref[:, wcols]
                for k in range(CONV_W):
                    acc = acc + convw_ref[k:k + 1, wcols] * xp_ref[TAP0 + k:TAP0 + k + seglen, :]
                conv_store(c, _silu(acc), rows)
                conv_out_ref[s, :, wcols] = xp_ref[TAIL_ROWS + seglen - (CONV_W - 1):TAIL_ROWS + seglen, :]

        return task_stream if pipelined else task_state

    m_convs = [_tag(m_conv(c), f"m_conv{c}", [f"m_dot{(OFF_XBC + c * LANES) // MXU_COLS}"])
               for c in range(CONV_DIM // LANES)]

    lane1 = lax.broadcasted_iota(jnp.int32, (TILE_ROWS, LANES), 1)

    def v_dt():
        a_row = -jnp.exp(alog_ref[...])
        dt = _softplus(pv[rows_all, OFF_DT:OFF_DT + LANES] + dtb_ref[...])
        dt = jnp.where(lane1 < HEADS, dt, 0.0)
        csum = _dot(tril_ref[...], _cat3(dt * a_row).astype(BF16))
        a_cum = csum + pltpu.roll(csum, LANES - HEADS, axis=1) + pltpu.roll(csum, LANES - 2 * HEADS, axis=1)
        a_cat = _cat3(jnp.where(lane1 < HEADS, a_cum, 0.0))
        aexp_ref[...] = _dot(a_cat.astype(BF16), expand_ref[...])
        dt_exp = _dot(_cat3(dt).astype(BF16), expand_ref[...])
        even = lax.rem(lane1, 2) == 0
        a_even = jnp.where(even, a_cat, 0.0)
        a_odd = (a_cat - a_even).astype(BF16)
        a_even = a_even.astype(BF16)
        parts = []
        for ck in range(TILE_ROWS // lc):
            rows = slice(ck * lc, (ck + 1) * lc)
            parts += pad_to_half(a_even[rows, :]) + pad_to_half(a_odd[rows, :])
        acumt_ref[...] = _dot_nt(psel_ref[...], jnp.concatenate(parts, axis=0))
        xdt = bv.xs[...] * dt_exp
        lane_w = lax.broadcasted_iota(jnp.int32, (TILE_ROWS, SSD_WIDTH), 1)
        xlo = jnp.where(lax.rem(lane_w, LANES) < HALF, xdt, 0.0)
        xdt_ref[...] = xdt
        xlo_ref[...] = xlo.astype(BF16)
        xhi_ref[...] = (xdt - xlo).astype(BF16)

    li = lax.broadcasted_iota(jnp.int32, (lc, LANES), 0)
    si = lax.rem(lax.broadcasted_iota(jnp.int32, (lc, LANES), 1), HALF)
    causal = (si <= li) & (si < lc)

    def v_local(ci, g):
        rows = slice(ci * lc, (ci + 1) * lc)
        hl = slice(g * GROUP_WIDTH, (g + 1) * GROUP_WIDTH)

        def products():
            last = aexp_ref[(ci + 1) * lc - 1:(ci + 1) * lc, hl]
            xw_b = (xdt_ref[rows, hl] * jnp.exp(last - aexp_ref[rows, hl])).astype(BF16)
            b_g = bv.bc[rows, g * STATE:(g + 1) * STATE]
            c_g = bv.bc[rows, (GROUPS + g) * STATE:(GROUPS + g + 1) * STATE]
            b2 = jnp.concatenate(pad_to_half(b_g) + pad_to_half(b_g), axis=0)
            cb_ref[ci * GROUPS + g] = _dot_nt(c_g, b2)
            stnew_ref[ci, :, hl] = _dot_tn(b_g, xw_b)

        def diag():
            cb2 = cb_ref[ci * GROUPS + g]
            for i in range(PAIRS_PER_GROUP):
                p = g * PAIRS_PER_GROUP + i
                pcols = slice(p * LANES, (p + 1) * LANES)
                seg = aexp_ref[rows, pcols] - acumt_ref[p:p + 1, ci * LANES:(ci + 1) * LANES]
                w_p = (cb2 * jnp.exp(jnp.where(causal, seg, -jnp.inf))).astype(BF16)
                bd = jnp.concatenate(pad_to_half(xlo_ref[rows, pcols]) + pad_to_half(xhi_ref[rows, pcols]),
                                     axis=0)
                ys_ref[rows, pcols] = _dot(w_p, bd) + dskip_ref[:, pcols] * bv.xs[rows, pcols]
        return products, diag

    def v_state(ci, s):
        def task():
            e_last = jnp.exp(aexp_ref[(ci + 1) * lc - 1:(ci + 1) * lc, :])
            st = st_ref[...]
            stb_ref[ci] = st.astype(BF16)
            st_ref[...] = st * e_last + stnew_ref[ci]
            if has_state and ci % nchunk == nchunk - 1:
                ssm_out_ref[s] = st_ref[...].T.reshape(HEADS, HEAD_DIM, STATE)
        return task

    def v_readout(ci, g):
        rows = slice(ci * lc, (ci + 1) * lc)
        prow = slice(TAIL_ROWS + ci * lc, TAIL_ROWS + (ci + 1) * lc)
        hl = slice(g * GROUP_WIDTH, (g + 1) * GROUP_WIDTH)

        def product():
            c_g = bv.bc[rows, (GROUPS + g) * STATE:(GROUPS + g + 1) * STATE]
            yoff_ref[rows, hl] = _dot(c_g, stb_ref[ci, :, hl])

        def gate():
            y = ys_ref[rows, hl] + jnp.exp(aexp_ref[rows, hl]) * yoff_ref[rows, hl]
            zg = _silu(pv[prow, OFF_Z + g * GROUP_WIDTH:OFF_Z + (g + 1) * GROUP_WIDTH])
            bvo.mix[rows, hl] = (y * zg).astype(BF16)
        return product, gate

    def v_state_load(s):
        def task():
            st_ref[...] = ssm0_ref[s].reshape(SSD_WIDTH, STATE).T
        return task

    def v_ln():
        v = pv[rows_all, OFF_V:OFF_V + GMLP_WIDTH]
        vc = v - _row_mean(v)
        vn = (vc * _rms_scale(vc)) * gv_ref[...] + betav_ref[...]
        if emit_v:
            vn_out_ref[...] = vn
        vnb_ref[...] = vn.astype(BF16)

    def v_gmlp(g):
        gcols = slice(g * GMLP_GROUP_DIM, (g + 1) * GMLP_GROUP_DIM)

        def task():
            if mixw_ref.shape[1] == TILE_ROWS:
                mixed = _dot(mixw_ref[g], vnb_ref[:, gcols])
            else:
                lcg = mixw_ref.shape[1]
                blocks = [vnb_ref[r:r + lcg, gcols] for r in range(0, TILE_ROWS, lcg)]
                wide = _dot(mixw_ref[g], jnp.concatenate(blocks, axis=1))
                mixed = jnp.concatenate(
                    [wide[:, k * GMLP_GROUP_DIM:(k + 1) * GMLP_GROUP_DIM] for k in range(len(blocks))],
                    axis=0)
            mixed = mixed + mixb_ref[:, gcols]
            u = pv[rows_all, OFF_U + g * GMLP_GROUP_DIM:OFF_U + (g + 1) * GMLP_GROUP_DIM]
            gt = pv[rows_all, OFF_G + g * GMLP_GROUP_DIM:OFF_G + (g + 1) * GMLP_GROUP_DIM]
            bvo.mix[:, SSD_WIDTH + g * GMLP_GROUP_DIM:SSD_WIDTH + (g + 1) * GMLP_GROUP_DIM] = (
                _silu(gt) * u * mixed).astype(BF16)
        return task

    chunk_groups = [(ci, g) for ci in range(nseq * nchunk) for g in range(GROUPS)]
    v_locals = [v_local(ci, g) for ci, g in chunk_groups]
    v_states = []
    for s in range(nseq):
        if has_state:
            v_states.append(v_state_load(s))
        v_states += [v_state(s * nchunk + c, s) for c in range(nchunk)]
    v_readouts = [v_readout(ci, g) for ci, g in chunk_groups]
    v_scan = ([(t[0], 0) for t in v_locals] + [(t[1], 1) for t in v_locals]
              + [(t, 1) for t in v_states]
              + [(t[0], 0) for t in v_readouts] + [(t[1], 1) for t in v_readouts])
    v_gmlps = [v_gmlp(g) for g in range(GMLP_GROUPS)]

    def o_out(n):
        def task():
            mo_ref[:, n * MXU_COLS:(n + 1) * MXU_COLS] = _dot(
                bo.mix[...], wout_ref[:, n * MXU_COLS:(n + 1) * MXU_COLS])
        return task

    def o_final():
        mo = mo_ref[...]
        mo = mo * _rms_scale(mo)
        for s in range(nseq):
            rows = slice(s * seglen, (s + 1) * seglen)
            gain = gpost_ref[...] * gate_ref[s:s + 1, :]
            y_ref[rows, :] = xo_ref[rows, :] + gain * mo[rows, :]

    o_outs = [_tag(o_out(n), f"o_out{n}") for n in range(D_MODEL // MXU_COLS)]
    _tag(o_final, "o_final", [t.name for t in o_outs])

    if pipelined:
        n_xbc_dots = CONV_DIM // MXU_COLS
        head = _interleave([(m_norm, 1), (v_dt, 2), (v_ln, 2)], o_outs + m_dots[:n_xbc_dots])
        body = [(t, 3) for t in m_convs] + [(o_final, 3)] + v_scan + [(t, 0) for t in v_gmlps]
        groups = head + _interleave(body, m_dots[n_xbc_dots:])
    else:
        order = ([m_norm] + m_dots + m_convs + [v_dt, v_ln] + [t for t, _ in v_scan] + v_gmlps
                 + o_outs + [o_final])
        groups = [[task] for task in order]
    _run_groups(groups)


def _const_spec(shape):
    nd = len(shape)
    return pl.BlockSpec(shape, lambda j: (0,) * nd, pipeline_mode=pl.Buffered(1))


def _chunk_constants(lc):
    expand = np.zeros((LANES, SSD_WIDTH), np.float32)
    psel = np.zeros((PAIRS, LANES), np.float32)
    for piece in range(PIECES):
        for hd in range(HEADS):
            expand[piece * HEADS + hd, hd * HEAD_DIM:(hd + 1) * HEAD_DIM] = 1.0
            psel[hd // 2, piece * HEADS + hd] = 1.0
    r = np.arange(TILE_ROWS)
    tril = ((r[:, None] >= r[None, :]) & (r[:, None] // lc == r[None, :] // lc)).astype(np.float32)
    return jnp.asarray(expand, BF16), jnp.asarray(tril, BF16), jnp.asarray(psel, BF16)


def _mix_weights(w_s, b_s, lc):
    nck = TILE_ROWS // lc
    mask = jnp.tril(jnp.ones((lc, lc), dtype=bool))
    w = jnp.where(mask[None], w_s[:, :lc, :lc], 0)
    if nck * GMLP_GROUP_DIM > MXU_COLS:
        eye = jnp.eye(nck, dtype=w.dtype)
        w = jnp.einsum("ab,gts->gatbs", eye, w).reshape(GMLP_GROUPS, TILE_ROWS, TILE_ROWS)
    bias = jnp.repeat(b_s[:, :lc].T, GMLP_GROUP_DIM, axis=1)
    return w.astype(BF16), jnp.tile(bias, (nck, 1))


def _layer(x, shift, scale, gate, conv0, ssm0, wl, *, nseq, seglen, lc_ssd, lc_gmlp, nt, emit_v,
           layer=0):
    (g_pre, g_post, w_a, w_b, w_dt, conv_w, conv_b, dtb, alog, dskip, g_v, beta_v, w_s, b_s,
     w_out) = wl
    assert lc_ssd == lc_gmlp or nseq == 1
    pipelined = conv0 is None
    nb = shift.shape[0]
    rows_total = x.shape[0]
    ntiles = rows_total // TILE_ROWS
    assert nseq * seglen == TILE_ROWS and nb * nt == ntiles
    expand, tril, psel = _chunk_constants(lc_ssd)
    mixw, mixb = _mix_weights(w_s, b_s, lc_gmlp)
    nck = TILE_ROWS // lc_ssd

    if pipelined:
        tps = TILES_PER_STEP
        assert ntiles % tps == 0

        def tile_m(j):
            return jnp.minimum(tps * j, ntiles - tps)

        def tile_v(j):
            return jnp.clip(tps * j - 1, 0, ntiles - 1)

        def tile_o(j):
            return jnp.maximum(tps * j - 2, 0)
        steps = ntiles // tps + 1
    else:
        tps = 1

        def tile_m(j):
            return j
        tile_v = tile_o = tile_m
        steps = ntiles

    def mod_spec(tile_of):
        return pl.BlockSpec((None, nseq, D_MODEL), lambda j: (tile_of(j) // nt, 0, 0))

    in_specs = [pl.BlockSpec((tps * TILE_ROWS, D_MODEL), lambda j: (tile_m(j) // tps, 0)),
                pl.BlockSpec((tps * TILE_ROWS, D_MODEL), lambda j: (tile_o(j) // tps, 0)),
                mod_spec(tile_m), mod_spec(tile_m), mod_spec(tile_o)]
    args = [x, x, shift, scale, gate]
    if not pipelined:
        in_specs += [pl.BlockSpec((None, nseq, CONV_W - 1, CONV_DIM), lambda j: (j // nt, 0, 0, 0)),
                     pl.BlockSpec((None, nseq, HEADS, HEAD_DIM, STATE), lambda j: (layer, j, 0, 0, 0))]
        args += [conv0, ssm0]
    consts = [g_pre, g_post, w_a, w_b, w_dt, conv_w, conv_b, dtb, alog, dskip, g_v, beta_v, mixw,
              mixb, w_out, expand, tril, psel]
    in_specs += [_const_spec(a.shape) for a in consts]
    args += consts

    out_shape = [jax.ShapeDtypeStruct((rows_total, D_MODEL), F32),
                 jax.ShapeDtypeStruct((nb, nseq, CONV_W - 1, CONV_DIM), F32),
                 jax.ShapeDtypeStruct((nb * nseq, HEADS, HEAD_DIM, STATE), F32)]
    out_specs = [pl.BlockSpec((tps * TILE_ROWS, D_MODEL), lambda j: (tile_o(j) // tps, 0)),
                 pl.BlockSpec((None, nseq, CONV_W - 1, CONV_DIM), lambda j: (tile_m(j) // nt, 0, 0, 0)),
                 pl.BlockSpec((nseq, HEADS, HEAD_DIM, STATE), lambda j: (tile_v(j) // nt, 0, 0, 0))]
    if emit_v:
        out_shape.append(jax.ShapeDtypeStruct((rows_total, GMLP_WIDTH), F32))
        out_specs.append(pl.BlockSpec((TILE_ROWS, GMLP_WIDTH), lambda j: (tile_v(j), 0)))

    spare = (TAIL_ROWS, LANES)
    stage_shapes = [((TAIL_ROWS + TILE_ROWS, W_CAT_COLS), F32),
                    ((TILE_ROWS, SSD_WIDTH), F32),
                    ((TILE_ROWS, 2 * GROUPS * STATE), BF16),
                    ((TILE_ROWS, SSD_WIDTH + GMLP_WIDTH), BF16)]
    scratch = [pltpu.VMEM((TILE_ROWS, D_MODEL), BF16),
               pltpu.VMEM((TAIL_ROWS, CONV_DIM), F32),
               pltpu.VMEM((TAIL_ROWS + seglen, LANES), F32),
               pltpu.VMEM((STATE, SSD_WIDTH), F32),
               pltpu.VMEM((nck, STATE, SSD_WIDTH), F32),
               pltpu.VMEM((TILE_ROWS, SSD_WIDTH), F32),
               pltpu.VMEM((TILE_ROWS, SSD_WIDTH), F32),
               pltpu.VMEM((TILE_ROWS, SSD_WIDTH), BF16),
               pltpu.VMEM((TILE_ROWS, SSD_WIDTH), BF16),
               pltpu.VMEM((PAIRS, nck * LANES), F32),
               pltpu.VMEM((TILE_ROWS, GMLP_WIDTH), BF16),
               pltpu.VMEM((TILE_ROWS, D_MODEL), F32),
               pltpu.VMEM((nck, STATE, SSD_WIDTH), BF16),
               pltpu.VMEM((TILE_ROWS, SSD_WIDTH), F32),
               pltpu.VMEM((nck * GROUPS, lc_ssd, LANES), F32),
               pltpu.VMEM((TILE_ROWS, SSD_WIDTH), F32)]
    assert len(scratch) == N_SHARED_SCRATCH
    for shape, dtype in stage_shapes:
        scratch += [pltpu.VMEM(shape, dtype), pltpu.VMEM(shape if pipelined else spare, dtype)]
    kern = functools.partial(_layer_kernel, nseq=nseq, seglen=seglen, lc=lc_ssd, nt=nt,
                             pipelined=pipelined, emit_v=emit_v)
    return pl.pallas_call(
        kern,
        grid=(steps,),
        in_specs=in_specs,
        out_specs=out_specs,
        out_shape=out_shape,
        scratch_shapes=scratch,
        compiler_params=pltpu.CompilerParams(
            dimension_semantics=("arbitrary",),
            vmem_limit_bytes=VMEM_LIMIT_BYTES),
        name="hybrid_layer_stream" if pipelined else "hybrid_layer_state",
    )(*args)


def _prep_weights(g_pre, g_post, w_in, conv_w, conv_b, dt_bias, a_log, d_skip, g_v, beta_v,
                  w_s, b_s, w_out):
    w_a, w_b, w_dt = _cast_weights(w_in)
    pad = (0, LANES - HEADS)
    return (g_pre.reshape(1, -1), g_post.reshape(1, -1), w_a, w_b, w_dt, conv_w, conv_b.reshape(1, -1),
            jnp.pad(dt_bias, pad).reshape(1, -1), jnp.pad(a_log, pad).reshape(1, -1),
            jnp.repeat(d_skip, HEAD_DIM).reshape(1, -1), g_v.reshape(1, -1),
            beta_v.reshape(1, -1), w_s, b_s, w_out.astype(BF16))


def kernel(x_prompt, x_sample, state_conv, state_ssm, c_prompt, c_sample, w_ada, b_ada, g_pre,
           g_post, w_in, conv_w, conv_b, dt_bias, a_log, d_skip, g_v, beta_v, w_s, b_s, w_out):
    depth = w_ada.shape[0]
    bp, lp, _ = x_prompt.shape
    bs, ls, _ = x_sample.shape
    seq_per_tile = TILE_ROWS // ls
    yp = x_prompt.reshape(bp * lp, D_MODEL)
    ys = x_sample.reshape(bs * ls, D_MODEL)
    c_all = jnp.concatenate([c_prompt, c_sample], axis=0)
    c_pad = jnp.pad(c_all, ((0, (-c_all.shape[0]) % 8), (0, 0)))
    outs = [[] for _ in range(5)]
    for l in range(depth):
        wl = _prep_weights(g_pre[l], g_post[l], w_in[l], conv_w[l], conv_b[l], dt_bias[l],
                           a_log[l], d_skip[l], g_v[l], beta_v[l], w_s[l], b_s[l], w_out[l])
        mod = _modulation(c_pad, w_ada[l], b_ada[l])
        shift, scale, gate = (mod[:, i * D_MODEL:(i + 1) * D_MODEL] for i in range(3))
        mp = [m[:bp].reshape(bp, 1, D_MODEL) for m in (shift, scale, gate)]
        ms = [m[bp:bp + bs].reshape(bs // seq_per_tile, seq_per_tile, D_MODEL)
              for m in (shift, scale, gate)]
        yp, conv_p, ssm_p = _layer(
            yp, *mp, None, None, wl, nseq=1, seglen=TILE_ROWS, lc_ssd=min(lp, SSD_CHUNK),
            lc_gmlp=min(lp, GMLP_CHUNK), nt=lp // TILE_ROWS, emit_v=False)
        ys, conv_s, ssm_s, v_s = _layer(
            ys, *ms,
            state_conv[l].reshape(bs // seq_per_tile, seq_per_tile, CONV_W - 1, CONV_DIM),
            state_ssm, wl,
            nseq=seq_per_tile, seglen=ls, lc_ssd=min(ls, SSD_CHUNK), lc_gmlp=min(ls, GMLP_CHUNK),
            nt=1, emit_v=True, layer=l)
        outs[0].append(conv_p.reshape(bp, CONV_W - 1, CONV_DIM))
        outs[1].append(ssm_p)
        outs[2].append(conv_s.reshape(bs, CONV_W - 1, CONV_DIM))
        outs[3].append(ssm_s)
        outs[4].append(v_s.reshape(bs, ls, GMLP_WIDTH))
    stacked = [o[0][None] if depth == 1 else jnp.stack(o) for o in outs]
    return (yp.reshape(bp, lp, D_MODEL), ys.reshape(bs, ls, D_MODEL), *stacked)
```

```python
import collections
import functools

import jax
import jax.numpy as jnp
import numpy as np
from jax import lax
from jax.experimental import pallas as pl
from jax.experimental.pallas import tpu as pltpu

F32 = jnp.float32
BF16 = jnp.bfloat16

D_MODEL = 1024
SSD_WIDTH = 1024
HEAD_DIM = 64
HEADS = SSD_WIDTH // HEAD_DIM
GROUPS = 2
GROUP_WIDTH = SSD_WIDTH // GROUPS
STATE = 128
CONV_W = 4
CONV_DIM = SSD_WIDTH + 2 * GROUPS * STATE
GMLP_WIDTH = 1024
GMLP_GROUPS = 8
GMLP_GROUP_DIM = GMLP_WIDTH // GMLP_GROUPS
SSD_CHUNK = 64
GMLP_CHUNK = 128
EPS = 1e-6
LOG2E = 1.4426950408889634

LANES = 128
MXU_COLS = 256
HALF = LANES // 2
PAIRS = HEADS // 2
PAIRS_PER_GROUP = PAIRS // GROUPS
PIECES = 3
TILE_ROWS = 256
TAIL_ROWS = 8
TAP0 = TAIL_ROWS - (CONV_W - 1)
MOD_BLOCK_COLS = 512
VMEM_LIMIT_BYTES = 60 * 1024 * 1024

OFF_Z = 0
OFF_XBC = OFF_Z + SSD_WIDTH
OFF_U = OFF_XBC + CONV_DIM
OFF_V = OFF_U + GMLP_WIDTH
OFF_G = OFF_V + GMLP_WIDTH
OFF_DT = OFF_G + GMLP_WIDTH
W_CAT_COLS = OFF_DT + LANES

Bufs = collections.namedtuple("Bufs", "proj xs bc mix")


def _dot(a, b):
    return jnp.dot(a, b, preferred_element_type=F32)


def _dot_nt(a, b):
    return lax.dot_general(a, b, (((1,), (1,)), ((), ())), preferred_element_type=F32)


def _dot_tn(a, b):
    return lax.dot_general(a, b, (((0,), (0,)), ((), ())), preferred_element_type=F32)


def _cat3(x):
    hi = x.astype(BF16).astype(F32)
    r1 = x - hi
    mid = r1.astype(BF16).astype(F32)
    lo = r1 - mid
    return hi + pltpu.roll(mid, HEADS, axis=1) + pltpu.roll(lo, 2 * HEADS, axis=1)


def _silu(x):
    return x / (1.0 + jnp.exp2(x * (-LOG2E)))


def _softplus(x):
    return jnp.maximum(x, 0.0) + jnp.log1p(jnp.exp(-jnp.abs(x)))


def _row_mean(x):
    width = x.shape[-1]
    part = x[:, 0:LANES]
    for c in range(LANES, width, LANES):
        part = part + x[:, c:c + LANES]
    return jnp.sum(part, axis=-1, keepdims=True) * (1.0 / width)


def _rms_scale(x):
    return lax.rsqrt(_row_mean(x * x) + EPS)


def _mod_kernel(c_ref, w_ref, b_ref, o_ref):
    c = _silu(c_ref[...]).astype(BF16)
    o_ref[...] = _dot(c, w_ref[...].astype(BF16)) + b_ref[...]


def _modulation(c, w_ada, b_ada):
    rows = c.shape[0]
    n = w_ada.shape[1]
    return pl.pallas_call(
        _mod_kernel,
        grid=(n // MOD_BLOCK_COLS,),
        in_specs=[
            pl.BlockSpec((rows, D_MODEL), lambda j: (0, 0)),
            pl.BlockSpec((D_MODEL, MOD_BLOCK_COLS), lambda j: (0, j)),
            pl.BlockSpec((1, MOD_BLOCK_COLS), lambda j: (0, j)),
        ],
        out_specs=pl.BlockSpec((rows, MOD_BLOCK_COLS), lambda j: (0, j)),
        out_shape=jax.ShapeDtypeStruct((rows, n), F32),
        name="adaln_mod",
    )(c, w_ada, b_ada.reshape(1, n))


CAST_BLOCK_COLS = 512
A_COLS = SSD_WIDTH + CONV_DIM
B_COLS = 3 * GMLP_WIDTH


def _cast_kernel(a_ref, b_ref, b_next_ref, wa_ref, wb_ref, wdt_ref):
    j = pl.program_id(0)
    wa_ref[...] = a_ref[...].astype(BF16)
    wide = jnp.concatenate([b_ref[...], b_next_ref[...]], axis=1)
    shifted = pltpu.roll(wide, wide.shape[1] - HEADS, axis=1)
    wb_ref[...] = shifted[:, :CAST_BLOCK_COLS].astype(BF16)

    @pl.when(j == 0)
    def _():
        head = b_ref[:, 0:LANES]
        lane = lax.broadcasted_iota(jnp.int32, head.shape, 1)
        wdt_ref[...] = jnp.where(lane < HEADS, head, 0.0).astype(BF16)


def _cast_weights(w_in):
    a_blocks = A_COLS // CAST_BLOCK_COLS
    b_blocks = B_COLS // CAST_BLOCK_COLS
    lanes_per_block = CAST_BLOCK_COLS // LANES
    return pl.pallas_call(
        _cast_kernel,
        grid=(b_blocks,),
        in_specs=[
            pl.BlockSpec((D_MODEL, CAST_BLOCK_COLS), lambda j: (0, jnp.minimum(j, a_blocks - 1))),
            pl.BlockSpec((D_MODEL, CAST_BLOCK_COLS), lambda j: (0, a_blocks + j)),
            pl.BlockSpec((D_MODEL, LANES), lambda j: (0, (a_blocks + j + 1) * lanes_per_block)),
        ],
        out_specs=[
            pl.BlockSpec((D_MODEL, CAST_BLOCK_COLS), lambda j: (0, jnp.minimum(j, a_blocks - 1))),
            pl.BlockSpec((D_MODEL, CAST_BLOCK_COLS), lambda j: (0, j)),
            pl.BlockSpec((D_MODEL, LANES), lambda j: (0, 0)),
        ],
        out_shape=[jax.ShapeDtypeStruct((D_MODEL, A_COLS), BF16),
                   jax.ShapeDtypeStruct((D_MODEL, B_COLS), BF16),
                   jax.ShapeDtypeStruct((D_MODEL, LANES), BF16)],
        compiler_params=pltpu.CompilerParams(dimension_semantics=("arbitrary",)),
        name="cast_proj_weight",
    )(w_in, w_in, w_in)


TILES_PER_STEP = 2
N_SHARED_SCRATCH = 16
N_STAGE_SCRATCH = 8


def _interleave(primary, secondary):
    total = sum(weight for _, weight in primary)
    groups = []
    done = 0
    seen = 0
    for task, weight in primary:
        seen += weight
        want = (seen * len(secondary)) // total
        groups.append([task] + secondary[done:want])
        done = want
    return groups


def _tag(task, name, needs=()):
    task.name, task.needs = name, tuple(needs)
    return task


def _run_groups(groups):
    stored = set()
    for group in groups:
        for task in group:
            missing = [n for n in getattr(task, "needs", ()) if n not in stored]
            assert not missing, (getattr(task, "name", task), missing)
        commits = [task() for task in group]
        for commit in commits:
            if commit is not None:
                commit()
        stored.update(getattr(task, "name", None) for task in group)


def _layer_kernel(*refs, nseq, seglen, lc, nt, pipelined, emit_v):
    statics = dict(nseq=nseq, seglen=seglen, lc=lc, pipelined=pipelined, emit_v=emit_v)
    shared, stage = refs[:-N_STAGE_SCRATCH], refs[-N_STAGE_SCRATCH:]
    buf_a, buf_b = Bufs(*stage[0::2]), Bufs(*stage[1::2])
    n_scratch = N_SHARED_SCRATCH + N_STAGE_SCRATCH
    ssm_out_ref = refs[len(refs) - n_scratch - (2 if emit_v else 1)]
    tail_ref, st_ref = refs[-n_scratch + 1], refs[-n_scratch + 3]
    if not pipelined:
        _tile_body(shared, buf_a, buf_a, buf_a, buf_a, **statics)
        return

    assert nt % TILES_PER_STEP == 0 and TILES_PER_STEP == 2
    i = pl.program_id(0)
    row_start = lax.rem(TILES_PER_STEP * i, nt) == 0
    n_in = len(shared) - N_SHARED_SCRATCH - 3
    views = []
    for half in range(TILES_PER_STEP):
        rows = slice(half * TILE_ROWS, (half + 1) * TILE_ROWS)
        view = list(shared)
        for k in (0, 1, n_in):
            view[k] = shared[k].at[rows, :]
        views.append(view)

    @pl.when(i == 0)
    def _():
        for ref in buf_b:
            ref[...] = jnp.zeros_like(ref)
        st_ref[...] = jnp.zeros_like(st_ref)

    @pl.when(row_start)
    def _():
        tail_ref[...] = jnp.zeros_like(tail_ref)

    _tile_body(views[0], buf_a, buf_b, buf_a, buf_b, **statics)

    @pl.when(row_start & (i > 0))
    def _():
        ssm_out_ref[0] = st_ref[...].T.reshape(HEADS, HEAD_DIM, STATE)

    @pl.when(row_start)
    def _():
        st_ref[...] = jnp.zeros_like(st_ref)

    _tile_body(views[1], buf_b, buf_a, buf_b, buf_a, **statics)


def _tile_body(refs, bm, bv, bvo, bo, *, nseq, seglen, lc, pipelined, emit_v):
    has_state = not pipelined
    it = iter(refs)
    x_ref, xo_ref = next(it), next(it)
    shift_ref, scale_ref, gate_ref = next(it), next(it), next(it)
    conv0_ref = next(it) if has_state else None
    ssm0_ref = next(it) if has_state else None
    (gpre_ref, gpost_ref, wa_ref, wb_ref, wdt_ref, convw_ref, convb_ref, dtb_ref, alog_ref,
     dskip_ref, gv_ref, betav_ref, mixw_ref, mixb_ref, wout_ref, expand_ref, tril_ref, psel_ref) = (
         next(it) for _ in range(18))
    y_ref, conv_out_ref, ssm_out_ref = next(it), next(it), next(it)
    vn_out_ref = next(it) if emit_v else None
    (h_ref, tail_ref, xp_ref, st_ref, stnew_ref, aexp_ref, xdt_ref, xlo_ref, xhi_ref, acumt_ref,
     vnb_ref, mo_ref, stb_ref, ys_ref, cb_ref, yoff_ref) = (next(it) for _ in range(N_SHARED_SCRATCH))
    pm, pv = bm.proj, bv.proj

    rows_all = slice(0, TILE_ROWS)
    nchunk = seglen // lc
    pad_rows = HALF - lc

    def pad_to_half(v):
        if pad_rows == 0:
            return [v]
        return [v, jnp.zeros((pad_rows, v.shape[1]), v.dtype)]

    def m_norm():
        for s in range(nseq):
            rows = slice(s * seglen, (s + 1) * seglen)
            xr = x_ref[rows, :]
            gain = gpre_ref[...] * (1.0 + scale_ref[s:s + 1, :])
            hs = (xr * _rms_scale(xr)) * gain + shift_ref[s:s + 1, :]
            h_ref[rows, :] = hs.astype(BF16)

    def m_dot(w_ref, w0, dst0, width):
        def task():
            val = _dot(h_ref[...], w_ref[:, w0:w0 + width])

            def commit():
                pm[rows_all, dst0:dst0 + width] = val
            return commit
        return _tag(task, f"m_dot{dst0 // MXU_COLS}")

    dst_starts = (list(range(OFF_XBC, OFF_U, MXU_COLS)) + list(range(OFF_Z, OFF_XBC, MXU_COLS))
                  + list(range(OFF_U, OFF_DT, MXU_COLS)))
    m_dots = [m_dot(wa_ref, d, d, MXU_COLS) if d < OFF_U else m_dot(wb_ref, d - OFF_U, d, MXU_COLS)
              for d in dst_starts] + [m_dot(wdt_ref, 0, OFF_DT, LANES)]

    def conv_store(c, val, rows):
        if c < SSD_WIDTH // LANES:
            bm.xs[rows, c * LANES:(c + 1) * LANES] = val
        else:
            c2 = c - SSD_WIDTH // LANES
            bm.bc[rows, c2 * LANES:(c2 + 1) * LANES] = val.astype(BF16)

    def m_conv(c):
        cols = slice(OFF_XBC + c * LANES, OFF_XBC + (c + 1) * LANES)
        wcols = slice(c * LANES, (c + 1) * LANES)

        def task_stream():
            xfull = jnp.concatenate([tail_ref[:, wcols], pm[:, cols]], axis=0)
            acc = convw_ref[0:1, wcols] * xfull
            for k in range(1, CONV_W):
                acc = pltpu.roll(acc, 1, axis=0) + convw_ref[k:k + 1, wcols] * xfull
            val = _silu(acc[TAIL_ROWS:, :] + convb_ref[:, wcols])
            tail_ref[:, wcols] = pm[TILE_ROWS - TAIL_ROWS:TILE_ROWS, cols]
            conv_out_ref[0, :, wcols] = pm[TILE_ROWS - (CONV_W - 1):TILE_ROWS, cols]

            def commit():
                conv_store(c, val, slice(0, TILE_ROWS))
            return commit

        def task_state():
            for s in range(nseq):
                rows = slice(s * seglen, (s + 1) * seglen)
                xp_ref[TAP0:TAIL_ROWS, :] = conv0_ref[s, :, wcols]
                xp_ref[TAIL_ROWS:TAIL_ROWS + seglen, :] = pm[s * seglen:(s + 1) * seglen, cols]
                acc = convb_ref[:, wcols]
                for k in range(CONV_W):
                    acc = acc + convw_ref[k:k + 1, wcols] * xp_ref[TAP0 + k:TAP0 + k + seglen, :]
                conv_store(c, _silu(acc), rows)
                conv_out_ref[s, :, wcols] = xp_ref[TAIL_ROWS + seglen - (CONV_W - 1):TAIL_ROWS + seglen, :]

        return task_stream if pipelined else task_state

    m_convs = [_tag(m_conv(c), f"m_conv{c}", [f"m_dot{(OFF_XBC + c * LANES) // MXU_COLS}"])
               for c in range(CONV_DIM // LANES)]

    lane1 = lax.broadcasted_iota(jnp.int32, (TILE_ROWS, LANES), 1)

    def v_dt():
        a_row = -jnp.exp(alog_ref[...])
        dt = _softplus(pv[rows_all, OFF_DT:OFF_DT + LANES] + dtb_ref[...])
        dt = jnp.where(lane1 < HEADS, dt, 0.0)
        csum = _dot(tril_ref[...], _cat3(dt * a_row).astype(BF16))
        a_cum = csum + pltpu.roll(csum, LANES - HEADS, axis=1) + pltpu.roll(csum, LANES - 2 * HEADS, axis=1)
        a_cat = _cat3(jnp.where(lane1 < HEADS, a_cum, 0.0))
        aexp_ref[...] = _dot(a_cat.astype(BF16), expand_ref[...])
        dt_exp = _dot(_cat3(dt).astype(BF16), expand_ref[...])
        even = lax.rem(lane1, 2) == 0
        a_even = jnp.where(even, a_cat, 0.0)
        a_odd = (a_cat - a_even).astype(BF16)
        a_even = a_even.astype(BF16)
        parts = []
        for ck in range(TILE_ROWS // lc):
            rows = slice(ck * lc, (ck + 1) * lc)
            parts += pad_to_half(a_even[rows, :]) + pad_to_half(a_odd[rows, :])
        acumt_ref[...] = _dot_nt(psel_ref[...], jnp.concatenate(parts, axis=0))
        xdt = bv.xs[...] * dt_exp
        lane_w = lax.broadcasted_iota(jnp.int32, (TILE_ROWS, SSD_WIDTH), 1)
        xlo = jnp.where(lax.rem(lane_w, LANES) < HALF, xdt, 0.0)
        xdt_ref[...] = xdt
        xlo_ref[...] = xlo.astype(BF16)
        xhi_ref[...] = (xdt - xlo).astype(BF16)

    li = lax.broadcasted_iota(jnp.int32, (lc, LANES), 0)
    si = lax.rem(lax.broadcasted_iota(jnp.int32, (lc, LANES), 1), HALF)
    causal = (si <= li) & (si < lc)

    def v_local(ci, g):
        rows = slice(ci * lc, (ci + 1) * lc)
        hl = slice(g * GROUP_WIDTH, (g + 1) * GROUP_WIDTH)

        def products():
            last = aexp_ref[(ci + 1) * lc - 1:(ci + 1) * lc, hl]
            xw_b = (xdt_ref[rows, hl] * jnp.exp(last - aexp_ref[rows, hl])).astype(BF16)
            b_g = bv.bc[rows, g * STATE:(g + 1) * STATE]
            c_g = bv.bc[rows, (GROUPS + g) * STATE:(GROUPS + g + 1) * STATE]
            b2 = jnp.concatenate(pad_to_half(b_g) + pad_to_half(b_g), axis=0)
            cb_ref[ci * GROUPS + g] = _dot_nt(c_g, b2)
            stnew_ref[ci, :, hl] = _dot_tn(b_g, xw_b)

        def diag():
            cb2 = cb_ref[ci * GROUPS + g]
            zeros = jnp.zeros((LANES, LANES), BF16)
            for i in range(0, PAIRS_PER_GROUP, 2):
                w_both, bd_both = [], []
                for p in (g * PAIRS_PER_GROUP + i, g * PAIRS_PER_GROUP + i + 1):
                    pcols = slice(p * LANES, (p + 1) * LANES)
                    seg = aexp_ref[rows, pcols] - acumt_ref[p:p + 1, ci * LANES:(ci + 1) * LANES]
                    w_both.append((cb2 * jnp.exp(jnp.where(causal, seg, -jnp.inf))).astype(BF16))
                    bd_both.append(jnp.concatenate(
                        pad_to_half(xlo_ref[rows, pcols]) + pad_to_half(xhi_ref[rows, pcols]), axis=0))
                bd2 = jnp.concatenate([jnp.concatenate([bd_both[0], zeros], axis=1),
                                       jnp.concatenate([zeros, bd_both[1]], axis=1)], axis=0)
                cols2 = slice((g * PAIRS_PER_GROUP + i) * LANES, (g * PAIRS_PER_GROUP + i + 2) * LANES)
                ys_ref[rows, cols2] = (_dot(jnp.concatenate(w_both, axis=1), bd2)
                                       + dskip_ref[:, cols2] * bv.xs[rows, cols2])
        return products, diag

    def v_state(ci, s):
        def task():
            e_last = jnp.exp(aexp_ref[(ci + 1) * lc - 1:(ci + 1) * lc, :])
            st = st_ref[...]
            stb_ref[ci] = st.astype(BF16)
            st_ref[...] = st * e_last + stnew_ref[ci]
            if has_state and ci % nchunk == nchunk - 1:
                ssm_out_ref[s] = st_ref[...].T.reshape(HEADS, HEAD_DIM, STATE)
        return task

    def v_readout(ci, g):
        rows = slice(ci * lc, (ci + 1) * lc)
        prow = rows
        hl = slice(g * GROUP_WIDTH, (g + 1) * GROUP_WIDTH)

        def product():
            c_g = bv.bc[rows, (GROUPS + g) * STATE:(GROUPS + g + 1) * STATE]
            yoff_ref[rows, hl] = _dot(c_g, stb_ref[ci, :, hl])

        def gate():
            y = ys_ref[rows, hl] + jnp.exp(aexp_ref[rows, hl]) * yoff_ref[rows, hl]
            zg = _silu(pv[prow, OFF_Z + g * GROUP_WIDTH:OFF_Z + (g + 1) * GROUP_WIDTH])
            bvo.mix[rows, hl] = (y * zg).astype(BF16)
        return product, gate

    def v_state_load(s):
        def task():
            st_ref[...] = ssm0_ref[s].reshape(SSD_WIDTH, STATE).T
        return task

    def v_ln():
        v = pv[rows_all, OFF_V:OFF_V + GMLP_WIDTH]
        vc = v - _row_mean(v)
        vn = (vc * _rms_scale(vc)) * gv_ref[...] + betav_ref[...]
        if emit_v:
            vn_out_ref[...] = vn
        vnb_ref[...] = vn.astype(BF16)

    def v_gmlp(g):
        gcols = slice(g * GMLP_GROUP_DIM, (g + 1) * GMLP_GROUP_DIM)

        def task():
            if mixw_ref.shape[1] == TILE_ROWS:
                mixed = _dot(mixw_ref[g], vnb_ref[:, gcols])
            else:
                lcg = mixw_ref.shape[1]
                blocks = [vnb_ref[r:r + lcg, gcols] for r in range(0, TILE_ROWS, lcg)]
                wide = _dot(mixw_ref[g], jnp.concatenate(blocks, axis=1))
                mixed = jnp.concatenate(
                    [wide[:, k * GMLP_GROUP_DIM:(k + 1) * GMLP_GROUP_DIM] for k in range(len(blocks))],
                    axis=0)
            mixed = mixed + mixb_ref[:, gcols]
            u = pv[rows_all, OFF_U + g * GMLP_GROUP_DIM:OFF_U + (g + 1) * GMLP_GROUP_DIM]
            gt = pv[rows_all, OFF_G + g * GMLP_GROUP_DIM:OFF_G + (g + 1) * GMLP_GROUP_DIM]
            bvo.mix[:, SSD_WIDTH + g * GMLP_GROUP_DIM:SSD_WIDTH + (g + 1) * GMLP_GROUP_DIM] = (
                _silu(gt) * u * mixed).astype(BF16)
        return task

    chunk_groups = [(ci, g) for ci in range(nseq * nchunk) for g in range(GROUPS)]
    v_locals = [v_local(ci, g) for ci, g in chunk_groups]
    v_states = []
    for s in range(nseq):
        if has_state:
            v_states.append(v_state_load(s))
        v_states += [v_state(s * nchunk + c, s) for c in range(nchunk)]
    v_readouts = [v_readout(ci, g) for ci, g in chunk_groups]
    v_scan = ([(t[0], 0) for t in v_locals] + [(t[1], 1) for t in v_locals]
              + [(t, 1) for t in v_states]
              + [(t[0], 0) for t in v_readouts] + [(t[1], 1) for t in v_readouts])
    v_gmlps = [v_gmlp(g) for g in range(GMLP_GROUPS)]

    def o_out(n):
        def task():
            mo_ref[:, n * MXU_COLS:(n + 1) * MXU_COLS] = _dot(
                bo.mix[...], wout_ref[:, n * MXU_COLS:(n + 1) * MXU_COLS])
        return task

    def o_final():
        mo = mo_ref[...]
        mo = mo * _rms_scale(mo)
        for s in range(nseq):
            rows = slice(s * seglen, (s + 1) * seglen)
            gain = gpost_ref[...] * gate_ref[s:s + 1, :]
            y_ref[rows, :] = xo_ref[rows, :] + gain * mo[rows, :]

    o_outs = [_tag(o_out(n), f"o_out{n}") for n in range(D_MODEL // MXU_COLS)]
    _tag(o_final, "o_final", [t.name for t in o_outs])

    if pipelined:
        n_xbc_dots = CONV_DIM // MXU_COLS
        head = _interleave([(m_norm, 1), (v_dt, 2), (v_ln, 2)], o_outs + m_dots[:n_xbc_dots])
        body = [(t, 3) for t in m_convs] + [(o_final, 3)] + v_scan + [(t, 0) for t in v_gmlps]
        groups = head + _interleave(body, m_dots[n_xbc_dots:])
    else:
        order = ([m_norm] + m_dots + m_convs + [v_dt, v_ln] + [t for t, _ in v_scan] + v_gmlps
                 + o_outs + [o_final])
        groups = [[task] for task in order]
    _run_groups(groups)


def _const_spec(shape):
    nd = len(shape)
    return pl.BlockSpec(shape, lambda j: (0,) * nd, pipeline_mode=pl.Buffered(1))


def _chunk_constants(lc):
    expand = np.zeros((LANES, SSD_WIDTH), np.float32)
    psel = np.zeros((PAIRS, LANES), np.float32)
    for piece in range(PIECES):
        for hd in range(HEADS):
            expand[piece * HEADS + hd, hd * HEAD_DIM:(hd + 1) * HEAD_DIM] = 1.0
            psel[hd // 2, piece * HEADS + hd] = 1.0
    r = np.arange(TILE_ROWS)
    tril = ((r[:, None] >= r[None, :]) & (r[:, None] // lc == r[None, :] // lc)).astype(np.float32)
    return jnp.asarray(expand, BF16), jnp.asarray(tril, BF16), jnp.asarray(psel, BF16)


def _mix_weights(w_s, b_s, lc):
    nck = TILE_ROWS // lc
    mask = jnp.tril(jnp.ones((lc, lc), dtype=bool))
    w = jnp.where(mask[None], w_s[:, :lc, :lc], 0)
    if nck * GMLP_GROUP_DIM > MXU_COLS:
        eye = jnp.eye(nck, dtype=w.dtype)
        w = jnp.einsum("ab,gts->gatbs", eye, w).reshape(GMLP_GROUPS, TILE_ROWS, TILE_ROWS)
    bias = jnp.repeat(b_s[:, :lc].T, GMLP_GROUP_DIM, axis=1)
    return w.astype(BF16), jnp.tile(bias, (nck, 1))


def _layer(x, shift, scale, gate, conv0, ssm0, wl, *, nseq, seglen, lc_ssd, lc_gmlp, nt, emit_v,
           layer=0):
    (g_pre, g_post, w_a, w_b, w_dt, conv_w, conv_b, dtb, alog, dskip, g_v, beta_v, w_s, b_s,
     w_out) = wl
    assert lc_ssd == lc_gmlp or nseq == 1
    pipelined = conv0 is None
    nb = shift.shape[0]
    rows_total = x.shape[0]
    ntiles = rows_total // TILE_ROWS
    assert nseq * seglen == TILE_ROWS and nb * nt == ntiles
    expand, tril, psel = _chunk_constants(lc_ssd)
    mixw, mixb = _mix_weights(w_s, b_s, lc_gmlp)
    nck = TILE_ROWS // lc_ssd

    if pipelined:
        tps = TILES_PER_STEP
        assert ntiles % tps == 0

        def tile_m(j):
            return jnp.minimum(tps * j, ntiles - tps)

        def tile_v(j):
            return jnp.clip(tps * j - 1, 0, ntiles - 1)

        def tile_o(j):
            return jnp.maximum(tps * j - 2, 0)
        steps = ntiles // tps + 1
    else:
        tps = 1

        def tile_m(j):
            return j
        tile_v = tile_o = tile_m
        steps = ntiles

    def mod_spec(tile_of):
        return pl.BlockSpec((None, nseq, D_MODEL), lambda j: (tile_of(j) // nt, 0, 0))

    in_specs = [pl.BlockSpec((tps * TILE_ROWS, D_MODEL), lambda j: (tile_m(j) // tps, 0)),
                pl.BlockSpec((tps * TILE_ROWS, D_MODEL), lambda j: (tile_o(j) // tps, 0)),
                mod_spec(tile_m), mod_spec(tile_m), mod_spec(tile_o)]
    args = [x, x, shift, scale, gate]
    if not pipelined:
        in_specs += [pl.BlockSpec((None, nseq, CONV_W - 1, CONV_DIM), lambda j: (j // nt, 0, 0, 0)),
                     pl.BlockSpec((None, nseq, HEADS, HEAD_DIM, STATE), lambda j: (layer, j, 0, 0, 0))]
        args += [conv0, ssm0]
    consts = [g_pre, g_post, w_a, w_b, w_dt, conv_w, conv_b, dtb, alog, dskip, g_v, beta_v, mixw,
              mixb, w_out, expand, tril, psel]
    in_specs += [_const_spec(a.shape) for a in consts]
    args += consts

    out_shape = [jax.ShapeDtypeStruct((rows_total, D_MODEL), F32),
                 jax.ShapeDtypeStruct((nb, nseq, CONV_W - 1, CONV_DIM), F32),
                 jax.ShapeDtypeStruct((nb * nseq, HEADS, HEAD_DIM, STATE), F32)]
    out_specs = [pl.BlockSpec((tps * TILE_ROWS, D_MODEL), lambda j: (tile_o(j) // tps, 0)),
                 pl.BlockSpec((None, nseq, CONV_W - 1, CONV_DIM), lambda j: (tile_m(j) // nt, 0, 0, 0)),
                 pl.BlockSpec((nseq, HEADS, HEAD_DIM, STATE), lambda j: (tile_v(j) // nt, 0, 0, 0))]
    if emit_v:
        out_shape.append(jax.ShapeDtypeStruct((rows_total, GMLP_WIDTH), F32))
        out_specs.append(pl.BlockSpec((TILE_ROWS, GMLP_WIDTH), lambda j: (tile_v(j), 0)))

    spare = (TAIL_ROWS, LANES)
    stage_shapes = [((TILE_ROWS, W_CAT_COLS), F32),
                    ((TILE_ROWS, SSD_WIDTH), F32),
                    ((TILE_ROWS, 2 * GROUPS * STATE), BF16),
                    ((TILE_ROWS, SSD_WIDTH + GMLP_WIDTH), BF16)]
    scratch = [pltpu.VMEM((TILE_ROWS, D_MODEL), BF16),
               pltpu.VMEM((TAIL_ROWS, CONV_DIM), F32),
               pltpu.VMEM((TAIL_ROWS + seglen, LANES), F32),
               pltpu.VMEM((STATE, SSD_WIDTH), F32),
               pltpu.VMEM((nck, STATE, SSD_WIDTH), F32),
               pltpu.VMEM((TILE_ROWS, SSD_WIDTH), F32),
               pltpu.VMEM((TILE_ROWS, SSD_WIDTH), F32),
               pltpu.VMEM((TILE_ROWS, SSD_WIDTH), BF16),
               pltpu.VMEM((TILE_ROWS, SSD_WIDTH), BF16),
               pltpu.VMEM((PAIRS, nck * LANES), F32),
               pltpu.VMEM((TILE_ROWS, GMLP_WIDTH), BF16),
               pltpu.VMEM((TILE_ROWS, D_MODEL), F32),
               pltpu.VMEM((nck, STATE, SSD_WIDTH), BF16),
               pltpu.VMEM((TILE_ROWS, SSD_WIDTH), F32),
               pltpu.VMEM((nck * GROUPS, lc_ssd, LANES), F32),
               pltpu.VMEM((TILE_ROWS, SSD_WIDTH), F32)]
    assert len(scratch) == N_SHARED_SCRATCH
    for shape, dtype in stage_shapes:
        scratch += [pltpu.VMEM(shape, dtype), pltpu.VMEM(shape if pipelined else spare, dtype)]
    kern = functools.partial(_layer_kernel, nseq=nseq, seglen=seglen, lc=lc_ssd, nt=nt,
                             pipelined=pipelined, emit_v=emit_v)
    return pl.pallas_call(
        kern,
        grid=(steps,),
        in_specs=in_specs,
        out_specs=out_specs,
        out_shape=out_shape,
        scratch_shapes=scratch,
        compiler_params=pltpu.CompilerParams(
            dimension_semantics=("arbitrary",),
            vmem_limit_bytes=VMEM_LIMIT_BYTES),
        name="hybrid_layer_stream" if pipelined else "hybrid_layer_state",
    )(*args)


def _prep_weights(g_pre, g_post, w_in, conv_w, conv_b, dt_bias, a_log, d_skip, g_v, beta_v,
                  w_s, b_s, w_out):
    w_a, w_b, w_dt = _cast_weights(w_in)
    pad = (0, LANES - HEADS)
    return (g_pre.reshape(1, -1), g_post.reshape(1, -1), w_a, w_b, w_dt, conv_w, conv_b.reshape(1, -1),
            jnp.pad(dt_bias, pad).reshape(1, -1), jnp.pad(a_log, pad).reshape(1, -1),
            jnp.repeat(d_skip, HEAD_DIM).reshape(1, -1), g_v.reshape(1, -1),
            beta_v.reshape(1, -1), w_s, b_s, w_out.astype(BF16))


def kernel(x_prompt, x_sample, state_conv, state_ssm, c_prompt, c_sample, w_ada, b_ada, g_pre,
           g_post, w_in, conv_w, conv_b, dt_bias, a_log, d_skip, g_v, beta_v, w_s, b_s, w_out):
    depth = w_ada.shape[0]
    bp, lp, _ = x_prompt.shape
    bs, ls, _ = x_sample.shape
    seq_per_tile = TILE_ROWS // ls
    yp = x_prompt.reshape(bp * lp, D_MODEL)
    ys = x_sample.reshape(bs * ls, D_MODEL)
    c_all = jnp.concatenate([c_prompt, c_sample], axis=0)
    c_pad = jnp.pad(c_all, ((0, (-c_all.shape[0]) % 8), (0, 0)))
    outs = [[] for _ in range(5)]
    for l in range(depth):
        wl = _prep_weights(g_pre[l], g_post[l], w_in[l], conv_w[l], conv_b[l], dt_bias[l],
                           a_log[l], d_skip[l], g_v[l], beta_v[l], w_s[l], b_s[l], w_out[l])
        mod = _modulation(c_pad, w_ada[l], b_ada[l])
        shift, scale, gate = (mod[:, i * D_MODEL:(i + 1) * D_MODEL] for i in range(3))
        mp = [m[:bp].reshape(bp, 1, D_MODEL) for m in (shift, scale, gate)]
        ms = [m[bp:bp + bs].reshape(bs // seq_per_tile, seq_per_tile, D_MODEL)
              for m in (shift, scale, gate)]
        yp, conv_p, ssm_p = _layer(
            yp, *mp, None, None, wl, nseq=1, seglen=TILE_ROWS, lc_ssd=min(lp, SSD_CHUNK),
            lc_gmlp=min(lp, GMLP_CHUNK), nt=lp // TILE_ROWS, emit_v=False)
        ys, conv_s, ssm_s, v_s = _layer(
            ys, *ms,
            state_conv[l].reshape(bs // seq_per_tile, seq_per_tile, CONV_W - 1, CONV_DIM),
            state_ssm, wl,
            nseq=seq_per_tile, seglen=ls, lc_ssd=min(ls, SSD_CHUNK), lc_gmlp=min(ls, GMLP_CHUNK),
            nt=1, emit_v=True, layer=l)
        outs[0].append(conv_p.reshape(bp, CONV_W - 1, CONV_DIM))
        outs[1].append(ssm_p)
        outs[2].append(conv_s.reshape(bs, CONV_W - 1, CONV_DIM))
        outs[3].append(ssm_s)
        outs[4].append(v_s.reshape(bs, ls, GMLP_WIDTH))
    stacked = [o[0][None] if depth == 1 else jnp.stack(o) for o in outs]
    return (yp.reshape(bp, lp, D_MODEL), ys.reshape(bs, ls, D_MODEL), *stacked)
```

```python
import collections
import functools

import jax
import jax.numpy as jnp
import numpy as np
from jax import lax
from jax.experimental import pallas as pl
from jax.experimental.pallas import tpu as pltpu

F32 = jnp.float32
BF16 = jnp.bfloat16

D_MODEL = 1024
SSD_WIDTH = 1024
HEAD_DIM = 64
HEADS = SSD_WIDTH // HEAD_DIM
GROUPS = 2
GROUP_WIDTH = SSD_WIDTH // GROUPS
STATE = 128
CONV_W = 4
CONV_DIM = SSD_WIDTH + 2 * GROUPS * STATE
GMLP_WIDTH = 1024
GMLP_GROUPS = 8
GMLP_GROUP_DIM = GMLP_WIDTH // GMLP_GROUPS
SSD_CHUNK = 64
GMLP_CHUNK = 128
EPS = 1e-6
LOG2E = 1.4426950408889634

LANES = 128
MXU_COLS = 256
HALF = LANES // 2
PAIRS = HEADS // 2
PAIRS_PER_GROUP = PAIRS // GROUPS
PIECES = 3
TILE_ROWS = 256
TAIL_ROWS = 8
TAP0 = TAIL_ROWS - (CONV_W - 1)
MOD_BLOCK_COLS = 512
VMEM_LIMIT_BYTES = 60 * 1024 * 1024

OFF_Z = 0
OFF_XBC = OFF_Z + SSD_WIDTH
OFF_U = OFF_XBC + CONV_DIM
OFF_V = OFF_U + GMLP_WIDTH
OFF_G = OFF_V + GMLP_WIDTH
OFF_DT = OFF_G + GMLP_WIDTH
W_CAT_COLS = OFF_DT + LANES

Bufs = collections.namedtuple("Bufs", "proj xs bc mix")


def _dot(a, b):
    return jnp.dot(a, b, preferred_element_type=F32)


def _dot_nt(a, b):
    return lax.dot_general(a, b, (((1,), (1,)), ((), ())), preferred_element_type=F32)


def _dot_tn(a, b):
    return lax.dot_general(a, b, (((0,), (0,)), ((), ())), preferred_element_type=F32)


def _cat3(x):
    hi = x.astype(BF16).astype(F32)
    r1 = x - hi
    mid = r1.astype(BF16).astype(F32)
    lo = r1 - mid
    return hi + pltpu.roll(mid, HEADS, axis=1) + pltpu.roll(lo, 2 * HEADS, axis=1)


def _silu(x):
    return x / (1.0 + jnp.exp2(x * (-LOG2E)))


def _softplus(x):
    return jnp.maximum(x, 0.0) + jnp.log1p(jnp.exp(-jnp.abs(x)))


def _row_mean(x):
    width = x.shape[-1]
    part = x[:, 0:LANES]
    for c in range(LANES, width, LANES):
        part = part + x[:, c:c + LANES]
    return jnp.sum(part, axis=-1, keepdims=True) * (1.0 / width)


def _rms_scale(x):
    return lax.rsqrt(_row_mean(x * x) + EPS)


def _mod_kernel(c_ref, w_ref, b_ref, o_ref):
    c = _silu(c_ref[...]).astype(BF16)
    o_ref[...] = _dot(c, w_ref[...].astype(BF16)) + b_ref[...]


def _modulation(c, w_ada, b_ada):
    rows = c.shape[0]
    n = w_ada.shape[1]
    return pl.pallas_call(
        _mod_kernel,
        grid=(n // MOD_BLOCK_COLS,),
        in_specs=[
            pl.BlockSpec((rows, D_MODEL), lambda j: (0, 0)),
            pl.BlockSpec((D_MODEL, MOD_BLOCK_COLS), lambda j: (0, j)),
            pl.BlockSpec((1, MOD_BLOCK_COLS), lambda j: (0, j)),
        ],
        out_specs=pl.BlockSpec((rows, MOD_BLOCK_COLS), lambda j: (0, j)),
        out_shape=jax.ShapeDtypeStruct((rows, n), F32),
        name="adaln_mod",
    )(c, w_ada, b_ada.reshape(1, n))


CAST_BLOCK_COLS = 512
A_COLS = SSD_WIDTH + CONV_DIM
B_COLS = 3 * GMLP_WIDTH


def _cast_kernel(a_ref, b_ref, b_next_ref, wa_ref, wb_ref, wdt_ref):
    j = pl.program_id(0)
    wa_ref[...] = a_ref[...].astype(BF16)
    wide = jnp.concatenate([b_ref[...], b_next_ref[...]], axis=1)
    shifted = pltpu.roll(wide, wide.shape[1] - HEADS, axis=1)
    wb_ref[...] = shifted[:, :CAST_BLOCK_COLS].astype(BF16)

    @pl.when(j == 0)
    def _():
        head = b_ref[:, 0:LANES]
        lane = lax.broadcasted_iota(jnp.int32, head.shape, 1)
        wdt_ref[...] = jnp.where(lane < HEADS, head, 0.0).astype(BF16)


def _cast_weights(w_in):
    a_blocks = A_COLS // CAST_BLOCK_COLS
    b_blocks = B_COLS // CAST_BLOCK_COLS
    lanes_per_block = CAST_BLOCK_COLS // LANES
    return pl.pallas_call(
        _cast_kernel,
        grid=(b_blocks,),
        in_specs=[
            pl.BlockSpec((D_MODEL, CAST_BLOCK_COLS), lambda j: (0, jnp.minimum(j, a_blocks - 1))),
            pl.BlockSpec((D_MODEL, CAST_BLOCK_COLS), lambda j: (0, a_blocks + j)),
            pl.BlockSpec((D_MODEL, LANES), lambda j: (0, (a_blocks + j + 1) * lanes_per_block)),
        ],
        out_specs=[
            pl.BlockSpec((D_MODEL, CAST_BLOCK_COLS), lambda j: (0, jnp.minimum(j, a_blocks - 1))),
            pl.BlockSpec((D_MODEL, CAST_BLOCK_COLS), lambda j: (0, j)),
            pl.BlockSpec((D_MODEL, LANES), lambda j: (0, 0)),
        ],
        out_shape=[jax.ShapeDtypeStruct((D_MODEL, A_COLS), BF16),
                   jax.ShapeDtypeStruct((D_MODEL, B_COLS), BF16),
                   jax.ShapeDtypeStruct((D_MODEL, LANES), BF16)],
        compiler_params=pltpu.CompilerParams(dimension_semantics=("arbitrary",)),
        name="cast_proj_weight",
    )(w_in, w_in, w_in)


TILES_PER_STEP = 2
N_SHARED_SCRATCH = 16
N_STAGE_SCRATCH = 8


def _interleave(primary, secondary):
    total = sum(weight for _, weight in primary)
    groups = []
    done = 0
    seen = 0
    for task, weight in primary:
        seen += weight
        want = (seen * len(secondary)) // total
        groups.append([task] + secondary[done:want])
        done = want
    return groups


def _tag(task, name, needs=()):
    task.name, task.needs = name, tuple(needs)
    return task


def _run_groups(groups):
    stored = set()
    for group in groups:
        for task in group:
            missing = [n for n in getattr(task, "needs", ()) if n not in stored]
            assert not missing, (getattr(task, "name", task), missing)
        commits = [task() for task in group]
        for commit in commits:
            if commit is not None:
                commit()
        stored.update(getattr(task, "name", None) for task in group)


def _layer_kernel(*refs, nseq, seglen, lc, nt, pipelined, emit_v):
    statics = dict(nseq=nseq, seglen=seglen, lc=lc, pipelined=pipelined, emit_v=emit_v)
    shared, stage = refs[:-N_STAGE_SCRATCH], refs[-N_STAGE_SCRATCH:]
    buf_a, buf_b = Bufs(*stage[0::2]), Bufs(*stage[1::2])
    n_scratch = N_SHARED_SCRATCH + N_STAGE_SCRATCH
    ssm_out_ref = refs[len(refs) - n_scratch - (2 if emit_v else 1)]
    tail_ref, st_ref = refs[-n_scratch + 1], refs[-n_scratch + 3]
    if not pipelined:
        _tile_body(shared, buf_a, buf_a, buf_a, buf_a, **statics)
        return

    assert nt % TILES_PER_STEP == 0 and TILES_PER_STEP == 2
    i = pl.program_id(0)
    row_start = lax.rem(TILES_PER_STEP * i, nt) == 0
    n_in = len(shared) - N_SHARED_SCRATCH - 3
    views = []
    for half in range(TILES_PER_STEP):
        rows = slice(half * TILE_ROWS, (half + 1) * TILE_ROWS)
        view = list(shared)
        for k in (0, 1, n_in):
            view[k] = shared[k].at[rows, :]
        views.append(view)

    @pl.when(i == 0)
    def _():
        for ref in buf_b:
            ref[...] = jnp.zeros_like(ref)
        st_ref[...] = jnp.zeros_like(st_ref)

    @pl.when(row_start)
    def _():
        tail_ref[...] = jnp.zeros_like(tail_ref)

    _tile_body(views[0], buf_a, buf_b, buf_a, buf_b, **statics)

    @pl.when(row_start & (i > 0))
    def _():
        ssm_out_ref[0] = st_ref[...].T.reshape(HEADS, HEAD_DIM, STATE)

    @pl.when(row_start)
    def _():
        st_ref[...] = jnp.zeros_like(st_ref)

    _tile_body(views[1], buf_b, buf_a, buf_b, buf_a, **statics)


def _tile_body(refs, bm, bv, bvo, bo, *, nseq, seglen, lc, pipelined, emit_v):
    has_state = not pipelined
    it = iter(refs)
    x_ref, xo_ref = next(it), next(it)
    shift_ref, scale_ref, gate_ref = next(it), next(it), next(it)
    conv0_ref = next(it) if has_state else None
    ssm0_ref = next(it) if has_state else None
    (gpre_ref, gpost_ref, wa_ref, wb_ref, wdt_ref, convw_ref, convb_ref, dtb_ref, alog_ref,
     dskip_ref, gv_ref, betav_ref, mixw_ref, mixb_ref, wout_ref, expand_ref, tril_ref, psel_ref) = (
         next(it) for _ in range(18))
    y_ref, conv_out_ref, ssm_out_ref = next(it), next(it), next(it)
    vn_out_ref = next(it) if emit_v else None
    (h_ref, tail_ref, xp_ref, st_ref, stnew_ref, aexp_ref, xdt_ref, xlo_ref, xhi_ref, acumt_ref,
     vnb_ref, mo_ref, stb_ref, ys_ref, cb_ref, yoff_ref) = (next(it) for _ in range(N_SHARED_SCRATCH))
    pm, pv = bm.proj, bv.proj

    rows_all = slice(0, TILE_ROWS)
    nchunk = seglen // lc
    pad_rows = HALF - lc

    def pad_to_half(v):
        if pad_rows == 0:
            return [v]
        return [v, jnp.zeros((pad_rows, v.shape[1]), v.dtype)]

    def m_norm():
        for s in range(nseq):
            rows = slice(s * seglen, (s + 1) * seglen)
            xr = x_ref[rows, :]
            gain = gpre_ref[...] * (1.0 + scale_ref[s:s + 1, :])
            hs = (xr * _rms_scale(xr)) * gain + shift_ref[s:s + 1, :]
            h_ref[rows, :] = hs.astype(BF16)

    def m_dot(w_ref, w0, dst0, width):
        def task():
            val = _dot(h_ref[...], w_ref[:, w0:w0 + width])

            def commit():
                pm[rows_all, dst0:dst0 + width] = val
            return commit
        return _tag(task, f"m_dot{dst0 // MXU_COLS}")

    dst_starts = (list(range(OFF_XBC, OFF_U, MXU_COLS)) + list(range(OFF_Z, OFF_XBC, MXU_COLS))
                  + list(range(OFF_U, OFF_DT, MXU_COLS)))
    m_dots = [m_dot(wa_ref, d, d, MXU_COLS) if d < OFF_U else m_dot(wb_ref, d - OFF_U, d, MXU_COLS)
              for d in dst_starts] + [m_dot(wdt_ref, 0, OFF_DT, LANES)]

    def conv_store(c, val, rows):
        if c < SSD_WIDTH // LANES:
            bm.xs[rows, c * LANES:(c + 1) * LANES] = val
        else:
            c2 = c - SSD_WIDTH // LANES
            bm.bc[rows, c2 * LANES:(c2 + 1) * LANES] = val.astype(BF16)

    def m_conv(c):
        cols = slice(OFF_XBC + c * LANES, OFF_XBC + (c + 1) * LANES)
        wcols = slice(c * LANES, (c + 1) * LANES)

        def task_stream():
            xfull = jnp.concatenate([tail_ref[:, wcols], pm[:, cols]], axis=0)
            acc = convw_ref[0:1, wcols] * xfull
            for k in range(1, CONV_W):
                acc = pltpu.roll(acc, 1, axis=0) + convw_ref[k:k + 1, wcols] * xfull
            val = _silu(acc[TAIL_ROWS:, :] + convb_ref[:, wcols])
            tail_ref[:, wcols] = pm[TILE_ROWS - TAIL_ROWS:TILE_ROWS, cols]
            conv_out_ref[0, :, wcols] = pm[TILE_ROWS - (CONV_W - 1):TILE_ROWS, cols]

            def commit():
                conv_store(c, val, slice(0, TILE_ROWS))
            return commit

        def task_state():
            for s in range(nseq):
                rows = slice(s * seglen, (s + 1) * seglen)
                xp_ref[TAP0:TAIL_ROWS, :] = conv0_ref[s, :, wcols]
                xp_ref[TAIL_ROWS:TAIL_ROWS + seglen, :] = pm[s * seglen:(s + 1) * seglen, cols]
                acc = convb_ref[:, wcols]
                for k in range(CONV_W):
                    acc = acc + convw_ref[k:k + 1, wcols] * xp_ref[TAP0 + k:TAP0 + k + seglen, :]
                conv_store(c, _silu(acc), rows)
                conv_out_ref[s, :, wcols] = xp_ref[TAIL_ROWS + seglen - (CONV_W - 1):TAIL_ROWS + seglen, :]

        return task_stream if pipelined else task_state

    m_convs = [_tag(m_conv(c), f"m_conv{c}", [f"m_dot{(OFF_XBC + c * LANES) // MXU_COLS}"])
               for c in range(CONV_DIM // LANES)]

    lane1 = lax.broadcasted_iota(jnp.int32, (TILE_ROWS, LANES), 1)

    def v_dt():
        a_row = -jnp.exp(alog_ref[...])
        dt = _softplus(pv[rows_all, OFF_DT:OFF_DT + LANES] + dtb_ref[...])
        dt = jnp.where(lane1 < HEADS, dt, 0.0)
        csum = _dot(tril_ref[...], _cat3(dt * a_row).astype(BF16))
        a_cum = csum + pltpu.roll(csum, LANES - HEADS, axis=1) + pltpu.roll(csum, LANES - 2 * HEADS, axis=1)
        a_cat = _cat3(jnp.where(lane1 < HEADS, a_cum, 0.0))
        aexp_ref[...] = _dot(a_cat.astype(BF16), expand_ref[...])
        dt_exp = _dot(_cat3(dt).astype(BF16), expand_ref[...])
        even = lax.rem(lane1, 2) == 0
        a_even = jnp.where(even, a_cat, 0.0)
        a_odd = (a_cat - a_even).astype(BF16)
        a_even = a_even.astype(BF16)
        parts = []
        for ck in range(TILE_ROWS // lc):
            rows = slice(ck * lc, (ck + 1) * lc)
            parts += pad_to_half(a_even[rows, :]) + pad_to_half(a_odd[rows, :])
        acumt_ref[...] = _dot_nt(psel_ref[...], jnp.concatenate(parts, axis=0))
        xdt = bv.xs[...] * dt_exp
        lane_w = lax.broadcasted_iota(jnp.int32, (TILE_ROWS, SSD_WIDTH), 1)
        xlo = jnp.where(lax.rem(lane_w, LANES) < HALF, xdt, 0.0)
        xdt_ref[...] = xdt
        xlo_ref[...] = xlo.astype(BF16)
        xhi_ref[...] = (xdt - xlo).astype(BF16)

    li = lax.broadcasted_iota(jnp.int32, (lc, LANES), 0)
    si = lax.rem(lax.broadcasted_iota(jnp.int32, (lc, LANES), 1), HALF)
    causal = (si <= li) & (si < lc)

    def v_local(ci, g):
        rows = slice(ci * lc, (ci + 1) * lc)
        hl = slice(g * GROUP_WIDTH, (g + 1) * GROUP_WIDTH)

        def products():
            last = aexp_ref[(ci + 1) * lc - 1:(ci + 1) * lc, hl]
            xw_b = (xdt_ref[rows, hl] * jnp.exp(last - aexp_ref[rows, hl])).astype(BF16)
            b_g = bv.bc[rows, g * STATE:(g + 1) * STATE]
            c_g = bv.bc[rows, (GROUPS + g) * STATE:(GROUPS + g + 1) * STATE]
            b2 = jnp.concatenate(pad_to_half(b_g) + pad_to_half(b_g), axis=0)
            cb_ref[ci * GROUPS + g] = _dot_nt(c_g, b2)
            stnew_ref[ci, :, hl] = _dot_tn(b_g, xw_b)

        def diag():
            cb2 = cb_ref[ci * GROUPS + g]
            for i in range(PAIRS_PER_GROUP):
                p = g * PAIRS_PER_GROUP + i
                pcols = slice(p * LANES, (p + 1) * LANES)
                seg = aexp_ref[rows, pcols] - acumt_ref[p:p + 1, ci * LANES:(ci + 1) * LANES]
                w_p = (cb2 * jnp.exp(jnp.where(causal, seg, -jnp.inf))).astype(BF16)
                bd = jnp.concatenate(pad_to_half(xlo_ref[rows, pcols]) + pad_to_half(xhi_ref[rows, pcols]),
                                     axis=0)
                ys_ref[rows, pcols] = _dot(w_p, bd) + dskip_ref[:, pcols] * bv.xs[rows, pcols]
        return products, diag

    def v_state(ci, s):
        def task():
            e_last = jnp.exp(aexp_ref[(ci + 1) * lc - 1:(ci + 1) * lc, :])
            st = st_ref[...]
            stb_ref[ci] = st.astype(BF16)
            st_ref[...] = st * e_last + stnew_ref[ci]
            if has_state and ci % nchunk == nchunk - 1:
                ssm_out_ref[s] = st_ref[...].T.reshape(HEADS, HEAD_DIM, STATE)
        return task

    def v_readout(ci, g):
        rows = slice(ci * lc, (ci + 1) * lc)
        prow = rows
        hl = slice(g * GROUP_WIDTH, (g + 1) * GROUP_WIDTH)

        def product():
            c_g = bv.bc[rows, (GROUPS + g) * STATE:(GROUPS + g + 1) * STATE]
            yoff_ref[rows, hl] = _dot(c_g, stb_ref[ci, :, hl])

        def gate():
            y = ys_ref[rows, hl] + jnp.exp(aexp_ref[rows, hl]) * yoff_ref[rows, hl]
            zg = _silu(pv[prow, OFF_Z + g * GROUP_WIDTH:OFF_Z + (g + 1) * GROUP_WIDTH])
            bvo.mix[rows, hl] = (y * zg).astype(BF16)
        return product, gate

    def v_state_load(s):
        def task():
            st_ref[...] = ssm0_ref[s].reshape(SSD_WIDTH, STATE).T
        return task

    def v_ln():
        v = pv[rows_all, OFF_V:OFF_V + GMLP_WIDTH]
        vc = v - _row_mean(v)
        vn = (vc * _rms_scale(vc)) * gv_ref[...] + betav_ref[...]
        if emit_v:
            vn_out_ref[...] = vn
        vnb_ref[...] = vn.astype(BF16)

    def v_gmlp(g):
        gcols = slice(g * GMLP_GROUP_DIM, (g + 1) * GMLP_GROUP_DIM)

        def task():
            if mixw_ref.shape[1] == TILE_ROWS:
                mixed = _dot(mixw_ref[g], vnb_ref[:, gcols])
            else:
                lcg = mixw_ref.shape[1]
                blocks = [vnb_ref[r:r + lcg, gcols] for r in range(0, TILE_ROWS, lcg)]
                wide = _dot(mixw_ref[g], jnp.concatenate(blocks, axis=1))
                mixed = jnp.concatenate(
                    [wide[:, k * GMLP_GROUP_DIM:(k + 1) * GMLP_GROUP_DIM] for k in range(len(blocks))],
                    axis=0)
            mixed = mixed + mixb_ref[:, gcols]
            u = pv[rows_all, OFF_U + g * GMLP_GROUP_DIM:OFF_U + (g + 1) * GMLP_GROUP_DIM]
            gt = pv[rows_all, OFF_G + g * GMLP_GROUP_DIM:OFF_G + (g + 1) * GMLP_GROUP_DIM]
            bvo.mix[:, SSD_WIDTH + g * GMLP_GROUP_DIM:SSD_WIDTH + (g + 1) * GMLP_GROUP_DIM] = (
                _silu(gt) * u * mixed).astype(BF16)
        return task

    chunk_groups = [(ci, g) for ci in range(nseq * nchunk) for g in range(GROUPS)]
    v_locals = [v_local(ci, g) for ci, g in chunk_groups]
    v_states = []
    for s in range(nseq):
        if has_state:
            v_states.append(v_state_load(s))
        v_states += [v_state(s * nchunk + c, s) for c in range(nchunk)]
    v_readouts = [v_readout(ci, g) for ci, g in chunk_groups]
    v_scan = ([(t[0], 0) for t in v_locals] + [(t[1], 1) for t in v_locals]
              + [(t, 1) for t in v_states]
              + [(t[0], 0) for t in v_readouts] + [(t[1], 1) for t in v_readouts])
    v_gmlps = [v_gmlp(g) for g in range(GMLP_GROUPS)]

    def o_out(n):
        def task():
            mo_ref[:, n * MXU_COLS:(n + 1) * MXU_COLS] = _dot(
                bo.mix[...], wout_ref[:, n * MXU_COLS:(n + 1) * MXU_COLS])
        return task

    def o_final():
        mo = mo_ref[...]
        mo = mo * _rms_scale(mo)
        for s in range(nseq):
            rows = slice(s * seglen, (s + 1) * seglen)
            gain = gpost_ref[...] * gate_ref[s:s + 1, :]
            y_ref[rows, :] = xo_ref[rows, :] + gain * mo[rows, :]

    o_outs = [_tag(o_out(n), f"o_out{n}") for n in range(D_MODEL // MXU_COLS)]
    _tag(o_final, "o_final", [t.name for t in o_outs])

    if pipelined:
        n_xbc_dots = CONV_DIM // MXU_COLS
        head = _interleave([(m_norm, 1), (v_dt, 2), (v_ln, 2)], o_outs + m_dots[:n_xbc_dots])
        body = [(t, 3) for t in m_convs] + [(o_final, 3)] + v_scan + [(t, 0) for t in v_gmlps]
        groups = head + _interleave(body, m_dots[n_xbc_dots:])
    else:
        order = ([m_norm] + m_dots + m_convs + [v_dt, v_ln] + [t for t, _ in v_scan] + v_gmlps
                 + o_outs + [o_final])
        groups = [[task] for task in order]
    _run_groups(groups)


def _const_spec(shape):
    nd = len(shape)
    return pl.BlockSpec(shape, lambda j: (0,) * nd, pipeline_mode=pl.Buffered(1))


def _chunk_constants(lc):
    expand = np.zeros((LANES, SSD_WIDTH), np.float32)
    psel = np.zeros((PAIRS, LANES), np.float32)
    for piece in range(PIECES):
        for hd in range(HEADS):
            expand[piece * HEADS + hd, hd * HEAD_DIM:(hd + 1) * HEAD_DIM] = 1.0
            psel[hd // 2, piece * HEADS + hd] = 1.0
    r = np.arange(TILE_ROWS)
    tril = ((r[:, None] >= r[None, :]) & (r[:, None] // lc == r[None, :] // lc)).astype(np.float32)
    return jnp.asarray(expand, BF16), jnp.asarray(tril, BF16), jnp.asarray(psel, BF16)


def _mix_weights(w_s, b_s, lc):
    nck = TILE_ROWS // lc
    mask = jnp.tril(jnp.ones((lc, lc), dtype=bool))
    w = jnp.where(mask[None], w_s[:, :lc, :lc], 0)
    if nck * GMLP_GROUP_DIM > MXU_COLS:
        eye = jnp.eye(nck, dtype=w.dtype)
        w = jnp.einsum("ab,gts->gatbs", eye, w).reshape(GMLP_GROUPS, TILE_ROWS, TILE_ROWS)
    bias = jnp.repeat(b_s[:, :lc].T, GMLP_GROUP_DIM, axis=1)
    return w.astype(BF16), jnp.tile(bias, (nck, 1))


def _layer(x, shift, scale, gate, conv0, ssm0, wl, *, nseq, seglen, lc_ssd, lc_gmlp, nt, emit_v,
           layer=0):
    (g_pre, g_post, w_a, w_b, w_dt, conv_w, conv_b, dtb, alog, dskip, g_v, beta_v, w_s, b_s,
     w_out) = wl
    assert lc_ssd == lc_gmlp or nseq == 1
    pipelined = conv0 is None
    nb = shift.shape[0]
    rows_total = x.shape[0]
    ntiles = rows_total // TILE_ROWS
    assert nseq * seglen == TILE_ROWS and nb * nt == ntiles
    expand, tril, psel = _chunk_constants(lc_ssd)
    mixw, mixb = _mix_weights(w_s, b_s, lc_gmlp)
    nck = TILE_ROWS // lc_ssd

    if pipelined:
        tps = TILES_PER_STEP
        assert ntiles % tps == 0

        def tile_m(j):
            return jnp.minimum(tps * j, ntiles - tps)

        def tile_v(j):
            return jnp.clip(tps * j - 1, 0, ntiles - 1)

        def tile_o(j):
            return jnp.maximum(tps * j - 2, 0)
        steps = ntiles // tps + 1
    else:
        tps = 1

        def tile_m(j):
            return j
        tile_v = tile_o = tile_m
        steps = ntiles

    def mod_spec(tile_of):
        return pl.BlockSpec((None, nseq, D_MODEL), lambda j: (tile_of(j) // nt, 0, 0))

    in_specs = [pl.BlockSpec((tps * TILE_ROWS, D_MODEL), lambda j: (tile_m(j) // tps, 0)),
                pl.BlockSpec((tps * TILE_ROWS, D_MODEL), lambda j: (tile_o(j) // tps, 0)),
                mod_spec(tile_m), mod_spec(tile_m), mod_spec(tile_o)]
    args = [x, x, shift, scale, gate]
    if not pipelined:
        in_specs += [pl.BlockSpec((None, nseq, CONV_W - 1, CONV_DIM), lambda j: (j // nt, 0, 0, 0)),
                     pl.BlockSpec((None, nseq, HEADS, HEAD_DIM, STATE), lambda j: (layer, j, 0, 0, 0))]
        args += [conv0, ssm0]
    consts = [g_pre, g_post, w_a, w_b, w_dt, conv_w, conv_b, dtb, alog, dskip, g_v, beta_v, mixw,
              mixb, w_out, expand, tril, psel]
    in_specs += [_const_spec(a.shape) for a in consts]
    args += consts

    out_shape = [jax.ShapeDtypeStruct((rows_total, D_MODEL), F32),
                 jax.ShapeDtypeStruct((nb, nseq, CONV_W - 1, CONV_DIM), F32),
                 jax.ShapeDtypeStruct((nb * nseq, HEADS, HEAD_DIM, STATE), F32)]
    out_specs = [pl.BlockSpec((tps * TILE_ROWS, D_MODEL), lambda j: (tile_o(j) // tps, 0)),
                 pl.BlockSpec((None, nseq, CONV_W - 1, CONV_DIM), lambda j: (tile_m(j) // nt, 0, 0, 0)),
                 pl.BlockSpec((nseq, HEADS, HEAD_DIM, STATE), lambda j: (tile_v(j) // nt, 0, 0, 0))]
    if emit_v:
        out_shape.append(jax.ShapeDtypeStruct((rows_total, GMLP_WIDTH), F32))
        out_specs.append(pl.BlockSpec((TILE_ROWS, GMLP_WIDTH), lambda j: (tile_v(j), 0)))

    spare = (TAIL_ROWS, LANES)
    stage_shapes = [((TILE_ROWS, W_CAT_COLS), F32),
                    ((TILE_ROWS, SSD_WIDTH), F32),
                    ((TILE_ROWS, 2 * GROUPS * STATE), BF16),
                    ((TILE_ROWS, SSD_WIDTH + GMLP_WIDTH), BF16)]
    scratch = [pltpu.VMEM((TILE_ROWS, D_MODEL), BF16),
               pltpu.VMEM((TAIL_ROWS, CONV_DIM), F32),
               pltpu.VMEM((TAIL_ROWS + seglen, LANES), F32),
               pltpu.VMEM((STATE, SSD_WIDTH), F32),
               pltpu.VMEM((nck, STATE, SSD_WIDTH), F32),
               pltpu.VMEM((TILE_ROWS, SSD_WIDTH), F32),
               pltpu.VMEM((TILE_ROWS, SSD_WIDTH), F32),
               pltpu.VMEM((TILE_ROWS, SSD_WIDTH), BF16),
               pltpu.VMEM((TILE_ROWS, SSD_WIDTH), BF16),
               pltpu.VMEM((PAIRS, nck * LANES), F32),
               pltpu.VMEM((TILE_ROWS, GMLP_WIDTH), BF16),
               pltpu.VMEM((TILE_ROWS, D_MODEL), F32),
               pltpu.VMEM((nck, STATE, SSD_WIDTH), BF16),
               pltpu.VMEM((TILE_ROWS, SSD_WIDTH), F32),
               pltpu.VMEM((nck * GROUPS, lc_ssd, LANES), F32),
               pltpu.VMEM((TILE_ROWS, SSD_WIDTH), F32)]
    assert len(scratch) == N_SHARED_SCRATCH
    for shape, dtype in stage_shapes:
        scratch += [pltpu.VMEM(shape, dtype), pltpu.VMEM(shape if pipelined else spare, dtype)]
    kern = functools.partial(_layer_kernel, nseq=nseq, seglen=seglen, lc=lc_ssd, nt=nt,
                             pipelined=pipelined, emit_v=emit_v)
    return pl.pallas_call(
        kern,
        grid=(steps,),
        in_specs=in_specs,
        out_specs=out_specs,
        out_shape=out_shape,
        scratch_shapes=scratch,
        compiler_params=pltpu.CompilerParams(
            dimension_semantics=("arbitrary",),
            vmem_limit_bytes=VMEM_LIMIT_BYTES),
        name="hybrid_layer_stream" if pipelined else "hybrid_layer_state",
    )(*args)


def _prep_weights(g_pre, g_post, w_in, conv_w, conv_b, dt_bias, a_log, d_skip, g_v, beta_v,
                  w_s, b_s, w_out):
    w_a, w_b, w_dt = _cast_weights(w_in)
    pad = (0, LANES - HEADS)
    return (g_pre.reshape(1, -1), g_post.reshape(1, -1), w_a, w_b, w_dt, conv_w, conv_b.reshape(1, -1),
            jnp.pad(dt_bias, pad).reshape(1, -1), jnp.pad(a_log, pad).reshape(1, -1),
            jnp.repeat(d_skip, HEAD_DIM).reshape(1, -1), g_v.reshape(1, -1),
            beta_v.reshape(1, -1), w_s, b_s, w_out.astype(BF16))


def kernel(x_prompt, x_sample, state_conv, state_ssm, c_prompt, c_sample, w_ada, b_ada, g_pre,
           g_post, w_in, conv_w, conv_b, dt_bias, a_log, d_skip, g_v, beta_v, w_s, b_s, w_out):
    depth = w_ada.shape[0]
    bp, lp, _ = x_prompt.shape
    bs, ls, _ = x_sample.shape
    seq_per_tile = TILE_ROWS // ls
    yp = x_prompt.reshape(bp * lp, D_MODEL)
    ys = x_sample.reshape(bs * ls, D_MODEL)
    c_all = jnp.concatenate([c_prompt, c_sample], axis=0)
    c_pad = jnp.pad(c_all, ((0, (-c_all.shape[0]) % 8), (0, 0)))
    outs = [[] for _ in range(5)]
    for l in range(depth):
        wl = _prep_weights(g_pre[l], g_post[l], w_in[l], conv_w[l], conv_b[l], dt_bias[l],
                           a_log[l], d_skip[l], g_v[l], beta_v[l], w_s[l], b_s[l], w_out[l])
        mod = _modulation(c_pad, w_ada[l], b_ada[l])
        shift, scale, gate = (mod[:, i * D_MODEL:(i + 1) * D_MODEL] for i in range(3))
        mp = [m[:bp].reshape(bp, 1, D_MODEL) for m in (shift, scale, gate)]
        ms = [m[bp:bp + bs].reshape(bs // seq_per_tile, seq_per_tile, D_MODEL)
              for m in (shift, scale, gate)]
        yp, conv_p, ssm_p = _layer(
            yp, *mp, None, None, wl, nseq=1, seglen=TILE_ROWS, lc_ssd=min(lp, SSD_CHUNK),
            lc_gmlp=min(lp, GMLP_CHUNK), nt=lp // TILE_ROWS, emit_v=False)
        ys, conv_s, ssm_s, v_s = _layer(
            ys, *ms,
            state_conv[l].reshape(bs // seq_per_tile, seq_per_tile, CONV_W - 1, CONV_DIM),
            state_ssm, wl,
            nseq=seq_per_tile, seglen=ls, lc_ssd=min(ls, SSD_CHUNK), lc_gmlp=min(ls, GMLP_CHUNK),
            nt=1, emit_v=True, layer=l)
        outs[0].append(conv_p.reshape(bp, CONV_W - 1, CONV_DIM))
        outs[1].append(ssm_p)
        outs[2].append(conv_s.reshape(bs, CONV_W - 1, CONV_DIM))
        outs[3].append(ssm_s)
        outs[4].append(v_s.reshape(bs, ls, GMLP_WIDTH))
    stacked = [o[0][None] if depth == 1 else jnp.stack(o) for o in outs]
    return (yp.reshape(bp, lp, D_MODEL), ys.reshape(bs, ls, D_MODEL), *stacked)
```

```python
import collections
import functools

import jax
import jax.numpy as jnp
import numpy as np
from jax import lax
from jax.experimental import pallas as pl
from jax.experimental.pallas import tpu as pltpu

F32 = jnp.float32
BF16 = jnp.bfloat16

D_MODEL = 1024
SSD_WIDTH = 1024
HEAD_DIM = 64
HEADS = SSD_WIDTH // HEAD_DIM
GROUPS = 2
GROUP_WIDTH = SSD_WIDTH // GROUPS
STATE = 128
CONV_W = 4
CONV_DIM = SSD_WIDTH + 2 * GROUPS * STATE
GMLP_WIDTH = 1024
GMLP_GROUPS = 8
GMLP_GROUP_DIM = GMLP_WIDTH // GMLP_GROUPS
SSD_CHUNK = 64
GMLP_CHUNK = 128
EPS = 1e-6
LOG2E = 1.4426950408889634

LANES = 128
MXU_COLS = 256
HALF = LANES // 2
PAIRS = HEADS // 2
PAIRS_PER_GROUP = PAIRS // GROUPS
PIECES = 3
TILE_ROWS = 256
TAIL_ROWS = 8
TAP0 = TAIL_ROWS - (CONV_W - 1)
MOD_BLOCK_COLS = 1024
VMEM_LIMIT_BYTES = 60 * 1024 * 1024

OFF_Z = 0
OFF_XBC = OFF_Z + SSD_WIDTH
OFF_U = OFF_XBC + CONV_DIM
OFF_V = OFF_U + GMLP_WIDTH
OFF_G = OFF_V + GMLP_WIDTH
OFF_DT = OFF_G + GMLP_WIDTH
W_CAT_COLS = OFF_DT + LANES

Bufs = collections.namedtuple("Bufs", "proj xs bc mix")


def _dot(a, b):
    return jnp.dot(a, b, preferred_element_type=F32)


def _dot_nt(a, b):
    return lax.dot_general(a, b, (((1,), (1,)), ((), ())), preferred_element_type=F32)


def _dot_tn(a, b):
    return lax.dot_general(a, b, (((0,), (0,)), ((), ())), preferred_element_type=F32)


def _cat3(x):
    hi = x.astype(BF16).astype(F32)
    r1 = x - hi
    mid = r1.astype(BF16).astype(F32)
    lo = r1 - mid
    return hi + pltpu.roll(mid, HEADS, axis=1) + pltpu.roll(lo, 2 * HEADS, axis=1)


def _silu(x):
    return x / (1.0 + jnp.exp2(x * (-LOG2E)))


def _softplus(x):
    return jnp.maximum(x, 0.0) + jnp.log1p(jnp.exp(-jnp.abs(x)))


def _row_mean(x):
    width = x.shape[-1]
    part = x[:, 0:LANES]
    for c in range(LANES, width, LANES):
        part = part + x[:, c:c + LANES]
    return jnp.sum(part, axis=-1, keepdims=True) * (1.0 / width)


def _rms_scale(x):
    return lax.rsqrt(_row_mean(x * x) + EPS)


def _mod_kernel(c_ref, w_ref, b_ref, o_ref):
    c = _silu(c_ref[...]).astype(BF16)
    o_ref[...] = _dot(c, w_ref[...].astype(BF16)) + b_ref[...]


def _modulation(c, w_ada, b_ada):
    rows = c.shape[0]
    n = w_ada.shape[1]
    return pl.pallas_call(
        _mod_kernel,
        grid=(n // MOD_BLOCK_COLS,),
        in_specs=[
            pl.BlockSpec((rows, D_MODEL), lambda j: (0, 0)),
            pl.BlockSpec((D_MODEL, MOD_BLOCK_COLS), lambda j: (0, j)),
            pl.BlockSpec((1, MOD_BLOCK_COLS), lambda j: (0, j)),
        ],
        out_specs=pl.BlockSpec((rows, MOD_BLOCK_COLS), lambda j: (0, j)),
        out_shape=jax.ShapeDtypeStruct((rows, n), F32),
        name="adaln_mod",
    )(c, w_ada, b_ada.reshape(1, n))


CAST_BLOCK_COLS = 512
A_COLS = SSD_WIDTH + CONV_DIM
B_COLS = 3 * GMLP_WIDTH


def _cast_kernel(a_ref, b_ref, b_next_ref, wa_ref, wb_ref, wdt_ref):
    j = pl.program_id(0)
    wa_ref[...] = a_ref[...].astype(BF16)
    wide = jnp.concatenate([b_ref[...], b_next_ref[...]], axis=1)
    shifted = pltpu.roll(wide, wide.shape[1] - HEADS, axis=1)
    wb_ref[...] = shifted[:, :CAST_BLOCK_COLS].astype(BF16)

    @pl.when(j == 0)
    def _():
        head = b_ref[:, 0:LANES]
        lane = lax.broadcasted_iota(jnp.int32, head.shape, 1)
        wdt_ref[...] = jnp.where(lane < HEADS, head, 0.0).astype(BF16)


def _cast_weights(w_in, layer):
    a_blocks = A_COLS // CAST_BLOCK_COLS
    b_blocks = B_COLS // CAST_BLOCK_COLS
    lanes_per_block = CAST_BLOCK_COLS // LANES
    return pl.pallas_call(
        _cast_kernel,
        grid=(b_blocks,),
        in_specs=[
            pl.BlockSpec((None, D_MODEL, CAST_BLOCK_COLS),
                         lambda j: (layer, 0, jnp.minimum(j, a_blocks - 1))),
            pl.BlockSpec((None, D_MODEL, CAST_BLOCK_COLS), lambda j: (layer, 0, a_blocks + j)),
            pl.BlockSpec((None, D_MODEL, LANES),
                         lambda j: (layer, 0, (a_blocks + j + 1) * lanes_per_block)),
        ],
        out_specs=[
            pl.BlockSpec((D_MODEL, CAST_BLOCK_COLS), lambda j: (0, jnp.minimum(j, a_blocks - 1))),
            pl.BlockSpec((D_MODEL, CAST_BLOCK_COLS), lambda j: (0, j)),
            pl.BlockSpec((D_MODEL, LANES), lambda j: (0, 0)),
        ],
        out_shape=[jax.ShapeDtypeStruct((D_MODEL, A_COLS), BF16),
                   jax.ShapeDtypeStruct((D_MODEL, B_COLS), BF16),
                   jax.ShapeDtypeStruct((D_MODEL, LANES), BF16)],
        compiler_params=pltpu.CompilerParams(dimension_semantics=("arbitrary",)),
        name="cast_proj_weight",
    )(w_in, w_in, w_in)


TILES_PER_STEP = 2
N_SHARED_SCRATCH = 16
N_STAGE_SCRATCH = 8


def _interleave(primary, secondary):
    total = sum(weight for _, weight in primary)
    groups = []
    done = 0
    seen = 0
    for task, weight in primary:
        seen += weight
        want = (seen * len(secondary)) // total
        groups.append([task] + secondary[done:want])
        done = want
    return groups


def _tag(task, name, needs=()):
    task.name, task.needs = name, tuple(needs)
    return task


def _run_groups(groups):
    stored = set()
    for group in groups:
        for task in group:
            missing = [n for n in getattr(task, "needs", ()) if n not in stored]
            assert not missing, (getattr(task, "name", task), missing)
        commits = [task() for task in group]
        for commit in commits:
            if commit is not None:
                commit()
        stored.update(getattr(task, "name", None) for task in group)


def _layer_kernel(*refs, nseq, seglen, lc, nt, pipelined, emit_v):
    statics = dict(nseq=nseq, seglen=seglen, lc=lc, pipelined=pipelined, emit_v=emit_v)
    shared, stage = refs[:-N_STAGE_SCRATCH], refs[-N_STAGE_SCRATCH:]
    buf_a, buf_b = Bufs(*stage[0::2]), Bufs(*stage[1::2])
    n_scratch = N_SHARED_SCRATCH + N_STAGE_SCRATCH
    ssm_out_ref = refs[len(refs) - n_scratch - (2 if emit_v else 1)]
    tail_ref, st_ref = refs[-n_scratch + 1], refs[-n_scratch + 3]
    if not pipelined:
        _tile_body(shared, buf_a, buf_a, buf_a, buf_a, **statics)
        return

    assert nt % TILES_PER_STEP == 0 and TILES_PER_STEP == 2
    i = pl.program_id(0)
    row_start = lax.rem(TILES_PER_STEP * i, nt) == 0
    n_in = len(shared) - N_SHARED_SCRATCH - 3
    views = []
    for half in range(TILES_PER_STEP):
        rows = slice(half * TILE_ROWS, (half + 1) * TILE_ROWS)
        view = list(shared)
        for k in (0, 1, n_in):
            view[k] = shared[k].at[rows, :]
        views.append(view)

    @pl.when(i == 0)
    def _():
        for ref in buf_b:
            ref[...] = jnp.zeros_like(ref)
        st_ref[...] = jnp.zeros_like(st_ref)

    @pl.when(row_start)
    def _():
        tail_ref[...] = jnp.zeros_like(tail_ref)

    _tile_body(views[0], buf_a, buf_b, buf_a, buf_b, **statics)

    @pl.when(row_start & (i > 0))
    def _():
        ssm_out_ref[0] = st_ref[...].T.reshape(HEADS, HEAD_DIM, STATE)

    @pl.when(row_start)
    def _():
        st_ref[...] = jnp.zeros_like(st_ref)

    _tile_body(views[1], buf_b, buf_a, buf_b, buf_a, **statics)


def _tile_body(refs, bm, bv, bvo, bo, *, nseq, seglen, lc, pipelined, emit_v):
    has_state = not pipelined
    it = iter(refs)
    x_ref, xo_ref = next(it), next(it)
    shift_ref, scale_ref, gate_ref = next(it), next(it), next(it)
    conv0_ref = next(it) if has_state else None
    ssm0_ref = next(it) if has_state else None
    (gpre_ref, gpost_ref, wa_ref, wb_ref, wdt_ref, convw_ref, convb_ref, dtb_ref, alog_ref,
     dskip_ref, gv_ref, betav_ref, mixw_ref, mixb_ref, wout_ref, expand_ref, tril_ref, psel_ref) = (
         next(it) for _ in range(18))
    y_ref, conv_out_ref, ssm_out_ref = next(it), next(it), next(it)
    vn_out_ref = next(it) if emit_v else None
    (h_ref, tail_ref, xp_ref, st_ref, stnew_ref, aexp_ref, xdt_ref, xlo_ref, xhi_ref, acumt_ref,
     vnb_ref, mo_ref, stb_ref, ys_ref, cb_ref, yoff_ref) = (next(it) for _ in range(N_SHARED_SCRATCH))
    pm, pv = bm.proj, bv.proj

    rows_all = slice(0, TILE_ROWS)
    nchunk = seglen // lc
    pad_rows = HALF - lc

    def pad_to_half(v):
        if pad_rows == 0:
            return [v]
        return [v, jnp.zeros((pad_rows, v.shape[1]), v.dtype)]

    def m_norm():
        for s in range(nseq):
            rows = slice(s * seglen, (s + 1) * seglen)
            xr = x_ref[rows, :]
            gain = gpre_ref[...] * (1.0 + scale_ref[s:s + 1, :])
            hs = (xr * _rms_scale(xr)) * gain + shift_ref[s:s + 1, :]
            h_ref[rows, :] = hs.astype(BF16)

    def m_dot(w_ref, w0, dst0, width):
        def task():
            val = _dot(h_ref[...], w_ref[:, w0:w0 + width])

            def commit():
                pm[rows_all, dst0:dst0 + width] = val
            return commit
        return _tag(task, f"m_dot{dst0 // MXU_COLS}")

    dst_starts = (list(range(OFF_XBC, OFF_U, MXU_COLS)) + list(range(OFF_Z, OFF_XBC, MXU_COLS))
                  + list(range(OFF_U, OFF_DT, MXU_COLS)))
    m_dots = [m_dot(wa_ref, d, d, MXU_COLS) if d < OFF_U else m_dot(wb_ref, d - OFF_U, d, MXU_COLS)
              for d in dst_starts] + [m_dot(wdt_ref, 0, OFF_DT, LANES)]

    def conv_store(c, val, rows):
        if c < SSD_WIDTH // LANES:
            bm.xs[rows, c * LANES:(c + 1) * LANES] = val
        else:
            c2 = c - SSD_WIDTH // LANES
            bm.bc[rows, c2 * LANES:(c2 + 1) * LANES] = val.astype(BF16)

    def m_conv(c):
        cols = slice(OFF_XBC + c * LANES, OFF_XBC + (c + 1) * LANES)
        wcols = slice(c * LANES, (c + 1) * LANES)

        def task_stream():
            xfull = jnp.concatenate([tail_ref[:, wcols], pm[:, cols]], axis=0)
            acc = convw_ref[0:1, wcols] * xfull
            for k in range(1, CONV_W):
                acc = pltpu.roll(acc, 1, axis=0) + convw_ref[k:k + 1, wcols] * xfull
            val = _silu(acc[TAIL_ROWS:, :] + convb_ref[:, wcols])
            tail_ref[:, wcols] = pm[TILE_ROWS - TAIL_ROWS:TILE_ROWS, cols]
            conv_out_ref[0, :, wcols] = pm[TILE_ROWS - (CONV_W - 1):TILE_ROWS, cols]

            def commit():
                conv_store(c, val, slice(0, TILE_ROWS))
            return commit

        def task_state():
            for s in range(nseq):
                rows = slice(s * seglen, (s + 1) * seglen)
                xp_ref[TAP0:TAIL_ROWS, :] = conv0_ref[s, :, wcols]
                xp_ref[TAIL_ROWS:TAIL_ROWS + seglen, :] = pm[s * seglen:(s + 1) * seglen, cols]
                acc = convb_ref[:, wcols]
                for k in range(CONV_W):
                    acc = acc + convw_ref[k:k + 1, wcols] * xp_ref[TAP0 + k:TAP0 + k + seglen, :]
                conv_store(c, _silu(acc), rows)
                conv_out_ref[s, :, wcols] = xp_ref[TAIL_ROWS + seglen - (CONV_W - 1):TAIL_ROWS + seglen, :]

        return task_stream if pipelined else task_state

    m_convs = [_tag(m_conv(c), f"m_conv{c}", [f"m_dot{(OFF_XBC + c * LANES) // MXU_COLS}"])
               for c in range(CONV_DIM // LANES)]

    lane1 = lax.broadcasted_iota(jnp.int32, (TILE_ROWS, LANES), 1)

    def v_dt():
        a_row = -jnp.exp(alog_ref[...])
        dt = _softplus(pv[rows_all, OFF_DT:OFF_DT + LANES] + dtb_ref[...])
        dt = jnp.where(lane1 < HEADS, dt, 0.0)
        csum = _dot(tril_ref[...], _cat3(dt * a_row).astype(BF16))
        a_cum = csum + pltpu.roll(csum, LANES - HEADS, axis=1) + pltpu.roll(csum, LANES - 2 * HEADS, axis=1)
        a_cat = _cat3(jnp.where(lane1 < HEADS, a_cum, 0.0))
        aexp_ref[...] = _dot(a_cat.astype(BF16), expand_ref[...])
        dt_exp = _dot(_cat3(dt).astype(BF16), expand_ref[...])
        even = lax.rem(lane1, 2) == 0
        a_even = jnp.where(even, a_cat, 0.0)
        a_odd = (a_cat - a_even).astype(BF16)
        a_even = a_even.astype(BF16)
        parts = []
        for ck in range(TILE_ROWS // lc):
            rows = slice(ck * lc, (ck + 1) * lc)
            parts += pad_to_half(a_even[rows, :]) + pad_to_half(a_odd[rows, :])
        acumt_ref[...] = _dot_nt(psel_ref[...], jnp.concatenate(parts, axis=0))
        xdt = bv.xs[...] * dt_exp
        lane_w = lax.broadcasted_iota(jnp.int32, (TILE_ROWS, SSD_WIDTH), 1)
        xlo = jnp.where(lax.rem(lane_w, LANES) < HALF, xdt, 0.0)
        xdt_ref[...] = xdt
        xlo_ref[...] = xlo.astype(BF16)
        xhi_ref[...] = (xdt - xlo).astype(BF16)

    li = lax.broadcasted_iota(jnp.int32, (lc, LANES), 0)
    si = lax.rem(lax.broadcasted_iota(jnp.int32, (lc, LANES), 1), HALF)
    causal = (si <= li) & (si < lc)

    def v_local(ci, g):
        rows = slice(ci * lc, (ci + 1) * lc)
        hl = slice(g * GROUP_WIDTH, (g + 1) * GROUP_WIDTH)

        def products():
            last = aexp_ref[(ci + 1) * lc - 1:(ci + 1) * lc, hl]
            xw_b = (xdt_ref[rows, hl] * jnp.exp(last - aexp_ref[rows, hl])).astype(BF16)
            b_g = bv.bc[rows, g * STATE:(g + 1) * STATE]
            c_g = bv.bc[rows, (GROUPS + g) * STATE:(GROUPS + g + 1) * STATE]
            b2 = jnp.concatenate(pad_to_half(b_g) + pad_to_half(b_g), axis=0)
            cb_ref[ci * GROUPS + g] = _dot_nt(c_g, b2)
            stnew_ref[ci, :, hl] = _dot_tn(b_g, xw_b)

        def diag():
            cb2 = cb_ref[ci * GROUPS + g]
            for i in range(PAIRS_PER_GROUP):
                p = g * PAIRS_PER_GROUP + i
                pcols = slice(p * LANES, (p + 1) * LANES)
                seg = aexp_ref[rows, pcols] - acumt_ref[p:p + 1, ci * LANES:(ci + 1) * LANES]
                w_p = (cb2 * jnp.exp(jnp.where(causal, seg, -jnp.inf))).astype(BF16)
                bd = jnp.concatenate(pad_to_half(xlo_ref[rows, pcols]) + pad_to_half(xhi_ref[rows, pcols]),
                                     axis=0)
                ys_ref[rows, pcols] = _dot(w_p, bd) + dskip_ref[:, pcols] * bv.xs[rows, pcols]
        return products, diag

    def v_state(ci, s):
        def task():
            e_last = jnp.exp(aexp_ref[(ci + 1) * lc - 1:(ci + 1) * lc, :])
            st = st_ref[...]
            stb_ref[ci] = st.astype(BF16)
            st_ref[...] = st * e_last + stnew_ref[ci]
            if has_state and ci % nchunk == nchunk - 1:
                ssm_out_ref[s] = st_ref[...].T.reshape(HEADS, HEAD_DIM, STATE)
        return task

    def v_readout(ci, g):
        rows = slice(ci * lc, (ci + 1) * lc)
        prow = rows
        hl = slice(g * GROUP_WIDTH, (g + 1) * GROUP_WIDTH)

        def product():
            c_g = bv.bc[rows, (GROUPS + g) * STATE:(GROUPS + g + 1) * STATE]
            yoff_ref[rows, hl] = _dot(c_g, stb_ref[ci, :, hl])

        def gate():
            y = ys_ref[rows, hl] + jnp.exp(aexp_ref[rows, hl]) * yoff_ref[rows, hl]
            zg = _silu(pv[prow, OFF_Z + g * GROUP_WIDTH:OFF_Z + (g + 1) * GROUP_WIDTH])
            bvo.mix[rows, hl] = (y * zg).astype(BF16)
        return product, gate

    def v_state_load(s):
        def task():
            st_ref[...] = ssm0_ref[s].reshape(SSD_WIDTH, STATE).T
        return task

    def v_ln():
        v = pv[rows_all, OFF_V:OFF_V + GMLP_WIDTH]
        vc = v - _row_mean(v)
        vn = (vc * _rms_scale(vc)) * gv_ref[...] + betav_ref[...]
        if emit_v:
            vn_out_ref[...] = vn
        vnb_ref[...] = vn.astype(BF16)

    def v_gmlp(g):
        gcols = slice(g * GMLP_GROUP_DIM, (g + 1) * GMLP_GROUP_DIM)

        def task():
            if mixw_ref.shape[1] == TILE_ROWS:
                mixed = _dot(mixw_ref[g], vnb_ref[:, gcols])
            else:
                lcg = mixw_ref.shape[1]
                blocks = [vnb_ref[r:r + lcg, gcols] for r in range(0, TILE_ROWS, lcg)]
                wide = _dot(mixw_ref[g], jnp.concatenate(blocks, axis=1))
                mixed = jnp.concatenate(
                    [wide[:, k * GMLP_GROUP_DIM:(k + 1) * GMLP_GROUP_DIM] for k in range(len(blocks))],
                    axis=0)
            mixed = mixed + mixb_ref[:, gcols]
            u = pv[rows_all, OFF_U + g * GMLP_GROUP_DIM:OFF_U + (g + 1) * GMLP_GROUP_DIM]
            gt = pv[rows_all, OFF_G + g * GMLP_GROUP_DIM:OFF_G + (g + 1) * GMLP_GROUP_DIM]
            bvo.mix[:, SSD_WIDTH + g * GMLP_GROUP_DIM:SSD_WIDTH + (g + 1) * GMLP_GROUP_DIM] = (
                _silu(gt) * u * mixed).astype(BF16)
        return task

    chunk_groups = [(ci, g) for ci in range(nseq * nchunk) for g in range(GROUPS)]
    v_locals = [v_local(ci, g) for ci, g in chunk_groups]
    v_states = []
    for s in range(nseq):
        if has_state:
            v_states.append(v_state_load(s))
        v_states += [v_state(s * nchunk + c, s) for c in range(nchunk)]
    v_readouts = [v_readout(ci, g) for ci, g in chunk_groups]
    v_scan = ([(t[0], 0) for t in v_locals] + [(t[1], 1) for t in v_locals]
              + [(t, 1) for t in v_states]
              + [(t[0], 0) for t in v_readouts] + [(t[1], 1) for t in v_readouts])
    v_gmlps = [v_gmlp(g) for g in range(GMLP_GROUPS)]

    def o_out(n):
        def task():
            mo_ref[:, n * MXU_COLS:(n + 1) * MXU_COLS] = _dot(
                bo.mix[...], wout_ref[:, n * MXU_COLS:(n + 1) * MXU_COLS])
        return task

    def o_final():
        mo = mo_ref[...]
        mo = mo * _rms_scale(mo)
        for s in range(nseq):
            rows = slice(s * seglen, (s + 1) * seglen)
            gain = gpost_ref[...] * gate_ref[s:s + 1, :]
            y_ref[rows, :] = xo_ref[rows, :] + gain * mo[rows, :]

    o_outs = [_tag(o_out(n), f"o_out{n}") for n in range(D_MODEL // MXU_COLS)]
    _tag(o_final, "o_final", [t.name for t in o_outs])

    if pipelined:
        n_xbc_dots = CONV_DIM // MXU_COLS
        head = _interleave([(m_norm, 1), (v_dt, 2), (v_ln, 2)], o_outs + m_dots[:n_xbc_dots])
        body = [(t, 3) for t in m_convs] + [(o_final, 3)] + v_scan + [(t, 0) for t in v_gmlps]
        groups = head + _interleave(body, m_dots[n_xbc_dots:])
    else:
        order = ([m_norm] + m_dots + m_convs + [v_dt, v_ln] + [t for t, _ in v_scan] + v_gmlps
                 + o_outs + [o_final])
        groups = [[task] for task in order]
    _run_groups(groups)


def _const_spec(shape):
    nd = len(shape)
    return pl.BlockSpec(shape, lambda j: (0,) * nd, pipeline_mode=pl.Buffered(1))


def _chunk_constants(lc):
    expand = np.zeros((LANES, SSD_WIDTH), np.float32)
    psel = np.zeros((PAIRS, LANES), np.float32)
    for piece in range(PIECES):
        for hd in range(HEADS):
            expand[piece * HEADS + hd, hd * HEAD_DIM:(hd + 1) * HEAD_DIM] = 1.0
            psel[hd // 2, piece * HEADS + hd] = 1.0
    r = np.arange(TILE_ROWS)
    tril = ((r[:, None] >= r[None, :]) & (r[:, None] // lc == r[None, :] // lc)).astype(np.float32)
    return jnp.asarray(expand, BF16), jnp.asarray(tril, BF16), jnp.asarray(psel, BF16)


def _mix_weights(w_s, b_s, lc):
    nck = TILE_ROWS // lc
    mask = jnp.tril(jnp.ones((lc, lc), dtype=bool))
    w = jnp.where(mask[None], w_s[:, :lc, :lc], 0)
    if nck * GMLP_GROUP_DIM > MXU_COLS:
        eye = jnp.eye(nck, dtype=w.dtype)
        w = jnp.einsum("ab,gts->gatbs", eye, w).reshape(GMLP_GROUPS, TILE_ROWS, TILE_ROWS)
    bias = jnp.repeat(b_s[:, :lc].T, GMLP_GROUP_DIM, axis=1)
    return w.astype(BF16), jnp.tile(bias, (nck, 1))


def _layer(x, shift, scale, gate, conv0, ssm0, wl, *, nseq, seglen, lc_ssd, lc_gmlp, nt, emit_v,
           layer=0):
    (g_pre, g_post, w_a, w_b, w_dt, conv_w, conv_b, dtb, alog, dskip, g_v, beta_v, w_s, b_s,
     w_out) = wl
    assert lc_ssd == lc_gmlp or nseq == 1
    pipelined = conv0 is None
    nb = shift.shape[0]
    rows_total = x.shape[0]
    ntiles = rows_total // TILE_ROWS
    assert nseq * seglen == TILE_ROWS and nb * nt == ntiles
    expand, tril, psel = _chunk_constants(lc_ssd)
    mixw, mixb = _mix_weights(w_s, b_s, lc_gmlp)
    nck = TILE_ROWS // lc_ssd

    if pipelined:
        tps = TILES_PER_STEP
        assert ntiles % tps == 0

        def tile_m(j):
            return jnp.minimum(tps * j, ntiles - tps)

        def tile_v(j):
            return jnp.clip(tps * j - 1, 0, ntiles - 1)

        def tile_o(j):
            return jnp.maximum(tps * j - 2, 0)
        steps = ntiles // tps + 1
    else:
        tps = 1

        def tile_m(j):
            return j
        tile_v = tile_o = tile_m
        steps = ntiles

    def mod_spec(tile_of):
        return pl.BlockSpec((None, nseq, D_MODEL), lambda j: (tile_of(j) // nt, 0, 0))

    in_specs = [pl.BlockSpec((tps * TILE_ROWS, D_MODEL), lambda j: (tile_m(j) // tps, 0)),
                pl.BlockSpec((tps * TILE_ROWS, D_MODEL), lambda j: (tile_o(j) // tps, 0)),
                mod_spec(tile_m), mod_spec(tile_m), mod_spec(tile_o)]
    args = [x, x, shift, scale, gate]
    if not pipelined:
        in_specs += [pl.BlockSpec((None, nseq, CONV_W - 1, CONV_DIM), lambda j: (j // nt, 0, 0, 0)),
                     pl.BlockSpec((None, nseq, HEADS, HEAD_DIM, STATE), lambda j: (layer, j, 0, 0, 0))]
        args += [conv0, ssm0]
    consts = [g_pre, g_post, w_a, w_b, w_dt, conv_w, conv_b, dtb, alog, dskip, g_v, beta_v, mixw,
              mixb, w_out, expand, tril, psel]
    in_specs += [_const_spec(a.shape) for a in consts]
    args += consts

    out_shape = [jax.ShapeDtypeStruct((rows_total, D_MODEL), F32),
                 jax.ShapeDtypeStruct((nb, nseq, CONV_W - 1, CONV_DIM), F32),
                 jax.ShapeDtypeStruct((nb * nseq, HEADS, HEAD_DIM, STATE), F32)]
    out_specs = [pl.BlockSpec((tps * TILE_ROWS, D_MODEL), lambda j: (tile_o(j) // tps, 0)),
                 pl.BlockSpec((None, nseq, CONV_W - 1, CONV_DIM), lambda j: (tile_m(j) // nt, 0, 0, 0)),
                 pl.BlockSpec((nseq, HEADS, HEAD_DIM, STATE), lambda j: (tile_v(j) // nt, 0, 0, 0))]
    if emit_v:
        out_shape.append(jax.ShapeDtypeStruct((rows_total, GMLP_WIDTH), F32))
        out_specs.append(pl.BlockSpec((TILE_ROWS, GMLP_WIDTH), lambda j: (tile_v(j), 0)))

    spare = (TAIL_ROWS, LANES)
    stage_shapes = [((TILE_ROWS, W_CAT_COLS), F32),
                    ((TILE_ROWS, SSD_WIDTH), F32),
                    ((TILE_ROWS, 2 * GROUPS * STATE), BF16),
                    ((TILE_ROWS, SSD_WIDTH + GMLP_WIDTH), BF16)]
    scratch = [pltpu.VMEM((TILE_ROWS, D_MODEL), BF16),
               pltpu.VMEM((TAIL_ROWS, CONV_DIM), F32),
               pltpu.VMEM((TAIL_ROWS + seglen, LANES), F32),
               pltpu.VMEM((STATE, SSD_WIDTH), F32),
               pltpu.VMEM((nck, STATE, SSD_WIDTH), F32),
               pltpu.VMEM((TILE_ROWS, SSD_WIDTH), F32),
               pltpu.VMEM((TILE_ROWS, SSD_WIDTH), F32),
               pltpu.VMEM((TILE_ROWS, SSD_WIDTH), BF16),
               pltpu.VMEM((TILE_ROWS, SSD_WIDTH), BF16),
               pltpu.VMEM((PAIRS, nck * LANES), F32),
               pltpu.VMEM((TILE_ROWS, GMLP_WIDTH), BF16),
               pltpu.VMEM((TILE_ROWS, D_MODEL), F32),
               pltpu.VMEM((nck, STATE, SSD_WIDTH), BF16),
               pltpu.VMEM((TILE_ROWS, SSD_WIDTH), F32),
               pltpu.VMEM((nck * GROUPS, lc_ssd, LANES), F32),
               pltpu.VMEM((TILE_ROWS, SSD_WIDTH), F32)]
    assert len(scratch) == N_SHARED_SCRATCH
    for shape, dtype in stage_shapes:
        scratch += [pltpu.VMEM(shape, dtype), pltpu.VMEM(shape if pipelined else spare, dtype)]
    kern = functools.partial(_layer_kernel, nseq=nseq, seglen=seglen, lc=lc_ssd, nt=nt,
                             pipelined=pipelined, emit_v=emit_v)
    return pl.pallas_call(
        kern,
        grid=(steps,),
        in_specs=in_specs,
        out_specs=out_specs,
        out_shape=out_shape,
        scratch_shapes=scratch,
        compiler_params=pltpu.CompilerParams(
            dimension_semantics=("arbitrary",),
            vmem_limit_bytes=VMEM_LIMIT_BYTES),
        name="hybrid_layer_stream" if pipelined else "hybrid_layer_state",
    )(*args)


def _prep_weights(g_pre, g_post, w_in, layer, conv_w, conv_b, dt_bias, a_log, d_skip, g_v, beta_v,
                  w_s, b_s, w_out):
    w_a, w_b, w_dt = _cast_weights(w_in, layer)
    pad = (0, LANES - HEADS)
    return (g_pre.reshape(1, -1), g_post.reshape(1, -1), w_a, w_b, w_dt, conv_w, conv_b.reshape(1, -1),
            jnp.pad(dt_bias, pad).reshape(1, -1), jnp.pad(a_log, pad).reshape(1, -1),
            jnp.repeat(d_skip, HEAD_DIM).reshape(1, -1), g_v.reshape(1, -1),
            beta_v.reshape(1, -1), w_s, b_s, w_out.astype(BF16))


def kernel(x_prompt, x_sample, state_conv, state_ssm, c_prompt, c_sample, w_ada, b_ada, g_pre,
           g_post, w_in, conv_w, conv_b, dt_bias, a_log, d_skip, g_v, beta_v, w_s, b_s, w_out):
    depth = w_ada.shape[0]
    bp, lp, _ = x_prompt.shape
    bs, ls, _ = x_sample.shape
    seq_per_tile = TILE_ROWS // ls
    yp = x_prompt.reshape(bp * lp, D_MODEL)
    ys = x_sample.reshape(bs * ls, D_MODEL)
    c_all = jnp.concatenate([c_prompt, c_sample], axis=0)
    c_pad = jnp.pad(c_all, ((0, (-c_all.shape[0]) % 8), (0, 0)))
    outs = [[] for _ in range(5)]
    for l in range(depth):
        wl = _prep_weights(g_pre[l], g_post[l], w_in, l, conv_w[l], conv_b[l], dt_bias[l],
                           a_log[l], d_skip[l], g_v[l], beta_v[l], w_s[l], b_s[l], w_out[l])
        mod = _modulation(c_pad, w_ada[l], b_ada[l])
        shift, scale, gate = (mod[:, i * D_MODEL:(i + 1) * D_MODEL] for i in range(3))
        mp = [m[:bp].reshape(bp, 1, D_MODEL) for m in (shift, scale, gate)]
        ms = [m[bp:bp + bs].reshape(bs // seq_per_tile, seq_per_tile, D_MODEL)
              for m in (shift, scale, gate)]
        yp, conv_p, ssm_p = _layer(
            yp, *mp, None, None, wl, nseq=1, seglen=TILE_ROWS, lc_ssd=min(lp, SSD_CHUNK),
            lc_gmlp=min(lp, GMLP_CHUNK), nt=lp // TILE_ROWS, emit_v=False)
        ys, conv_s, ssm_s, v_s = _layer(
            ys, *ms,
            state_conv[l].reshape(bs // seq_per_tile, seq_per_tile, CONV_W - 1, CONV_DIM),
            state_ssm, wl,
            nseq=seq_per_tile, seglen=ls, lc_ssd=min(ls, SSD_CHUNK), lc_gmlp=min(ls, GMLP_CHUNK),
            nt=1, emit_v=True, layer=l)
        outs[0].append(conv_p.reshape(bp, CONV_W - 1, CONV_DIM))
        outs[1].append(ssm_p)
        outs[2].append(conv_s.reshape(bs, CONV_W - 1, CONV_DIM))
        outs[3].append(ssm_s)
        outs[4].append(v_s.reshape(bs, ls, GMLP_WIDTH))
    stacked = [o[0][None] if depth == 1 else jnp.stack(o) for o in outs]
    return (yp.reshape(bp, lp, D_MODEL), ys.reshape(bs, ls, D_MODEL), *stacked)
```

```python
import collections
import functools

import jax
import jax.numpy as jnp
import numpy as np
from jax import lax
from jax.experimental import pallas as pl
from jax.experimental.pallas import tpu as pltpu

F32 = jnp.float32
BF16 = jnp.bfloat16

D_MODEL = 1024
SSD_WIDTH = 1024
HEAD_DIM = 64
HEADS = SSD_WIDTH // HEAD_DIM
GROUPS = 2
GROUP_WIDTH = SSD_WIDTH // GROUPS
STATE = 128
CONV_W = 4
CONV_DIM = SSD_WIDTH + 2 * GROUPS * STATE
GMLP_WIDTH = 1024
GMLP_GROUPS = 8
GMLP_GROUP_DIM = GMLP_WIDTH // GMLP_GROUPS
SSD_CHUNK = 64
GMLP_CHUNK = 128
EPS = 1e-6
LOG2E = 1.4426950408889634

LANES = 128
MXU_COLS = 256
HALF = LANES // 2
PAIRS = HEADS // 2
PAIRS_PER_GROUP = PAIRS // GROUPS
PIECES = 3
TILE_ROWS = 256
TAIL_ROWS = 8
TAP0 = TAIL_ROWS - (CONV_W - 1)
MOD_BLOCK_COLS = 1024
VMEM_LIMIT_BYTES = 60 * 1024 * 1024

OFF_Z = 0
OFF_XBC = OFF_Z + SSD_WIDTH
OFF_U = OFF_XBC + CONV_DIM
OFF_V = OFF_U + GMLP_WIDTH
OFF_G = OFF_V + GMLP_WIDTH
OFF_DT = OFF_G + GMLP_WIDTH
W_CAT_COLS = OFF_DT + LANES

Bufs = collections.namedtuple("Bufs", "proj xs bc mix")


def _dot(a, b):
    return jnp.dot(a, b, preferred_element_type=F32)


def _dot_nt(a, b):
    return lax.dot_general(a, b, (((1,), (1,)), ((), ())), preferred_element_type=F32)


def _dot_tn(a, b):
    return lax.dot_general(a, b, (((0,), (0,)), ((), ())), preferred_element_type=F32)


def _cat3(x):
    hi = x.astype(BF16).astype(F32)
    r1 = x - hi
    mid = r1.astype(BF16).astype(F32)
    lo = r1 - mid
    return hi + pltpu.roll(mid, HEADS, axis=1) + pltpu.roll(lo, 2 * HEADS, axis=1)


def _silu(x):
    return x / (1.0 + jnp.exp2(x * (-LOG2E)))


def _softplus(x):
    return jnp.maximum(x, 0.0) + jnp.log1p(jnp.exp(-jnp.abs(x)))


def _row_mean(x):
    width = x.shape[-1]
    part = x[:, 0:LANES]
    for c in range(LANES, width, LANES):
        part = part + x[:, c:c + LANES]
    return jnp.sum(part, axis=-1, keepdims=True) * (1.0 / width)


def _rms_scale(x):
    return lax.rsqrt(_row_mean(x * x) + EPS)


def _mod_kernel(c_ref, w_ref, b_ref, o_ref):
    c = _silu(c_ref[...]).astype(BF16)
    o_ref[...] = _dot(c, w_ref[...].astype(BF16)) + b_ref[...]


def _modulation(c, w_ada, b_ada):
    rows = c.shape[0]
    n = w_ada.shape[1]
    return pl.pallas_call(
        _mod_kernel,
        grid=(n // MOD_BLOCK_COLS,),
        in_specs=[
            pl.BlockSpec((rows, D_MODEL), lambda j: (0, 0)),
            pl.BlockSpec((D_MODEL, MOD_BLOCK_COLS), lambda j: (0, j)),
            pl.BlockSpec((1, MOD_BLOCK_COLS), lambda j: (0, j)),
        ],
        out_specs=pl.BlockSpec((rows, MOD_BLOCK_COLS), lambda j: (0, j)),
        out_shape=jax.ShapeDtypeStruct((rows, n), F32),
        name="adaln_mod",
    )(c, w_ada, b_ada.reshape(1, n))


CAST_BLOCK_COLS = 512
A_COLS = SSD_WIDTH + CONV_DIM
B_COLS = 3 * GMLP_WIDTH


def _cast_kernel(a_ref, b_ref, b_next_ref, wa_ref, wb_ref, wdt_ref):
    j = pl.program_id(0)
    wa_ref[...] = a_ref[...].astype(BF16)
    wide = jnp.concatenate([b_ref[...], b_next_ref[...]], axis=1)
    shifted = pltpu.roll(wide, wide.shape[1] - HEADS, axis=1)
    wb_ref[...] = shifted[:, :CAST_BLOCK_COLS].astype(BF16)

    @pl.when(j == 0)
    def _():
        head = b_ref[:, 0:LANES]
        lane = lax.broadcasted_iota(jnp.int32, head.shape, 1)
        wdt_ref[...] = jnp.where(lane < HEADS, head, 0.0).astype(BF16)


def _cast_weights(w_in, layer):
    a_blocks = A_COLS // CAST_BLOCK_COLS
    b_blocks = B_COLS // CAST_BLOCK_COLS
    lanes_per_block = CAST_BLOCK_COLS // LANES
    return pl.pallas_call(
        _cast_kernel,
        grid=(b_blocks,),
        in_specs=[
            pl.BlockSpec((None, D_MODEL, CAST_BLOCK_COLS),
                         lambda j: (layer, 0, jnp.minimum(j, a_blocks - 1))),
            pl.BlockSpec((None, D_MODEL, CAST_BLOCK_COLS), lambda j: (layer, 0, a_blocks + j)),
            pl.BlockSpec((None, D_MODEL, LANES),
                         lambda j: (layer, 0, (a_blocks + j + 1) * lanes_per_block)),
        ],
        out_specs=[
            pl.BlockSpec((D_MODEL, CAST_BLOCK_COLS), lambda j: (0, jnp.minimum(j, a_blocks - 1))),
            pl.BlockSpec((D_MODEL, CAST_BLOCK_COLS), lambda j: (0, j)),
            pl.BlockSpec((D_MODEL, LANES), lambda j: (0, 0)),
        ],
        out_shape=[jax.ShapeDtypeStruct((D_MODEL, A_COLS), BF16),
                   jax.ShapeDtypeStruct((D_MODEL, B_COLS), BF16),
                   jax.ShapeDtypeStruct((D_MODEL, LANES), BF16)],
        compiler_params=pltpu.CompilerParams(dimension_semantics=("arbitrary",)),
        name="cast_proj_weight",
    )(w_in, w_in, w_in)


TILES_PER_STEP = 2
N_SHARED_SCRATCH = 16
N_STAGE_SCRATCH = 8


def _interleave(primary, secondary):
    total = sum(weight for _, weight in primary)
    groups = []
    done = 0
    seen = 0
    for task, weight in primary:
        seen += weight
        want = (seen * len(secondary)) // total
        groups.append([task] + secondary[done:want])
        done = want
    return groups


def _tag(task, name, needs=()):
    task.name, task.needs = name, tuple(needs)
    return task


def _run_groups(groups):
    done = set()
    for group in groups:
        for task in group:
            missing = [n for n in getattr(task, "needs", ()) if n not in done]
            assert not missing, (getattr(task, "name", task), missing)
            task()
            done.add(getattr(task, "name", None))


def _layer_kernel(*refs, nseq, seglen, lc, nt, pipelined, emit_v):
    statics = dict(nseq=nseq, seglen=seglen, lc=lc, pipelined=pipelined, emit_v=emit_v)
    shared, stage = refs[:-N_STAGE_SCRATCH], refs[-N_STAGE_SCRATCH:]
    buf_a, buf_b = Bufs(*stage[0::2]), Bufs(*stage[1::2])
    n_scratch = N_SHARED_SCRATCH + N_STAGE_SCRATCH
    ssm_out_ref = refs[len(refs) - n_scratch - (2 if emit_v else 1)]
    tail_ref, st_ref = refs[-n_scratch + 1], refs[-n_scratch + 3]
    if not pipelined:
        _tile_body(shared, buf_a, buf_a, buf_a, buf_a, **statics)
        return

    assert nt % TILES_PER_STEP == 0 and TILES_PER_STEP == 2
    i = pl.program_id(0)
    row_start = lax.rem(TILES_PER_STEP * i, nt) == 0
    n_in = len(shared) - N_SHARED_SCRATCH - 3
    views = []
    for half in range(TILES_PER_STEP):
        rows = slice(half * TILE_ROWS, (half + 1) * TILE_ROWS)
        view = list(shared)
        for k in (0, 1, n_in):
            view[k] = shared[k].at[rows, :]
        views.append(view)

    @pl.when(i == 0)
    def _():
        for ref in buf_b:
            ref[...] = jnp.zeros_like(ref)
        st_ref[...] = jnp.zeros_like(st_ref)

    @pl.when(row_start)
    def _():
        tail_ref[...] = jnp.zeros_like(tail_ref)

    _tile_body(views[0], buf_a, buf_b, buf_a, buf_b, **statics)

    @pl.when(row_start & (i > 0))
    def _():
        ssm_out_ref[0] = st_ref[...].T.reshape(HEADS, HEAD_DIM, STATE)

    @pl.when(row_start)
    def _():
        st_ref[...] = jnp.zeros_like(st_ref)

    _tile_body(views[1], buf_b, buf_a, buf_b, buf_a, **statics)


def _tile_body(refs, bm, bv, bvo, bo, *, nseq, seglen, lc, pipelined, emit_v):
    has_state = not pipelined
    it = iter(refs)
    x_ref, xo_ref = next(it), next(it)
    shift_ref, scale_ref, gate_ref = next(it), next(it), next(it)
    conv0_ref = next(it) if has_state else None
    ssm0_ref = next(it) if has_state else None
    (gpre_ref, gpost_ref, wa_ref, wb_ref, wdt_ref, convw_ref, convb_ref, dtb_ref, alog_ref,
     dskip_ref, gv_ref, betav_ref, mixw_ref, mixb_ref, wout_ref, expand_ref, tril_ref, psel_ref) = (
         next(it) for _ in range(18))
    y_ref, conv_out_ref, ssm_out_ref = next(it), next(it), next(it)
    vn_out_ref = next(it) if emit_v else None
    (h_ref, tail_ref, xp_ref, st_ref, stnew_ref, aexp_ref, xdt_ref, xlo_ref, xhi_ref, acumt_ref,
     vnb_ref, mo_ref, stb_ref, ys_ref, cb_ref, yoff_ref) = (next(it) for _ in range(N_SHARED_SCRATCH))
    pm, pv = bm.proj, bv.proj

    rows_all = slice(0, TILE_ROWS)
    nchunk = seglen // lc
    pad_rows = HALF - lc

    def pad_to_half(v):
        if pad_rows == 0:
            return [v]
        return [v, jnp.zeros((pad_rows, v.shape[1]), v.dtype)]

    def m_norm():
        for s in range(nseq):
            rows = slice(s * seglen, (s + 1) * seglen)
            xr = x_ref[rows, :]
            gain = gpre_ref[...] * (1.0 + scale_ref[s:s + 1, :])
            hs = (xr * _rms_scale(xr)) * gain + shift_ref[s:s + 1, :]
            h_ref[rows, :] = hs.astype(BF16)

    def m_dot(w_ref, w0, dst0, width):
        def task():
            pm[rows_all, dst0:dst0 + width] = _dot(h_ref[...], w_ref[:, w0:w0 + width])
        return _tag(task, f"m_dot{dst0 // MXU_COLS}")

    dst_starts = (list(range(OFF_XBC, OFF_U, MXU_COLS)) + list(range(OFF_Z, OFF_XBC, MXU_COLS))
                  + list(range(OFF_U, OFF_DT, MXU_COLS)))
    m_dots = [m_dot(wa_ref, d, d, MXU_COLS) if d < OFF_U else m_dot(wb_ref, d - OFF_U, d, MXU_COLS)
              for d in dst_starts] + [m_dot(wdt_ref, 0, OFF_DT, LANES)]

    def conv_store(c, val, rows):
        if c < SSD_WIDTH // LANES:
            bm.xs[rows, c * LANES:(c + 1) * LANES] = val
        else:
            c2 = c - SSD_WIDTH // LANES
            bm.bc[rows, c2 * LANES:(c2 + 1) * LANES] = val.astype(BF16)

    def m_conv(c):
        cols = slice(OFF_XBC + c * LANES, OFF_XBC + (c + 1) * LANES)
        wcols = slice(c * LANES, (c + 1) * LANES)

        def task_stream():
            xfull = jnp.concatenate([tail_ref[:, wcols], pm[:, cols]], axis=0)
            acc = convw_ref[0:1, wcols] * xfull
            for k in range(1, CONV_W):
                acc = pltpu.roll(acc, 1, axis=0) + convw_ref[k:k + 1, wcols] * xfull
            val = _silu(acc[TAIL_ROWS:, :] + convb_ref[:, wcols])
            tail_ref[:, wcols] = pm[TILE_ROWS - TAIL_ROWS:TILE_ROWS, cols]
            conv_out_ref[0, :, wcols] = pm[TILE_ROWS - (CONV_W - 1):TILE_ROWS, cols]
            conv_store(c, val, slice(0, TILE_ROWS))

        def task_state():
            for s in range(nseq):
                rows = slice(s * seglen, (s + 1) * seglen)
                xp_ref[TAP0:TAIL_ROWS, :] = conv0_ref[s, :, wcols]
                xp_ref[TAIL_ROWS:TAIL_ROWS + seglen, :] = pm[s * seglen:(s + 1) * seglen, cols]
                acc = convb_ref[:, wcols]
                for k in range(CONV_W):
                    acc = acc + convw_ref[k:k + 1, wcols] * xp_ref[TAP0 + k:TAP0 + k + seglen, :]
                conv_store(c, _silu(acc), rows)
                conv_out_ref[s, :, wcols] = xp_ref[TAIL_ROWS + seglen - (CONV_W - 1):TAIL_ROWS + seglen, :]

        return task_stream if pipelined else task_state

    m_convs = [_tag(m_conv(c), f"m_conv{c}", [f"m_dot{(OFF_XBC + c * LANES) // MXU_COLS}"])
               for c in range(CONV_DIM // LANES)]

    lane1 = lax.broadcasted_iota(jnp.int32, (TILE_ROWS, LANES), 1)

    def v_dt():
        a_row = -jnp.exp(alog_ref[...])
        dt = _softplus(pv[rows_all, OFF_DT:OFF_DT + LANES] + dtb_ref[...])
        dt = jnp.where(lane1 < HEADS, dt, 0.0)
        csum = _dot(tril_ref[...], _cat3(dt * a_row).astype(BF16))
        a_cum = csum + pltpu.roll(csum, LANES - HEADS, axis=1) + pltpu.roll(csum, LANES - 2 * HEADS, axis=1)
        a_cat = _cat3(jnp.where(lane1 < HEADS, a_cum, 0.0))
        aexp_ref[...] = _dot(a_cat.astype(BF16), expand_ref[...])
        dt_exp = _dot(_cat3(dt).astype(BF16), expand_ref[...])
        even = lax.rem(lane1, 2) == 0
        a_even = jnp.where(even, a_cat, 0.0)
        a_odd = (a_cat - a_even).astype(BF16)
        a_even = a_even.astype(BF16)
        parts = []
        for ck in range(TILE_ROWS // lc):
            rows = slice(ck * lc, (ck + 1) * lc)
            parts += pad_to_half(a_even[rows, :]) + pad_to_half(a_odd[rows, :])
        acumt_ref[...] = _dot_nt(psel_ref[...], jnp.concatenate(parts, axis=0))
        xdt = bv.xs[...] * dt_exp
        lane_w = lax.broadcasted_iota(jnp.int32, (TILE_ROWS, SSD_WIDTH), 1)
        xlo = jnp.where(lax.rem(lane_w, LANES) < HALF, xdt, 0.0)
        xdt_ref[...] = xdt
        xlo_ref[...] = xlo.astype(BF16)
        xhi_ref[...] = (xdt - xlo).astype(BF16)

    li = lax.broadcasted_iota(jnp.int32, (lc, LANES), 0)
    si = lax.rem(lax.broadcasted_iota(jnp.int32, (lc, LANES), 1), HALF)
    causal = (si <= li) & (si < lc)

    def v_local(ci, g):
        rows = slice(ci * lc, (ci + 1) * lc)
        hl = slice(g * GROUP_WIDTH, (g + 1) * GROUP_WIDTH)

        def products():
            last = aexp_ref[(ci + 1) * lc - 1:(ci + 1) * lc, hl]
            xw_b = (xdt_ref[rows, hl] * jnp.exp(last - aexp_ref[rows, hl])).astype(BF16)
            b_g = bv.bc[rows, g * STATE:(g + 1) * STATE]
            c_g = bv.bc[rows, (GROUPS + g) * STATE:(GROUPS + g + 1) * STATE]
            b2 = jnp.concatenate(pad_to_half(b_g) + pad_to_half(b_g), axis=0)
            cb_ref[ci * GROUPS + g] = _dot_nt(c_g, b2)
            stnew_ref[ci, :, hl] = _dot_tn(b_g, xw_b)

        def diag():
            cb2 = cb_ref[ci * GROUPS + g]
            for i in range(PAIRS_PER_GROUP):
                p = g * PAIRS_PER_GROUP + i
                pcols = slice(p * LANES, (p + 1) * LANES)
                seg = aexp_ref[rows, pcols] - acumt_ref[p:p + 1, ci * LANES:(ci + 1) * LANES]
                w_p = (cb2 * jnp.exp(jnp.where(causal, seg, -jnp.inf))).astype(BF16)
                bd = jnp.concatenate(pad_to_half(xlo_ref[rows, pcols]) + pad_to_half(xhi_ref[rows, pcols]),
                                     axis=0)
                ys_ref[rows, pcols] = _dot(w_p, bd) + dskip_ref[:, pcols] * bv.xs[rows, pcols]
        return products, diag

    def v_state(ci, s):
        def task():
            e_last = jnp.exp(aexp_ref[(ci + 1) * lc - 1:(ci + 1) * lc, :])
            st = st_ref[...]
            stb_ref[ci] = st.astype(BF16)
            st_ref[...] = st * e_last + stnew_ref[ci]
            if has_state and ci % nchunk == nchunk - 1:
                ssm_out_ref[s] = st_ref[...].T.reshape(HEADS, HEAD_DIM, STATE)
        return task

    def v_readout(ci, g):
        rows = slice(ci * lc, (ci + 1) * lc)
        prow = rows
        hl = slice(g * GROUP_WIDTH, (g + 1) * GROUP_WIDTH)

        def product():
            c_g = bv.bc[rows, (GROUPS + g) * STATE:(GROUPS + g + 1) * STATE]
            yoff_ref[rows, hl] = _dot(c_g, stb_ref[ci, :, hl])

        def gate():
            y = ys_ref[rows, hl] + jnp.exp(aexp_ref[rows, hl]) * yoff_ref[rows, hl]
            zg = _silu(pv[prow, OFF_Z + g * GROUP_WIDTH:OFF_Z + (g + 1) * GROUP_WIDTH])
            bvo.mix[rows, hl] = (y * zg).astype(BF16)
        return product, gate

    def v_state_load(s):
        def task():
            st_ref[...] = ssm0_ref[s].reshape(SSD_WIDTH, STATE).T
        return task

    def v_ln():
        v = pv[rows_all, OFF_V:OFF_V + GMLP_WIDTH]
        vc = v - _row_mean(v)
        vn = (vc * _rms_scale(vc)) * gv_ref[...] + betav_ref[...]
        if emit_v:
            vn_out_ref[...] = vn
        vnb_ref[...] = vn.astype(BF16)

    def v_gmlp(g):
        gcols = slice(g * GMLP_GROUP_DIM, (g + 1) * GMLP_GROUP_DIM)

        def task():
            if mixw_ref.shape[1] == TILE_ROWS:
                mixed = _dot(mixw_ref[g], vnb_ref[:, gcols])
            else:
                lcg = mixw_ref.shape[1]
                blocks = [vnb_ref[r:r + lcg, gcols] for r in range(0, TILE_ROWS, lcg)]
                wide = _dot(mixw_ref[g], jnp.concatenate(blocks, axis=1))
                mixed = jnp.concatenate(
                    [wide[:, k * GMLP_GROUP_DIM:(k + 1) * GMLP_GROUP_DIM] for k in range(len(blocks))],
                    axis=0)
            mixed = mixed + mixb_ref[:, gcols]
            u = pv[rows_all, OFF_U + g * GMLP_GROUP_DIM:OFF_U + (g + 1) * GMLP_GROUP_DIM]
            gt = pv[rows_all, OFF_G + g * GMLP_GROUP_DIM:OFF_G + (g + 1) * GMLP_GROUP_DIM]
            bvo.mix[:, SSD_WIDTH + g * GMLP_GROUP_DIM:SSD_WIDTH + (g + 1) * GMLP_GROUP_DIM] = (
                _silu(gt) * u * mixed).astype(BF16)
        return task

    chunk_groups = [(ci, g) for ci in range(nseq * nchunk) for g in range(GROUPS)]
    v_locals = [v_local(ci, g) for ci, g in chunk_groups]
    v_states = []
    for s in range(nseq):
        if has_state:
            v_states.append(v_state_load(s))
        v_states += [v_state(s * nchunk + c, s) for c in range(nchunk)]
    v_readouts = [v_readout(ci, g) for ci, g in chunk_groups]
    v_scan = ([(t[0], 0) for t in v_locals] + [(t[1], 1) for t in v_locals]
              + [(t, 1) for t in v_states]
              + [(t[0], 0) for t in v_readouts] + [(t[1], 1) for t in v_readouts])
    v_gmlps = [v_gmlp(g) for g in range(GMLP_GROUPS)]

    def o_out(n):
        def task():
            mo_ref[:, n * MXU_COLS:(n + 1) * MXU_COLS] = _dot(
                bo.mix[...], wout_ref[:, n * MXU_COLS:(n + 1) * MXU_COLS])
        return task

    def o_final():
        mo = mo_ref[...]
        mo = mo * _rms_scale(mo)
        for s in range(nseq):
            rows = slice(s * seglen, (s + 1) * seglen)
            gain = gpost_ref[...] * gate_ref[s:s + 1, :]
            y_ref[rows, :] = xo_ref[rows, :] + gain * mo[rows, :]

    o_outs = [_tag(o_out(n), f"o_out{n}") for n in range(D_MODEL // MXU_COLS)]
    _tag(o_final, "o_final", [t.name for t in o_outs])

    if pipelined:
        n_xbc_dots = CONV_DIM // MXU_COLS
        head = _interleave([(m_norm, 1), (v_dt, 2), (v_ln, 2)], o_outs + m_dots[:n_xbc_dots])
        body = [(t, 3) for t in m_convs] + [(o_final, 3)] + v_scan + [(t, 0) for t in v_gmlps]
        groups = head + _interleave(body, m_dots[n_xbc_dots:])
    else:
        order = ([m_norm] + m_dots + m_convs + [v_dt, v_ln] + [t for t, _ in v_scan] + v_gmlps
                 + o_outs + [o_final])
        groups = [[task] for task in order]
    _run_groups(groups)


def _const_spec(shape):
    nd = len(shape)
    return pl.BlockSpec(shape, lambda j: (0,) * nd, pipeline_mode=pl.Buffered(1))


def _chunk_constants(lc):
    expand = np.zeros((LANES, SSD_WIDTH), np.float32)
    psel = np.zeros((PAIRS, LANES), np.float32)
    for piece in range(PIECES):
        for hd in range(HEADS):
            expand[piece * HEADS + hd, hd * HEAD_DIM:(hd + 1) * HEAD_DIM] = 1.0
            psel[hd // 2, piece * HEADS + hd] = 1.0
    r = np.arange(TILE_ROWS)
    tril = ((r[:, None] >= r[None, :]) & (r[:, None] // lc == r[None, :] // lc)).astype(np.float32)
    return jnp.asarray(expand, BF16), jnp.asarray(tril, BF16), jnp.asarray(psel, BF16)


def _mix_weights(w_s, b_s, lc):
    nck = TILE_ROWS // lc
    mask = jnp.tril(jnp.ones((lc, lc), dtype=bool))
    w = jnp.where(mask[None], w_s[:, :lc, :lc], 0)
    if nck * GMLP_GROUP_DIM > MXU_COLS:
        eye = jnp.eye(nck, dtype=w.dtype)
        w = jnp.einsum("ab,gts->gatbs", eye, w).reshape(GMLP_GROUPS, TILE_ROWS, TILE_ROWS)
    bias = jnp.repeat(b_s[:, :lc].T, GMLP_GROUP_DIM, axis=1)
    return w.astype(BF16), jnp.tile(bias, (nck, 1))


def _layer(x, shift, scale, gate, conv0, ssm0, wl, *, nseq, seglen, lc_ssd, lc_gmlp, nt, emit_v,
           layer=0):
    (g_pre, g_post, w_a, w_b, w_dt, conv_w, conv_b, dtb, alog, dskip, g_v, beta_v, w_s, b_s,
     w_out) = wl
    assert lc_ssd == lc_gmlp or nseq == 1
    pipelined = conv0 is None
    nb = shift.shape[0]
    rows_total = x.shape[0]
    ntiles = rows_total // TILE_ROWS
    assert nseq * seglen == TILE_ROWS and nb * nt == ntiles
    expand, tril, psel = _chunk_constants(lc_ssd)
    mixw, mixb = _mix_weights(w_s, b_s, lc_gmlp)
    nck = TILE_ROWS // lc_ssd

    if pipelined:
        tps = TILES_PER_STEP
        assert ntiles % tps == 0

        def tile_m(j):
            return jnp.minimum(tps * j, ntiles - tps)

        def tile_v(j):
            return jnp.clip(tps * j - 1, 0, ntiles - 1)

        def tile_o(j):
            return jnp.maximum(tps * j - 2, 0)
        steps = ntiles // tps + 1
    else:
        tps = 1

        def tile_m(j):
            return j
        tile_v = tile_o = tile_m
        steps = ntiles

    def mod_spec(tile_of):
        return pl.BlockSpec((None, nseq, D_MODEL), lambda j: (tile_of(j) // nt, 0, 0))

    in_specs = [pl.BlockSpec((tps * TILE_ROWS, D_MODEL), lambda j: (tile_m(j) // tps, 0)),
                pl.BlockSpec((tps * TILE_ROWS, D_MODEL), lambda j: (tile_o(j) // tps, 0)),
                mod_spec(tile_m), mod_spec(tile_m), mod_spec(tile_o)]
    args = [x, x, shift, scale, gate]
    if not pipelined:
        in_specs += [pl.BlockSpec((None, nseq, CONV_W - 1, CONV_DIM), lambda j: (j // nt, 0, 0, 0)),
                     pl.BlockSpec((None, nseq, HEADS, HEAD_DIM, STATE), lambda j: (layer, j, 0, 0, 0))]
        args += [conv0, ssm0]
    consts = [g_pre, g_post, w_a, w_b, w_dt, conv_w, conv_b, dtb, alog, dskip, g_v, beta_v, mixw,
              mixb, w_out, expand, tril, psel]
    in_specs += [_const_spec(a.shape) for a in consts]
    args += consts

    out_shape = [jax.ShapeDtypeStruct((rows_total, D_MODEL), F32),
                 jax.ShapeDtypeStruct((nb, nseq, CONV_W - 1, CONV_DIM), F32),
                 jax.ShapeDtypeStruct((nb * nseq, HEADS, HEAD_DIM, STATE), F32)]
    out_specs = [pl.BlockSpec((tps * TILE_ROWS, D_MODEL), lambda j: (tile_o(j) // tps, 0)),
                 pl.BlockSpec((None, nseq, CONV_W - 1, CONV_DIM), lambda j: (tile_m(j) // nt, 0, 0, 0)),
                 pl.BlockSpec((nseq, HEADS, HEAD_DIM, STATE), lambda j: (tile_v(j) // nt, 0, 0, 0))]
    if emit_v:
        out_shape.append(jax.ShapeDtypeStruct((rows_total, GMLP_WIDTH), F32))
        out_specs.append(pl.BlockSpec((TILE_ROWS, GMLP_WIDTH), lambda j: (tile_v(j), 0)))

    spare = (TAIL_ROWS, LANES)
    stage_shapes = [((TILE_ROWS, W_CAT_COLS), F32),
                    ((TILE_ROWS, SSD_WIDTH), F32),
                    ((TILE_ROWS, 2 * GROUPS * STATE), BF16),
                    ((TILE_ROWS, SSD_WIDTH + GMLP_WIDTH), BF16)]
    scratch = [pltpu.VMEM((TILE_ROWS, D_MODEL), BF16),
               pltpu.VMEM((TAIL_ROWS, CONV_DIM), F32),
               pltpu.VMEM((TAIL_ROWS + seglen, LANES), F32),
               pltpu.VMEM((STATE, SSD_WIDTH), F32),
               pltpu.VMEM((nck, STATE, SSD_WIDTH), F32),
               pltpu.VMEM((TILE_ROWS, SSD_WIDTH), F32),
               pltpu.VMEM((TILE_ROWS, SSD_WIDTH), F32),
               pltpu.VMEM((TILE_ROWS, SSD_WIDTH), BF16),
               pltpu.VMEM((TILE_ROWS, SSD_WIDTH), BF16),
               pltpu.VMEM((PAIRS, nck * LANES), F32),
               pltpu.VMEM((TILE_ROWS, GMLP_WIDTH), BF16),
               pltpu.VMEM((TILE_ROWS, D_MODEL), F32),
               pltpu.VMEM((nck, STATE, SSD_WIDTH), BF16),
               pltpu.VMEM((TILE_ROWS, SSD_WIDTH), F32),
               pltpu.VMEM((nck * GROUPS, lc_ssd, LANES), F32),
               pltpu.VMEM((TILE_ROWS, SSD_WIDTH), F32)]
    assert len(scratch) == N_SHARED_SCRATCH
    for shape, dtype in stage_shapes:
        scratch += [pltpu.VMEM(shape, dtype), pltpu.VMEM(shape if pipelined else spare, dtype)]
    kern = functools.partial(_layer_kernel, nseq=nseq, seglen=seglen, lc=lc_ssd, nt=nt,
                             pipelined=pipelined, emit_v=emit_v)
    return pl.pallas_call(
        kern,
        grid=(steps,),
        in_specs=in_specs,
        out_specs=out_specs,
        out_shape=out_shape,
        scratch_shapes=scratch,
        compiler_params=pltpu.CompilerParams(
            dimension_semantics=("arbitrary",),
            vmem_limit_bytes=VMEM_LIMIT_BYTES),
        name="hybrid_layer_stream" if pipelined else "hybrid_layer_state",
    )(*args)


def _prep_weights(g_pre, g_post, w_in, layer, conv_w, conv_b, dt_bias, a_log, d_skip, g_v, beta_v,
                  w_s, b_s, w_out):
    w_a, w_b, w_dt = _cast_weights(w_in, layer)
    pad = (0, LANES - HEADS)
    return (g_pre.reshape(1, -1), g_post.reshape(1, -1), w_a, w_b, w_dt, conv_w, conv_b.reshape(1, -1),
            jnp.pad(dt_bias, pad).reshape(1, -1), jnp.pad(a_log, pad).reshape(1, -1),
            jnp.repeat(d_skip, HEAD_DIM).reshape(1, -1), g_v.reshape(1, -1),
            beta_v.reshape(1, -1), w_s, b_s, w_out.astype(BF16))


def kernel(x_prompt, x_sample, state_conv, state_ssm, c_prompt, c_sample, w_ada, b_ada, g_pre,
           g_post, w_in, conv_w, conv_b, dt_bias, a_log, d_skip, g_v, beta_v, w_s, b_s, w_out):
    depth = w_ada.shape[0]
    bp, lp, _ = x_prompt.shape
    bs, ls, _ = x_sample.shape
    seq_per_tile = TILE_ROWS // ls
    yp = x_prompt.reshape(bp * lp, D_MODEL)
    ys = x_sample.reshape(bs * ls, D_MODEL)
    c_all = jnp.concatenate([c_prompt, c_sample], axis=0)
    c_pad = jnp.pad(c_all, ((0, (-c_all.shape[0]) % 8), (0, 0)))
    outs = [[] for _ in range(5)]
    for l in range(depth):
        wl = _prep_weights(g_pre[l], g_post[l], w_in, l, conv_w[l], conv_b[l], dt_bias[l],
                           a_log[l], d_skip[l], g_v[l], beta_v[l], w_s[l], b_s[l], w_out[l])
        mod = _modulation(c_pad, w_ada[l], b_ada[l])
        shift, scale, gate = (mod[:, i * D_MODEL:(i + 1) * D_MODEL] for i in range(3))
        mp = [m[:bp].reshape(bp, 1, D_MODEL) for m in (shift, scale, gate)]
        ms = [m[bp:bp + bs].reshape(bs // seq_per_tile, seq_per_tile, D_MODEL)
              for m in (shift, scale, gate)]
        yp, conv_p, ssm_p = _layer(
            yp, *mp, None, None, wl, nseq=1, seglen=TILE_ROWS, lc_ssd=min(lp, SSD_CHUNK),
            lc_gmlp=min(lp, GMLP_CHUNK), nt=lp // TILE_ROWS, emit_v=False)
        ys, conv_s, ssm_s, v_s = _layer(
            ys, *ms,
            state_conv[l].reshape(bs // seq_per_tile, seq_per_tile, CONV_W - 1, CONV_DIM),
            state_ssm, wl,
            nseq=seq_per_tile, seglen=ls, lc_ssd=min(ls, SSD_CHUNK), lc_gmlp=min(ls, GMLP_CHUNK),
            nt=1, emit_v=True, layer=l)
        outs[0].append(conv_p.reshape(bp, CONV_W - 1, CONV_DIM))
        outs[1].append(ssm_p)
        outs[2].append(conv_s.reshape(bs, CONV_W - 1, CONV_DIM))
        outs[3].append(ssm_s)
        outs[4].append(v_s.reshape(bs, ls, GMLP_WIDTH))
    stacked = [o[0][None] if depth == 1 else jnp.stack(o) for o in outs]
    return (yp.reshape(bp, lp, D_MODEL), ys.reshape(bs, ls, D_MODEL), *stacked)
```

```python
import collections
import functools

import jax
import jax.numpy as jnp
import numpy as np
from jax import lax
from jax.experimental import pallas as pl
from jax.experimental.pallas import tpu as pltpu

F32 = jnp.float32
BF16 = jnp.bfloat16

D_MODEL = 1024
SSD_WIDTH = 1024
HEAD_DIM = 64
HEADS = SSD_WIDTH // HEAD_DIM
GROUPS = 2
GROUP_WIDTH = SSD_WIDTH // GROUPS
STATE = 128
CONV_W = 4
CONV_DIM = SSD_WIDTH + 2 * GROUPS * STATE
GMLP_WIDTH = 1024
GMLP_GROUPS = 8
GMLP_GROUP_DIM = GMLP_WIDTH // GMLP_GROUPS
SSD_CHUNK = 64
GMLP_CHUNK = 128
EPS = 1e-6
LOG2E = 1.4426950408889634

LANES = 128
MXU_COLS = 256
HALF = LANES // 2
PAIRS = HEADS // 2
PAIRS_PER_GROUP = PAIRS // GROUPS
PIECES = 3
TILE_ROWS = 256
TAIL_ROWS = 8
TAP0 = TAIL_ROWS - (CONV_W - 1)
MOD_BLOCK_COLS = 512
VMEM_LIMIT_BYTES = 60 * 1024 * 1024

OFF_Z = 0
OFF_XBC = OFF_Z + SSD_WIDTH
OFF_U = OFF_XBC + CONV_DIM
OFF_V = OFF_U + GMLP_WIDTH
OFF_G = OFF_V + GMLP_WIDTH
OFF_DT = OFF_G + GMLP_WIDTH
W_CAT_COLS = OFF_DT + LANES

Bufs = collections.namedtuple("Bufs", "proj xs bc mix")


def _dot(a, b):
    return jnp.dot(a, b, preferred_element_type=F32)


def _dot_nt(a, b):
    return lax.dot_general(a, b, (((1,), (1,)), ((), ())), preferred_element_type=F32)


def _dot_tn(a, b):
    return lax.dot_general(a, b, (((0,), (0,)), ((), ())), preferred_element_type=F32)


def _cat3(x):
    hi = x.astype(BF16).astype(F32)
    r1 = x - hi
    mid = r1.astype(BF16).astype(F32)
    lo = r1 - mid
    return hi + pltpu.roll(mid, HEADS, axis=1) + pltpu.roll(lo, 2 * HEADS, axis=1)


def _silu(x):
    return x / (1.0 + jnp.exp2(x * (-LOG2E)))


def _softplus(x):
    return jnp.maximum(x, 0.0) + jnp.log1p(jnp.exp(-jnp.abs(x)))


def _row_mean(x):
    width = x.shape[-1]
    part = x[:, 0:LANES]
    for c in range(LANES, width, LANES):
        part = part + x[:, c:c + LANES]
    return jnp.sum(part, axis=-1, keepdims=True) * (1.0 / width)


def _rms_scale(x):
    return lax.rsqrt(_row_mean(x * x) + EPS)


def _mod_kernel(c_ref, w_ref, b_ref, o_ref):
    c = _silu(c_ref[...]).astype(BF16)
    o_ref[...] = _dot(c, w_ref[...].astype(BF16)) + b_ref[...]


def _modulation(c, w_ada, b_ada):
    rows = c.shape[0]
    n = w_ada.shape[1]
    return pl.pallas_call(
        _mod_kernel,
        grid=(n // MOD_BLOCK_COLS,),
        in_specs=[
            pl.BlockSpec((rows, D_MODEL), lambda j: (0, 0)),
            pl.BlockSpec((D_MODEL, MOD_BLOCK_COLS), lambda j: (0, j)),
            pl.BlockSpec((1, MOD_BLOCK_COLS), lambda j: (0, j)),
        ],
        out_specs=pl.BlockSpec((rows, MOD_BLOCK_COLS), lambda j: (0, j)),
        out_shape=jax.ShapeDtypeStruct((rows, n), F32),
        name="adaln_mod",
    )(c, w_ada, b_ada.reshape(1, n))


CAST_BLOCK_COLS = 512
A_COLS = SSD_WIDTH + CONV_DIM
B_COLS = 3 * GMLP_WIDTH


def _cast_kernel(a_ref, b_ref, b_next_ref, wa_ref, wb_ref, wdt_ref):
    j = pl.program_id(0)
    wa_ref[...] = a_ref[...].astype(BF16)
    wide = jnp.concatenate([b_ref[...], b_next_ref[...]], axis=1)
    shifted = pltpu.roll(wide, wide.shape[1] - HEADS, axis=1)
    wb_ref[...] = shifted[:, :CAST_BLOCK_COLS].astype(BF16)

    @pl.when(j == 0)
    def _():
        head = b_ref[:, 0:LANES]
        lane = lax.broadcasted_iota(jnp.int32, head.shape, 1)
        wdt_ref[...] = jnp.where(lane < HEADS, head, 0.0).astype(BF16)


def _cast_weights(w_in):
    a_blocks = A_COLS // CAST_BLOCK_COLS
    b_blocks = B_COLS // CAST_BLOCK_COLS
    lanes_per_block = CAST_BLOCK_COLS // LANES
    return pl.pallas_call(
        _cast_kernel,
        grid=(b_blocks,),
        in_specs=[
            pl.BlockSpec((D_MODEL, CAST_BLOCK_COLS), lambda j: (0, jnp.minimum(j, a_blocks - 1))),
            pl.BlockSpec((D_MODEL, CAST_BLOCK_COLS), lambda j: (0, a_blocks + j)),
            pl.BlockSpec((D_MODEL, LANES), lambda j: (0, (a_blocks + j + 1) * lanes_per_block)),
        ],
        out_specs=[
            pl.BlockSpec((D_MODEL, CAST_BLOCK_COLS), lambda j: (0, jnp.minimum(j, a_blocks - 1))),
            pl.BlockSpec((D_MODEL, CAST_BLOCK_COLS), lambda j: (0, j)),
            pl.BlockSpec((D_MODEL, LANES), lambda j: (0, 0)),
        ],
        out_shape=[jax.ShapeDtypeStruct((D_MODEL, A_COLS), BF16),
                   jax.ShapeDtypeStruct((D_MODEL, B_COLS), BF16),
                   jax.ShapeDtypeStruct((D_MODEL, LANES), BF16)],
        compiler_params=pltpu.CompilerParams(dimension_semantics=("arbitrary",)),
        name="cast_proj_weight",
    )(w_in, w_in, w_in)


TILES_PER_STEP = 2
N_SHARED_SCRATCH = 15
N_STAGE_SCRATCH = 8


def _interleave(primary, secondary):
    total = sum(weight for _, weight in primary)
    groups = []
    done = 0
    seen = 0
    for task, weight in primary:
        seen += weight
        want = (seen * len(secondary)) // total
        groups.append([task] + secondary[done:want])
        done = want
    return groups


def _tag(task, name, needs=()):
    task.name, task.needs = name, tuple(needs)
    return task


def _run_groups(groups):
    stored = set()
    for group in groups:
        for task in group:
            missing = [n for n in getattr(task, "needs", ()) if n not in stored]
            assert not missing, (getattr(task, "name", task), missing)
        commits = [task() for task in group]
        for commit in commits:
            if commit is not None:
                commit()
        stored.update(getattr(task, "name", None) for task in group)


def _layer_kernel(*refs, nseq, seglen, lc, nt, pipelined, emit_v):
    statics = dict(nseq=nseq, seglen=seglen, lc=lc, pipelined=pipelined, emit_v=emit_v)
    shared, stage = refs[:-N_STAGE_SCRATCH], refs[-N_STAGE_SCRATCH:]
    buf_a, buf_b = Bufs(*stage[0::2]), Bufs(*stage[1::2])
    n_scratch = N_SHARED_SCRATCH + N_STAGE_SCRATCH
    ssm_out_ref = refs[len(refs) - n_scratch - (2 if emit_v else 1)]
    tail_ref, st_ref = refs[-n_scratch + 1], refs[-n_scratch + 3]
    if not pipelined:
        _tile_body(shared, buf_a, buf_a, buf_a, buf_a, **statics)
        return

    assert nt % TILES_PER_STEP == 0 and TILES_PER_STEP == 2
    i = pl.program_id(0)
    row_start = lax.rem(TILES_PER_STEP * i, nt) == 0
    n_in = len(shared) - N_SHARED_SCRATCH - 3
    views = []
    for half in range(TILES_PER_STEP):
        rows = slice(half * TILE_ROWS, (half + 1) * TILE_ROWS)
        view = list(shared)
        for k in (0, 1, n_in):
            view[k] = shared[k].at[rows, :]
        views.append(view)

    @pl.when(i == 0)
    def _():
        for ref in buf_b:
            ref[...] = jnp.zeros_like(ref)
        st_ref[...] = jnp.zeros_like(st_ref)

    @pl.when(row_start)
    def _():
        tail_ref[...] = jnp.zeros_like(tail_ref)

    _tile_body(views[0], buf_a, buf_b, buf_a, buf_b, **statics)

    @pl.when(row_start & (i > 0))
    def _():
        ssm_out_ref[0] = st_ref[...].T.reshape(HEADS, HEAD_DIM, STATE)

    @pl.when(row_start)
    def _():
        st_ref[...] = jnp.zeros_like(st_ref)

    _tile_body(views[1], buf_b, buf_a, buf_b, buf_a, **statics)


def _tile_body(refs, bm, bv, bvo, bo, *, nseq, seglen, lc, pipelined, emit_v):
    has_state = not pipelined
    it = iter(refs)
    x_ref, xo_ref = next(it), next(it)
    shift_ref, scale_ref, gate_ref = next(it), next(it), next(it)
    conv0_ref = next(it) if has_state else None
    ssm0_ref = next(it) if has_state else None
    (gpre_ref, gpost_ref, wa_ref, wb_ref, wdt_ref, convw_ref, convb_ref, dtb_ref, alog_ref,
     dskip_ref, gv_ref, betav_ref, mixw_ref, mixb_ref, wout_ref, expand_ref, tril_ref, psel_ref) = (
         next(it) for _ in range(18))
    y_ref, conv_out_ref, ssm_out_ref = next(it), next(it), next(it)
    vn_out_ref = next(it) if emit_v else None
    (h_ref, tail_ref, xp_ref, st_ref, stnew_ref, aexp_ref, xdt_ref, xlo_ref, xhi_ref, acumt_ref,
     vnb_ref, mo_ref, stb_ref, ys_ref, cb_ref) = (next(it) for _ in range(N_SHARED_SCRATCH))
    pm, pv = bm.proj, bv.proj

    rows_all = slice(TAIL_ROWS, TAIL_ROWS + TILE_ROWS)
    nchunk = seglen // lc
    pad_rows = HALF - lc

    def pad_to_half(v):
        if pad_rows == 0:
            return [v]
        return [v, jnp.zeros((pad_rows, v.shape[1]), v.dtype)]

    def m_norm():
        for s in range(nseq):
            rows = slice(s * seglen, (s + 1) * seglen)
            xr = x_ref[rows, :]
            gain = gpre_ref[...] * (1.0 + scale_ref[s:s + 1, :])
            hs = (xr * _rms_scale(xr)) * gain + shift_ref[s:s + 1, :]
            h_ref[rows, :] = hs.astype(BF16)

    def m_dot(w_ref, w0, dst0, width):
        def task():
            val = _dot(h_ref[...], w_ref[:, w0:w0 + width])

            def commit():
                pm[rows_all, dst0:dst0 + width] = val
            return commit
        return _tag(task, f"m_dot{dst0 // MXU_COLS}")

    dst_starts = (list(range(OFF_XBC, OFF_U, MXU_COLS)) + list(range(OFF_Z, OFF_XBC, MXU_COLS))
                  + list(range(OFF_U, OFF_DT, MXU_COLS)))
    m_dots = [m_dot(wa_ref, d, d, MXU_COLS) if d < OFF_U else m_dot(wb_ref, d - OFF_U, d, MXU_COLS)
              for d in dst_starts] + [m_dot(wdt_ref, 0, OFF_DT, LANES)]

    def conv_store(c, val, rows):
        if c < SSD_WIDTH // LANES:
            bm.xs[rows, c * LANES:(c + 1) * LANES] = val
        else:
            c2 = c - SSD_WIDTH // LANES
            bm.bc[rows, c2 * LANES:(c2 + 1) * LANES] = val.astype(BF16)

    def m_conv(c):
        cols = slice(OFF_XBC + c * LANES, OFF_XBC + (c + 1) * LANES)
        wcols = slice(c * LANES, (c + 1) * LANES)

        def task_stream():
            pm[0:TAIL_ROWS, cols] = tail_ref[:, wcols]
            xfull = pm[:, cols]
            acc = convw_ref[0:1, wcols] * xfull
            for k in range(1, CONV_W):
                acc = pltpu.roll(acc, 1, axis=0) + convw_ref[k:k + 1, wcols] * xfull
            val = _silu(acc[TAIL_ROWS:, :] + convb_ref[:, wcols])
            tail_ref[:, wcols] = pm[TILE_ROWS:TILE_ROWS + TAIL_ROWS, cols]
            conv_out_ref[0, :, wcols] = pm[TAIL_ROWS + TILE_ROWS - (CONV_W - 1):TAIL_ROWS + TILE_ROWS, cols]

            def commit():
                conv_store(c, val, slice(0, TILE_ROWS))
            return commit

        def task_state():
            for s in range(nseq):
                rows = slice(s * seglen, (s + 1) * seglen)
                xp_ref[TAP0:TAIL_ROWS, :] = conv0_ref[s, :, wcols]
                xp_ref[TAIL_ROWS:TAIL_ROWS + seglen, :] = pm[TAIL_ROWS + s * seglen:TAIL_ROWS + (s + 1) * seglen, cols]
                acc = convb_ref[:, wcols]
                for k in range(CONV_W):
                    acc = acc + convw_ref[k:k + 1, wcols] * xp_ref[TAP0 + k:TAP0 + k + seglen, :]
                conv_store(c, _silu(acc), rows)
                conv_out_ref[s, :, wcols] = xp_ref[TAIL_ROWS + seglen - (CONV_W - 1):TAIL_ROWS + seglen, :]

        return task_stream if pipelined else task_state

    m_convs = [_tag(m_conv(c), f"m_conv{c}", [f"m_dot{(OFF_XBC + c * LANES) // MXU_COLS}"])
               for c in range(CONV_DIM // LANES)]

    lane1 = lax.broadcasted_iota(jnp.int32, (TILE_ROWS, LANES), 1)

    def v_dt():
        a_row = -jnp.exp(alog_ref[...])
        dt = _softplus(pv[rows_all, OFF_DT:OFF_DT + LANES] + dtb_ref[...])
        dt = jnp.where(lane1 < HEADS, dt, 0.0)
        csum = _dot(tril_ref[...], _cat3(dt * a_row).astype(BF16))
        a_cum = csum + pltpu.roll(csum, LANES - HEADS, axis=1) + pltpu.roll(csum, LANES - 2 * HEADS, axis=1)
        a_cat = _cat3(jnp.where(lane1 < HEADS, a_cum, 0.0))
        aexp_ref[...] = _dot(a_cat.astype(BF16), expand_ref[...])
        dt_exp = _dot(_cat3(dt).astype(BF16), expand_ref[...])
        even = lax.rem(lane1, 2) == 0
        a_even = jnp.where(even, a_cat, 0.0)
        a_odd = (a_cat - a_even).astype(BF16)
        a_even = a_even.astype(BF16)
        parts = []
        for ck in range(TILE_ROWS // lc):
            rows = slice(ck * lc, (ck + 1) * lc)
            parts += pad_to_half(a_even[rows, :]) + pad_to_half(a_odd[rows, :])
        acumt_ref[...] = _dot_nt(psel_ref[...], jnp.concatenate(parts, axis=0))
        xdt = bv.xs[...] * dt_exp
        lane_w = lax.broadcasted_iota(jnp.int32, (TILE_ROWS, SSD_WIDTH), 1)
        xlo = jnp.where(lax.rem(lane_w, LANES) < HALF, xdt, 0.0)
        xdt_ref[...] = xdt
        xlo_ref[...] = xlo.astype(BF16)
        xhi_ref[...] = (xdt - xlo).astype(BF16)

    li = lax.broadcasted_iota(jnp.int32, (lc, LANES), 0)
    si = lax.rem(lax.broadcasted_iota(jnp.int32, (lc, LANES), 1), HALF)
    causal = (si <= li) & (si < lc)

    def v_local(ci, g):
        rows = slice(ci * lc, (ci + 1) * lc)
        hl = slice(g * GROUP_WIDTH, (g + 1) * GROUP_WIDTH)

        def products():
            last = aexp_ref[(ci + 1) * lc - 1:(ci + 1) * lc, hl]
            xw_b = (xdt_ref[rows, hl] * jnp.exp(last - aexp_ref[rows, hl])).astype(BF16)
            b_g = bv.bc[rows, g * STATE:(g + 1) * STATE]
            c_g = bv.bc[rows, (GROUPS + g) * STATE:(GROUPS + g + 1) * STATE]
            b2 = jnp.concatenate(pad_to_half(b_g) + pad_to_half(b_g), axis=0)
            cb_ref[ci * GROUPS + g] = _dot_nt(c_g, b2)
            stnew_ref[ci, :, hl] = _dot_tn(b_g, xw_b)

        def diag():
            cb2 = cb_ref[ci * GROUPS + g]
            for i in range(PAIRS_PER_GROUP):
                p = g * PAIRS_PER_GROUP + i
                pcols = slice(p * LANES, (p + 1) * LANES)
                seg = aexp_ref[rows, pcols] - acumt_ref[p:p + 1, ci * LANES:(ci + 1) * LANES]
                w_p = (cb2 * jnp.exp(jnp.where(causal, seg, -jnp.inf))).astype(BF16)
                bd = jnp.concatenate(pad_to_half(xlo_ref[rows, pcols]) + pad_to_half(xhi_ref[rows, pcols]),
                                     axis=0)
                ys_ref[rows, pcols] = _dot(w_p, bd) + dskip_ref[:, pcols] * bv.xs[rows, pcols]
        return products, diag

    def v_state(ci, s):
        def task():
            e_last = jnp.exp(aexp_ref[(ci + 1) * lc - 1:(ci + 1) * lc, :])
            st = st_ref[...]
            stb_ref[ci] = st.astype(BF16)
            st_ref[...] = st * e_last + stnew_ref[ci]
            if has_state and ci % nchunk == nchunk - 1:
                ssm_out_ref[s] = st_ref[...].T.reshape(HEADS, HEAD_DIM, STATE)
        return task

    def v_readout(ci, g):
        rows = slice(ci * lc, (ci + 1) * lc)
        prow = slice(TAIL_ROWS + ci * lc, TAIL_ROWS + (ci + 1) * lc)
        hl = slice(g * GROUP_WIDTH, (g + 1) * GROUP_WIDTH)

        def task():
            c_g = bv.bc[rows, (GROUPS + g) * STATE:(GROUPS + g + 1) * STATE]
            y_off = _dot(c_g, stb_ref[ci, :, hl])
            y = ys_ref[rows, hl] + jnp.exp(aexp_ref[rows, hl]) * y_off
            zg = _silu(pv[prow, OFF_Z + g * GROUP_WIDTH:OFF_Z + (g + 1) * GROUP_WIDTH])
            bvo.mix[rows, hl] = (y * zg).astype(BF16)
        return task

    def v_state_load(s):
        def task():
            st_ref[...] = ssm0_ref[s].reshape(SSD_WIDTH, STATE).T
        return task

    def v_ln():
        v = pv[rows_all, OFF_V:OFF_V + GMLP_WIDTH]
        vc = v - _row_mean(v)
        vn = (vc * _rms_scale(vc)) * gv_ref[...] + betav_ref[...]
        if emit_v:
            vn_out_ref[...] = vn
        vnb_ref[...] = vn.astype(BF16)

    def v_gmlp(g):
        gcols = slice(g * GMLP_GROUP_DIM, (g + 1) * GMLP_GROUP_DIM)

        def task():
            if mixw_ref.shape[1] == TILE_ROWS:
                mixed = _dot(mixw_ref[g], vnb_ref[:, gcols])
            else:
                lcg = mixw_ref.shape[1]
                blocks = [vnb_ref[r:r + lcg, gcols] for r in range(0, TILE_ROWS, lcg)]
                wide = _dot(mixw_ref[g], jnp.concatenate(blocks, axis=1))
                mixed = jnp.concatenate(
                    [wide[:, k * GMLP_GROUP_DIM:(k + 1) * GMLP_GROUP_DIM] for k in range(len(blocks))],
                    axis=0)
            mixed = mixed + mixb_ref[:, gcols]
            u = pv[rows_all, OFF_U + g * GMLP_GROUP_DIM:OFF_U + (g + 1) * GMLP_GROUP_DIM]
            gt = pv[rows_all, OFF_G + g * GMLP_GROUP_DIM:OFF_G + (g + 1) * GMLP_GROUP_DIM]
            bvo.mix[:, SSD_WIDTH + g * GMLP_GROUP_DIM:SSD_WIDTH + (g + 1) * GMLP_GROUP_DIM] = (
                _silu(gt) * u * mixed).astype(BF16)
        return task

    chunk_groups = [(ci, g) for ci in range(nseq * nchunk) for g in range(GROUPS)]
    v_locals = [v_local(ci, g) for ci, g in chunk_groups]
    v_states = []
    for s in range(nseq):
        if has_state:
            v_states.append(v_state_load(s))
        v_states += [v_state(s * nchunk + c, s) for c in range(nchunk)]
    v_readouts = [v_readout(ci, g) for ci, g in chunk_groups]
    v_scan = ([(t[0], 0) for t in v_locals] + [(t[1], 1) for t in v_locals]
              + [(t, 1) for t in v_states] + [(t, 1) for t in v_readouts])
    v_gmlps = [v_gmlp(g) for g in range(GMLP_GROUPS)]

    def o_out(n):
        def task():
            mo_ref[:, n * MXU_COLS:(n + 1) * MXU_COLS] = _dot(
                bo.mix[...], wout_ref[:, n * MXU_COLS:(n + 1) * MXU_COLS])
        return task

    def o_final():
        mo = mo_ref[...]
        mo = mo * _rms_scale(mo)
        for s in range(nseq):
            rows = slice(s * seglen, (s + 1) * seglen)
            gain = gpost_ref[...] * gate_ref[s:s + 1, :]
            y_ref[rows, :] = xo_ref[rows, :] + gain * mo[rows, :]

    o_outs = [_tag(o_out(n), f"o_out{n}") for n in range(D_MODEL // MXU_COLS)]
    _tag(o_final, "o_final", [t.name for t in o_outs])

    if pipelined:
        n_xbc_dots = CONV_DIM // MXU_COLS
        head = _interleave([(m_norm, 1), (v_dt, 2), (v_ln, 2)], o_outs + m_dots[:n_xbc_dots])
        body = [(t, 3) for t in m_convs] + [(o_final, 3)] + v_scan + [(t, 0) for t in v_gmlps]
        groups = head + _interleave(body, m_dots[n_xbc_dots:])
    else:
        order = ([m_norm] + m_dots + m_convs + [v_dt, v_ln] + [t for t, _ in v_scan] + v_gmlps
                 + o_outs + [o_final])
        groups = [[task] for task in order]
    _run_groups(groups)


def _const_spec(shape):
    nd = len(shape)
    return pl.BlockSpec(shape, lambda j: (0,) * nd, pipeline_mode=pl.Buffered(1))


def _chunk_constants(lc):
    expand = np.zeros((LANES, SSD_WIDTH), np.float32)
    psel = np.zeros((PAIRS, LANES), np.float32)
    for piece in range(PIECES):
        for hd in range(HEADS):
            expand[piece * HEADS + hd, hd * HEAD_DIM:(hd + 1) * HEAD_DIM] = 1.0
            psel[hd // 2, piece * HEADS + hd] = 1.0
    r = np.arange(TILE_ROWS)
    tril = ((r[:, None] >= r[None, :]) & (r[:, None] // lc == r[None, :] // lc)).astype(np.float32)
    return jnp.asarray(expand, BF16), jnp.asarray(tril, BF16), jnp.asarray(psel, BF16)


def _mix_weights(w_s, b_s, lc):
    nck = TILE_ROWS // lc
    mask = jnp.tril(jnp.ones((lc, lc), dtype=bool))
    w = jnp.where(mask[None], w_s[:, :lc, :lc], 0)
    if nck * GMLP_GROUP_DIM > MXU_COLS:
        eye = jnp.eye(nck, dtype=w.dtype)
        w = jnp.einsum("ab,gts->gatbs", eye, w).reshape(GMLP_GROUPS, TILE_ROWS, TILE_ROWS)
    bias = jnp.repeat(b_s[:, :lc].T, GMLP_GROUP_DIM, axis=1)
    return w.astype(BF16), jnp.tile(bias, (nck, 1))


def _layer(x, shift, scale, gate, conv0, ssm0, wl, *, nseq, seglen, lc_ssd, lc_gmlp, nt, emit_v,
           layer=0):
    (g_pre, g_post, w_a, w_b, w_dt, conv_w, conv_b, dtb, alog, dskip, g_v, beta_v, w_s, b_s,
     w_out) = wl
    assert lc_ssd == lc_gmlp or nseq == 1
    pipelined = conv0 is None
    nb = shift.shape[0]
    rows_total = x.shape[0]
    ntiles = rows_total // TILE_ROWS
    assert nseq * seglen == TILE_ROWS and nb * nt == ntiles
    expand, tril, psel = _chunk_constants(lc_ssd)
    mixw, mixb = _mix_weights(w_s, b_s, lc_gmlp)
    nck = TILE_ROWS // lc_ssd

    if pipelined:
        tps = TILES_PER_STEP
        assert ntiles % tps == 0

        def tile_m(j):
            return jnp.minimum(tps * j, ntiles - tps)

        def tile_v(j):
            return jnp.clip(tps * j - 1, 0, ntiles - 1)

        def tile_o(j):
            return jnp.maximum(tps * j - 2, 0)
        steps = ntiles // tps + 1
    else:
        tps = 1

        def tile_m(j):
            return j
        tile_v = tile_o = tile_m
        steps = ntiles

    def mod_spec(tile_of):
        return pl.BlockSpec((None, nseq, D_MODEL), lambda j: (tile_of(j) // nt, 0, 0))

    in_specs = [pl.BlockSpec((tps * TILE_ROWS, D_MODEL), lambda j: (tile_m(j) // tps, 0)),
                pl.BlockSpec((tps * TILE_ROWS, D_MODEL), lambda j: (tile_o(j) // tps, 0)),
                mod_spec(tile_m), mod_spec(tile_m), mod_spec(tile_o)]
    args = [x, x, shift, scale, gate]
    if not pipelined:
        in_specs += [pl.BlockSpec((None, nseq, CONV_W - 1, CONV_DIM), lambda j: (j // nt, 0, 0, 0)),
                     pl.BlockSpec((None, nseq, HEADS, HEAD_DIM, STATE), lambda j: (layer, j, 0, 0, 0))]
        args += [conv0, ssm0]
    consts = [g_pre, g_post, w_a, w_b, w_dt, conv_w, conv_b, dtb, alog, dskip, g_v, beta_v, mixw,
              mixb, w_out, expand, tril, psel]
    in_specs += [_const_spec(a.shape) for a in consts]
    args += consts

    out_shape = [jax.ShapeDtypeStruct((rows_total, D_MODEL), F32),
                 jax.ShapeDtypeStruct((nb, nseq, CONV_W - 1, CONV_DIM), F32),
                 jax.ShapeDtypeStruct((nb * nseq, HEADS, HEAD_DIM, STATE), F32)]
    out_specs = [pl.BlockSpec((tps * TILE_ROWS, D_MODEL), lambda j: (tile_o(j) // tps, 0)),
                 pl.BlockSpec((None, nseq, CONV_W - 1, CONV_DIM), lambda j: (tile_m(j) // nt, 0, 0, 0)),
                 pl.BlockSpec((nseq, HEADS, HEAD_DIM, STATE), lambda j: (tile_v(j) // nt, 0, 0, 0))]
    if emit_v:
        out_shape.append(jax.ShapeDtypeStruct((rows_total, GMLP_WIDTH), F32))
        out_specs.append(pl.BlockSpec((TILE_ROWS, GMLP_WIDTH), lambda j: (tile_v(j), 0)))

    spare = (TAIL_ROWS, LANES)
    stage_shapes = [((TAIL_ROWS + TILE_ROWS, W_CAT_COLS), F32),
                    ((TILE_ROWS, SSD_WIDTH), F32),
                    ((TILE_ROWS, 2 * GROUPS * STATE), BF16),
                    ((TILE_ROWS, SSD_WIDTH + GMLP_WIDTH), BF16)]
    scratch = [pltpu.VMEM((TILE_ROWS, D_MODEL), BF16),
               pltpu.VMEM((TAIL_ROWS, CONV_DIM), F32),
               pltpu.VMEM((TAIL_ROWS + seglen, LANES), F32),
               pltpu.VMEM((STATE, SSD_WIDTH), F32),
               pltpu.VMEM((nck, STATE, SSD_WIDTH), F32),
               pltpu.VMEM((TILE_ROWS, SSD_WIDTH), F32),
               pltpu.VMEM((TILE_ROWS, SSD_WIDTH), F32),
               pltpu.VMEM((TILE_ROWS, SSD_WIDTH), BF16),
               pltpu.VMEM((TILE_ROWS, SSD_WIDTH), BF16),
               pltpu.VMEM((PAIRS, nck * LANES), F32),
               pltpu.VMEM((TILE_ROWS, GMLP_WIDTH), BF16),
               pltpu.VMEM((TILE_ROWS, D_MODEL), F32),
               pltpu.VMEM((nck, STATE, SSD_WIDTH), BF16),
               pltpu.VMEM((TILE_ROWS, SSD_WIDTH), F32),
               pltpu.VMEM((nck * GROUPS, lc_ssd, LANES), F32)]
    assert len(scratch) == N_SHARED_SCRATCH
    for shape, dtype in stage_shapes:
        scratch += [pltpu.VMEM(shape, dtype), pltpu.VMEM(shape if pipelined else spare, dtype)]
    kern = functools.partial(_layer_kernel, nseq=nseq, seglen=seglen, lc=lc_ssd, nt=nt,
                             pipelined=pipelined, emit_v=emit_v)
    return pl.pallas_call(
        kern,
        grid=(steps,),
        in_specs=in_specs,
        out_specs=out_specs,
        out_shape=out_shape,
        scratch_shapes=scratch,
        compiler_params=pltpu.CompilerParams(
            dimension_semantics=("arbitrary",),
            vmem_limit_bytes=VMEM_LIMIT_BYTES),
        name="hybrid_layer_stream" if pipelined else "hybrid_layer_state",
    )(*args)


def _prep_weights(g_pre, g_post, w_in, conv_w, conv_b, dt_bias, a_log, d_skip, g_v, beta_v,
                  w_s, b_s, w_out):
    w_a, w_b, w_dt = _cast_weights(w_in)
    pad = (0, LANES - HEADS)
    return (g_pre.reshape(1, -1), g_post.reshape(1, -1), w_a, w_b, w_dt, conv_w, conv_b.reshape(1, -1),
            jnp.pad(dt_bias, pad).reshape(1, -1), jnp.pad(a_log, pad).reshape(1, -1),
            jnp.repeat(d_skip, HEAD_DIM).reshape(1, -1), g_v.reshape(1, -1),
            beta_v.reshape(1, -1), w_s, b_s, w_out.astype(BF16))


def kernel(x_prompt, x_sample, state_conv, state_ssm, c_prompt, c_sample, w_ada, b_ada, g_pre,
           g_post, w_in, conv_w, conv_b, dt_bias, a_log, d_skip, g_v, beta_v, w_s, b_s, w_out):
    depth = w_ada.shape[0]
    bp, lp, _ = x_prompt.shape
    bs, ls, _ = x_sample.shape
    seq_per_tile = TILE_ROWS // ls
    yp = x_prompt.reshape(bp * lp, D_MODEL)
    ys = x_sample.reshape(bs * ls, D_MODEL)
    c_all = jnp.concatenate([c_prompt, c_sample], axis=0)
    c_pad = jnp.pad(c_all, ((0, (-c_all.shape[0]) % 8), (0, 0)))
    outs = [[] for _ in range(5)]
    for l in range(depth):
        wl = _prep_weights(g_pre[l], g_post[l], w_in[l], conv_w[l], conv_b[l], dt_bias[l],
                           a_log[l], d_skip[l], g_v[l], beta_v[l], w_s[l], b_s[l], w_out[l])
        mod = _modulation(c_pad, w_ada[l], b_ada[l])
        shift, scale, gate = (mod[:, i * D_MODEL:(i + 1) * D_MODEL] for i in range(3))
        mp = [m[:bp].reshape(bp, 1, D_MODEL) for m in (shift, scale, gate)]
        ms = [m[bp:bp + bs].reshape(bs // seq_per_tile, seq_per_tile, D_MODEL)
              for m in (shift, scale, gate)]
        yp, conv_p, ssm_p = _layer(
            yp, *mp, None, None, wl, nseq=1, seglen=TILE_ROWS, lc_ssd=min(lp, SSD_CHUNK),
            lc_gmlp=min(lp, GMLP_CHUNK), nt=lp // TILE_ROWS, emit_v=False)
        ys, conv_s, ssm_s, v_s = _layer(
            ys, *ms,
            state_conv[l].reshape(bs // seq_per_tile, seq_per_tile, CONV_W - 1, CONV_DIM),
            state_ssm, wl,
            nseq=seq_per_tile, seglen=ls, lc_ssd=min(ls, SSD_CHUNK), lc_gmlp=min(ls, GMLP_CHUNK),
            nt=1, emit_v=True, layer=l)
        outs[0].append(conv_p.reshape(bp, CONV_W - 1, CONV_DIM))
        outs[1].append(ssm_p)
        outs[2].append(conv_s.reshape(bs, CONV_W - 1, CONV_DIM))
        outs[3].append(ssm_s)
        outs[4].append(v_s.reshape(bs, ls, GMLP_WIDTH))
    stacked = [o[0][None] if depth == 1 else jnp.stack(o) for o in outs]
    return (yp.reshape(bp, lp, D_MODEL), ys.reshape(bs, ls, D_MODEL), *stacked)
```

```python
import collections
import functools

import jax
import jax.numpy as jnp
import numpy as np
from jax import lax
from jax.experimental import pallas as pl
from jax.experimental.pallas import tpu as pltpu

F32 = jnp.float32
BF16 = jnp.bfloat16

D_MODEL = 1024
SSD_WIDTH = 1024
HEAD_DIM = 64
HEADS = SSD_WIDTH // HEAD_DIM
GROUPS = 2
GROUP_WIDTH = SSD_WIDTH // GROUPS
STATE = 128
CONV_W = 4
CONV_DIM = SSD_WIDTH + 2 * GROUPS * STATE
GMLP_WIDTH = 1024
GMLP_GROUPS = 8
GMLP_GROUP_DIM = GMLP_WIDTH // GMLP_GROUPS
SSD_CHUNK = 64
GMLP_CHUNK = 128
EPS = 1e-6
LOG2E = 1.4426950408889634

LANES = 128
MXU_COLS = 256
HALF = LANES // 2
PAIRS = HEADS // 2
PAIRS_PER_GROUP = PAIRS // GROUPS
PIECES = 3
TILE_ROWS = 256
TAIL_ROWS = 8
TAP0 = TAIL_ROWS - (CONV_W - 1)
MOD_BLOCK_COLS = 512
VMEM_LIMIT_BYTES = 60 * 1024 * 1024

OFF_Z = 0
OFF_XBC = OFF_Z + SSD_WIDTH
OFF_U = OFF_XBC + CONV_DIM
OFF_V = OFF_U + GMLP_WIDTH
OFF_G = OFF_V + GMLP_WIDTH
OFF_DT = OFF_G + GMLP_WIDTH
W_CAT_COLS = OFF_DT + LANES

Bufs = collections.namedtuple("Bufs", "proj xs bc mix")


def _dot(a, b):
    return jnp.dot(a, b, preferred_element_type=F32)


def _dot_nt(a, b):
    return lax.dot_general(a, b, (((1,), (1,)), ((), ())), preferred_element_type=F32)


def _dot_tn(a, b):
    return lax.dot_general(a, b, (((0,), (0,)), ((), ())), preferred_element_type=F32)


def _cat3(x):
    hi = x.astype(BF16).astype(F32)
    r1 = x - hi
    mid = r1.astype(BF16).astype(F32)
    lo = r1 - mid
    return hi + pltpu.roll(mid, HEADS, axis=1) + pltpu.roll(lo, 2 * HEADS, axis=1)


def _silu(x):
    return x / (1.0 + jnp.exp2(x * (-LOG2E)))


def _softplus(x):
    return jnp.maximum(x, 0.0) + jnp.log1p(jnp.exp(-jnp.abs(x)))


def _row_mean(x):
    width = x.shape[-1]
    part = x[:, 0:LANES]
    for c in range(LANES, width, LANES):
        part = part + x[:, c:c + LANES]
    return jnp.sum(part, axis=-1, keepdims=True) * (1.0 / width)


def _rms_scale(x):
    return lax.rsqrt(_row_mean(x * x) + EPS)


def _mod_kernel(c_ref, w_ref, b_ref, o_ref):
    c = _silu(c_ref[...]).astype(BF16)
    o_ref[...] = _dot(c, w_ref[...].astype(BF16)) + b_ref[...]


def _modulation(c, w_ada, b_ada):
    rows = c.shape[0]
    n = w_ada.shape[1]
    return pl.pallas_call(
        _mod_kernel,
        grid=(n // MOD_BLOCK_COLS,),
        in_specs=[
            pl.BlockSpec((rows, D_MODEL), lambda j: (0, 0)),
            pl.BlockSpec((D_MODEL, MOD_BLOCK_COLS), lambda j: (0, j)),
            pl.BlockSpec((1, MOD_BLOCK_COLS), lambda j: (0, j)),
        ],
        out_specs=pl.BlockSpec((rows, MOD_BLOCK_COLS), lambda j: (0, j)),
        out_shape=jax.ShapeDtypeStruct((rows, n), F32),
        name="adaln_mod",
    )(c, w_ada, b_ada.reshape(1, n))


CAST_BLOCK_COLS = 512
A_COLS = SSD_WIDTH + CONV_DIM
B_COLS = 3 * GMLP_WIDTH


def _cast_kernel(a_ref, b_ref, b_next_ref, wa_ref, wb_ref, wdt_ref):
    j = pl.program_id(0)
    wa_ref[...] = a_ref[...].astype(BF16)
    wide = jnp.concatenate([b_ref[...], b_next_ref[...]], axis=1)
    shifted = pltpu.roll(wide, wide.shape[1] - HEADS, axis=1)
    wb_ref[...] = shifted[:, :CAST_BLOCK_COLS].astype(BF16)

    @pl.when(j == 0)
    def _():
        head = b_ref[:, 0:LANES]
        lane = lax.broadcasted_iota(jnp.int32, head.shape, 1)
        wdt_ref[...] = jnp.where(lane < HEADS, head, 0.0).astype(BF16)


def _cast_weights(w_in):
    a_blocks = A_COLS // CAST_BLOCK_COLS
    b_blocks = B_COLS // CAST_BLOCK_COLS
    lanes_per_block = CAST_BLOCK_COLS // LANES
    return pl.pallas_call(
        _cast_kernel,
        grid=(b_blocks,),
        in_specs=[
            pl.BlockSpec((D_MODEL, CAST_BLOCK_COLS), lambda j: (0, jnp.minimum(j, a_blocks - 1))),
            pl.BlockSpec((D_MODEL, CAST_BLOCK_COLS), lambda j: (0, a_blocks + j)),
            pl.BlockSpec((D_MODEL, LANES), lambda j: (0, (a_blocks + j + 1) * lanes_per_block)),
        ],
        out_specs=[
            pl.BlockSpec((D_MODEL, CAST_BLOCK_COLS), lambda j: (0, jnp.minimum(j, a_blocks - 1))),
            pl.BlockSpec((D_MODEL, CAST_BLOCK_COLS), lambda j: (0, j)),
            pl.BlockSpec((D_MODEL, LANES), lambda j: (0, 0)),
        ],
        out_shape=[jax.ShapeDtypeStruct((D_MODEL, A_COLS), BF16),
                   jax.ShapeDtypeStruct((D_MODEL, B_COLS), BF16),
                   jax.ShapeDtypeStruct((D_MODEL, LANES), BF16)],
        compiler_params=pltpu.CompilerParams(dimension_semantics=("arbitrary",)),
        name="cast_proj_weight",
    )(w_in, w_in, w_in)


TILES_PER_STEP = 2
N_SHARED_SCRATCH = 16
N_STAGE_SCRATCH = 8


def _interleave(primary, secondary):
    total = sum(weight for _, weight in primary)
    groups = []
    done = 0
    seen = 0
    for task, weight in primary:
        seen += weight
        want = (seen * len(secondary)) // total
        groups.append([task] + secondary[done:want])
        done = want
    return groups


def _tag(task, name, needs=()):
    task.name, task.needs = name, tuple(needs)
    return task


def _run_groups(groups):
    stored = set()
    for group in groups:
        for task in group:
            missing = [n for n in getattr(task, "needs", ()) if n not in stored]
            assert not missing, (getattr(task, "name", task), missing)
        commits = [task() for task in group]
        for commit in commits:
            if commit is not None:
                commit()
        stored.update(getattr(task, "name", None) for task in group)


def _layer_kernel(*refs, nseq, seglen, lc, nt, pipelined, emit_v):
    statics = dict(nseq=nseq, seglen=seglen, lc=lc, pipelined=pipelined, emit_v=emit_v)
    shared, stage = refs[:-N_STAGE_SCRATCH], refs[-N_STAGE_SCRATCH:]
    buf_a, buf_b = Bufs(*stage[0::2]), Bufs(*stage[1::2])
    n_scratch = N_SHARED_SCRATCH + N_STAGE_SCRATCH
    ssm_out_ref = refs[len(refs) - n_scratch - (2 if emit_v else 1)]
    tail_ref, st_ref = refs[-n_scratch + 1], refs[-n_scratch + 3]
    if not pipelined:
        _tile_body(shared, buf_a, buf_a, buf_a, buf_a, **statics)
        return

    assert nt % TILES_PER_STEP == 0 and TILES_PER_STEP == 2
    i = pl.program_id(0)
    row_start = lax.rem(TILES_PER_STEP * i, nt) == 0
    n_in = len(shared) - N_SHARED_SCRATCH - 3
    views = []
    for half in range(TILES_PER_STEP):
        rows = slice(half * TILE_ROWS, (half + 1) * TILE_ROWS)
        view = list(shared)
        for k in (0, 1, n_in):
            view[k] = shared[k].at[rows, :]
        views.append(view)

    @pl.when(i == 0)
    def _():
        for ref in buf_b:
            ref[...] = jnp.zeros_like(ref)
        st_ref[...] = jnp.zeros_like(st_ref)

    @pl.when(row_start)
    def _():
        tail_ref[...] = jnp.zeros_like(tail_ref)

    _tile_body(views[0], buf_a, buf_b, buf_a, buf_b, **statics)

    @pl.when(row_start & (i > 0))
    def _():
        ssm_out_ref[0] = st_ref[...].T.reshape(HEADS, HEAD_DIM, STATE)

    @pl.when(row_start)
    def _():
        st_ref[...] = jnp.zeros_like(st_ref)

    _tile_body(views[1], buf_b, buf_a, buf_b, buf_a, **statics)


def _tile_body(refs, bm, bv, bvo, bo, *, nseq, seglen, lc, pipelined, emit_v):
    has_state = not pipelined
    it = iter(refs)
    x_ref, xo_ref = next(it), next(it)
    shift_ref, scale_ref, gate_ref = next(it), next(it), next(it)
    conv0_ref = next(it) if has_state else None
    ssm0_ref = next(it) if has_state else None
    (gpre_ref, gpost_ref, wa_ref, wb_ref, wdt_ref, convw_ref, convb_ref, dtb_ref, alog_ref,
     dskip_ref, gv_ref, betav_ref, mixw_ref, mixb_ref, wout_ref, expand_ref, tril_ref, psel_ref) = (
         next(it) for _ in range(18))
    y_ref, conv_out_ref, ssm_out_ref = next(it), next(it), next(it)
    vn_out_ref = next(it) if emit_v else None
    (h_ref, tail_ref, xp_ref, st_ref, stnew_ref, aexp_ref, xdt_ref, xlo_ref, xhi_ref, acumt_ref,
     vnb_ref, mo_ref, stb_ref, ys_ref, cb_ref, yoff_ref) = (next(it) for _ in range(N_SHARED_SCRATCH))
    pm, pv = bm.proj, bv.proj

    rows_all = slice(TAIL_ROWS, TAIL_ROWS + TILE_ROWS)
    nchunk = seglen // lc
    pad_rows = HALF - lc

    def pad_to_half(v):
        if pad_rows == 0:
            return [v]
        return [v, jnp.zeros((pad_rows, v.shape[1]), v.dtype)]

    def m_norm():
        for s in range(nseq):
            rows = slice(s * seglen, (s + 1) * seglen)
            xr = x_ref[rows, :]
            gain = gpre_ref[...] * (1.0 + scale_ref[s:s + 1, :])
            hs = (xr * _rms_scale(xr)) * gain + shift_ref[s:s + 1, :]
            h_ref[rows, :] = hs.astype(BF16)

    def m_dot(w_ref, w0, dst0, width):
        def task():
            val = _dot(h_ref[...], w_ref[:, w0:w0 + width])

            def commit():
                pm[rows_all, dst0:dst0 + width] = val
            return commit
        return _tag(task, f"m_dot{dst0 // MXU_COLS}")

    dst_starts = (list(range(OFF_XBC, OFF_U, MXU_COLS)) + list(range(OFF_Z, OFF_XBC, MXU_COLS))
                  + list(range(OFF_U, OFF_DT, MXU_COLS)))
    m_dots = [m_dot(wa_ref, d, d, MXU_COLS) if d < OFF_U else m_dot(wb_ref, d - OFF_U, d, MXU_COLS)
              for d in dst_starts] + [m_dot(wdt_ref, 0, OFF_DT, LANES)]

    def conv_store(c, val, rows):
        if c < SSD_WIDTH // LANES:
            bm.xs[rows, c * LANES:(c + 1) * LANES] = val
        else:
            c2 = c - SSD_WIDTH // LANES
            bm.bc[rows, c2 * LANES:(c2 + 1) * LANES] = val.astype(BF16)

    def m_conv(c):
        cols = slice(OFF_XBC + c * LANES, OFF_XBC + (c + 1) * LANES)
        wcols = slice(c * LANES, (c + 1) * LANES)

        def task_stream():
            pm[0:TAIL_ROWS, cols] = tail_ref[:, wcols]
            xfull = pm[:, cols]
            acc = convw_ref[0:1, wcols] * xfull
            for k in range(1, CONV_W):
                acc = pltpu.roll(acc, 1, axis=0) + convw_ref[k:k + 1, wcols] * xfull
            val = _silu(acc[TAIL_ROWS:, :] + convb_ref[:, wcols])
            tail_ref[:, wcols] = pm[TILE_ROWS:TILE_ROWS + TAIL_ROWS, cols]
            conv_out_ref[0, :, wcols] = pm[TAIL_ROWS + TILE_ROWS - (CONV_W - 1):TAIL_ROWS + TILE_ROWS, cols]

            def commit():
                conv_store(c, val, slice(0, TILE_ROWS))
            return commit

        def task_state():
            for s in range(nseq):
                rows = slice(s * seglen, (s + 1) * seglen)
                xp_ref[TAP0:TAIL_ROWS, :] = conv0_ref[s, :, wcols]
                xp_ref[TAIL_ROWS:TAIL_ROWS + seglen, :] = pm[TAIL_ROWS + s * seglen:TAIL_ROWS + (s + 1) * seglen, cols]
                acc = convb_ref[:, wcols]
                for k in range(CONV_W):
                    acc = acc + convw_ref[k:k + 1, wcols] * xp_ref[TAP0 + k:TAP0 + k + seglen, :]
                conv_store(c, _silu(acc), rows)
                conv_out_ref[s, :, wcols] = xp_ref[TAIL_ROWS + seglen - (CONV_W - 1):TAIL_ROWS + seglen, :]

        return task_stream if pipelined else task_state

    m_convs = [_tag(m_conv(c), f"m_conv{c}", [f"m_dot{(OFF_XBC + c * LANES) // MXU_COLS}"])
               for c in range(CONV_DIM // LANES)]

    lane1 = lax.broadcasted_iota(jnp.int32, (TILE_ROWS, LANES), 1)

    def v_dt():
        a_row = -jnp.exp(alog_ref[...])
        dt = _softplus(pv[rows_all, OFF_DT:OFF_DT + LANES] + dtb_ref[...])
        dt = jnp.where(lane1 < HEADS, dt, 0.0)
        csum = _dot(tril_ref[...], _cat3(dt * a_row).astype(BF16))
        a_cum = csum + pltpu.roll(csum, LANES - HEADS, axis=1) + pltpu.roll(csum, LANES - 2 * HEADS, axis=1)
        a_cat = _cat3(jnp.where(lane1 < HEADS, a_cum, 0.0))

        def per_head_lanes(v):
            blocks = []
            for p in range(PAIRS):
                lo = jnp.broadcast_to(v[:, 2 * p:2 * p + 1], (TILE_ROWS, LANES))
                hi = jnp.broadcast_to(v[:, 2 * p + 1:2 * p + 2], (TILE_ROWS, LANES))
                blocks.append(jnp.where(lane1 < HALF, lo, hi))
            return jnp.concatenate(blocks, axis=1)

        aexp_ref[...] = per_head_lanes(a_cum)
        dt_exp = per_head_lanes(dt)
        even = lax.rem(lane1, 2) == 0
        a_even = jnp.where(even, a_cat, 0.0)
        a_odd = (a_cat - a_even).astype(BF16)
        a_even = a_even.astype(BF16)
        parts = []
        for ck in range(TILE_ROWS // lc):
            rows = slice(ck * lc, (ck + 1) * lc)
            parts += pad_to_half(a_even[rows, :]) + pad_to_half(a_odd[rows, :])
        acumt_ref[...] = _dot_nt(psel_ref[...], jnp.concatenate(parts, axis=0))
        xdt = bv.xs[...] * dt_exp
        lane_w = lax.broadcasted_iota(jnp.int32, (TILE_ROWS, SSD_WIDTH), 1)
        xlo = jnp.where(lax.rem(lane_w, LANES) < HALF, xdt, 0.0)
        xdt_ref[...] = xdt
        xlo_ref[...] = xlo.astype(BF16)
        xhi_ref[...] = (xdt - xlo).astype(BF16)

    li = lax.broadcasted_iota(jnp.int32, (lc, LANES), 0)
    si = lax.rem(lax.broadcasted_iota(jnp.int32, (lc, LANES), 1), HALF)
    causal = (si <= li) & (si < lc)

    def v_local(ci, g):
        rows = slice(ci * lc, (ci + 1) * lc)
        hl = slice(g * GROUP_WIDTH, (g + 1) * GROUP_WIDTH)

        def products():
            last = aexp_ref[(ci + 1) * lc - 1:(ci + 1) * lc, hl]
            xw_b = (xdt_ref[rows, hl] * jnp.exp(last - aexp_ref[rows, hl])).astype(BF16)
            b_g = bv.bc[rows, g * STATE:(g + 1) * STATE]
            c_g = bv.bc[rows, (GROUPS + g) * STATE:(GROUPS + g + 1) * STATE]
            b2 = jnp.concatenate(pad_to_half(b_g) + pad_to_half(b_g), axis=0)
            cb_ref[ci * GROUPS + g] = _dot_nt(c_g, b2)
            stnew_ref[ci, :, hl] = _dot_tn(b_g, xw_b)

        def diag():
            cb2 = cb_ref[ci * GROUPS + g]
            for i in range(PAIRS_PER_GROUP):
                p = g * PAIRS_PER_GROUP + i
                pcols = slice(p * LANES, (p + 1) * LANES)
                seg = aexp_ref[rows, pcols] - acumt_ref[p:p + 1, ci * LANES:(ci + 1) * LANES]
                w_p = (cb2 * jnp.exp(jnp.where(causal, seg, -jnp.inf))).astype(BF16)
                bd = jnp.concatenate(pad_to_half(xlo_ref[rows, pcols]) + pad_to_half(xhi_ref[rows, pcols]),
                                     axis=0)
                ys_ref[rows, pcols] = _dot(w_p, bd) + dskip_ref[:, pcols] * bv.xs[rows, pcols]
        return products, diag

    def v_state(ci, s):
        def task():
            e_last = jnp.exp(aexp_ref[(ci + 1) * lc - 1:(ci + 1) * lc, :])
            st = st_ref[...]
            stb_ref[ci] = st.astype(BF16)
            st_ref[...] = st * e_last + stnew_ref[ci]
            if has_state and ci % nchunk == nchunk - 1:
                ssm_out_ref[s] = st_ref[...].T.reshape(HEADS, HEAD_DIM, STATE)
        return task

    def v_readout(ci, g):
        rows = slice(ci * lc, (ci + 1) * lc)
        prow = slice(TAIL_ROWS + ci * lc, TAIL_ROWS + (ci + 1) * lc)
        hl = slice(g * GROUP_WIDTH, (g + 1) * GROUP_WIDTH)

        def product():
            c_g = bv.bc[rows, (GROUPS + g) * STATE:(GROUPS + g + 1) * STATE]
            yoff_ref[rows, hl] = _dot(c_g, stb_ref[ci, :, hl])

        def gate():
            y = ys_ref[rows, hl] + jnp.exp(aexp_ref[rows, hl]) * yoff_ref[rows, hl]
            zg = _silu(pv[prow, OFF_Z + g * GROUP_WIDTH:OFF_Z + (g + 1) * GROUP_WIDTH])
            bvo.mix[rows, hl] = (y * zg).astype(BF16)
        return product, gate

    def v_state_load(s):
        def task():
            st_ref[...] = ssm0_ref[s].reshape(SSD_WIDTH, STATE).T
        return task

    def v_ln():
        v = pv[rows_all, OFF_V:OFF_V + GMLP_WIDTH]
        vc = v - _row_mean(v)
        vn = (vc * _rms_scale(vc)) * gv_ref[...] + betav_ref[...]
        if emit_v:
            vn_out_ref[...] = vn
        vnb_ref[...] = vn.astype(BF16)

    def v_gmlp(g):
        gcols = slice(g * GMLP_GROUP_DIM, (g + 1) * GMLP_GROUP_DIM)

        def task():
            if mixw_ref.shape[1] == TILE_ROWS:
                mixed = _dot(mixw_ref[g], vnb_ref[:, gcols])
            else:
                lcg = mixw_ref.shape[1]
                blocks = [vnb_ref[r:r + lcg, gcols] for r in range(0, TILE_ROWS, lcg)]
                wide = _dot(mixw_ref[g], jnp.concatenate(blocks, axis=1))
                mixed = jnp.concatenate(
                    [wide[:, k * GMLP_GROUP_DIM:(k + 1) * GMLP_GROUP_DIM] for k in range(len(blocks))],
                    axis=0)
            mixed = mixed + mixb_ref[:, gcols]
            u = pv[rows_all, OFF_U + g * GMLP_GROUP_DIM:OFF_U + (g + 1) * GMLP_GROUP_DIM]
            gt = pv[rows_all, OFF_G + g * GMLP_GROUP_DIM:OFF_G + (g + 1) * GMLP_GROUP_DIM]
            bvo.mix[:, SSD_WIDTH + g * GMLP_GROUP_DIM:SSD_WIDTH + (g + 1) * GMLP_GROUP_DIM] = (
                _silu(gt) * u * mixed).astype(BF16)
        return task

    chunk_groups = [(ci, g) for ci in range(nseq * nchunk) for g in range(GROUPS)]
    v_locals = [v_local(ci, g) for ci, g in chunk_groups]
    v_states = []
    for s in range(nseq):
        if has_state:
            v_states.append(v_state_load(s))
        v_states += [v_state(s * nchunk + c, s) for c in range(nchunk)]
    v_readouts = [v_readout(ci, g) for ci, g in chunk_groups]
    v_scan = ([(t[0], 0) for t in v_locals] + [(t[1], 1) for t in v_locals]
              + [(t, 1) for t in v_states]
              + [(t[0], 0) for t in v_readouts] + [(t[1], 1) for t in v_readouts])
    v_gmlps = [v_gmlp(g) for g in range(GMLP_GROUPS)]

    def o_out(n):
        def task():
            mo_ref[:, n * MXU_COLS:(n + 1) * MXU_COLS] = _dot(
                bo.mix[...], wout_ref[:, n * MXU_COLS:(n + 1) * MXU_COLS])
        return task

    def o_final():
        mo = mo_ref[...]
        mo = mo * _rms_scale(mo)
        for s in range(nseq):
            rows = slice(s * seglen, (s + 1) * seglen)
            gain = gpost_ref[...] * gate_ref[s:s + 1, :]
            y_ref[rows, :] = xo_ref[rows, :] + gain * mo[rows, :]

    o_outs = [_tag(o_out(n), f"o_out{n}") for n in range(D_MODEL // MXU_COLS)]
    _tag(o_final, "o_final", [t.name for t in o_outs])

    if pipelined:
        n_xbc_dots = CONV_DIM // MXU_COLS
        head = _interleave([(m_norm, 1), (v_dt, 2), (v_ln, 2)], o_outs + m_dots[:n_xbc_dots])
        body = [(t, 3) for t in m_convs] + [(o_final, 3)] + v_scan + [(t, 0) for t in v_gmlps]
        groups = head + _interleave(body, m_dots[n_xbc_dots:])
    else:
        order = ([m_norm] + m_dots + m_convs + [v_dt, v_ln] + [t for t, _ in v_scan] + v_gmlps
                 + o_outs + [o_final])
        groups = [[task] for task in order]
    _run_groups(groups)


def _const_spec(shape):
    nd = len(shape)
    return pl.BlockSpec(shape, lambda j: (0,) * nd, pipeline_mode=pl.Buffered(1))


def _chunk_constants(lc):
    expand = np.zeros((LANES, SSD_WIDTH), np.float32)
    psel = np.zeros((PAIRS, LANES), np.float32)
    for piece in range(PIECES):
        for hd in range(HEADS):
            expand[piece * HEADS + hd, hd * HEAD_DIM:(hd + 1) * HEAD_DIM] = 1.0
            psel[hd // 2, piece * HEADS + hd] = 1.0
    r = np.arange(TILE_ROWS)
    tril = ((r[:, None] >= r[None, :]) & (r[:, None] // lc == r[None, :] // lc)).astype(np.float32)
    return jnp.asarray(expand, BF16), jnp.asarray(tril, BF16), jnp.asarray(psel, BF16)


def _mix_weights(w_s, b_s, lc):
    nck = TILE_ROWS // lc
    mask = jnp.tril(jnp.ones((lc, lc), dtype=bool))
    w = jnp.where(mask[None], w_s[:, :lc, :lc], 0)
    if nck * GMLP_GROUP_DIM > MXU_COLS:
        eye = jnp.eye(nck, dtype=w.dtype)
        w = jnp.einsum("ab,gts->gatbs", eye, w).reshape(GMLP_GROUPS, TILE_ROWS, TILE_ROWS)
    bias = jnp.repeat(b_s[:, :lc].T, GMLP_GROUP_DIM, axis=1)
    return w.astype(BF16), jnp.tile(bias, (nck, 1))


def _layer(x, shift, scale, gate, conv0, ssm0, wl, *, nseq, seglen, lc_ssd, lc_gmlp, nt, emit_v,
           layer=0):
    (g_pre, g_post, w_a, w_b, w_dt, conv_w, conv_b, dtb, alog, dskip, g_v, beta_v, w_s, b_s,
     w_out) = wl
    assert lc_ssd == lc_gmlp or nseq == 1
    pipelined = conv0 is None
    nb = shift.shape[0]
    rows_total = x.shape[0]
    ntiles = rows_total // TILE_ROWS
    assert nseq * seglen == TILE_ROWS and nb * nt == ntiles
    expand, tril, psel = _chunk_constants(lc_ssd)
    mixw, mixb = _mix_weights(w_s, b_s, lc_gmlp)
    nck = TILE_ROWS // lc_ssd

    if pipelined:
        tps = TILES_PER_STEP
        assert ntiles % tps == 0

        def tile_m(j):
            return jnp.minimum(tps * j, ntiles - tps)

        def tile_v(j):
            return jnp.clip(tps * j - 1, 0, ntiles - 1)

        def tile_o(j):
            return jnp.maximum(tps * j - 2, 0)
        steps = ntiles // tps + 1
    else:
        tps = 1

        def tile_m(j):
            return j
        tile_v = tile_o = tile_m
        steps = ntiles

    def mod_spec(tile_of):
        return pl.BlockSpec((None, nseq, D_MODEL), lambda j: (tile_of(j) // nt, 0, 0))

    in_specs = [pl.BlockSpec((tps * TILE_ROWS, D_MODEL), lambda j: (tile_m(j) // tps, 0)),
                pl.BlockSpec((tps * TILE_ROWS, D_MODEL), lambda j: (tile_o(j) // tps, 0)),
                mod_spec(tile_m), mod_spec(tile_m), mod_spec(tile_o)]
    args = [x, x, shift, scale, gate]
    if not pipelined:
        in_specs += [pl.BlockSpec((None, nseq, CONV_W - 1, CONV_DIM), lambda j: (j // nt, 0, 0, 0)),
                     pl.BlockSpec((None, nseq, HEADS, HEAD_DIM, STATE), lambda j: (layer, j, 0, 0, 0))]
        args += [conv0, ssm0]
    consts = [g_pre, g_post, w_a, w_b, w_dt, conv_w, conv_b, dtb, alog, dskip, g_v, beta_v, mixw,
              mixb, w_out, expand, tril, psel]
    in_specs += [_const_spec(a.shape) for a in consts]
    args += consts

    out_shape = [jax.ShapeDtypeStruct((rows_total, D_MODEL), F32),
                 jax.ShapeDtypeStruct((nb, nseq, CONV_W - 1, CONV_DIM), F32),
                 jax.ShapeDtypeStruct((nb * nseq, HEADS, HEAD_DIM, STATE), F32)]
    out_specs = [pl.BlockSpec((tps * TILE_ROWS, D_MODEL), lambda j: (tile_o(j) // tps, 0)),
                 pl.BlockSpec((None, nseq, CONV_W - 1, CONV_DIM), lambda j: (tile_m(j) // nt, 0, 0, 0)),
                 pl.BlockSpec((nseq, HEADS, HEAD_DIM, STATE), lambda j: (tile_v(j) // nt, 0, 0, 0))]
    if emit_v:
        out_shape.append(jax.ShapeDtypeStruct((rows_total, GMLP_WIDTH), F32))
        out_specs.append(pl.BlockSpec((TILE_ROWS, GMLP_WIDTH), lambda j: (tile_v(j), 0)))

    spare = (TAIL_ROWS, LANES)
    stage_shapes = [((TAIL_ROWS + TILE_ROWS, W_CAT_COLS), F32),
                    ((TILE_ROWS, SSD_WIDTH), F32),
                    ((TILE_ROWS, 2 * GROUPS * STATE), BF16),
                    ((TILE_ROWS, SSD_WIDTH + GMLP_WIDTH), BF16)]
    scratch = [pltpu.VMEM((TILE_ROWS, D_MODEL), BF16),
               pltpu.VMEM((TAIL_ROWS, CONV_DIM), F32),
               pltpu.VMEM((TAIL_ROWS + seglen, LANES), F32),
               pltpu.VMEM((STATE, SSD_WIDTH), F32),
               pltpu.VMEM((nck, STATE, SSD_WIDTH), F32),
               pltpu.VMEM((TILE_ROWS, SSD_WIDTH), F32),
               pltpu.VMEM((TILE_ROWS, SSD_WIDTH), F32),
               pltpu.VMEM((TILE_ROWS, SSD_WIDTH), BF16),
               pltpu.VMEM((TILE_ROWS, SSD_WIDTH), BF16),
               pltpu.VMEM((PAIRS, nck * LANES), F32),
               pltpu.VMEM((TILE_ROWS, GMLP_WIDTH), BF16),
               pltpu.VMEM((TILE_ROWS, D_MODEL), F32),
               pltpu.VMEM((nck, STATE, SSD_WIDTH), BF16),
               pltpu.VMEM((TILE_ROWS, SSD_WIDTH), F32),
               pltpu.VMEM((nck * GROUPS, lc_ssd, LANES), F32),
               pltpu.VMEM((TILE_ROWS, SSD_WIDTH), F32)]
    assert len(scratch) == N_SHARED_SCRATCH
    for shape, dtype in stage_shapes:
        scratch += [pltpu.VMEM(shape, dtype), pltpu.VMEM(shape if pipelined else spare, dtype)]
    kern = functools.partial(_layer_kernel, nseq=nseq, seglen=seglen, lc=lc_ssd, nt=nt,
                             pipelined=pipelined, emit_v=emit_v)
    return pl.pallas_call(
        kern,
        grid=(steps,),
        in_specs=in_specs,
        out_specs=out_specs,
        out_shape=out_shape,
        scratch_shapes=scratch,
        compiler_params=pltpu.CompilerParams(
            dimension_semantics=("arbitrary",),
            vmem_limit_bytes=VMEM_LIMIT_BYTES),
        name="hybrid_layer_stream" if pipelined else "hybrid_layer_state",
    )(*args)


def _prep_weights(g_pre, g_post, w_in, conv_w, conv_b, dt_bias, a_log, d_skip, g_v, beta_v,
                  w_s, b_s, w_out):
    w_a, w_b, w_dt = _cast_weights(w_in)
    pad = (0, LANES - HEADS)
    return (g_pre.reshape(1, -1), g_post.reshape(1, -1), w_a, w_b, w_dt, conv_w, conv_b.reshape(1, -1),
            jnp.pad(dt_bias, pad).reshape(1, -1), jnp.pad(a_log, pad).reshape(1, -1),
            jnp.repeat(d_skip, HEAD_DIM).reshape(1, -1), g_v.reshape(1, -1),
            beta_v.reshape(1, -1), w_s, b_s, w_out.astype(BF16))


def kernel(x_prompt, x_sample, state_conv, state_ssm, c_prompt, c_sample, w_ada, b_ada, g_pre,
           g_post, w_in, conv_w, conv_b, dt_bias, a_log, d_skip, g_v, beta_v, w_s, b_s, w_out):
    depth = w_ada.shape[0]
    bp, lp, _ = x_prompt.shape
    bs, ls, _ = x_sample.shape
    seq_per_tile = TILE_ROWS // ls
    yp = x_prompt.reshape(bp * lp, D_MODEL)
    ys = x_sample.reshape(bs * ls, D_MODEL)
    c_all = jnp.concatenate([c_prompt, c_sample], axis=0)
    c_pad = jnp.pad(c_all, ((0, (-c_all.shape[0]) % 8), (0, 0)))
    outs = [[] for _ in range(5)]
    for l in range(depth):
        wl = _prep_weights(g_pre[l], g_post[l], w_in[l], conv_w[l], conv_b[l], dt_bias[l],
                           a_log[l], d_skip[l], g_v[l], beta_v[l], w_s[l], b_s[l], w_out[l])
        mod = _modulation(c_pad, w_ada[l], b_ada[l])
        shift, scale, gate = (mod[:, i * D_MODEL:(i + 1) * D_MODEL] for i in range(3))
        mp = [m[:bp].reshape(bp, 1, D_MODEL) for m in (shift, scale, gate)]
        ms = [m[bp:bp + bs].reshape(bs // seq_per_tile, seq_per_tile, D_MODEL)
              for m in (shift, scale, gate)]
        yp, conv_p, ssm_p = _layer(
            yp, *mp, None, None, wl, nseq=1, seglen=TILE_ROWS, lc_ssd=min(lp, SSD_CHUNK),
            lc_gmlp=min(lp, GMLP_CHUNK), nt=lp // TILE_ROWS, emit_v=False)
        ys, conv_s, ssm_s, v_s = _layer(
            ys, *ms,
            state_conv[l].reshape(bs // seq_per_tile, seq_per_tile, CONV_W - 1, CONV_DIM),
            state_ssm, wl,
            nseq=seq_per_tile, seglen=ls, lc_ssd=min(ls, SSD_CHUNK), lc_gmlp=min(ls, GMLP_CHUNK),
            nt=1, emit_v=True, layer=l)
        outs[0].append(conv_p.reshape(bp, CONV_W - 1, CONV_DIM))
        outs[1].append(ssm_p)
        outs[2].append(conv_s.reshape(bs, CONV_W - 1, CONV_DIM))
        outs[3].append(ssm_s)
        outs[4].append(v_s.reshape(bs, ls, GMLP_WIDTH))
    stacked = [o[0][None] if depth == 1 else jnp.stack(o) for o in outs]
    return (yp.reshape(bp, lp, D_MODEL), ys.reshape(bs, ls, D_MODEL), *stacked)
```

```python
import collections
import functools

import jax
import jax.numpy as jnp
import numpy as np
from jax import lax
from jax.experimental import pallas as pl
from jax.experimental.pallas import tpu as pltpu

F32 = jnp.float32
BF16 = jnp.bfloat16

D_MODEL = 1024
SSD_WIDTH = 1024
HEAD_DIM = 64
HEADS = SSD_WIDTH // HEAD_DIM
GROUPS = 2
GROUP_WIDTH = SSD_WIDTH // GROUPS
STATE = 128
CONV_W = 4
CONV_DIM = SSD_WIDTH + 2 * GROUPS * STATE
GMLP_WIDTH = 1024
GMLP_GROUPS = 8
GMLP_GROUP_DIM = GMLP_WIDTH // GMLP_GROUPS
SSD_CHUNK = 64
GMLP_CHUNK = 128
EPS = 1e-6
LOG2E = 1.4426950408889634

LANES = 128
MXU_COLS = 256
HALF = LANES // 2
PAIRS = HEADS // 2
PAIRS_PER_GROUP = PAIRS // GROUPS
PIECES = 3
TILE_ROWS = 256
TAIL_ROWS = 8
TAP0 = TAIL_ROWS - (CONV_W - 1)
MOD_BLOCK_COLS = 512
VMEM_LIMIT_BYTES = 60 * 1024 * 1024

OFF_Z = 0
OFF_XBC = OFF_Z + SSD_WIDTH
OFF_U = OFF_XBC + CONV_DIM
OFF_V = OFF_U + GMLP_WIDTH
OFF_G = OFF_V + GMLP_WIDTH
OFF_DT = OFF_G + GMLP_WIDTH
W_CAT_COLS = OFF_DT + LANES

Bufs = collections.namedtuple("Bufs", "proj xs bc mix")


def _dot(a, b):
    return jnp.dot(a, b, preferred_element_type=F32)


def _dot_nt(a, b):
    return lax.dot_general(a, b, (((1,), (1,)), ((), ())), preferred_element_type=F32)


def _dot_tn(a, b):
    return lax.dot_general(a, b, (((0,), (0,)), ((), ())), preferred_element_type=F32)


def _cat3(x):
    hi = x.astype(BF16).astype(F32)
    r1 = x - hi
    mid = r1.astype(BF16).astype(F32)
    lo = r1 - mid
    return hi + pltpu.roll(mid, HEADS, axis=1) + pltpu.roll(lo, 2 * HEADS, axis=1)


def _silu(x):
    return x / (1.0 + jnp.exp2(x * (-LOG2E)))


def _softplus(x):
    return jnp.maximum(x, 0.0) + jnp.log1p(jnp.exp(-jnp.abs(x)))


def _row_mean(x):
    width = x.shape[-1]
    part = x[:, 0:LANES]
    for c in range(LANES, width, LANES):
        part = part + x[:, c:c + LANES]
    return jnp.sum(part, axis=-1, keepdims=True) * (1.0 / width)


def _rms_scale(x):
    return lax.rsqrt(_row_mean(x * x) + EPS)


def _mod_kernel(c_ref, w_ref, b_ref, o_ref):
    c = _silu(c_ref[...]).astype(BF16)
    o_ref[...] = _dot(c, w_ref[...].astype(BF16)) + b_ref[...]


def _modulation(c, w_ada, b_ada):
    rows = c.shape[0]
    n = w_ada.shape[1]
    return pl.pallas_call(
        _mod_kernel,
        grid=(n // MOD_BLOCK_COLS,),
        in_specs=[
            pl.BlockSpec((rows, D_MODEL), lambda j: (0, 0)),
            pl.BlockSpec((D_MODEL, MOD_BLOCK_COLS), lambda j: (0, j)),
            pl.BlockSpec((1, MOD_BLOCK_COLS), lambda j: (0, j)),
        ],
        out_specs=pl.BlockSpec((rows, MOD_BLOCK_COLS), lambda j: (0, j)),
        out_shape=jax.ShapeDtypeStruct((rows, n), F32),
        name="adaln_mod",
    )(c, w_ada, b_ada.reshape(1, n))


CAST_BLOCK_COLS = 512
A_COLS = SSD_WIDTH + CONV_DIM
B_COLS = 3 * GMLP_WIDTH


def _cast_kernel(a_ref, b_ref, b_next_ref, wa_ref, wb_ref, wdt_ref):
    j = pl.program_id(0)
    wa_ref[...] = a_ref[...].astype(BF16)
    wide = jnp.concatenate([b_ref[...], b_next_ref[...]], axis=1)
    shifted = pltpu.roll(wide, wide.shape[1] - HEADS, axis=1)
    wb_ref[...] = shifted[:, :CAST_BLOCK_COLS].astype(BF16)

    @pl.when(j == 0)
    def _():
        head = b_ref[:, 0:LANES]
        lane = lax.broadcasted_iota(jnp.int32, head.shape, 1)
        wdt_ref[...] = jnp.where(lane < HEADS, head, 0.0).astype(BF16)


def _cast_weights(w_in):
    a_blocks = A_COLS // CAST_BLOCK_COLS
    b_blocks = B_COLS // CAST_BLOCK_COLS
    lanes_per_block = CAST_BLOCK_COLS // LANES
    return pl.pallas_call(
        _cast_kernel,
        grid=(b_blocks,),
        in_specs=[
            pl.BlockSpec((D_MODEL, CAST_BLOCK_COLS), lambda j: (0, jnp.minimum(j, a_blocks - 1))),
            pl.BlockSpec((D_MODEL, CAST_BLOCK_COLS), lambda j: (0, a_blocks + j)),
            pl.BlockSpec((D_MODEL, LANES), lambda j: (0, (a_blocks + j + 1) * lanes_per_block)),
        ],
        out_specs=[
            pl.BlockSpec((D_MODEL, CAST_BLOCK_COLS), lambda j: (0, jnp.minimum(j, a_blocks - 1))),
            pl.BlockSpec((D_MODEL, CAST_BLOCK_COLS), lambda j: (0, j)),
            pl.BlockSpec((D_MODEL, LANES), lambda j: (0, 0)),
        ],
        out_shape=[jax.ShapeDtypeStruct((D_MODEL, A_COLS), BF16),
                   jax.ShapeDtypeStruct((D_MODEL, B_COLS), BF16),
                   jax.ShapeDtypeStruct((D_MODEL, LANES), BF16)],
        compiler_params=pltpu.CompilerParams(dimension_semantics=("arbitrary",)),
        name="cast_proj_weight",
    )(w_in, w_in, w_in)


TILES_PER_STEP = 2
N_SHARED_SCRATCH = 16
N_STAGE_SCRATCH = 8


def _interleave(primary, secondary):
    total = sum(weight for _, weight in primary)
    groups = []
    done = 0
    seen = 0
    for task, weight in primary:
        seen += weight
        want = (seen * len(secondary)) // total
        groups.append([task] + secondary[done:want])
        done = want
    return groups


def _tag(task, name, needs=()):
    task.name, task.needs = name, tuple(needs)
    return task


def _run_groups(groups):
    stored = set()
    for group in groups:
        for task in group:
            missing = [n for n in getattr(task, "needs", ()) if n not in stored]
            assert not missing, (getattr(task, "name", task), missing)
        commits = [task() for task in group]
        for commit in commits:
            if commit is not None:
                commit()
        stored.update(getattr(task, "name", None) for task in group)


def _layer_kernel(*refs, nseq, seglen, lc, nt, pipelined, emit_v):
    statics = dict(nseq=nseq, seglen=seglen, lc=lc, pipelined=pipelined, emit_v=emit_v)
    shared, stage = refs[:-N_STAGE_SCRATCH], refs[-N_STAGE_SCRATCH:]
    buf_a, buf_b = Bufs(*stage[0::2]), Bufs(*stage[1::2])
    n_scratch = N_SHARED_SCRATCH + N_STAGE_SCRATCH
    ssm_out_ref = refs[len(refs) - n_scratch - (2 if emit_v else 1)]
    tail_ref, st_ref = refs[-n_scratch + 1], refs[-n_scratch + 3]
    if not pipelined:
        _tile_body(shared, buf_a, buf_a, buf_a, buf_a, **statics)
        return

    assert nt % TILES_PER_STEP == 0 and TILES_PER_STEP == 2
    i = pl.program_id(0)
    row_start = lax.rem(TILES_PER_STEP * i, nt) == 0
    n_in = len(shared) - N_SHARED_SCRATCH - 3
    views = []
    for half in range(TILES_PER_STEP):
        rows = slice(half * TILE_ROWS, (half + 1) * TILE_ROWS)
        view = list(shared)
        for k in (0, 1, n_in):
            view[k] = shared[k].at[rows, :]
        views.append(view)

    @pl.when(i == 0)
    def _():
        for ref in buf_b:
            ref[...] = jnp.zeros_like(ref)
        st_ref[...] = jnp.zeros_like(st_ref)

    @pl.when(row_start)
    def _():
        tail_ref[...] = jnp.zeros_like(tail_ref)

    _tile_body(views[0], buf_a, buf_b, buf_a, buf_b, **statics)

    @pl.when(row_start & (i > 0))
    def _():
        ssm_out_ref[0] = st_ref[...].T.reshape(HEADS, HEAD_DIM, STATE)

    @pl.when(row_start)
    def _():
        st_ref[...] = jnp.zeros_like(st_ref)

    _tile_body(views[1], buf_b, buf_a, buf_b, buf_a, **statics)


def _tile_body(refs, bm, bv, bvo, bo, *, nseq, seglen, lc, pipelined, emit_v):
    has_state = not pipelined
    it = iter(refs)
    x_ref, xo_ref = next(it), next(it)
    shift_ref, scale_ref, gate_ref = next(it), next(it), next(it)
    conv0_ref = next(it) if has_state else None
    ssm0_ref = next(it) if has_state else None
    (gpre_ref, gpost_ref, wa_ref, wb_ref, wdt_ref, convw_ref, convb_ref, dtb_ref, alog_ref,
     dskip_ref, gv_ref, betav_ref, mixw_ref, mixb_ref, wout_ref, tril_ref, psel_ref) = (
         next(it) for _ in range(17))
    y_ref, conv_out_ref, ssm_out_ref = next(it), next(it), next(it)
    vn_out_ref = next(it) if emit_v else None
    (h_ref, tail_ref, xp_ref, st_ref, stnew_ref, aexp_ref, xdt_ref, xlo_ref, xhi_ref, acumt_ref,
     vnb_ref, mo_ref, stb_ref, ys_ref, cb_ref, yoff_ref) = (next(it) for _ in range(N_SHARED_SCRATCH))
    pm, pv = bm.proj, bv.proj

    rows_all = slice(TAIL_ROWS, TAIL_ROWS + TILE_ROWS)
    nchunk = seglen // lc
    pad_rows = HALF - lc

    def pad_to_half(v):
        if pad_rows == 0:
            return [v]
        return [v, jnp.zeros((pad_rows, v.shape[1]), v.dtype)]

    def m_norm():
        for s in range(nseq):
            rows = slice(s * seglen, (s + 1) * seglen)
            xr = x_ref[rows, :]
            gain = gpre_ref[...] * (1.0 + scale_ref[s:s + 1, :])
            hs = (xr * _rms_scale(xr)) * gain + shift_ref[s:s + 1, :]
            h_ref[rows, :] = hs.astype(BF16)

    def m_dot(w_ref, w0, dst0, width):
        def task():
            val = _dot(h_ref[...], w_ref[:, w0:w0 + width])

            def commit():
                pm[rows_all, dst0:dst0 + width] = val
            return commit
        return _tag(task, f"m_dot{dst0 // MXU_COLS}")

    dst_starts = (list(range(OFF_XBC, OFF_U, MXU_COLS)) + list(range(OFF_Z, OFF_XBC, MXU_COLS))
                  + list(range(OFF_U, OFF_DT, MXU_COLS)))
    m_dots = [m_dot(wa_ref, d, d, MXU_COLS) if d < OFF_U else m_dot(wb_ref, d - OFF_U, d, MXU_COLS)
              for d in dst_starts] + [m_dot(wdt_ref, 0, OFF_DT, LANES)]

    def conv_store(c, val, rows):
        if c < SSD_WIDTH // LANES:
            bm.xs[rows, c * LANES:(c + 1) * LANES] = val
        else:
            c2 = c - SSD_WIDTH // LANES
            bm.bc[rows, c2 * LANES:(c2 + 1) * LANES] = val.astype(BF16)

    def m_conv(c):
        cols = slice(OFF_XBC + c * LANES, OFF_XBC + (c + 1) * LANES)
        wcols = slice(c * LANES, (c + 1) * LANES)

        def task_stream():
            pm[0:TAIL_ROWS, cols] = tail_ref[:, wcols]
            xfull = pm[:, cols]
            acc = convw_ref[0:1, wcols] * xfull
            for k in range(1, CONV_W):
                acc = pltpu.roll(acc, 1, axis=0) + convw_ref[k:k + 1, wcols] * xfull
            val = _silu(acc[TAIL_ROWS:, :] + convb_ref[:, wcols])
            tail_ref[:, wcols] = pm[TILE_ROWS:TILE_ROWS + TAIL_ROWS, cols]
            conv_out_ref[0, :, wcols] = pm[TAIL_ROWS + TILE_ROWS - (CONV_W - 1):TAIL_ROWS + TILE_ROWS, cols]

            def commit():
                conv_store(c, val, slice(0, TILE_ROWS))
            return commit

        def task_state():
            for s in range(nseq):
                rows = slice(s * seglen, (s + 1) * seglen)
                xp_ref[TAP0:TAIL_ROWS, :] = conv0_ref[s, :, wcols]
                xp_ref[TAIL_ROWS:TAIL_ROWS + seglen, :] = pm[TAIL_ROWS + s * seglen:TAIL_ROWS + (s + 1) * seglen, cols]
                acc = convb_ref[:, wcols]
                for k in range(CONV_W):
                    acc = acc + convw_ref[k:k + 1, wcols] * xp_ref[TAP0 + k:TAP0 + k + seglen, :]
                conv_store(c, _silu(acc), rows)
                conv_out_ref[s, :, wcols] = xp_ref[TAIL_ROWS + seglen - (CONV_W - 1):TAIL_ROWS + seglen, :]

        return task_stream if pipelined else task_state

    m_convs = [_tag(m_conv(c), f"m_conv{c}", [f"m_dot{(OFF_XBC + c * LANES) // MXU_COLS}"])
               for c in range(CONV_DIM // LANES)]

    lane1 = lax.broadcasted_iota(jnp.int32, (TILE_ROWS, LANES), 1)

    def v_dt():
        a_row = -jnp.exp(alog_ref[...])
        dt = _softplus(pv[rows_all, OFF_DT:OFF_DT + LANES] + dtb_ref[...])
        dt = jnp.where(lane1 < HEADS, dt, 0.0)
        csum = _dot(tril_ref[...], _cat3(dt * a_row).astype(BF16))
        a_cum = csum + pltpu.roll(csum, LANES - HEADS, axis=1) + pltpu.roll(csum, LANES - 2 * HEADS, axis=1)
        a_cat = _cat3(jnp.where(lane1 < HEADS, a_cum, 0.0))

        def per_head_lanes(v):
            blocks = []
            for p in range(PAIRS):
                lo = jnp.broadcast_to(v[:, 2 * p:2 * p + 1], (TILE_ROWS, LANES))
                hi = jnp.broadcast_to(v[:, 2 * p + 1:2 * p + 2], (TILE_ROWS, LANES))
                blocks.append(jnp.where(lane1 < HALF, lo, hi))
            return jnp.concatenate(blocks, axis=1)

        aexp_ref[...] = per_head_lanes(a_cum)
        dt_exp = per_head_lanes(dt)
        even = lax.rem(lane1, 2) == 0
        a_even = jnp.where(even, a_cat, 0.0)
        a_odd = (a_cat - a_even).astype(BF16)
        a_even = a_even.astype(BF16)
        parts = []
        for ck in range(TILE_ROWS // lc):
            rows = slice(ck * lc, (ck + 1) * lc)
            parts += pad_to_half(a_even[rows, :]) + pad_to_half(a_odd[rows, :])
        acumt_ref[...] = _dot_nt(psel_ref[...], jnp.concatenate(parts, axis=0))
        xdt = bv.xs[...] * dt_exp
        lane_w = lax.broadcasted_iota(jnp.int32, (TILE_ROWS, SSD_WIDTH), 1)
        xlo = jnp.where(lax.rem(lane_w, LANES) < HALF, xdt, 0.0)
        xdt_ref[...] = xdt
        xlo_ref[...] = xlo.astype(BF16)
        xhi_ref[...] = (xdt - xlo).astype(BF16)

    li = lax.broadcasted_iota(jnp.int32, (lc, LANES), 0)
    si = lax.rem(lax.broadcasted_iota(jnp.int32, (lc, LANES), 1), HALF)
    causal = (si <= li) & (si < lc)

    def v_local(ci, g):
        rows = slice(ci * lc, (ci + 1) * lc)
        hl = slice(g * GROUP_WIDTH, (g + 1) * GROUP_WIDTH)

        def products():
            last = aexp_ref[(ci + 1) * lc - 1:(ci + 1) * lc, hl]
            xw_b = (xdt_ref[rows, hl] * jnp.exp(last - aexp_ref[rows, hl])).astype(BF16)
            b_g = bv.bc[rows, g * STATE:(g + 1) * STATE]
            c_g = bv.bc[rows, (GROUPS + g) * STATE:(GROUPS + g + 1) * STATE]
            b2 = jnp.concatenate(pad_to_half(b_g) + pad_to_half(b_g), axis=0)
            cb_ref[ci * GROUPS + g] = _dot_nt(c_g, b2)
            stnew_ref[ci, :, hl] = _dot_tn(b_g, xw_b)

        def diag():
            cb2 = cb_ref[ci * GROUPS + g]
            for i in range(PAIRS_PER_GROUP):
                p = g * PAIRS_PER_GROUP + i
                pcols = slice(p * LANES, (p + 1) * LANES)
                seg = aexp_ref[rows, pcols] - acumt_ref[p:p + 1, ci * LANES:(ci + 1) * LANES]
                w_p = (cb2 * jnp.exp(jnp.where(causal, seg, -jnp.inf))).astype(BF16)
                bd = jnp.concatenate(pad_to_half(xlo_ref[rows, pcols]) + pad_to_half(xhi_ref[rows, pcols]),
                                     axis=0)
                ys_ref[rows, pcols] = _dot(w_p, bd) + dskip_ref[:, pcols] * bv.xs[rows, pcols]
        return products, diag

    def v_state(ci, s):
        def task():
            e_last = jnp.exp(aexp_ref[(ci + 1) * lc - 1:(ci + 1) * lc, :])
            st = st_ref[...]
            stb_ref[ci] = st.astype(BF16)
            st_ref[...] = st * e_last + stnew_ref[ci]
            if has_state and ci % nchunk == nchunk - 1:
                ssm_out_ref[s] = st_ref[...].T.reshape(HEADS, HEAD_DIM, STATE)
        return task

    def v_readout(ci, g):
        rows = slice(ci * lc, (ci + 1) * lc)
        prow = slice(TAIL_ROWS + ci * lc, TAIL_ROWS + (ci + 1) * lc)
        hl = slice(g * GROUP_WIDTH, (g + 1) * GROUP_WIDTH)

        def product():
            c_g = bv.bc[rows, (GROUPS + g) * STATE:(GROUPS + g + 1) * STATE]
            yoff_ref[rows, hl] = _dot(c_g, stb_ref[ci, :, hl])

        def gate():
            y = ys_ref[rows, hl] + jnp.exp(aexp_ref[rows, hl]) * yoff_ref[rows, hl]
            zg = _silu(pv[prow, OFF_Z + g * GROUP_WIDTH:OFF_Z + (g + 1) * GROUP_WIDTH])
            bvo.mix[rows, hl] = (y * zg).astype(BF16)
        return product, gate

    def v_state_load(s):
        def task():
            st_ref[...] = ssm0_ref[s].reshape(SSD_WIDTH, STATE).T
        return task

    def v_ln():
        v = pv[rows_all, OFF_V:OFF_V + GMLP_WIDTH]
        vc = v - _row_mean(v)
        vn = (vc * _rms_scale(vc)) * gv_ref[...] + betav_ref[...]
        if emit_v:
            vn_out_ref[...] = vn
        vnb_ref[...] = vn.astype(BF16)

    def v_gmlp(g):
        gcols = slice(g * GMLP_GROUP_DIM, (g + 1) * GMLP_GROUP_DIM)

        def task():
            if mixw_ref.shape[1] == TILE_ROWS:
                mixed = _dot(mixw_ref[g], vnb_ref[:, gcols])
            else:
                lcg = mixw_ref.shape[1]
                blocks = [vnb_ref[r:r + lcg, gcols] for r in range(0, TILE_ROWS, lcg)]
                wide = _dot(mixw_ref[g], jnp.concatenate(blocks, axis=1))
                mixed = jnp.concatenate(
                    [wide[:, k * GMLP_GROUP_DIM:(k + 1) * GMLP_GROUP_DIM] for k in range(len(blocks))],
                    axis=0)
            mixed = mixed + mixb_ref[:, gcols]
            u = pv[rows_all, OFF_U + g * GMLP_GROUP_DIM:OFF_U + (g + 1) * GMLP_GROUP_DIM]
            gt = pv[rows_all, OFF_G + g * GMLP_GROUP_DIM:OFF_G + (g + 1) * GMLP_GROUP_DIM]
            bvo.mix[:, SSD_WIDTH + g * GMLP_GROUP_DIM:SSD_WIDTH + (g + 1) * GMLP_GROUP_DIM] = (
                _silu(gt) * u * mixed).astype(BF16)
        return task

    chunk_groups = [(ci, g) for ci in range(nseq * nchunk) for g in range(GROUPS)]
    v_locals = [v_local(ci, g) for ci, g in chunk_groups]
    v_states = []
    for s in range(nseq):
        if has_state:
            v_states.append(v_state_load(s))
        v_states += [v_state(s * nchunk + c, s) for c in range(nchunk)]
    v_readouts = [v_readout(ci, g) for ci, g in chunk_groups]
    v_scan = ([(t[0], 0) for t in v_locals] + [(t[1], 1) for t in v_locals]
              + [(t, 1) for t in v_states]
              + [(t[0], 0) for t in v_readouts] + [(t[1], 1) for t in v_readouts])
    v_gmlps = [v_gmlp(g) for g in range(GMLP_GROUPS)]

    def o_out(n):
        def task():
            mo_ref[:, n * MXU_COLS:(n + 1) * MXU_COLS] = _dot(
                bo.mix[...], wout_ref[:, n * MXU_COLS:(n + 1) * MXU_COLS])
        return task

    def o_final():
        mo = mo_ref[...]
        mo = mo * _rms_scale(mo)
        for s in range(nseq):
            rows = slice(s * seglen, (s + 1) * seglen)
            gain = gpost_ref[...] * gate_ref[s:s + 1, :]
            y_ref[rows, :] = xo_ref[rows, :] + gain * mo[rows, :]

    o_outs = [_tag(o_out(n), f"o_out{n}") for n in range(D_MODEL // MXU_COLS)]
    _tag(o_final, "o_final", [t.name for t in o_outs])

    if pipelined:
        n_xbc_dots = CONV_DIM // MXU_COLS
        head = _interleave([(m_norm, 1), (v_dt, 2), (v_ln, 2)], o_outs + m_dots[:n_xbc_dots])
        body = [(t, 3) for t in m_convs] + [(o_final, 3)] + v_scan + [(t, 0) for t in v_gmlps]
        groups = head + _interleave(body, m_dots[n_xbc_dots:])
    else:
        order = ([m_norm] + m_dots + m_convs + [v_dt, v_ln] + [t for t, _ in v_scan] + v_gmlps
                 + o_outs + [o_final])
        groups = [[task] for task in order]
    _run_groups(groups)


def _const_spec(shape):
    nd = len(shape)
    return pl.BlockSpec(shape, lambda j: (0,) * nd, pipeline_mode=pl.Buffered(1))


def _chunk_constants(lc):
    psel = np.zeros((PAIRS, LANES), np.float32)
    for piece in range(PIECES):
        for hd in range(HEADS):
            psel[hd // 2, piece * HEADS + hd] = 1.0
    r = np.arange(TILE_ROWS)
    tril = ((r[:, None] >= r[None, :]) & (r[:, None] // lc == r[None, :] // lc)).astype(np.float32)
    return jnp.asarray(tril, BF16), jnp.asarray(psel, BF16)


def _mix_weights(w_s, b_s, lc):
    nck = TILE_ROWS // lc
    mask = jnp.tril(jnp.ones((lc, lc), dtype=bool))
    w = jnp.where(mask[None], w_s[:, :lc, :lc], 0)
    if nck * GMLP_GROUP_DIM > MXU_COLS:
        eye = jnp.eye(nck, dtype=w.dtype)
        w = jnp.einsum("ab,gts->gatbs", eye, w).reshape(GMLP_GROUPS, TILE_ROWS, TILE_ROWS)
    bias = jnp.repeat(b_s[:, :lc].T, GMLP_GROUP_DIM, axis=1)
    return w.astype(BF16), jnp.tile(bias, (nck, 1))


def _layer(x, shift, scale, gate, conv0, ssm0, wl, *, nseq, seglen, lc_ssd, lc_gmlp, nt, emit_v,
           layer=0):
    (g_pre, g_post, w_a, w_b, w_dt, conv_w, conv_b, dtb, alog, dskip, g_v, beta_v, w_s, b_s,
     w_out) = wl
    assert lc_ssd == lc_gmlp or nseq == 1
    pipelined = conv0 is None
    nb = shift.shape[0]
    rows_total = x.shape[0]
    ntiles = rows_total // TILE_ROWS
    assert nseq * seglen == TILE_ROWS and nb * nt == ntiles
    tril, psel = _chunk_constants(lc_ssd)
    mixw, mixb = _mix_weights(w_s, b_s, lc_gmlp)
    nck = TILE_ROWS // lc_ssd

    if pipelined:
        tps = TILES_PER_STEP
        assert ntiles % tps == 0

        def tile_m(j):
            return jnp.minimum(tps * j, ntiles - tps)

        def tile_v(j):
            return jnp.clip(tps * j - 1, 0, ntiles - 1)

        def tile_o(j):
            return jnp.maximum(tps * j - 2, 0)
        steps = ntiles // tps + 1
    else:
        tps = 1

        def tile_m(j):
            return j
        tile_v = tile_o = tile_m
        steps = ntiles

    def mod_spec(tile_of):
        return pl.BlockSpec((None, nseq, D_MODEL), lambda j: (tile_of(j) // nt, 0, 0))

    in_specs = [pl.BlockSpec((tps * TILE_ROWS, D_MODEL), lambda j: (tile_m(j) // tps, 0)),
                pl.BlockSpec((tps * TILE_ROWS, D_MODEL), lambda j: (tile_o(j) // tps, 0)),
                mod_spec(tile_m), mod_spec(tile_m), mod_spec(tile_o)]
    args = [x, x, shift, scale, gate]
    if not pipelined:
        in_specs += [pl.BlockSpec((None, nseq, CONV_W - 1, CONV_DIM), lambda j: (j // nt, 0, 0, 0)),
                     pl.BlockSpec((None, nseq, HEADS, HEAD_DIM, STATE), lambda j: (layer, j, 0, 0, 0))]
        args += [conv0, ssm0]
    consts = [g_pre, g_post, w_a, w_b, w_dt, conv_w, conv_b, dtb, alog, dskip, g_v, beta_v, mixw,
              mixb, w_out, tril, psel]
    in_specs += [_const_spec(a.shape) for a in consts]
    args += consts

    out_shape = [jax.ShapeDtypeStruct((rows_total, D_MODEL), F32),
                 jax.ShapeDtypeStruct((nb, nseq, CONV_W - 1, CONV_DIM), F32),
                 jax.ShapeDtypeStruct((nb * nseq, HEADS, HEAD_DIM, STATE), F32)]
    out_specs = [pl.BlockSpec((tps * TILE_ROWS, D_MODEL), lambda j: (tile_o(j) // tps, 0)),
                 pl.BlockSpec((None, nseq, CONV_W - 1, CONV_DIM), lambda j: (tile_m(j) // nt, 0, 0, 0)),
                 pl.BlockSpec((nseq, HEADS, HEAD_DIM, STATE), lambda j: (tile_v(j) // nt, 0, 0, 0))]
    if emit_v:
        out_shape.append(jax.ShapeDtypeStruct((rows_total, GMLP_WIDTH), F32))
        out_specs.append(pl.BlockSpec((TILE_ROWS, GMLP_WIDTH), lambda j: (tile_v(j), 0)))

    spare = (TAIL_ROWS, LANES)
    stage_shapes = [((TAIL_ROWS + TILE_ROWS, W_CAT_COLS), F32),
                    ((TILE_ROWS, SSD_WIDTH), F32),
                    ((TILE_ROWS, 2 * GROUPS * STATE), BF16),
                    ((TILE_ROWS, SSD_WIDTH + GMLP_WIDTH), BF16)]
    scratch = [pltpu.VMEM((TILE_ROWS, D_MODEL), BF16),
               pltpu.VMEM((TAIL_ROWS, CONV_DIM), F32),
               pltpu.VMEM((TAIL_ROWS + seglen, LANES), F32),
               pltpu.VMEM((STATE, SSD_WIDTH), F32),
               pltpu.VMEM((nck, STATE, SSD_WIDTH), F32),
               pltpu.VMEM((TILE_ROWS, SSD_WIDTH), F32),
               pltpu.VMEM((TILE_ROWS, SSD_WIDTH), F32),
               pltpu.VMEM((TILE_ROWS, SSD_WIDTH), BF16),
               pltpu.VMEM((TILE_ROWS, SSD_WIDTH), BF16),
               pltpu.VMEM((PAIRS, nck * LANES), F32),
               pltpu.VMEM((TILE_ROWS, GMLP_WIDTH), BF16),
               pltpu.VMEM((TILE_ROWS, D_MODEL), F32),
               pltpu.VMEM((nck, STATE, SSD_WIDTH), BF16),
               pltpu.VMEM((TILE_ROWS, SSD_WIDTH), F32),
               pltpu.VMEM((nck * GROUPS, lc_ssd, LANES), F32),
               pltpu.VMEM((TILE_ROWS, SSD_WIDTH), F32)]
    assert len(scratch) == N_SHARED_SCRATCH
    for shape, dtype in stage_shapes:
        scratch += [pltpu.VMEM(shape, dtype), pltpu.VMEM(shape if pipelined else spare, dtype)]
    kern = functools.partial(_layer_kernel, nseq=nseq, seglen=seglen, lc=lc_ssd, nt=nt,
                             pipelined=pipelined, emit_v=emit_v)
    return pl.pallas_call(
        kern,
        grid=(steps,),
        in_specs=in_specs,
        out_specs=out_specs,
        out_shape=out_shape,
        scratch_shapes=scratch,
        compiler_params=pltpu.CompilerParams(
            dimension_semantics=("arbitrary",),
            vmem_limit_bytes=VMEM_LIMIT_BYTES),
        name="hybrid_layer_stream" if pipelined else "hybrid_layer_state",
    )(*args)


def _prep_weights(g_pre, g_post, w_in, conv_w, conv_b, dt_bias, a_log, d_skip, g_v, beta_v,
                  w_s, b_s, w_out):
    w_a, w_b, w_dt = _cast_weights(w_in)
    pad = (0, LANES - HEADS)
    return (g_pre.reshape(1, -1), g_post.reshape(1, -1), w_a, w_b, w_dt, conv_w, conv_b.reshape(1, -1),
            jnp.pad(dt_bias, pad).reshape(1, -1), jnp.pad(a_log, pad).reshape(1, -1),
            jnp.repeat(d_skip, HEAD_DIM).reshape(1, -1), g_v.reshape(1, -1),
            beta_v.reshape(1, -1), w_s, b_s, w_out.astype(BF16))


def kernel(x_prompt, x_sample, state_conv, state_ssm, c_prompt, c_sample, w_ada, b_ada, g_pre,
           g_post, w_in, conv_w, conv_b, dt_bias, a_log, d_skip, g_v, beta_v, w_s, b_s, w_out):
    depth = w_ada.shape[0]
    bp, lp, _ = x_prompt.shape
    bs, ls, _ = x_sample.shape
    seq_per_tile = TILE_ROWS // ls
    yp = x_prompt.reshape(bp * lp, D_MODEL)
    ys = x_sample.reshape(bs * ls, D_MODEL)
    c_all = jnp.concatenate([c_prompt, c_sample], axis=0)
    c_pad = jnp.pad(c_all, ((0, (-c_all.shape[0]) % 8), (0, 0)))
    outs = [[] for _ in range(5)]
    for l in range(depth):
        wl = _prep_weights(g_pre[l], g_post[l], w_in[l], conv_w[l], conv_b[l], dt_bias[l],
                           a_log[l], d_skip[l], g_v[l], beta_v[l], w_s[l], b_s[l], w_out[l])
        mod = _modulation(c_pad, w_ada[l], b_ada[l])
        shift, scale, gate = (mod[:, i * D_MODEL:(i + 1) * D_MODEL] for i in range(3))
        mp = [m[:bp].reshape(bp, 1, D_MODEL) for m in (shift, scale, gate)]
        ms = [m[bp:bp + bs].reshape(bs // seq_per_tile, seq_per_tile, D_MODEL)
              for m in (shift, scale, gate)]
        yp, conv_p, ssm_p = _layer(
            yp, *mp, None, None, wl, nseq=1, seglen=TILE_ROWS, lc_ssd=min(lp, SSD_CHUNK),
            lc_gmlp=min(lp, GMLP_CHUNK), nt=lp // TILE_ROWS, emit_v=False)
        ys, conv_s, ssm_s, v_s = _layer(
            ys, *ms,
            state_conv[l].reshape(bs // seq_per_tile, seq_per_tile, CONV_W - 1, CONV_DIM),
            state_ssm, wl,
            nseq=seq_per_tile, seglen=ls, lc_ssd=min(ls, SSD_CHUNK), lc_gmlp=min(ls, GMLP_CHUNK),
            nt=1, emit_v=True, layer=l)
        outs[0].append(conv_p.reshape(bp, CONV_W - 1, CONV_DIM))
        outs[1].append(ssm_p)
        outs[2].append(conv_s.reshape(bs, CONV_W - 1, CONV_DIM))
        outs[3].append(ssm_s)
        outs[4].append(v_s.reshape(bs, ls, GMLP_WIDTH))
    stacked = [o[0][None] if depth == 1 else jnp.stack(o) for o in outs]
    return (yp.reshape(bp, lp, D_MODEL), ys.reshape(bs, ls, D_MODEL), *stacked)
```

```python
import collections
import functools

import jax
import jax.numpy as jnp
import numpy as np
from jax import lax
from jax.experimental import pallas as pl
from jax.experimental.pallas import tpu as pltpu

F32 = jnp.float32
BF16 = jnp.bfloat16

D_MODEL = 1024
SSD_WIDTH = 1024
HEAD_DIM = 64
HEADS = SSD_WIDTH // HEAD_DIM
GROUPS = 2
GROUP_WIDTH = SSD_WIDTH // GROUPS
STATE = 128
CONV_W = 4
CONV_DIM = SSD_WIDTH + 2 * GROUPS * STATE
GMLP_WIDTH = 1024
GMLP_GROUPS = 8
GMLP_GROUP_DIM = GMLP_WIDTH // GMLP_GROUPS
SSD_CHUNK = 64
GMLP_CHUNK = 128
EPS = 1e-6
LOG2E = 1.4426950408889634

LANES = 128
MXU_COLS = 256
HALF = LANES // 2
PAIRS = HEADS // 2
PAIRS_PER_GROUP = PAIRS // GROUPS
PIECES = 3
TILE_ROWS = 256
TAIL_ROWS = 8
TAP0 = TAIL_ROWS - (CONV_W - 1)
MOD_BLOCK_COLS = 512
VMEM_LIMIT_BYTES = 60 * 1024 * 1024

OFF_Z = 0
OFF_XBC = OFF_Z + SSD_WIDTH
OFF_U = OFF_XBC + CONV_DIM
OFF_V = OFF_U + GMLP_WIDTH
OFF_G = OFF_V + GMLP_WIDTH
OFF_DT = OFF_G + GMLP_WIDTH
W_CAT_COLS = OFF_DT + LANES

Bufs = collections.namedtuple("Bufs", "proj xs bc mix")


def _dot(a, b):
    return jnp.dot(a, b, preferred_element_type=F32)


def _dot_nt(a, b):
    return lax.dot_general(a, b, (((1,), (1,)), ((), ())), preferred_element_type=F32)


def _dot_tn(a, b):
    return lax.dot_general(a, b, (((0,), (0,)), ((), ())), preferred_element_type=F32)


def _cat3(x):
    hi = x.astype(BF16).astype(F32)
    r1 = x - hi
    mid = r1.astype(BF16).astype(F32)
    lo = r1 - mid
    return hi + pltpu.roll(mid, HEADS, axis=1) + pltpu.roll(lo, 2 * HEADS, axis=1)


def _silu(x):
    return x / (1.0 + jnp.exp2(x * (-LOG2E)))


def _softplus(x):
    return jnp.maximum(x, 0.0) + jnp.log1p(jnp.exp(-jnp.abs(x)))


def _row_mean(x):
    width = x.shape[-1]
    part = x[:, 0:LANES]
    for c in range(LANES, width, LANES):
        part = part + x[:, c:c + LANES]
    return jnp.sum(part, axis=-1, keepdims=True) * (1.0 / width)


def _rms_scale(x):
    return lax.rsqrt(_row_mean(x * x) + EPS)


def _mod_kernel(c_ref, w_ref, b_ref, o_ref):
    c = _silu(c_ref[...]).astype(BF16)
    o_ref[...] = _dot(c, w_ref[...].astype(BF16)) + b_ref[...]


def _modulation(c, w_ada, b_ada):
    rows = c.shape[0]
    n = w_ada.shape[1]
    return pl.pallas_call(
        _mod_kernel,
        grid=(n // MOD_BLOCK_COLS,),
        in_specs=[
            pl.BlockSpec((rows, D_MODEL), lambda j: (0, 0)),
            pl.BlockSpec((D_MODEL, MOD_BLOCK_COLS), lambda j: (0, j)),
            pl.BlockSpec((1, MOD_BLOCK_COLS), lambda j: (0, j)),
        ],
        out_specs=pl.BlockSpec((rows, MOD_BLOCK_COLS), lambda j: (0, j)),
        out_shape=jax.ShapeDtypeStruct((rows, n), F32),
        name="adaln_mod",
    )(c, w_ada, b_ada.reshape(1, n))


CAST_BLOCK_COLS = 512
A_COLS = SSD_WIDTH + CONV_DIM
B_COLS = 3 * GMLP_WIDTH


def _cast_kernel(a_ref, b_ref, b_next_ref, wa_ref, wb_ref, wdt_ref):
    j = pl.program_id(0)
    wa_ref[...] = a_ref[...].astype(BF16)
    wide = jnp.concatenate([b_ref[...], b_next_ref[...]], axis=1)
    shifted = pltpu.roll(wide, wide.shape[1] - HEADS, axis=1)
    wb_ref[...] = shifted[:, :CAST_BLOCK_COLS].astype(BF16)

    @pl.when(j == 0)
    def _():
        head = b_ref[:, 0:LANES]
        lane = lax.broadcasted_iota(jnp.int32, head.shape, 1)
        wdt_ref[...] = jnp.where(lane < HEADS, head, 0.0).astype(BF16)


def _cast_weights(w_in):
    a_blocks = A_COLS // CAST_BLOCK_COLS
    b_blocks = B_COLS // CAST_BLOCK_COLS
    lanes_per_block = CAST_BLOCK_COLS // LANES
    return pl.pallas_call(
        _cast_kernel,
        grid=(b_blocks,),
        in_specs=[
            pl.BlockSpec((D_MODEL, CAST_BLOCK_COLS), lambda j: (0, jnp.minimum(j, a_blocks - 1))),
            pl.BlockSpec((D_MODEL, CAST_BLOCK_COLS), lambda j: (0, a_blocks + j)),
            pl.BlockSpec((D_MODEL, LANES), lambda j: (0, (a_blocks + j + 1) * lanes_per_block)),
        ],
        out_specs=[
            pl.BlockSpec((D_MODEL, CAST_BLOCK_COLS), lambda j: (0, jnp.minimum(j, a_blocks - 1))),
            pl.BlockSpec((D_MODEL, CAST_BLOCK_COLS), lambda j: (0, j)),
            pl.BlockSpec((D_MODEL, LANES), lambda j: (0, 0)),
        ],
        out_shape=[jax.ShapeDtypeStruct((D_MODEL, A_COLS), BF16),
                   jax.ShapeDtypeStruct((D_MODEL, B_COLS), BF16),
                   jax.ShapeDtypeStruct((D_MODEL, LANES), BF16)],
        compiler_params=pltpu.CompilerParams(dimension_semantics=("arbitrary",)),
        name="cast_proj_weight",
    )(w_in, w_in, w_in)


TILES_PER_STEP = 2
N_SHARED_SCRATCH = 16
N_STAGE_SCRATCH = 8


def _interleave(primary, secondary):
    total = sum(weight for _, weight in primary)
    groups = []
    done = 0
    seen = 0
    for task, weight in primary:
        seen += weight
        want = (seen * len(secondary)) // total
        groups.append([task] + secondary[done:want])
        done = want
    return groups


def _tag(task, name, needs=()):
    task.name, task.needs = name, tuple(needs)
    return task


def _run_groups(groups):
    done = set()
    for group in groups:
        for task in group:
            missing = [n for n in getattr(task, "needs", ()) if n not in done]
            assert not missing, (getattr(task, "name", task), missing)
            task()
            done.add(getattr(task, "name", None))


def _layer_kernel(*refs, nseq, seglen, lc, nt, pipelined, emit_v):
    statics = dict(nseq=nseq, seglen=seglen, lc=lc, pipelined=pipelined, emit_v=emit_v)
    shared, stage = refs[:-N_STAGE_SCRATCH], refs[-N_STAGE_SCRATCH:]
    buf_a, buf_b = Bufs(*stage[0::2]), Bufs(*stage[1::2])
    n_scratch = N_SHARED_SCRATCH + N_STAGE_SCRATCH
    ssm_out_ref = refs[len(refs) - n_scratch - (2 if emit_v else 1)]
    tail_ref, st_ref = refs[-n_scratch + 1], refs[-n_scratch + 3]
    if not pipelined:
        _tile_body(shared, buf_a, buf_a, buf_a, buf_a, **statics)
        return

    assert nt % TILES_PER_STEP == 0 and TILES_PER_STEP == 2
    i = pl.program_id(0)
    row_start = lax.rem(TILES_PER_STEP * i, nt) == 0
    n_in = len(shared) - N_SHARED_SCRATCH - 3
    views = []
    for half in range(TILES_PER_STEP):
        rows = slice(half * TILE_ROWS, (half + 1) * TILE_ROWS)
        view = list(shared)
        for k in (0, 1, n_in):
            view[k] = shared[k].at[rows, :]
        views.append(view)

    @pl.when(i == 0)
    def _():
        for ref in buf_b:
            ref[...] = jnp.zeros_like(ref)
        st_ref[...] = jnp.zeros_like(st_ref)

    @pl.when(row_start)
    def _():
        tail_ref[...] = jnp.zeros_like(tail_ref)

    _tile_body(views[0], buf_a, buf_b, buf_a, buf_b, **statics)

    @pl.when(row_start & (i > 0))
    def _():
        ssm_out_ref[0] = st_ref[...].T.reshape(HEADS, HEAD_DIM, STATE)

    @pl.when(row_start)
    def _():
        st_ref[...] = jnp.zeros_like(st_ref)

    _tile_body(views[1], buf_b, buf_a, buf_b, buf_a, **statics)


def _tile_body(refs, bm, bv, bvo, bo, *, nseq, seglen, lc, pipelined, emit_v):
    has_state = not pipelined
    it = iter(refs)
    x_ref, xo_ref = next(it), next(it)
    shift_ref, scale_ref, gate_ref = next(it), next(it), next(it)
    conv0_ref = next(it) if has_state else None
    ssm0_ref = next(it) if has_state else None
    (gpre_ref, gpost_ref, wa_ref, wb_ref, wdt_ref, convw_ref, convb_ref, dtb_ref, alog_ref,
     dskip_ref, gv_ref, betav_ref, mixw_ref, mixb_ref, wout_ref, tril_ref, psel_ref) = (
         next(it) for _ in range(17))
    y_ref, conv_out_ref, ssm_out_ref = next(it), next(it), next(it)
    vn_out_ref = next(it) if emit_v else None
    (h_ref, tail_ref, xp_ref, st_ref, stnew_ref, aexp_ref, xdt_ref, xlo_ref, xhi_ref, acumt_ref,
     vnb_ref, mo_ref, stb_ref, ys_ref, cb_ref, yoff_ref) = (next(it) for _ in range(N_SHARED_SCRATCH))
    pm, pv = bm.proj, bv.proj

    rows_all = slice(0, TILE_ROWS)
    nchunk = seglen // lc
    pad_rows = HALF - lc

    def pad_to_half(v):
        if pad_rows == 0:
            return [v]
        return [v, jnp.zeros((pad_rows, v.shape[1]), v.dtype)]

    def m_norm():
        for s in range(nseq):
            rows = slice(s * seglen, (s + 1) * seglen)
            xr = x_ref[rows, :]
            gain = gpre_ref[...] * (1.0 + scale_ref[s:s + 1, :])
            hs = (xr * _rms_scale(xr)) * gain + shift_ref[s:s + 1, :]
            h_ref[rows, :] = hs.astype(BF16)

    def m_dot(w_ref, w0, dst0, width):
        def task():
            pm[rows_all, dst0:dst0 + width] = _dot(h_ref[...], w_ref[:, w0:w0 + width])
        return _tag(task, f"m_dot{dst0 // MXU_COLS}")

    dst_starts = (list(range(OFF_XBC, OFF_U, MXU_COLS)) + list(range(OFF_Z, OFF_XBC, MXU_COLS))
                  + list(range(OFF_U, OFF_DT, MXU_COLS)))
    m_dots = [m_dot(wa_ref, d, d, MXU_COLS) if d < OFF_U else m_dot(wb_ref, d - OFF_U, d, MXU_COLS)
              for d in dst_starts] + [m_dot(wdt_ref, 0, OFF_DT, LANES)]

    def conv_store(c, val, rows):
        if c < SSD_WIDTH // LANES:
            bm.xs[rows, c * LANES:(c + 1) * LANES] = val
        else:
            c2 = c - SSD_WIDTH // LANES
            bm.bc[rows, c2 * LANES:(c2 + 1) * LANES] = val.astype(BF16)

    def m_conv(c):
        cols = slice(OFF_XBC + c * LANES, OFF_XBC + (c + 1) * LANES)
        wcols = slice(c * LANES, (c + 1) * LANES)

        def task_stream():
            xfull = jnp.concatenate([tail_ref[:, wcols], pm[:, cols]], axis=0)
            acc = convw_ref[0:1, wcols] * xfull
            for k in range(1, CONV_W):
                acc = pltpu.roll(acc, 1, axis=0) + convw_ref[k:k + 1, wcols] * xfull
            val = _silu(acc[TAIL_ROWS:, :] + convb_ref[:, wcols])
            tail_ref[:, wcols] = pm[TILE_ROWS - TAIL_ROWS:TILE_ROWS, cols]
            conv_out_ref[0, :, wcols] = pm[TILE_ROWS - (CONV_W - 1):TILE_ROWS, cols]
            conv_store(c, val, slice(0, TILE_ROWS))

        def task_state():
            for s in range(nseq):
                rows = slice(s * seglen, (s + 1) * seglen)
                xp_ref[TAP0:TAIL_ROWS, :] = conv0_ref[s, :, wcols]
                xp_ref[TAIL_ROWS:TAIL_ROWS + seglen, :] = pm[s * seglen:(s + 1) * seglen, cols]
                acc = convb_ref[:, wcols]
                for k in range(CONV_W):
                    acc = acc + convw_ref[k:k + 1, wcols] * xp_ref[TAP0 + k:TAP0 + k + seglen, :]
                conv_store(c, _silu(acc), rows)
                conv_out_ref[s, :, wcols] = xp_ref[TAIL_ROWS + seglen - (CONV_W - 1):TAIL_ROWS + seglen, :]

        return task_stream if pipelined else task_state

    m_convs = [_tag(m_conv(c), f"m_conv{c}", [f"m_dot{(OFF_XBC + c * LANES) // MXU_COLS}"])
               for c in range(CONV_DIM // LANES)]

    lane1 = lax.broadcasted_iota(jnp.int32, (TILE_ROWS, LANES), 1)

    def v_dt():
        a_row = -jnp.exp(alog_ref[...])
        dt = _softplus(pv[rows_all, OFF_DT:OFF_DT + LANES] + dtb_ref[...])
        dt = jnp.where(lane1 < HEADS, dt, 0.0)
        csum = _dot(tril_ref[...], _cat3(dt * a_row).astype(BF16))
        a_cum = csum + pltpu.roll(csum, LANES - HEADS, axis=1) + pltpu.roll(csum, LANES - 2 * HEADS, axis=1)
        a_cat = _cat3(jnp.where(lane1 < HEADS, a_cum, 0.0))

        def per_head_lanes(v):
            blocks = []
            for p in range(PAIRS):
                lo = jnp.broadcast_to(v[:, 2 * p:2 * p + 1], (TILE_ROWS, LANES))
                hi = jnp.broadcast_to(v[:, 2 * p + 1:2 * p + 2], (TILE_ROWS, LANES))
                blocks.append(jnp.where(lane1 < HALF, lo, hi))
            return jnp.concatenate(blocks, axis=1)

        aexp_ref[...] = per_head_lanes(a_cum)
        dt_exp = per_head_lanes(dt)
        even = lax.rem(lane1, 2) == 0
        a_even = jnp.where(even, a_cat, 0.0)
        a_odd = (a_cat - a_even).astype(BF16)
        a_even = a_even.astype(BF16)
        parts = []
        for ck in range(TILE_ROWS // lc):
            rows = slice(ck * lc, (ck + 1) * lc)
            parts += pad_to_half(a_even[rows, :]) + pad_to_half(a_odd[rows, :])
        acumt_ref[...] = _dot_nt(psel_ref[...], jnp.concatenate(parts, axis=0))
        xdt = bv.xs[...] * dt_exp
        lane_w = lax.broadcasted_iota(jnp.int32, (TILE_ROWS, SSD_WIDTH), 1)
        xlo = jnp.where(lax.rem(lane_w, LANES) < HALF, xdt, 0.0)
        xdt_ref[...] = xdt
        xlo_ref[...] = xlo.astype(BF16)
        xhi_ref[...] = (xdt - xlo).astype(BF16)

    li = lax.broadcasted_iota(jnp.int32, (lc, LANES), 0)
    si = lax.rem(lax.broadcasted_iota(jnp.int32, (lc, LANES), 1), HALF)
    causal = (si <= li) & (si < lc)

    def v_local(ci, g):
        rows = slice(ci * lc, (ci + 1) * lc)
        hl = slice(g * GROUP_WIDTH, (g + 1) * GROUP_WIDTH)

        def products():
            last = aexp_ref[(ci + 1) * lc - 1:(ci + 1) * lc, hl]
            xw_b = (xdt_ref[rows, hl] * jnp.exp(last - aexp_ref[rows, hl])).astype(BF16)
            b_g = bv.bc[rows, g * STATE:(g + 1) * STATE]
            c_g = bv.bc[rows, (GROUPS + g) * STATE:(GROUPS + g + 1) * STATE]
            b2 = jnp.concatenate(pad_to_half(b_g) + pad_to_half(b_g), axis=0)
            cb_ref[ci * GROUPS + g] = _dot_nt(c_g, b2)
            stnew_ref[ci, :, hl] = _dot_tn(b_g, xw_b)

        def diag():
            cb2 = cb_ref[ci * GROUPS + g]
            for i in range(PAIRS_PER_GROUP):
                p = g * PAIRS_PER_GROUP + i
                pcols = slice(p * LANES, (p + 1) * LANES)
                seg = aexp_ref[rows, pcols] - acumt_ref[p:p + 1, ci * LANES:(ci + 1) * LANES]
                w_p = (cb2 * jnp.exp(jnp.where(causal, seg, -jnp.inf))).astype(BF16)
                bd = jnp.concatenate(pad_to_half(xlo_ref[rows, pcols]) + pad_to_half(xhi_ref[rows, pcols]),
                                     axis=0)
                ys_ref[rows, pcols] = _dot(w_p, bd) + dskip_ref[:, pcols] * bv.xs[rows, pcols]
        return products, diag

    def v_state(ci, s):
        def task():
            e_last = jnp.exp(aexp_ref[(ci + 1) * lc - 1:(ci + 1) * lc, :])
            st = st_ref[...]
            stb_ref[ci] = st.astype(BF16)
            st_ref[...] = st * e_last + stnew_ref[ci]
            if has_state and ci % nchunk == nchunk - 1:
                ssm_out_ref[s] = st_ref[...].T.reshape(HEADS, HEAD_DIM, STATE)
        return task

    def v_readout(ci, g):
        rows = slice(ci * lc, (ci + 1) * lc)
        prow = rows
        hl = slice(g * GROUP_WIDTH, (g + 1) * GROUP_WIDTH)

        def product():
            c_g = bv.bc[rows, (GROUPS + g) * STATE:(GROUPS + g + 1) * STATE]
            yoff_ref[rows, hl] = _dot(c_g, stb_ref[ci, :, hl])

        def gate():
            y = ys_ref[rows, hl] + jnp.exp(aexp_ref[rows, hl]) * yoff_ref[rows, hl]
            zg = _silu(pv[prow, OFF_Z + g * GROUP_WIDTH:OFF_Z + (g + 1) * GROUP_WIDTH])
            bvo.mix[rows, hl] = (y * zg).astype(BF16)
        return product, gate

    def v_state_load(s):
        def task():
            st_ref[...] = ssm0_ref[s].reshape(SSD_WIDTH, STATE).T
        return task

    def v_ln():
        v = pv[rows_all, OFF_V:OFF_V + GMLP_WIDTH]
        vc = v - _row_mean(v)
        vn = (vc * _rms_scale(vc)) * gv_ref[...] + betav_ref[...]
        if emit_v:
            vn_out_ref[...] = vn
        vnb_ref[...] = vn.astype(BF16)

    def v_gmlp(g):
        gcols = slice(g * GMLP_GROUP_DIM, (g + 1) * GMLP_GROUP_DIM)

        def task():
            if mixw_ref.shape[1] == TILE_ROWS:
                mixed = _dot(mixw_ref[g], vnb_ref[:, gcols])
            else:
                lcg = mixw_ref.shape[1]
                blocks = [vnb_ref[r:r + lcg, gcols] for r in range(0, TILE_ROWS, lcg)]
                wide = _dot(mixw_ref[g], jnp.concatenate(blocks, axis=1))
                mixed = jnp.concatenate(
                    [wide[:, k * GMLP_GROUP_DIM:(k + 1) * GMLP_GROUP_DIM] for k in range(len(blocks))],
                    axis=0)
            mixed = mixed + mixb_ref[:, gcols]
            u = pv[rows_all, OFF_U + g * GMLP_GROUP_DIM:OFF_U + (g + 1) * GMLP_GROUP_DIM]
            gt = pv[rows_all, OFF_G + g * GMLP_GROUP_DIM:OFF_G + (g + 1) * GMLP_GROUP_DIM]
            bvo.mix[:, SSD_WIDTH + g * GMLP_GROUP_DIM:SSD_WIDTH + (g + 1) * GMLP_GROUP_DIM] = (
                _silu(gt) * u * mixed).astype(BF16)
        return task

    chunk_groups = [(ci, g) for ci in range(nseq * nchunk) for g in range(GROUPS)]
    v_locals = [v_local(ci, g) for ci, g in chunk_groups]
    v_states = []
    for s in range(nseq):
        if has_state:
            v_states.append(v_state_load(s))
        v_states += [v_state(s * nchunk + c, s) for c in range(nchunk)]
    v_readouts = [v_readout(ci, g) for ci, g in chunk_groups]
    v_scan = ([(t[0], 0) for t in v_locals] + [(t[1], 1) for t in v_locals]
              + [(t, 1) for t in v_states]
              + [(t[0], 0) for t in v_readouts] + [(t[1], 1) for t in v_readouts])
    v_gmlps = [v_gmlp(g) for g in range(GMLP_GROUPS)]

    def o_out(n):
        def task():
            mo_ref[:, n * MXU_COLS:(n + 1) * MXU_COLS] = _dot(
                bo.mix[...], wout_ref[:, n * MXU_COLS:(n + 1) * MXU_COLS])
        return task

    def o_final():
        mo = mo_ref[...]
        mo = mo * _rms_scale(mo)
        for s in range(nseq):
            rows = slice(s * seglen, (s + 1) * seglen)
            gain = gpost_ref[...] * gate_ref[s:s + 1, :]
            y_ref[rows, :] = xo_ref[rows, :] + gain * mo[rows, :]

    o_outs = [_tag(o_out(n), f"o_out{n}") for n in range(D_MODEL // MXU_COLS)]
    _tag(o_final, "o_final", [t.name for t in o_outs])

    if pipelined:
        n_xbc_dots = CONV_DIM // MXU_COLS
        head = _interleave([(m_norm, 1), (v_dt, 2), (v_ln, 2)], o_outs + m_dots[:n_xbc_dots])
        body = [(t, 3) for t in m_convs] + [(o_final, 3)] + v_scan + [(t, 0) for t in v_gmlps]
        groups = head + _interleave(body, m_dots[n_xbc_dots:])
    else:
        order = ([m_norm] + m_dots + m_convs + [v_dt, v_ln] + [t for t, _ in v_scan] + v_gmlps
                 + o_outs + [o_final])
        groups = [[task] for task in order]
    _run_groups(groups)


def _const_spec(shape):
    nd = len(shape)
    return pl.BlockSpec(shape, lambda j: (0,) * nd, pipeline_mode=pl.Buffered(1))


def _chunk_constants(lc):
    psel = np.zeros((PAIRS, LANES), np.float32)
    for piece in range(PIECES):
        for hd in range(HEADS):
            psel[hd // 2, piece * HEADS + hd] = 1.0
    r = np.arange(TILE_ROWS)
    tril = ((r[:, None] >= r[None, :]) & (r[:, None] // lc == r[None, :] // lc)).astype(np.float32)
    return jnp.asarray(tril, BF16), jnp.asarray(psel, BF16)


def _mix_weights(w_s, b_s, lc):
    nck = TILE_ROWS // lc
    mask = jnp.tril(jnp.ones((lc, lc), dtype=bool))
    w = jnp.where(mask[None], w_s[:, :lc, :lc], 0)
    if nck * GMLP_GROUP_DIM > MXU_COLS:
        eye = jnp.eye(nck, dtype=w.dtype)
        w = jnp.einsum("ab,gts->gatbs", eye, w).reshape(GMLP_GROUPS, TILE_ROWS, TILE_ROWS)
    bias = jnp.repeat(b_s[:, :lc].T, GMLP_GROUP_DIM, axis=1)
    return w.astype(BF16), jnp.tile(bias, (nck, 1))


def _layer(x, shift, scale, gate, conv0, ssm0, wl, *, nseq, seglen, lc_ssd, lc_gmlp, nt, emit_v,
           layer=0):
    (g_pre, g_post, w_a, w_b, w_dt, conv_w, conv_b, dtb, alog, dskip, g_v, beta_v, w_s, b_s,
     w_out) = wl
    assert lc_ssd == lc_gmlp or nseq == 1
    pipelined = conv0 is None
    nb = shift.shape[0]
    rows_total = x.shape[0]
    ntiles = rows_total // TILE_ROWS
    assert nseq * seglen == TILE_ROWS and nb * nt == ntiles
    tril, psel = _chunk_constants(lc_ssd)
    mixw, mixb = _mix_weights(w_s, b_s, lc_gmlp)
    nck = TILE_ROWS // lc_ssd

    if pipelined:
        tps = TILES_PER_STEP
        assert ntiles % tps == 0

        def tile_m(j):
            return jnp.minimum(tps * j, ntiles - tps)

        def tile_v(j):
            return jnp.clip(tps * j - 1, 0, ntiles - 1)

        def tile_o(j):
            return jnp.maximum(tps * j - 2, 0)
        steps = ntiles // tps + 1
    else:
        tps = 1

        def tile_m(j):
            return j
        tile_v = tile_o = tile_m
        steps = ntiles

    def mod_spec(tile_of):
        return pl.BlockSpec((None, nseq, D_MODEL), lambda j: (tile_of(j) // nt, 0, 0))

    in_specs = [pl.BlockSpec((tps * TILE_ROWS, D_MODEL), lambda j: (tile_m(j) // tps, 0)),
                pl.BlockSpec((tps * TILE_ROWS, D_MODEL), lambda j: (tile_o(j) // tps, 0)),
                mod_spec(tile_m), mod_spec(tile_m), mod_spec(tile_o)]
    args = [x, x, shift, scale, gate]
    if not pipelined:
        in_specs += [pl.BlockSpec((None, nseq, CONV_W - 1, CONV_DIM), lambda j: (j // nt, 0, 0, 0)),
                     pl.BlockSpec((None, nseq, HEADS, HEAD_DIM, STATE), lambda j: (layer, j, 0, 0, 0))]
        args += [conv0, ssm0]
    consts = [g_pre, g_post, w_a, w_b, w_dt, conv_w, conv_b, dtb, alog, dskip, g_v, beta_v, mixw,
              mixb, w_out, tril, psel]
    in_specs += [_const_spec(a.shape) for a in consts]
    args += consts

    out_shape = [jax.ShapeDtypeStruct((rows_total, D_MODEL), F32),
                 jax.ShapeDtypeStruct((nb, nseq, CONV_W - 1, CONV_DIM), F32),
                 jax.ShapeDtypeStruct((nb * nseq, HEADS, HEAD_DIM, STATE), F32)]
    out_specs = [pl.BlockSpec((tps * TILE_ROWS, D_MODEL), lambda j: (tile_o(j) // tps, 0)),
                 pl.BlockSpec((None, nseq, CONV_W - 1, CONV_DIM), lambda j: (tile_m(j) // nt, 0, 0, 0)),
                 pl.BlockSpec((nseq, HEADS, HEAD_DIM, STATE), lambda j: (tile_v(j) // nt, 0, 0, 0))]
    if emit_v:
        out_shape.append(jax.ShapeDtypeStruct((rows_total, GMLP_WIDTH), F32))
        out_specs.append(pl.BlockSpec((TILE_ROWS, GMLP_WIDTH), lambda j: (tile_v(j), 0)))

    spare = (TAIL_ROWS, LANES)
    stage_shapes = [((TILE_ROWS, W_CAT_COLS), F32),
                    ((TILE_ROWS, SSD_WIDTH), F32),
                    ((TILE_ROWS, 2 * GROUPS * STATE), BF16),
                    ((TILE_ROWS, SSD_WIDTH + GMLP_WIDTH), BF16)]
    scratch = [pltpu.VMEM((TILE_ROWS, D_MODEL), BF16),
               pltpu.VMEM((TAIL_ROWS, CONV_DIM), F32),
               pltpu.VMEM((TAIL_ROWS + seglen, LANES), F32),
               pltpu.VMEM((STATE, SSD_WIDTH), F32),
               pltpu.VMEM((nck, STATE, SSD_WIDTH), F32),
               pltpu.VMEM((TILE_ROWS, SSD_WIDTH), F32),
               pltpu.VMEM((TILE_ROWS, SSD_WIDTH), F32),
               pltpu.VMEM((TILE_ROWS, SSD_WIDTH), BF16),
               pltpu.VMEM((TILE_ROWS, SSD_WIDTH), BF16),
               pltpu.VMEM((PAIRS, nck * LANES), F32),
               pltpu.VMEM((TILE_ROWS, GMLP_WIDTH), BF16),
               pltpu.VMEM((TILE_ROWS, D_MODEL), F32),
               pltpu.VMEM((nck, STATE, SSD_WIDTH), BF16),
               pltpu.VMEM((TILE_ROWS, SSD_WIDTH), F32),
               pltpu.VMEM((nck * GROUPS, lc_ssd, LANES), F32),
               pltpu.VMEM((TILE_ROWS, SSD_WIDTH), F32)]
    assert len(scratch) == N_SHARED_SCRATCH
    for shape, dtype in stage_shapes:
        scratch += [pltpu.VMEM(shape, dtype), pltpu.VMEM(shape if pipelined else spare, dtype)]
    kern = functools.partial(_layer_kernel, nseq=nseq, seglen=seglen, lc=lc_ssd, nt=nt,
                             pipelined=pipelined, emit_v=emit_v)
    return pl.pallas_call(
        kern,
        grid=(steps,),
        in_specs=in_specs,
        out_specs=out_specs,
        out_shape=out_shape,
        scratch_shapes=scratch,
        compiler_params=pltpu.CompilerParams(
            dimension_semantics=("arbitrary",),
            vmem_limit_bytes=VMEM_LIMIT_BYTES),
        name="hybrid_layer_stream" if pipelined else "hybrid_layer_state",
    )(*args)


def _prep_weights(g_pre, g_post, w_in, conv_w, conv_b, dt_bias, a_log, d_skip, g_v, beta_v,
                  w_s, b_s, w_out):
    w_a, w_b, w_dt = _cast_weights(w_in)
    pad = (0, LANES - HEADS)
    return (g_pre.reshape(1, -1), g_post.reshape(1, -1), w_a, w_b, w_dt, conv_w, conv_b.reshape(1, -1),
            jnp.pad(dt_bias, pad).reshape(1, -1), jnp.pad(a_log, pad).reshape(1, -1),
            jnp.repeat(d_skip, HEAD_DIM).reshape(1, -1), g_v.reshape(1, -1),
            beta_v.reshape(1, -1), w_s, b_s, w_out.astype(BF16))


def kernel(x_prompt, x_sample, state_conv, state_ssm, c_prompt, c_sample, w_ada, b_ada, g_pre,
           g_post, w_in, conv_w, conv_b, dt_bias, a_log, d_skip, g_v, beta_v, w_s, b_s, w_out):
    depth = w_ada.shape[0]
    bp, lp, _ = x_prompt.shape
    bs, ls, _ = x_sample.shape
    seq_per_tile = TILE_ROWS // ls
    yp = x_prompt.reshape(bp * lp, D_MODEL)
    ys = x_sample.reshape(bs * ls, D_MODEL)
    c_all = jnp.concatenate([c_prompt, c_sample], axis=0)
    c_pad = jnp.pad(c_all, ((0, (-c_all.shape[0]) % 8), (0, 0)))
    outs = [[] for _ in range(5)]
    for l in range(depth):
        wl = _prep_weights(g_pre[l], g_post[l], w_in[l], conv_w[l], conv_b[l], dt_bias[l],
                           a_log[l], d_skip[l], g_v[l], beta_v[l], w_s[l], b_s[l], w_out[l])
        mod = _modulation(c_pad, w_ada[l], b_ada[l])
        shift, scale, gate = (mod[:, i * D_MODEL:(i + 1) * D_MODEL] for i in range(3))
        mp = [m[:bp].reshape(bp, 1, D_MODEL) for m in (shift, scale, gate)]
        ms = [m[bp:bp + bs].reshape(bs // seq_per_tile, seq_per_tile, D_MODEL)
              for m in (shift, scale, gate)]
        yp, conv_p, ssm_p = _layer(
            yp, *mp, None, None, wl, nseq=1, seglen=TILE_ROWS, lc_ssd=min(lp, SSD_CHUNK),
            lc_gmlp=min(lp, GMLP_CHUNK), nt=lp // TILE_ROWS, emit_v=False)
        ys, conv_s, ssm_s, v_s = _layer(
            ys, *ms,
            state_conv[l].reshape(bs // seq_per_tile, seq_per_tile, CONV_W - 1, CONV_DIM),
            state_ssm, wl,
            nseq=seq_per_tile, seglen=ls, lc_ssd=min(ls, SSD_CHUNK), lc_gmlp=min(ls, GMLP_CHUNK),
            nt=1, emit_v=True, layer=l)
        outs[0].append(conv_p.reshape(bp, CONV_W - 1, CONV_DIM))
        outs[1].append(ssm_p)
        outs[2].append(conv_s.reshape(bs, CONV_W - 1, CONV_DIM))
        outs[3].append(ssm_s)
        outs[4].append(v_s.reshape(bs, ls, GMLP_WIDTH))
    stacked = [o[0][None] if depth == 1 else jnp.stack(o) for o in outs]
    return (yp.reshape(bp, lp, D_MODEL), ys.reshape(bs, ls, D_MODEL), *stacked)
```

```python
import collections
import functools

import jax
import jax.numpy as jnp
import numpy as np
from jax import lax
from jax.experimental import pallas as pl
from jax.experimental.pallas import tpu as pltpu

F32 = jnp.float32
BF16 = jnp.bfloat16

D_MODEL = 1024
SSD_WIDTH = 1024
HEAD_DIM = 64
HEADS = SSD_WIDTH // HEAD_DIM
GROUPS = 2
GROUP_WIDTH = SSD_WIDTH // GROUPS
STATE = 128
CONV_W = 4
CONV_DIM = SSD_WIDTH + 2 * GROUPS * STATE
GMLP_WIDTH = 1024
GMLP_GROUPS = 8
GMLP_GROUP_DIM = GMLP_WIDTH // GMLP_GROUPS
SSD_CHUNK = 64
GMLP_CHUNK = 128
EPS = 1e-6
LOG2E = 1.4426950408889634

LANES = 128
MXU_COLS = 256
HALF = LANES // 2
PAIRS = HEADS // 2
PAIRS_PER_GROUP = PAIRS // GROUPS
PIECES = 3
TILE_ROWS = 256
TAIL_ROWS = 8
TAP0 = TAIL_ROWS - (CONV_W - 1)
MOD_BLOCK_COLS = 512
VMEM_LIMIT_BYTES = 60 * 1024 * 1024

OFF_Z = 0
OFF_XBC = OFF_Z + SSD_WIDTH
OFF_U = OFF_XBC + CONV_DIM
OFF_V = OFF_U + GMLP_WIDTH
OFF_G = OFF_V + GMLP_WIDTH
OFF_DT = OFF_G + GMLP_WIDTH
W_CAT_COLS = OFF_DT + LANES

Bufs = collections.namedtuple("Bufs", "proj xs bc mix")


def _dot(a, b):
    return jnp.dot(a, b, preferred_element_type=F32)


def _dot_nt(a, b):
    return lax.dot_general(a, b, (((1,), (1,)), ((), ())), preferred_element_type=F32)


def _dot_tn(a, b):
    return lax.dot_general(a, b, (((0,), (0,)), ((), ())), preferred_element_type=F32)


def _cat3(x):
    hi = x.astype(BF16).astype(F32)
    r1 = x - hi
    mid = r1.astype(BF16).astype(F32)
    lo = r1 - mid
    return hi + pltpu.roll(mid, HEADS, axis=1) + pltpu.roll(lo, 2 * HEADS, axis=1)


def _silu(x):
    return x / (1.0 + jnp.exp2(x * (-LOG2E)))


def _softplus(x):
    return jnp.maximum(x, 0.0) + jnp.log1p(jnp.exp(-jnp.abs(x)))


def _row_mean(x):
    width = x.shape[-1]
    part = x[:, 0:LANES]
    for c in range(LANES, width, LANES):
        part = part + x[:, c:c + LANES]
    return jnp.sum(part, axis=-1, keepdims=True) * (1.0 / width)


def _rms_scale(x):
    return lax.rsqrt(_row_mean(x * x) + EPS)


def _mod_kernel(c_ref, w_ref, b_ref, o_ref):
    c = _silu(c_ref[...]).astype(BF16)
    o_ref[...] = _dot(c, w_ref[...].astype(BF16)) + b_ref[...]


def _modulation(c, w_ada, b_ada):
    rows = c.shape[0]
    n = w_ada.shape[1]
    return pl.pallas_call(
        _mod_kernel,
        grid=(n // MOD_BLOCK_COLS,),
        in_specs=[
            pl.BlockSpec((rows, D_MODEL), lambda j: (0, 0)),
            pl.BlockSpec((D_MODEL, MOD_BLOCK_COLS), lambda j: (0, j)),
            pl.BlockSpec((1, MOD_BLOCK_COLS), lambda j: (0, j)),
        ],
        out_specs=pl.BlockSpec((rows, MOD_BLOCK_COLS), lambda j: (0, j)),
        out_shape=jax.ShapeDtypeStruct((rows, n), F32),
        name="adaln_mod",
    )(c, w_ada, b_ada.reshape(1, n))


CAST_BLOCK_COLS = 512
A_COLS = SSD_WIDTH + CONV_DIM
B_COLS = 3 * GMLP_WIDTH


def _cast_kernel(a_ref, b_ref, b_next_ref, wa_ref, wb_ref, wdt_ref):
    j = pl.program_id(0)
    wa_ref[...] = a_ref[...].astype(BF16)
    wide = jnp.concatenate([b_ref[...], b_next_ref[...]], axis=1)
    shifted = pltpu.roll(wide, wide.shape[1] - HEADS, axis=1)
    wb_ref[...] = shifted[:, :CAST_BLOCK_COLS].astype(BF16)

    @pl.when(j == 0)
    def _():
        head = b_ref[:, 0:LANES]
        lane = lax.broadcasted_iota(jnp.int32, head.shape, 1)
        wdt_ref[...] = jnp.where(lane < HEADS, head, 0.0).astype(BF16)


def _cast_weights(w_in):
    a_blocks = A_COLS // CAST_BLOCK_COLS
    b_blocks = B_COLS // CAST_BLOCK_COLS
    lanes_per_block = CAST_BLOCK_COLS // LANES
    return pl.pallas_call(
        _cast_kernel,
        grid=(b_blocks,),
        in_specs=[
            pl.BlockSpec((D_MODEL, CAST_BLOCK_COLS), lambda j: (0, jnp.minimum(j, a_blocks - 1))),
            pl.BlockSpec((D_MODEL, CAST_BLOCK_COLS), lambda j: (0, a_blocks + j)),
            pl.BlockSpec((D_MODEL, LANES), lambda j: (0, (a_blocks + j + 1) * lanes_per_block)),
        ],
        out_specs=[
            pl.BlockSpec((D_MODEL, CAST_BLOCK_COLS), lambda j: (0, jnp.minimum(j, a_blocks - 1))),
            pl.BlockSpec((D_MODEL, CAST_BLOCK_COLS), lambda j: (0, j)),
            pl.BlockSpec((D_MODEL, LANES), lambda j: (0, 0)),
        ],
        out_shape=[jax.ShapeDtypeStruct((D_MODEL, A_COLS), BF16),
                   jax.ShapeDtypeStruct((D_MODEL, B_COLS), BF16),
                   jax.ShapeDtypeStruct((D_MODEL, LANES), BF16)],
        compiler_params=pltpu.CompilerParams(dimension_semantics=("arbitrary",)),
        name="cast_proj_weight",
    )(w_in, w_in, w_in)


TILES_PER_STEP = 2
N_SHARED_SCRATCH = 16
N_STAGE_SCRATCH = 8


def _interleave(primary, secondary):
    total = sum(weight for _, weight in primary)
    groups = []
    done = 0
    seen = 0
    for task, weight in primary:
        seen += weight
        want = (seen * len(secondary)) // total
        groups.append([task] + secondary[done:want])
        done = want
    return groups


def _tag(task, name, needs=()):
    task.name, task.needs = name, tuple(needs)
    return task


def _run_groups(groups):
    stored = set()
    for group in groups:
        for task in group:
            missing = [n for n in getattr(task, "needs", ()) if n not in stored]
            assert not missing, (getattr(task, "name", task), missing)
        commits = [task() for task in group]
        for commit in commits:
            if commit is not None:
                commit()
        stored.update(getattr(task, "name", None) for task in group)


def _layer_kernel(*refs, nseq, seglen, lc, nt, pipelined, emit_v):
    statics = dict(nseq=nseq, seglen=seglen, lc=lc, pipelined=pipelined, emit_v=emit_v)
    shared, stage = refs[:-N_STAGE_SCRATCH], refs[-N_STAGE_SCRATCH:]
    buf_a, buf_b = Bufs(*stage[0::2]), Bufs(*stage[1::2])
    n_scratch = N_SHARED_SCRATCH + N_STAGE_SCRATCH
    ssm_out_ref = refs[len(refs) - n_scratch - (2 if emit_v else 1)]
    tail_ref, st_ref = refs[-n_scratch + 1], refs[-n_scratch + 3]
    if not pipelined:
        _tile_body(shared, buf_a, buf_a, buf_a, buf_a, **statics)
        return

    assert nt % TILES_PER_STEP == 0 and TILES_PER_STEP == 2
    i = pl.program_id(0)
    row_start = lax.rem(TILES_PER_STEP * i, nt) == 0
    n_in = len(shared) - N_SHARED_SCRATCH - 3
    views = []
    for half in range(TILES_PER_STEP):
        rows = slice(half * TILE_ROWS, (half + 1) * TILE_ROWS)
        view = list(shared)
        for k in (0, 1, n_in):
            view[k] = shared[k].at[rows, :]
        views.append(view)

    @pl.when(i == 0)
    def _():
        for ref in buf_b:
            ref[...] = jnp.zeros_like(ref)
        st_ref[...] = jnp.zeros_like(st_ref)

    @pl.when(row_start)
    def _():
        tail_ref[...] = jnp.zeros_like(tail_ref)

    _tile_body(views[0], buf_a, buf_b, buf_a, buf_b, **statics)

    @pl.when(row_start & (i > 0))
    def _():
        ssm_out_ref[0] = st_ref[...].T.reshape(HEADS, HEAD_DIM, STATE)

    @pl.when(row_start)
    def _():
        st_ref[...] = jnp.zeros_like(st_ref)

    _tile_body(views[1], buf_b, buf_a, buf_b, buf_a, **statics)


def _tile_body(refs, bm, bv, bvo, bo, *, nseq, seglen, lc, pipelined, emit_v):
    has_state = not pipelined
    it = iter(refs)
    x_ref, xo_ref = next(it), next(it)
    shift_ref, scale_ref, gate_ref = next(it), next(it), next(it)
    conv0_ref = next(it) if has_state else None
    ssm0_ref = next(it) if has_state else None
    (gpre_ref, gpost_ref, wa_ref, wb_ref, wdt_ref, convw_ref, convb_ref, dtb_ref, alog_ref,
     dskip_ref, gv_ref, betav_ref, mixw_ref, mixb_ref, wout_ref, tril_ref, psel_ref) = (
         next(it) for _ in range(17))
    y_ref, conv_out_ref, ssm_out_ref = next(it), next(it), next(it)
    vn_out_ref = next(it) if emit_v else None
    (h_ref, tail_ref, xp_ref, st_ref, stnew_ref, aexp_ref, xdt_ref, xlo_ref, xhi_ref, acumt_ref,
     vnb_ref, mo_ref, stb_ref, ys_ref, cb_ref, yoff_ref) = (next(it) for _ in range(N_SHARED_SCRATCH))
    pm, pv = bm.proj, bv.proj

    rows_all = slice(TAIL_ROWS, TAIL_ROWS + TILE_ROWS)
    nchunk = seglen // lc
    pad_rows = HALF - lc

    def pad_to_half(v):
        if pad_rows == 0:
            return [v]
        return [v, jnp.zeros((pad_rows, v.shape[1]), v.dtype)]

    def m_norm():
        for s in range(nseq):
            rows = slice(s * seglen, (s + 1) * seglen)
            xr = x_ref[rows, :]
            gain = gpre_ref[...] * (1.0 + scale_ref[s:s + 1, :])
            hs = (xr * _rms_scale(xr)) * gain + shift_ref[s:s + 1, :]
            h_ref[rows, :] = hs.astype(BF16)

    def m_dot(w_ref, w0, dst0, width):
        def task():
            val = _dot(h_ref[...], w_ref[:, w0:w0 + width])

            def commit():
                pm[rows_all, dst0:dst0 + width] = val
            return commit
        return _tag(task, f"m_dot{dst0 // MXU_COLS}")

    dst_starts = (list(range(OFF_XBC, OFF_U, MXU_COLS)) + list(range(OFF_Z, OFF_XBC, MXU_COLS))
                  + list(range(OFF_U, OFF_DT, MXU_COLS)))
    m_dots = [m_dot(wa_ref, d, d, MXU_COLS) if d < OFF_U else m_dot(wb_ref, d - OFF_U, d, MXU_COLS)
              for d in dst_starts] + [m_dot(wdt_ref, 0, OFF_DT, LANES)]

    def conv_store(c, val, rows):
        if c < SSD_WIDTH // LANES:
            bm.xs[rows, c * LANES:(c + 1) * LANES] = val
        else:
            c2 = c - SSD_WIDTH // LANES
            bm.bc[rows, c2 * LANES:(c2 + 1) * LANES] = val.astype(BF16)

    def m_conv(c):
        cols = slice(OFF_XBC + c * LANES, OFF_XBC + (c + 1) * LANES)
        wcols = slice(c * LANES, (c + 1) * LANES)

        def task_stream():
            pm[0:TAIL_ROWS, cols] = tail_ref[:, wcols]
            xfull = pm[:, cols]
            acc = convw_ref[0:1, wcols] * xfull
            for k in range(1, CONV_W):
                acc = pltpu.roll(acc, 1, axis=0) + convw_ref[k:k + 1, wcols] * xfull
            val = _silu(acc[TAIL_ROWS:, :] + convb_ref[:, wcols])
            tail_ref[:, wcols] = pm[TILE_ROWS:TILE_ROWS + TAIL_ROWS, cols]
            conv_out_ref[0, :, wcols] = pm[TAIL_ROWS + TILE_ROWS - (CONV_W - 1):TAIL_ROWS + TILE_ROWS, cols]

            def commit():
                conv_store(c, val, slice(0, TILE_ROWS))
            return commit

        def task_state():
            for s in range(nseq):
                rows = slice(s * seglen, (s + 1) * seglen)
                xp_ref[TAP0:TAIL_ROWS, :] = conv0_ref[s, :, wcols]
                xp_ref[TAIL_ROWS:TAIL_ROWS + seglen, :] = pm[TAIL_ROWS + s * seglen:TAIL_ROWS + (s + 1) * seglen, cols]
                acc = convb_ref[:, wcols]
                for k in range(CONV_W):
                    acc = acc + convw_ref[k:k + 1, wcols] * xp_ref[TAP0 + k:TAP0 + k + seglen, :]
                conv_store(c, _silu(acc), rows)
                conv_out_ref[s, :, wcols] = xp_ref[TAIL_ROWS + seglen - (CONV_W - 1):TAIL_ROWS + seglen, :]

        return task_stream if pipelined else task_state

    m_convs = [_tag(m_conv(c), f"m_conv{c}", [f"m_dot{(OFF_XBC + c * LANES) // MXU_COLS}"])
               for c in range(CONV_DIM // LANES)]

    lane1 = lax.broadcasted_iota(jnp.int32, (TILE_ROWS, LANES), 1)

    def v_dt():
        a_row = -jnp.exp(alog_ref[...])
        dt = _softplus(pv[rows_all, OFF_DT:OFF_DT + LANES] + dtb_ref[...])
        dt = jnp.where(lane1 < HEADS, dt, 0.0)
        csum = _dot(tril_ref[...], _cat3(dt * a_row).astype(BF16))
        a_cum = csum + pltpu.roll(csum, LANES - HEADS, axis=1) + pltpu.roll(csum, LANES - 2 * HEADS, axis=1)
        def per_head_lanes(v):
            blocks = []
            for p in range(PAIRS):
                lo = jnp.broadcast_to(v[:, p:p + 1], (TILE_ROWS, LANES))
                hi = jnp.broadcast_to(v[:, PAIRS + p:PAIRS + p + 1], (TILE_ROWS, LANES))
                blocks.append(jnp.where(lane1 < HALF, lo, hi))
            return jnp.concatenate(blocks, axis=1)

        aexp_ref[...] = per_head_lanes(a_cum)
        dt_exp = per_head_lanes(dt)
        ac_t = jnp.where(lane1 < HEADS, a_cum, 0.0).T
        even, odd = ac_t[0:PAIRS, :], ac_t[PAIRS:HEADS, :]
        gap = [jnp.zeros((PAIRS, pad_rows), F32)] if pad_rows else []
        blocks = []
        for ck in range(TILE_ROWS // lc):
            cols = slice(ck * lc, (ck + 1) * lc)
            blocks += [even[:, cols]] + gap + [odd[:, cols]] + gap
        acumt_ref[...] = jnp.concatenate(blocks, axis=1)
        xdt = bv.xs[...] * dt_exp
        lane_w = lax.broadcasted_iota(jnp.int32, (TILE_ROWS, SSD_WIDTH), 1)
        xlo = jnp.where(lax.rem(lane_w, LANES) < HALF, xdt, 0.0)
        xdt_ref[...] = xdt
        xlo_ref[...] = xlo.astype(BF16)
        xhi_ref[...] = (xdt - xlo).astype(BF16)

    li = lax.broadcasted_iota(jnp.int32, (lc, LANES), 0)
    si = lax.rem(lax.broadcasted_iota(jnp.int32, (lc, LANES), 1), HALF)
    causal = (si <= li) & (si < lc)

    def v_local(ci, g):
        rows = slice(ci * lc, (ci + 1) * lc)
        hl = slice(g * GROUP_WIDTH, (g + 1) * GROUP_WIDTH)

        def products():
            last = aexp_ref[(ci + 1) * lc - 1:(ci + 1) * lc, hl]
            xw_b = (xdt_ref[rows, hl] * jnp.exp(last - aexp_ref[rows, hl])).astype(BF16)
            b_g = bv.bc[rows, g * STATE:(g + 1) * STATE]
            c_g = bv.bc[rows, (GROUPS + g) * STATE:(GROUPS + g + 1) * STATE]
            b2 = jnp.concatenate(pad_to_half(b_g) + pad_to_half(b_g), axis=0)
            cb_ref[ci * GROUPS + g] = _dot_nt(c_g, b2)
            stnew_ref[ci, :, hl] = _dot_tn(b_g, xw_b)

        def diag():
            cb2 = cb_ref[ci * GROUPS + g]
            for i in range(PAIRS_PER_GROUP):
                p = g * PAIRS_PER_GROUP + i
                pcols = slice(p * LANES, (p + 1) * LANES)
                seg = aexp_ref[rows, pcols] - acumt_ref[p:p + 1, ci * LANES:(ci + 1) * LANES]
                w_p = (cb2 * jnp.exp(jnp.where(causal, seg, -jnp.inf))).astype(BF16)
                bd = jnp.concatenate(pad_to_half(xlo_ref[rows, pcols]) + pad_to_half(xhi_ref[rows, pcols]),
                                     axis=0)
                ys_ref[rows, pcols] = _dot(w_p, bd) + dskip_ref[:, pcols] * bv.xs[rows, pcols]
        return products, diag

    def v_state(ci, s):
        def task():
            e_last = jnp.exp(aexp_ref[(ci + 1) * lc - 1:(ci + 1) * lc, :])
            st = st_ref[...]
            stb_ref[ci] = st.astype(BF16)
            st_ref[...] = st * e_last + stnew_ref[ci]
            if has_state and ci % nchunk == nchunk - 1:
                ssm_out_ref[s] = st_ref[...].T.reshape(HEADS, HEAD_DIM, STATE)
        return task

    def v_readout(ci, g):
        rows = slice(ci * lc, (ci + 1) * lc)
        prow = slice(TAIL_ROWS + ci * lc, TAIL_ROWS + (ci + 1) * lc)
        hl = slice(g * GROUP_WIDTH, (g + 1) * GROUP_WIDTH)

        def product():
            c_g = bv.bc[rows, (GROUPS + g) * STATE:(GROUPS + g + 1) * STATE]
            yoff_ref[rows, hl] = _dot(c_g, stb_ref[ci, :, hl])

        def gate():
            y = ys_ref[rows, hl] + jnp.exp(aexp_ref[rows, hl]) * yoff_ref[rows, hl]
            zg = _silu(pv[prow, OFF_Z + g * GROUP_WIDTH:OFF_Z + (g + 1) * GROUP_WIDTH])
            bvo.mix[rows, hl] = (y * zg).astype(BF16)
        return product, gate

    def v_state_load(s):
        def task():
            st_ref[...] = ssm0_ref[s].reshape(SSD_WIDTH, STATE).T
        return task

    def v_ln():
        v = pv[rows_all, OFF_V:OFF_V + GMLP_WIDTH]
        vc = v - _row_mean(v)
        vn = (vc * _rms_scale(vc)) * gv_ref[...] + betav_ref[...]
        if emit_v:
            vn_out_ref[...] = vn
        vnb_ref[...] = vn.astype(BF16)

    def v_gmlp(g):
        gcols = slice(g * GMLP_GROUP_DIM, (g + 1) * GMLP_GROUP_DIM)

        def task():
            if mixw_ref.shape[1] == TILE_ROWS:
                mixed = _dot(mixw_ref[g], vnb_ref[:, gcols])
            else:
                lcg = mixw_ref.shape[1]
                blocks = [vnb_ref[r:r + lcg, gcols] for r in range(0, TILE_ROWS, lcg)]
                wide = _dot(mixw_ref[g], jnp.concatenate(blocks, axis=1))
                mixed = jnp.concatenate(
                    [wide[:, k * GMLP_GROUP_DIM:(k + 1) * GMLP_GROUP_DIM] for k in range(len(blocks))],
                    axis=0)
            mixed = mixed + mixb_ref[:, gcols]
            u = pv[rows_all, OFF_U + g * GMLP_GROUP_DIM:OFF_U + (g + 1) * GMLP_GROUP_DIM]
            gt = pv[rows_all, OFF_G + g * GMLP_GROUP_DIM:OFF_G + (g + 1) * GMLP_GROUP_DIM]
            bvo.mix[:, SSD_WIDTH + g * GMLP_GROUP_DIM:SSD_WIDTH + (g + 1) * GMLP_GROUP_DIM] = (
                _silu(gt) * u * mixed).astype(BF16)
        return task

    chunk_groups = [(ci, g) for ci in range(nseq * nchunk) for g in range(GROUPS)]
    v_locals = [v_local(ci, g) for ci, g in chunk_groups]
    v_states = []
    for s in range(nseq):
        if has_state:
            v_states.append(v_state_load(s))
        v_states += [v_state(s * nchunk + c, s) for c in range(nchunk)]
    v_readouts = [v_readout(ci, g) for ci, g in chunk_groups]
    v_scan = ([(t[0], 0) for t in v_locals] + [(t[1], 1) for t in v_locals]
              + [(t, 1) for t in v_states]
              + [(t[0], 0) for t in v_readouts] + [(t[1], 1) for t in v_readouts])
    v_gmlps = [v_gmlp(g) for g in range(GMLP_GROUPS)]

    def o_out(n):
        def task():
            mo_ref[:, n * MXU_COLS:(n + 1) * MXU_COLS] = _dot(
                bo.mix[...], wout_ref[:, n * MXU_COLS:(n + 1) * MXU_COLS])
        return task

    def o_final():
        mo = mo_ref[...]
        mo = mo * _rms_scale(mo)
        for s in range(nseq):
            rows = slice(s * seglen, (s + 1) * seglen)
            gain = gpost_ref[...] * gate_ref[s:s + 1, :]
            y_ref[rows, :] = xo_ref[rows, :] + gain * mo[rows, :]

    o_outs = [_tag(o_out(n), f"o_out{n}") for n in range(D_MODEL // MXU_COLS)]
    _tag(o_final, "o_final", [t.name for t in o_outs])

    if pipelined:
        n_xbc_dots = CONV_DIM // MXU_COLS
        head = _interleave([(m_norm, 1), (v_dt, 2), (v_ln, 2)], o_outs + m_dots[:n_xbc_dots])
        body = [(t, 3) for t in m_convs] + [(o_final, 3)] + v_scan + [(t, 0) for t in v_gmlps]
        groups = head + _interleave(body, m_dots[n_xbc_dots:])
    else:
        order = ([m_norm] + m_dots + m_convs + [v_dt, v_ln] + [t for t, _ in v_scan] + v_gmlps
                 + o_outs + [o_final])
        groups = [[task] for task in order]
    _run_groups(groups)


def _const_spec(shape):
    nd = len(shape)
    return pl.BlockSpec(shape, lambda j: (0,) * nd, pipeline_mode=pl.Buffered(1))


def _chunk_constants(lc):
    psel = np.zeros((PAIRS, LANES), np.float32)
    for piece in range(PIECES):
        for hd in range(HEADS):
            psel[hd // 2, piece * HEADS + hd] = 1.0
    r = np.arange(TILE_ROWS)
    tril = ((r[:, None] >= r[None, :]) & (r[:, None] // lc == r[None, :] // lc)).astype(np.float32)
    return jnp.asarray(tril, BF16), jnp.asarray(psel, BF16)


def _mix_weights(w_s, b_s, lc):
    nck = TILE_ROWS // lc
    mask = jnp.tril(jnp.ones((lc, lc), dtype=bool))
    w = jnp.where(mask[None], w_s[:, :lc, :lc], 0)
    if nck * GMLP_GROUP_DIM > MXU_COLS:
        eye = jnp.eye(nck, dtype=w.dtype)
        w = jnp.einsum("ab,gts->gatbs", eye, w).reshape(GMLP_GROUPS, TILE_ROWS, TILE_ROWS)
    bias = jnp.repeat(b_s[:, :lc].T, GMLP_GROUP_DIM, axis=1)
    return w.astype(BF16), jnp.tile(bias, (nck, 1))


def _layer(x, shift, scale, gate, conv0, ssm0, wl, *, nseq, seglen, lc_ssd, lc_gmlp, nt, emit_v,
           layer=0):
    (g_pre, g_post, w_a, w_b, w_dt, conv_w, conv_b, dtb, alog, dskip, g_v, beta_v, w_s, b_s,
     w_out) = wl
    assert lc_ssd == lc_gmlp or nseq == 1
    pipelined = conv0 is None
    nb = shift.shape[0]
    rows_total = x.shape[0]
    ntiles = rows_total // TILE_ROWS
    assert nseq * seglen == TILE_ROWS and nb * nt == ntiles
    tril, psel = _chunk_constants(lc_ssd)
    mixw, mixb = _mix_weights(w_s, b_s, lc_gmlp)
    nck = TILE_ROWS // lc_ssd

    if pipelined:
        tps = TILES_PER_STEP
        assert ntiles % tps == 0

        def tile_m(j):
            return jnp.minimum(tps * j, ntiles - tps)

        def tile_v(j):
            return jnp.clip(tps * j - 1, 0, ntiles - 1)

        def tile_o(j):
            return jnp.maximum(tps * j - 2, 0)
        steps = ntiles // tps + 1
    else:
        tps = 1

        def tile_m(j):
            return j
        tile_v = tile_o = tile_m
        steps = ntiles

    def mod_spec(tile_of):
        return pl.BlockSpec((None, nseq, D_MODEL), lambda j: (tile_of(j) // nt, 0, 0))

    in_specs = [pl.BlockSpec((tps * TILE_ROWS, D_MODEL), lambda j: (tile_m(j) // tps, 0)),
                pl.BlockSpec((tps * TILE_ROWS, D_MODEL), lambda j: (tile_o(j) // tps, 0)),
                mod_spec(tile_m), mod_spec(tile_m), mod_spec(tile_o)]
    args = [x, x, shift, scale, gate]
    if not pipelined:
        in_specs += [pl.BlockSpec((None, nseq, CONV_W - 1, CONV_DIM), lambda j: (j // nt, 0, 0, 0)),
                     pl.BlockSpec((None, nseq, HEADS, HEAD_DIM, STATE), lambda j: (layer, j, 0, 0, 0))]
        args += [conv0, ssm0]
    consts = [g_pre, g_post, w_a, w_b, w_dt, conv_w, conv_b, dtb, alog, dskip, g_v, beta_v, mixw,
              mixb, w_out, tril, psel]
    in_specs += [_const_spec(a.shape) for a in consts]
    args += consts

    out_shape = [jax.ShapeDtypeStruct((rows_total, D_MODEL), F32),
                 jax.ShapeDtypeStruct((nb, nseq, CONV_W - 1, CONV_DIM), F32),
                 jax.ShapeDtypeStruct((nb * nseq, HEADS, HEAD_DIM, STATE), F32)]
    out_specs = [pl.BlockSpec((tps * TILE_ROWS, D_MODEL), lambda j: (tile_o(j) // tps, 0)),
                 pl.BlockSpec((None, nseq, CONV_W - 1, CONV_DIM), lambda j: (tile_m(j) // nt, 0, 0, 0)),
                 pl.BlockSpec((nseq, HEADS, HEAD_DIM, STATE), lambda j: (tile_v(j) // nt, 0, 0, 0))]
    if emit_v:
        out_shape.append(jax.ShapeDtypeStruct((rows_total, GMLP_WIDTH), F32))
        out_specs.append(pl.BlockSpec((TILE_ROWS, GMLP_WIDTH), lambda j: (tile_v(j), 0)))

    spare = (TAIL_ROWS, LANES)
    stage_shapes = [((TAIL_ROWS + TILE_ROWS, W_CAT_COLS), F32),
                    ((TILE_ROWS, SSD_WIDTH), F32),
                    ((TILE_ROWS, 2 * GROUPS * STATE), BF16),
                    ((TILE_ROWS, SSD_WIDTH + GMLP_WIDTH), BF16)]
    scratch = [pltpu.VMEM((TILE_ROWS, D_MODEL), BF16),
               pltpu.VMEM((TAIL_ROWS, CONV_DIM), F32),
               pltpu.VMEM((TAIL_ROWS + seglen, LANES), F32),
               pltpu.VMEM((STATE, SSD_WIDTH), F32),
               pltpu.VMEM((nck, STATE, SSD_WIDTH), F32),
               pltpu.VMEM((TILE_ROWS, SSD_WIDTH), F32),
               pltpu.VMEM((TILE_ROWS, SSD_WIDTH), F32),
               pltpu.VMEM((TILE_ROWS, SSD_WIDTH), BF16),
               pltpu.VMEM((TILE_ROWS, SSD_WIDTH), BF16),
               pltpu.VMEM((PAIRS, nck * LANES), F32),
               pltpu.VMEM((TILE_ROWS, GMLP_WIDTH), BF16),
               pltpu.VMEM((TILE_ROWS, D_MODEL), F32),
               pltpu.VMEM((nck, STATE, SSD_WIDTH), BF16),
               pltpu.VMEM((TILE_ROWS, SSD_WIDTH), F32),
               pltpu.VMEM((nck * GROUPS, lc_ssd, LANES), F32),
               pltpu.VMEM((TILE_ROWS, SSD_WIDTH), F32)]
    assert len(scratch) == N_SHARED_SCRATCH
    for shape, dtype in stage_shapes:
        scratch += [pltpu.VMEM(shape, dtype), pltpu.VMEM(shape if pipelined else spare, dtype)]
    kern = functools.partial(_layer_kernel, nseq=nseq, seglen=seglen, lc=lc_ssd, nt=nt,
                             pipelined=pipelined, emit_v=emit_v)
    return pl.pallas_call(
        kern,
        grid=(steps,),
        in_specs=in_specs,
        out_specs=out_specs,
        out_shape=out_shape,
        scratch_shapes=scratch,
        compiler_params=pltpu.CompilerParams(
            dimension_semantics=("arbitrary",),
            vmem_limit_bytes=VMEM_LIMIT_BYTES),
        name="hybrid_layer_stream" if pipelined else "hybrid_layer_state",
    )(*args)


def _prep_weights(g_pre, g_post, w_in, conv_w, conv_b, dt_bias, a_log, d_skip, g_v, beta_v,
                  w_s, b_s, w_out):
    w_a, w_b, w_dt = _cast_weights(w_in)
    pad = (0, LANES - HEADS)
    perm = np.concatenate([np.arange(0, HEADS, 2), np.arange(1, HEADS, 2)])
    w_dt = w_dt[:, np.concatenate([perm, np.arange(HEADS, LANES)])]
    dt_bias, a_log = dt_bias[perm], a_log[perm]
    return (g_pre.reshape(1, -1), g_post.reshape(1, -1), w_a, w_b, w_dt, conv_w, conv_b.reshape(1, -1),
            jnp.pad(dt_bias, pad).reshape(1, -1), jnp.pad(a_log, pad).reshape(1, -1),
            jnp.repeat(d_skip, HEAD_DIM).reshape(1, -1), g_v.reshape(1, -1),
            beta_v.reshape(1, -1), w_s, b_s, w_out.astype(BF16))


def kernel(x_prompt, x_sample, state_conv, state_ssm, c_prompt, c_sample, w_ada, b_ada, g_pre,
           g_post, w_in, conv_w, conv_b, dt_bias, a_log, d_skip, g_v, beta_v, w_s, b_s, w_out):
    depth = w_ada.shape[0]
    bp, lp, _ = x_prompt.shape
    bs, ls, _ = x_sample.shape
    seq_per_tile = TILE_ROWS // ls
    yp = x_prompt.reshape(bp * lp, D_MODEL)
    ys = x_sample.reshape(bs * ls, D_MODEL)
    c_all = jnp.concatenate([c_prompt, c_sample], axis=0)
    c_pad = jnp.pad(c_all, ((0, (-c_all.shape[0]) % 8), (0, 0)))
    outs = [[] for _ in range(5)]
    for l in range(depth):
        wl = _prep_weights(g_pre[l], g_post[l], w_in[l], conv_w[l], conv_b[l], dt_bias[l],
                           a_log[l], d_skip[l], g_v[l], beta_v[l], w_s[l], b_s[l], w_out[l])
        mod = _modulation(c_pad, w_ada[l], b_ada[l])
        shift, scale, gate = (mod[:, i * D_MODEL:(i + 1) * D_MODEL] for i in range(3))
        mp = [m[:bp].reshape(bp, 1, D_MODEL) for m in (shift, scale, gate)]
        ms = [m[bp:bp + bs].reshape(bs // seq_per_tile, seq_per_tile, D_MODEL)
              for m in (shift, scale, gate)]
        yp, conv_p, ssm_p = _layer(
            yp, *mp, None, None, wl, nseq=1, seglen=TILE_ROWS, lc_ssd=min(lp, SSD_CHUNK),
            lc_gmlp=min(lp, GMLP_CHUNK), nt=lp // TILE_ROWS, emit_v=False)
        ys, conv_s, ssm_s, v_s = _layer(
            ys, *ms,
            state_conv[l].reshape(bs // seq_per_tile, seq_per_tile, CONV_W - 1, CONV_DIM),
            state_ssm, wl,
            nseq=seq_per_tile, seglen=ls, lc_ssd=min(ls, SSD_CHUNK), lc_gmlp=min(ls, GMLP_CHUNK),
            nt=1, emit_v=True, layer=l)
        outs[0].append(conv_p.reshape(bp, CONV_W - 1, CONV_DIM))
        outs[1].append(ssm_p)
        outs[2].append(conv_s.reshape(bs, CONV_W - 1, CONV_DIM))
        outs[3].append(ssm_s)
        outs[4].append(v_s.reshape(bs, ls, GMLP_WIDTH))
    stacked = [o[0][None] if depth == 1 else jnp.stack(o) for o in outs]
    return (yp.reshape(bp, lp, D_MODEL), ys.reshape(bs, ls, D_MODEL), *stacked)
```

```python
import collections
import functools

import jax
import jax.numpy as jnp
import numpy as np
from jax import lax
from jax.experimental import pallas as pl
from jax.experimental.pallas import tpu as pltpu

F32 = jnp.float32
BF16 = jnp.bfloat16

D_MODEL = 1024
SSD_WIDTH = 1024
HEAD_DIM = 64
HEADS = SSD_WIDTH // HEAD_DIM
GROUPS = 2
GROUP_WIDTH = SSD_WIDTH // GROUPS
STATE = 128
CONV_W = 4
CONV_DIM = SSD_WIDTH + 2 * GROUPS * STATE
GMLP_WIDTH = 1024
GMLP_GROUPS = 8
GMLP_GROUP_DIM = GMLP_WIDTH // GMLP_GROUPS
SSD_CHUNK = 64
GMLP_CHUNK = 128
EPS = 1e-6
LOG2E = 1.4426950408889634

LANES = 128
MXU_COLS = 256
HALF = LANES // 2
PAIRS = HEADS // 2
PAIRS_PER_GROUP = PAIRS // GROUPS
TILE_ROWS = 256
TAIL_ROWS = 8
TAP0 = TAIL_ROWS - (CONV_W - 1)
MOD_BLOCK_COLS = 512
VMEM_LIMIT_BYTES = 60 * 1024 * 1024

OFF_Z = 0
OFF_XBC = OFF_Z + SSD_WIDTH
OFF_U = OFF_XBC + CONV_DIM
OFF_V = OFF_U + GMLP_WIDTH
OFF_G = OFF_V + GMLP_WIDTH
OFF_DT = OFF_G + GMLP_WIDTH
W_CAT_COLS = OFF_DT + LANES

Bufs = collections.namedtuple("Bufs", "proj xs bc mix")


def _dot(a, b):
    return jnp.dot(a, b, preferred_element_type=F32)


def _dot_nt(a, b):
    return lax.dot_general(a, b, (((1,), (1,)), ((), ())), preferred_element_type=F32)


def _dot_tn(a, b):
    return lax.dot_general(a, b, (((0,), (0,)), ((), ())), preferred_element_type=F32)


def _cat3(x):
    hi = x.astype(BF16).astype(F32)
    r1 = x - hi
    mid = r1.astype(BF16).astype(F32)
    lo = r1 - mid
    return hi + pltpu.roll(mid, HEADS, axis=1) + pltpu.roll(lo, 2 * HEADS, axis=1)


def _silu(x):
    return x / (1.0 + jnp.exp2(x * (-LOG2E)))


def _softplus(x):
    return jnp.maximum(x, 0.0) + jnp.log1p(jnp.exp(-jnp.abs(x)))


def _row_mean(x):
    width = x.shape[-1]
    part = x[:, 0:LANES]
    for c in range(LANES, width, LANES):
        part = part + x[:, c:c + LANES]
    return jnp.sum(part, axis=-1, keepdims=True) * (1.0 / width)


def _rms_scale(x):
    return lax.rsqrt(_row_mean(x * x) + EPS)


def _mod_kernel(c_ref, w_ref, b_ref, o_ref):
    c = _silu(c_ref[...]).astype(BF16)
    o_ref[...] = _dot(c, w_ref[...].astype(BF16)) + b_ref[...]


def _modulation(c, w_ada, b_ada):
    rows = c.shape[0]
    n = w_ada.shape[1]
    return pl.pallas_call(
        _mod_kernel,
        grid=(n // MOD_BLOCK_COLS,),
        in_specs=[
            pl.BlockSpec((rows, D_MODEL), lambda j: (0, 0)),
            pl.BlockSpec((D_MODEL, MOD_BLOCK_COLS), lambda j: (0, j)),
            pl.BlockSpec((1, MOD_BLOCK_COLS), lambda j: (0, j)),
        ],
        out_specs=pl.BlockSpec((rows, MOD_BLOCK_COLS), lambda j: (0, j)),
        out_shape=jax.ShapeDtypeStruct((rows, n), F32),
        name="adaln_mod",
    )(c, w_ada, b_ada.reshape(1, n))


CAST_BLOCK_COLS = 512
A_COLS = SSD_WIDTH + CONV_DIM
B_COLS = 3 * GMLP_WIDTH


def _cast_kernel(a_ref, b_ref, b_next_ref, wa_ref, wb_ref, wdt_ref):
    j = pl.program_id(0)
    wa_ref[...] = a_ref[...].astype(BF16)
    wide = jnp.concatenate([b_ref[...], b_next_ref[...]], axis=1)
    shifted = pltpu.roll(wide, wide.shape[1] - HEADS, axis=1)
    wb_ref[...] = shifted[:, :CAST_BLOCK_COLS].astype(BF16)

    @pl.when(j == 0)
    def _():
        head = b_ref[:, 0:LANES]
        lane = lax.broadcasted_iota(jnp.int32, head.shape, 1)
        wdt_ref[...] = jnp.where(lane < HEADS, head, 0.0).astype(BF16)


def _cast_weights(w_in):
    a_blocks = A_COLS // CAST_BLOCK_COLS
    b_blocks = B_COLS // CAST_BLOCK_COLS
    lanes_per_block = CAST_BLOCK_COLS // LANES
    return pl.pallas_call(
        _cast_kernel,
        grid=(b_blocks,),
        in_specs=[
            pl.BlockSpec((D_MODEL, CAST_BLOCK_COLS), lambda j: (0, jnp.minimum(j, a_blocks - 1))),
            pl.BlockSpec((D_MODEL, CAST_BLOCK_COLS), lambda j: (0, a_blocks + j)),
            pl.BlockSpec((D_MODEL, LANES), lambda j: (0, (a_blocks + j + 1) * lanes_per_block)),
        ],
        out_specs=[
            pl.BlockSpec((D_MODEL, CAST_BLOCK_COLS), lambda j: (0, jnp.minimum(j, a_blocks - 1))),
            pl.BlockSpec((D_MODEL, CAST_BLOCK_COLS), lambda j: (0, j)),
            pl.BlockSpec((D_MODEL, LANES), lambda j: (0, 0)),
        ],
        out_shape=[jax.ShapeDtypeStruct((D_MODEL, A_COLS), BF16),
                   jax.ShapeDtypeStruct((D_MODEL, B_COLS), BF16),
                   jax.ShapeDtypeStruct((D_MODEL, LANES), BF16)],
        compiler_params=pltpu.CompilerParams(dimension_semantics=("arbitrary",)),
        name="cast_proj_weight",
    )(w_in, w_in, w_in)


TILES_PER_STEP = 2
N_SHARED_SCRATCH = 16
N_STAGE_SCRATCH = 8


def _interleave(primary, secondary):
    total = sum(weight for _, weight in primary)
    groups = []
    done = 0
    seen = 0
    for task, weight in primary:
        seen += weight
        want = (seen * len(secondary)) // total
        groups.append([task] + secondary[done:want])
        done = want
    return groups


def _tag(task, name, needs=()):
    task.name, task.needs = name, tuple(needs)
    return task


def _run_groups(groups):
    stored = set()
    for group in groups:
        for task in group:
            missing = [n for n in getattr(task, "needs", ()) if n not in stored]
            assert not missing, (getattr(task, "name", task), missing)
        commits = [task() for task in group]
        for commit in commits:
            if commit is not None:
                commit()
        stored.update(getattr(task, "name", None) for task in group)


def _layer_kernel(*refs, nseq, seglen, lc, nt, pipelined, emit_v):
    statics = dict(nseq=nseq, seglen=seglen, lc=lc, pipelined=pipelined, emit_v=emit_v)
    shared, stage = refs[:-N_STAGE_SCRATCH], refs[-N_STAGE_SCRATCH:]
    buf_a, buf_b = Bufs(*stage[0::2]), Bufs(*stage[1::2])
    n_scratch = N_SHARED_SCRATCH + N_STAGE_SCRATCH
    ssm_out_ref = refs[len(refs) - n_scratch - (2 if emit_v else 1)]
    tail_ref, st_ref = refs[-n_scratch + 1], refs[-n_scratch + 3]
    if not pipelined:
        _tile_body(shared, buf_a, buf_a, buf_a, buf_a, **statics)
        return

    assert nt % TILES_PER_STEP == 0 and TILES_PER_STEP == 2
    i = pl.program_id(0)
    row_start = lax.rem(TILES_PER_STEP * i, nt) == 0
    n_in = len(shared) - N_SHARED_SCRATCH - 3
    views = []
    for half in range(TILES_PER_STEP):
        rows = slice(half * TILE_ROWS, (half + 1) * TILE_ROWS)
        view = list(shared)
        for k in (0, 1, n_in):
            view[k] = shared[k].at[rows, :]
        views.append(view)

    @pl.when(i == 0)
    def _():
        for ref in buf_b:
            ref[...] = jnp.zeros_like(ref)
        st_ref[...] = jnp.zeros_like(st_ref)

    @pl.when(row_start)
    def _():
        tail_ref[...] = jnp.zeros_like(tail_ref)

    _tile_body(views[0], buf_a, buf_b, buf_a, buf_b, **statics)

    @pl.when(row_start & (i > 0))
    def _():
        ssm_out_ref[0] = st_ref[...].T.reshape(HEADS, HEAD_DIM, STATE)

    @pl.when(row_start)
    def _():
        st_ref[...] = jnp.zeros_like(st_ref)

    _tile_body(views[1], buf_b, buf_a, buf_b, buf_a, **statics)


def _tile_body(refs, bm, bv, bvo, bo, *, nseq, seglen, lc, pipelined, emit_v):
    has_state = not pipelined
    it = iter(refs)
    x_ref, xo_ref = next(it), next(it)
    shift_ref, scale_ref, gate_ref = next(it), next(it), next(it)
    conv0_ref = next(it) if has_state else None
    ssm0_ref = next(it) if has_state else None
    (gpre_ref, gpost_ref, wa_ref, wb_ref, wdt_ref, convw_ref, convb_ref, dtb_ref, alog_ref,
     dskip_ref, gv_ref, betav_ref, mixw_ref, mixb_ref, wout_ref, tril_ref) = (
         next(it) for _ in range(16))
    y_ref, conv_out_ref, ssm_out_ref = next(it), next(it), next(it)
    vn_out_ref = next(it) if emit_v else None
    (h_ref, tail_ref, xp_ref, st_ref, stnew_ref, aexp_ref, xdt_ref, xlo_ref, xhi_ref, acumt_ref,
     vnb_ref, mo_ref, stb_ref, ys_ref, cb_ref, yoff_ref) = (next(it) for _ in range(N_SHARED_SCRATCH))
    pm, pv = bm.proj, bv.proj

    rows_all = slice(TAIL_ROWS, TAIL_ROWS + TILE_ROWS)
    nchunk = seglen // lc
    pad_rows = HALF - lc

    def pad_to_half(v):
        if pad_rows == 0:
            return [v]
        return [v, jnp.zeros((pad_rows, v.shape[1]), v.dtype)]

    def m_norm():
        for s in range(nseq):
            rows = slice(s * seglen, (s + 1) * seglen)
            xr = x_ref[rows, :]
            gain = gpre_ref[...] * (1.0 + scale_ref[s:s + 1, :])
            hs = (xr * _rms_scale(xr)) * gain + shift_ref[s:s + 1, :]
            h_ref[rows, :] = hs.astype(BF16)

    def m_dot(w_ref, w0, dst0, width):
        def task():
            val = _dot(h_ref[...], w_ref[:, w0:w0 + width])

            def commit():
                pm[rows_all, dst0:dst0 + width] = val
            return commit
        return _tag(task, f"m_dot{dst0 // MXU_COLS}")

    dst_starts = (list(range(OFF_XBC, OFF_U, MXU_COLS)) + list(range(OFF_Z, OFF_XBC, MXU_COLS))
                  + list(range(OFF_U, OFF_DT, MXU_COLS)))
    m_dots = [m_dot(wa_ref, d, d, MXU_COLS) if d < OFF_U else m_dot(wb_ref, d - OFF_U, d, MXU_COLS)
              for d in dst_starts] + [m_dot(wdt_ref, 0, OFF_DT, LANES)]

    def conv_store(c, val, rows):
        if c < SSD_WIDTH // LANES:
            bm.xs[rows, c * LANES:(c + 1) * LANES] = val
        else:
            c2 = c - SSD_WIDTH // LANES
            bm.bc[rows, c2 * LANES:(c2 + 1) * LANES] = val.astype(BF16)

    def m_conv(c):
        cols = slice(OFF_XBC + c * LANES, OFF_XBC + (c + 1) * LANES)
        wcols = slice(c * LANES, (c + 1) * LANES)

        def task_stream():
            pm[0:TAIL_ROWS, cols] = tail_ref[:, wcols]
            xfull = pm[:, cols]
            acc = convw_ref[0:1, wcols] * xfull
            for k in range(1, CONV_W):
                acc = pltpu.roll(acc, 1, axis=0) + convw_ref[k:k + 1, wcols] * xfull
            val = _silu(acc[TAIL_ROWS:, :] + convb_ref[:, wcols])
            tail_ref[:, wcols] = pm[TILE_ROWS:TILE_ROWS + TAIL_ROWS, cols]
            conv_out_ref[0, :, wcols] = pm[TAIL_ROWS + TILE_ROWS - (CONV_W - 1):TAIL_ROWS + TILE_ROWS, cols]

            def commit():
                conv_store(c, val, slice(0, TILE_ROWS))
            return commit

        def task_state():
            for s in range(nseq):
                rows = slice(s * seglen, (s + 1) * seglen)
                xp_ref[TAP0:TAIL_ROWS, :] = conv0_ref[s, :, wcols]
                xp_ref[TAIL_ROWS:TAIL_ROWS + seglen, :] = pm[TAIL_ROWS + s * seglen:TAIL_ROWS + (s + 1) * seglen, cols]
                acc = convb_ref[:, wcols]
                for k in range(CONV_W):
                    acc = acc + convw_ref[k:k + 1, wcols] * xp_ref[TAP0 + k:TAP0 + k + seglen, :]
                conv_store(c, _silu(acc), rows)
                conv_out_ref[s, :, wcols] = xp_ref[TAIL_ROWS + seglen - (CONV_W - 1):TAIL_ROWS + seglen, :]

        return task_stream if pipelined else task_state

    m_convs = [_tag(m_conv(c), f"m_conv{c}", [f"m_dot{(OFF_XBC + c * LANES) // MXU_COLS}"])
               for c in range(CONV_DIM // LANES)]

    lane1 = lax.broadcasted_iota(jnp.int32, (TILE_ROWS, LANES), 1)

    def v_dt():
        a_row = -jnp.exp(alog_ref[...])
        dt = _softplus(pv[rows_all, OFF_DT:OFF_DT + LANES] + dtb_ref[...])
        dt = jnp.where(lane1 < HEADS, dt, 0.0)
        csum = _dot(tril_ref[...], _cat3(dt * a_row).astype(BF16))
        a_cum = csum + pltpu.roll(csum, LANES - HEADS, axis=1) + pltpu.roll(csum, LANES - 2 * HEADS, axis=1)
        def per_head_lanes(v):
            blocks = []
            for p in range(PAIRS):
                lo = jnp.broadcast_to(v[:, p:p + 1], (TILE_ROWS, LANES))
                hi = jnp.broadcast_to(v[:, PAIRS + p:PAIRS + p + 1], (TILE_ROWS, LANES))
                blocks.append(jnp.where(lane1 < HALF, lo, hi))
            return jnp.concatenate(blocks, axis=1)

        aexp_ref[...] = per_head_lanes(a_cum)
        dt_exp = per_head_lanes(dt)
        ac_t = jnp.where(lane1 < HEADS, a_cum, 0.0).T
        even, odd = ac_t[0:PAIRS, :], ac_t[PAIRS:HEADS, :]
        gap = [jnp.zeros((PAIRS, pad_rows), F32)] if pad_rows else []
        blocks = []
        for ck in range(TILE_ROWS // lc):
            cols = slice(ck * lc, (ck + 1) * lc)
            blocks += [even[:, cols]] + gap + [odd[:, cols]] + gap
        acumt_ref[...] = jnp.concatenate(blocks, axis=1)
        xdt = bv.xs[...] * dt_exp
        lane_w = lax.broadcasted_iota(jnp.int32, (TILE_ROWS, SSD_WIDTH), 1)
        xlo = jnp.where(lax.rem(lane_w, LANES) < HALF, xdt, 0.0)
        xdt_ref[...] = xdt
        xlo_ref[...] = xlo.astype(BF16)
        xhi_ref[...] = (xdt - xlo).astype(BF16)

    li = lax.broadcasted_iota(jnp.int32, (lc, LANES), 0)
    si = lax.rem(lax.broadcasted_iota(jnp.int32, (lc, LANES), 1), HALF)
    causal = (si <= li) & (si < lc)

    def v_local(ci, g):
        rows = slice(ci * lc, (ci + 1) * lc)
        hl = slice(g * GROUP_WIDTH, (g + 1) * GROUP_WIDTH)

        def products():
            last = aexp_ref[(ci + 1) * lc - 1:(ci + 1) * lc, hl]
            xw_b = (xdt_ref[rows, hl] * jnp.exp(last - aexp_ref[rows, hl])).astype(BF16)
            b_g = bv.bc[rows, g * STATE:(g + 1) * STATE]
            c_g = bv.bc[rows, (GROUPS + g) * STATE:(GROUPS + g + 1) * STATE]
            b2 = jnp.concatenate(pad_to_half(b_g) + pad_to_half(b_g), axis=0)
            cb_ref[ci * GROUPS + g] = _dot_nt(c_g, b2)
            stnew_ref[ci, :, hl] = _dot_tn(b_g, xw_b)

        def diag():
            cb2 = cb_ref[ci * GROUPS + g]
            for i in range(PAIRS_PER_GROUP):
                p = g * PAIRS_PER_GROUP + i
                pcols = slice(p * LANES, (p + 1) * LANES)
                seg = aexp_ref[rows, pcols] - acumt_ref[p:p + 1, ci * LANES:(ci + 1) * LANES]
                w_p = (cb2 * jnp.exp(jnp.where(causal, seg, -jnp.inf))).astype(BF16)
                bd = jnp.concatenate(pad_to_half(xlo_ref[rows, pcols]) + pad_to_half(xhi_ref[rows, pcols]),
                                     axis=0)
                ys_ref[rows, pcols] = _dot(w_p, bd) + dskip_ref[:, pcols] * bv.xs[rows, pcols]
        return products, diag

    def v_state(ci, s):
        def task():
            e_last = jnp.exp(aexp_ref[(ci + 1) * lc - 1:(ci + 1) * lc, :])
            st = st_ref[...]
            stb_ref[ci] = st.astype(BF16)
            st_ref[...] = st * e_last + stnew_ref[ci]
            if has_state and ci % nchunk == nchunk - 1:
                ssm_out_ref[s] = st_ref[...].T.reshape(HEADS, HEAD_DIM, STATE)
        return task

    def v_readout(ci, g):
        rows = slice(ci * lc, (ci + 1) * lc)
        prow = slice(TAIL_ROWS + ci * lc, TAIL_ROWS + (ci + 1) * lc)
        hl = slice(g * GROUP_WIDTH, (g + 1) * GROUP_WIDTH)

        def product():
            c_g = bv.bc[rows, (GROUPS + g) * STATE:(GROUPS + g + 1) * STATE]
            yoff_ref[rows, hl] = _dot(c_g, stb_ref[ci, :, hl])

        def gate():
            y = ys_ref[rows, hl] + jnp.exp(aexp_ref[rows, hl]) * yoff_ref[rows, hl]
            zg = _silu(pv[prow, OFF_Z + g * GROUP_WIDTH:OFF_Z + (g + 1) * GROUP_WIDTH])
            bvo.mix[rows, hl] = (y * zg).astype(BF16)
        return product, gate

    def v_state_load(s):
        def task():
            st_ref[...] = ssm0_ref[s].reshape(SSD_WIDTH, STATE).T
        return task

    def v_ln():
        v = pv[rows_all, OFF_V:OFF_V + GMLP_WIDTH]
        vc = v - _row_mean(v)
        vn = (vc * _rms_scale(vc)) * gv_ref[...] + betav_ref[...]
        if emit_v:
            vn_out_ref[...] = vn
        vnb_ref[...] = vn.astype(BF16)

    def v_gmlp(g):
        gcols = slice(g * GMLP_GROUP_DIM, (g + 1) * GMLP_GROUP_DIM)

        def task():
            if mixw_ref.shape[1] == TILE_ROWS:
                mixed = _dot(mixw_ref[g], vnb_ref[:, gcols])
            else:
                lcg = mixw_ref.shape[1]
                blocks = [vnb_ref[r:r + lcg, gcols] for r in range(0, TILE_ROWS, lcg)]
                wide = _dot(mixw_ref[g], jnp.concatenate(blocks, axis=1))
                mixed = jnp.concatenate(
                    [wide[:, k * GMLP_GROUP_DIM:(k + 1) * GMLP_GROUP_DIM] for k in range(len(blocks))],
                    axis=0)
            mixed = mixed + mixb_ref[:, gcols]
            u = pv[rows_all, OFF_U + g * GMLP_GROUP_DIM:OFF_U + (g + 1) * GMLP_GROUP_DIM]
            gt = pv[rows_all, OFF_G + g * GMLP_GROUP_DIM:OFF_G + (g + 1) * GMLP_GROUP_DIM]
            bvo.mix[:, SSD_WIDTH + g * GMLP_GROUP_DIM:SSD_WIDTH + (g + 1) * GMLP_GROUP_DIM] = (
                _silu(gt) * u * mixed).astype(BF16)
        return task

    chunk_groups = [(ci, g) for ci in range(nseq * nchunk) for g in range(GROUPS)]
    v_locals = [v_local(ci, g) for ci, g in chunk_groups]
    v_states = []
    for s in range(nseq):
        if has_state:
            v_states.append(v_state_load(s))
        v_states += [v_state(s * nchunk + c, s) for c in range(nchunk)]
    v_readouts = [v_readout(ci, g) for ci, g in chunk_groups]
    v_scan = ([(t[0], 0) for t in v_locals] + [(t[1], 1) for t in v_locals]
              + [(t, 1) for t in v_states]
              + [(t[0], 0) for t in v_readouts] + [(t[1], 1) for t in v_readouts])
    v_gmlps = [v_gmlp(g) for g in range(GMLP_GROUPS)]

    def o_out(n):
        def task():
            mo_ref[:, n * MXU_COLS:(n + 1) * MXU_COLS] = _dot(
                bo.mix[...], wout_ref[:, n * MXU_COLS:(n + 1) * MXU_COLS])
        return task

    def o_final():
        mo = mo_ref[...]
        mo = mo * _rms_scale(mo)
        for s in range(nseq):
            rows = slice(s * seglen, (s + 1) * seglen)
            gain = gpost_ref[...] * gate_ref[s:s + 1, :]
            y_ref[rows, :] = xo_ref[rows, :] + gain * mo[rows, :]

    o_outs = [_tag(o_out(n), f"o_out{n}") for n in range(D_MODEL // MXU_COLS)]
    _tag(o_final, "o_final", [t.name for t in o_outs])

    if pipelined:
        n_xbc_dots = CONV_DIM // MXU_COLS
        head = _interleave([(m_norm, 1), (v_dt, 2), (v_ln, 2)], o_outs + m_dots[:n_xbc_dots])
        body = [(t, 3) for t in m_convs] + [(o_final, 3)] + v_scan + [(t, 0) for t in v_gmlps]
        groups = head + _interleave(body, m_dots[n_xbc_dots:])
    else:
        order = ([m_norm] + m_dots + m_convs + [v_dt, v_ln] + [t for t, _ in v_scan] + v_gmlps
                 + o_outs + [o_final])
        groups = [[task] for task in order]
    _run_groups(groups)


def _const_spec(shape):
    nd = len(shape)
    return pl.BlockSpec(shape, lambda j: (0,) * nd, pipeline_mode=pl.Buffered(1))


def _chunk_constants(lc):
    r = np.arange(TILE_ROWS)
    tril = ((r[:, None] >= r[None, :]) & (r[:, None] // lc == r[None, :] // lc)).astype(np.float32)
    return jnp.asarray(tril, BF16)


def _mix_weights(w_s, b_s, lc):
    nck = TILE_ROWS // lc
    mask = jnp.tril(jnp.ones((lc, lc), dtype=bool))
    w = jnp.where(mask[None], w_s[:, :lc, :lc], 0)
    if nck * GMLP_GROUP_DIM > MXU_COLS:
        eye = jnp.eye(nck, dtype=w.dtype)
        w = jnp.einsum("ab,gts->gatbs", eye, w).reshape(GMLP_GROUPS, TILE_ROWS, TILE_ROWS)
    bias = jnp.repeat(b_s[:, :lc].T, GMLP_GROUP_DIM, axis=1)
    return w.astype(BF16), jnp.tile(bias, (nck, 1))


def _layer(x, shift, scale, gate, conv0, ssm0, wl, *, nseq, seglen, lc_ssd, lc_gmlp, nt, emit_v,
           layer=0):
    (g_pre, g_post, w_a, w_b, w_dt, conv_w, conv_b, dtb, alog, dskip, g_v, beta_v, w_s, b_s,
     w_out) = wl
    assert lc_ssd == lc_gmlp or nseq == 1
    pipelined = conv0 is None
    nb = shift.shape[0]
    rows_total = x.shape[0]
    ntiles = rows_total // TILE_ROWS
    assert nseq * seglen == TILE_ROWS and nb * nt == ntiles
    tril = _chunk_constants(lc_ssd)
    mixw, mixb = _mix_weights(w_s, b_s, lc_gmlp)
    nck = TILE_ROWS // lc_ssd

    if pipelined:
        tps = TILES_PER_STEP
        assert ntiles % tps == 0

        def tile_m(j):
            return jnp.minimum(tps * j, ntiles - tps)

        def tile_v(j):
            return jnp.clip(tps * j - 1, 0, ntiles - 1)

        def tile_o(j):
            return jnp.maximum(tps * j - 2, 0)
        steps = ntiles // tps + 1
    else:
        tps = 1

        def tile_m(j):
            return j
        tile_v = tile_o = tile_m
        steps = ntiles

    def mod_spec(tile_of):
        return pl.BlockSpec((None, nseq, D_MODEL), lambda j: (tile_of(j) // nt, 0, 0))

    in_specs = [pl.BlockSpec((tps * TILE_ROWS, D_MODEL), lambda j: (tile_m(j) // tps, 0)),
                pl.BlockSpec((tps * TILE_ROWS, D_MODEL), lambda j: (tile_o(j) // tps, 0)),
                mod_spec(tile_m), mod_spec(tile_m), mod_spec(tile_o)]
    args = [x, x, shift, scale, gate]
    if not pipelined:
        in_specs += [pl.BlockSpec((None, nseq, CONV_W - 1, CONV_DIM), lambda j: (j // nt, 0, 0, 0)),
                     pl.BlockSpec((None, nseq, HEADS, HEAD_DIM, STATE), lambda j: (layer, j, 0, 0, 0))]
        args += [conv0, ssm0]
    consts = [g_pre, g_post, w_a, w_b, w_dt, conv_w, conv_b, dtb, alog, dskip, g_v, beta_v, mixw,
              mixb, w_out, tril]
    in_specs += [_const_spec(a.shape) for a in consts]
    args += consts

    out_shape = [jax.ShapeDtypeStruct((rows_total, D_MODEL), F32),
                 jax.ShapeDtypeStruct((nb, nseq, CONV_W - 1, CONV_DIM), F32),
                 jax.ShapeDtypeStruct((nb * nseq, HEADS, HEAD_DIM, STATE), F32)]
    out_specs = [pl.BlockSpec((tps * TILE_ROWS, D_MODEL), lambda j: (tile_o(j) // tps, 0)),
                 pl.BlockSpec((None, nseq, CONV_W - 1, CONV_DIM), lambda j: (tile_m(j) // nt, 0, 0, 0)),
                 pl.BlockSpec((nseq, HEADS, HEAD_DIM, STATE), lambda j: (tile_v(j) // nt, 0, 0, 0))]
    if emit_v:
        out_shape.append(jax.ShapeDtypeStruct((rows_total, GMLP_WIDTH), F32))
        out_specs.append(pl.BlockSpec((TILE_ROWS, GMLP_WIDTH), lambda j: (tile_v(j), 0)))

    spare = (TAIL_ROWS, LANES)
    stage_shapes = [((TAIL_ROWS + TILE_ROWS, W_CAT_COLS), F32),
                    ((TILE_ROWS, SSD_WIDTH), F32),
                    ((TILE_ROWS, 2 * GROUPS * STATE), BF16),
                    ((TILE_ROWS, SSD_WIDTH + GMLP_WIDTH), BF16)]
    scratch = [pltpu.VMEM((TILE_ROWS, D_MODEL), BF16),
               pltpu.VMEM((TAIL_ROWS, CONV_DIM), F32),
               pltpu.VMEM((TAIL_ROWS + seglen, LANES), F32),
               pltpu.VMEM((STATE, SSD_WIDTH), F32),
               pltpu.VMEM((nck, STATE, SSD_WIDTH), F32),
               pltpu.VMEM((TILE_ROWS, SSD_WIDTH), F32),
               pltpu.VMEM((TILE_ROWS, SSD_WIDTH), F32),
               pltpu.VMEM((TILE_ROWS, SSD_WIDTH), BF16),
               pltpu.VMEM((TILE_ROWS, SSD_WIDTH), BF16),
               pltpu.VMEM((PAIRS, nck * LANES), F32),
               pltpu.VMEM((TILE_ROWS, GMLP_WIDTH), BF16),
               pltpu.VMEM((TILE_ROWS, D_MODEL), F32),
               pltpu.VMEM((nck, STATE, SSD_WIDTH), BF16),
               pltpu.VMEM((TILE_ROWS, SSD_WIDTH), F32),
               pltpu.VMEM((nck * GROUPS, lc_ssd, LANES), F32),
               pltpu.VMEM((TILE_ROWS, SSD_WIDTH), F32)]
    assert len(scratch) == N_SHARED_SCRATCH
    for shape, dtype in stage_shapes:
        scratch += [pltpu.VMEM(shape, dtype), pltpu.VMEM(shape if pipelined else spare, dtype)]
    kern = functools.partial(_layer_kernel, nseq=nseq, seglen=seglen, lc=lc_ssd, nt=nt,
                             pipelined=pipelined, emit_v=emit_v)
    return pl.pallas_call(
        kern,
        grid=(steps,),
        in_specs=in_specs,
        out_specs=out_specs,
        out_shape=out_shape,
        scratch_shapes=scratch,
        compiler_params=pltpu.CompilerParams(
            dimension_semantics=("arbitrary",),
            vmem_limit_bytes=VMEM_LIMIT_BYTES),
        name="hybrid_layer_stream" if pipelined else "hybrid_layer_state",
    )(*args)


def _prep_weights(g_pre, g_post, w_in, conv_w, conv_b, dt_bias, a_log, d_skip, g_v, beta_v,
                  w_s, b_s, w_out):
    w_a, w_b, w_dt = _cast_weights(w_in)
    pad = (0, LANES - HEADS)
    perm = np.concatenate([np.arange(0, HEADS, 2), np.arange(1, HEADS, 2)])
    w_dt = w_dt[:, np.concatenate([perm, np.arange(HEADS, LANES)])]
    dt_bias, a_log = dt_bias[perm], a_log[perm]
    return (g_pre.reshape(1, -1), g_post.reshape(1, -1), w_a, w_b, w_dt, conv_w, conv_b.reshape(1, -1),
            jnp.pad(dt_bias, pad).reshape(1, -1), jnp.pad(a_log, pad).reshape(1, -1),
            jnp.repeat(d_skip, HEAD_DIM).reshape(1, -1), g_v.reshape(1, -1),
            beta_v.reshape(1, -1), w_s, b_s, w_out.astype(BF16))


def kernel(x_prompt, x_sample, state_conv, state_ssm, c_prompt, c_sample, w_ada, b_ada, g_pre,
           g_post, w_in, conv_w, conv_b, dt_bias, a_log, d_skip, g_v, beta_v, w_s, b_s, w_out):
    depth = w_ada.shape[0]
    bp, lp, _ = x_prompt.shape
    bs, ls, _ = x_sample.shape
    seq_per_tile = TILE_ROWS // ls
    yp = x_prompt.reshape(bp * lp, D_MODEL)
    ys = x_sample.reshape(bs * ls, D_MODEL)
    c_all = jnp.concatenate([c_prompt, c_sample], axis=0)
    c_pad = jnp.pad(c_all, ((0, (-c_all.shape[0]) % 8), (0, 0)))
    outs = [[] for _ in range(5)]
    for l in range(depth):
        wl = _prep_weights(g_pre[l], g_post[l], w_in[l], conv_w[l], conv_b[l], dt_bias[l],
                           a_log[l], d_skip[l], g_v[l], beta_v[l], w_s[l], b_s[l], w_out[l])
        mod = _modulation(c_pad, w_ada[l], b_ada[l])
        shift, scale, gate = (mod[:, i * D_MODEL:(i + 1) * D_MODEL] for i in range(3))
        mp = [m[:bp].reshape(bp, 1, D_MODEL) for m in (shift, scale, gate)]
        ms = [m[bp:bp + bs].reshape(bs // seq_per_tile, seq_per_tile, D_MODEL)
              for m in (shift, scale, gate)]
        yp, conv_p, ssm_p = _layer(
            yp, *mp, None, None, wl, nseq=1, seglen=TILE_ROWS, lc_ssd=min(lp, SSD_CHUNK),
            lc_gmlp=min(lp, GMLP_CHUNK), nt=lp // TILE_ROWS, emit_v=False)
        ys, conv_s, ssm_s, v_s = _layer(
            ys, *ms,
            state_conv[l].reshape(bs // seq_per_tile, seq_per_tile, CONV_W - 1, CONV_DIM),
            state_ssm, wl,
            nseq=seq_per_tile, seglen=ls, lc_ssd=min(ls, SSD_CHUNK), lc_gmlp=min(ls, GMLP_CHUNK),
            nt=1, emit_v=True, layer=l)
        outs[0].append(conv_p.reshape(bp, CONV_W - 1, CONV_DIM))
        outs[1].append(ssm_p)
        outs[2].append(conv_s.reshape(bs, CONV_W - 1, CONV_DIM))
        outs[3].append(ssm_s)
        outs[4].append(v_s.reshape(bs, ls, GMLP_WIDTH))
    stacked = [o[0][None] if depth == 1 else jnp.stack(o) for o in outs]
    return (yp.reshape(bp, lp, D_MODEL), ys.reshape(bs, ls, D_MODEL), *stacked)
```
